```python
import math
import jax, jax.numpy as jnp
from jax import lax
import numpy as np

D_MODEL = 1024
BATCH = 4
SEQ = 4096
DEPTH = 4
DEC_BATCH = 32
DEC_SEQ = 8
PAST_LEN = 8192
PAGE_SIZE = 128

N_A_LAYERS = DEPTH // 2
N_B_LAYERS = DEPTH - N_A_LAYERS
HEAD_DIM = 64
MIX_WIDTH = D_MODEL
MEM_HEADS = 4
MEM_DIM = MEM_HEADS * HEAD_DIM
N_MEM = 256
CONV_DIM = MIX_WIDTH - MEM_DIM
CONV_WIDTH = 3
NSA_HEADS = (MIX_WIDTH - MEM_DIM) // HEAD_DIM
NSA_KV_HEADS = 4
NSA_GROUP = NSA_HEADS // NSA_KV_HEADS
NSA_DIM = NSA_HEADS * HEAD_DIM
KV_DIM = NSA_KV_HEADS * HEAD_DIM
CMP_STRIDE = 16
CMP_LEN = 2 * CMP_STRIDE
CMP_HID = 2 * HEAD_DIM
SEL_BLOCK = 64
N_SEL = 16
WINDOW = 512
Q_BLOCK = 64
D_FF = -(-8 * D_MODEL // (3 * 256)) * 256
RMS_EPS = 1e-6
NEG_INF = -1e30
FORCE_SCORE = 1e4
ATT_SCALE = HEAD_DIM ** -0.5

kernel_name = 'yoco_shortconv_nsa_memory_decode_step'


def _alibi_list(n):
    def pow2(m):
        start = 2.0 ** (-8.0 / m)
        return [start ** (i + 1) for i in range(m)]
    if n & (n - 1) == 0:
        return pow2(n)
    c = 2 ** int(math.floor(math.log2(n)))
    return pow2(c) + _alibi_list(2 * c)[0::2][: n - c]


def rmsnorm(x, g):
    x32 = x.astype(jnp.float32)
    y = x32 * lax.rsqrt(jnp.mean(x32 * x32, axis=-1, keepdims=True) + RMS_EPS)
    return (y * g.astype(jnp.float32)).astype(x.dtype)


def swiglu(h, w_gu, w_dn):
    gate, up = jnp.split(h @ w_gu, 2, axis=-1)
    return (jax.nn.silu(gate) * up) @ w_dn


def short_conv(proj, conv_w, state):
    b_gate, c_gate, h = jnp.split(proj, 3, axis=-1)
    u = c_gate * h
    t = u.shape[1]
    u_ext = jnp.concatenate([state.astype(u.dtype), u], axis=1)
    y = conv_w[0] * u_ext[:, 0:t]
    for j in range(1, CONV_WIDTH):
        y = y + conv_w[j] * u_ext[:, j:j + t]
    return b_gate * y, u_ext[:, t:]


def mem_attention(mq, mk, mv):
    b, t = mq.shape[:2]
    q = mq.reshape(b, t, MEM_HEADS, HEAD_DIM)
    s = jnp.einsum('bthd,bmhd->bhtm', q, mk).astype(jnp.float32) * ATT_SCALE
    p = jax.nn.softmax(s, axis=-1).astype(mv.dtype)
    return jnp.einsum('bhtm,bmhd->bthd', p, mv).reshape(b, t, MEM_DIM)


def mix_a(h, conv_state, mk, mv, w_in, conv_w):
    proj = h @ w_in
    y_conv, new_state = short_conv(proj[..., :3 * CONV_DIM], conv_w, conv_state)
    y_mem = mem_attention(proj[..., 3 * CONV_DIM:], mk, mv)
    return jnp.concatenate([y_conv, y_mem], axis=-1), new_state


def project_b(h, w_in):
    proj = h @ w_in
    b, t = h.shape[:2]
    q = proj[..., :NSA_DIM].reshape(b, t, NSA_HEADS, HEAD_DIM)
    gates = jax.nn.sigmoid(proj[..., NSA_DIM:NSA_DIM + 3 * NSA_HEADS]).reshape(b, t, NSA_HEADS, 3)
    mq = proj[..., NSA_DIM + 3 * NSA_HEADS:]
    return q, gates, mq


def shared_kv(x, g_kv, w_kv):
    b, t = x.shape[:2]
    return (rmsnorm(x, g_kv) @ w_kv).reshape(b, t, 3, 2, NSA_KV_HEADS, HEAD_DIM)


def gather_pages(pool, page_table):
    g = pool[page_table]
    return g.reshape(page_table.shape[0], page_table.shape[1] * pool.shape[1], *pool.shape[2:])


def compress_blocks(k, pe, w1, w2):
    b, t = k.shape[:2]
    n_chunks = -(-t // CMP_STRIDE)
    k = jnp.pad(k, ((0, 0), (0, n_chunks * CMP_STRIDE - t), (0, 0), (0, 0)))
    ch = k.reshape(b, n_chunks, CMP_STRIDE, NSA_KV_HEADS, HEAD_DIM)
    lead = jnp.einsum('bcjkd,jde->bcke', ch + pe[None, None, :CMP_STRIDE, None, :], w1[:CMP_STRIDE])
    tail = jnp.einsum('bcjkd,jde->bcke', ch + pe[None, None, CMP_STRIDE:, None, :], w1[CMP_STRIDE:])
    out = jax.nn.silu(lead[:, :-1] + tail[:, 1:]) @ w2
    end_pos = jnp.arange(n_chunks - 1, dtype=jnp.int32) * CMP_STRIDE + (CMP_LEN - 1)
    return out, end_pos


def sel_blocks(k):
    b, t = k.shape[:2]
    n_sb = max(-(-t // SEL_BLOCK), N_SEL)
    k = jnp.pad(k, ((0, 0), (0, n_sb * SEL_BLOCK - t), (0, 0), (0, 0)))
    return k.reshape(b, n_sb, SEL_BLOCK, NSA_KV_HEADS, HEAD_DIM).transpose(0, 3, 1, 2, 4)


def nsa_context(cmp_rows, slc_rows, pe_ck, w1_ck, w2_ck, pe_cv, w1_cv, w2_cv):
    ck, c_end = compress_blocks(cmp_rows[:, :, 0], pe_ck, w1_ck, w2_ck)
    cv, _ = compress_blocks(cmp_rows[:, :, 1], pe_cv, w1_cv, w2_cv)
    return (ck, cv, c_end, sel_blocks(slc_rows[:, :, 0]), sel_blocks(slc_rows[:, :, 1]))


def nsa_block(q, gates, q_pos, ck, cv, c_end, ks_blk, vs_blk, kw, vw, w_pos, slopes):
    b, tq = q.shape[:2]
    qg = q.reshape(b, tq, NSA_KV_HEADS, NSA_GROUP, HEAD_DIM)
    sl = slopes.reshape(NSA_KV_HEADS, NSA_GROUP)
    s_c = jnp.einsum('btkgd,bnkd->btkgn', qg, ck).astype(jnp.float32) * ATT_SCALE
    dist_c = q_pos[:, None] - c_end[None, :]
    vis_c = dist_c >= 0
    s_c = s_c - sl[None, None, :, :, None] * dist_c.astype(jnp.float32)[None, :, None, None, :]
    s_c = jnp.where(vis_c[None, :, None, None, :], s_c, NEG_INF)
    any_c = jnp.any(vis_c, axis=-1).astype(jnp.float32)
    p_c = jax.nn.softmax(s_c, axis=-1) * any_c[None, :, None, None, None]
    o_c = jnp.einsum('btkgn,bnkd->btkgd', p_c.astype(cv.dtype), cv)
    n_sb = ks_blk.shape[2]
    per = SEL_BLOCK // CMP_STRIDE
    imp = p_c.sum(axis=3)
    imp = jnp.pad(imp, ((0, 0), (0, 0), (0, 0), (0, n_sb * per - imp.shape[-1])))
    imp = imp.reshape(b, tq, NSA_KV_HEADS, n_sb, per).sum(-1)
    blk = jnp.arange(n_sb, dtype=jnp.int32)[None, :]
    cur = (q_pos // SEL_BLOCK)[:, None]
    causal_b = blk <= cur
    forced = (blk == 0) | (blk == cur) | (blk == cur - 1)
    score = jnp.where(forced[None, :, None, :], FORCE_SCORE,
                      jnp.where(causal_b[None, :, None, :], imp, -jnp.inf))
    _, idx = lax.top_k(score, N_SEL)
    idx = idx.transpose(0, 2, 1, 3)
    b_ix = jnp.arange(b)[:, None, None, None]
    k_ix = jnp.arange(NSA_KV_HEADS)[None, :, None, None]
    ks = ks_blk[b_ix, k_ix, idx]
    vs = vs_blk[b_ix, k_ix, idx]
    pos_s = idx[..., None] * SEL_BLOCK + jnp.arange(SEL_BLOCK, dtype=jnp.int32)
    dist_s = q_pos[None, None, :, None, None] - pos_s
    qt = qg.transpose(0, 2, 1, 3, 4)
    s_s = jnp.einsum('bktgd,bktsld->bktgsl', qt, ks).astype(jnp.float32) * ATT_SCALE
    s_s = s_s - sl[None, :, None, :, None, None] * dist_s[:, :, :, None].astype(jnp.float32)
    s_s = jnp.where((dist_s >= 0)[:, :, :, None], s_s, NEG_INF)
    p_s = jax.nn.softmax(s_s.reshape(*s_s.shape[:4], N_SEL * SEL_BLOCK), axis=-1).reshape(s_s.shape)
    o_s = jnp.einsum('bktgsl,bktsld->bktgd', p_s.astype(vs.dtype), vs).transpose(0, 2, 1, 3, 4)
    s_w = jnp.einsum('btkgd,blkd->btkgl', qg, kw).astype(jnp.float32) * ATT_SCALE
    dist_w = q_pos[:, None] - w_pos[None, :]
    vis_w = (dist_w >= 0) & (dist_w <= WINDOW) & (w_pos >= 0)[None, :]
    s_w = s_w - sl[None, None, :, :, None] * dist_w.astype(jnp.float32)[None, :, None, None, :]
    s_w = jnp.where(vis_w[None, :, None, None, :], s_w, NEG_INF)
    p_w = jax.nn.softmax(s_w, axis=-1)
    o_w = jnp.einsum('btkgl,blkd->btkgd', p_w.astype(vw.dtype), vw)
    gg = gates.reshape(b, tq, NSA_KV_HEADS, NSA_GROUP, 3)
    o = gg[..., 0:1] * o_c + gg[..., 1:2] * o_s + gg[..., 2:3] * o_w
    return o.reshape(b, tq, NSA_DIM)


def nsa_prompt(q, gates, ctx, win_rows, slopes):
    ck, cv, c_end, ks_blk, vs_blk = ctx
    b, t = q.shape[:2]
    nqb = t // Q_BLOCK
    win_pad = jnp.pad(win_rows, ((0, 0), (WINDOW, 0), (0, 0), (0, 0), (0, 0)))
    q_b = q.reshape(b, nqb, Q_BLOCK, NSA_HEADS, HEAD_DIM).swapaxes(0, 1)
    g_b = gates.reshape(b, nqb, Q_BLOCK, NSA_HEADS, 3).swapaxes(0, 1)

    def one_block(args):
        i, qb, gb = args
        start = i * Q_BLOCK
        q_pos = start + jnp.arange(Q_BLOCK, dtype=jnp.int32)
        wb = lax.dynamic_slice_in_dim(win_pad, start, WINDOW + Q_BLOCK, axis=1)
        w_pos = start - WINDOW + jnp.arange(WINDOW + Q_BLOCK, dtype=jnp.int32)
        return nsa_block(qb, gb, q_pos, ck, cv, c_end, ks_blk, vs_blk, wb[:, :, 0], wb[:, :, 1], w_pos, slopes)

    out = lax.map(one_block, (jnp.arange(nqb, dtype=jnp.int32), q_b, g_b))
    return out.swapaxes(0, 1).reshape(b, t, NSA_DIM)


def setup_inputs(seed: int = 0) -> dict:
    key = jax.random.key(seed)
    ks = jax.random.split(key, 32)
    f32 = jnp.float32
    n_pages = PAST_LEN // PAGE_SIZE
    n_pool = (5 * DEC_BATCH * n_pages + 3) // 4
    win_len = min(WINDOW, PAST_LEN)

    def nrm(k, shape, scale=1.0):
        return jax.random.normal(k, shape, f32) * scale

    def gain(k, shape):
        return 1.0 + 0.02 * jax.random.normal(k, shape, f32)

    page_table = jax.random.permutation(ks[0], n_pool)[: DEC_BATCH * n_pages].reshape(DEC_BATCH, n_pages).astype(jnp.int32)
    return {
        'x_prompt': nrm(ks[1], (BATCH, SEQ, D_MODEL)),
        'x_sample': nrm(ks[2], (DEC_BATCH, DEC_SEQ, D_MODEL)),
        'state_conv': nrm(ks[3], (N_A_LAYERS, DEC_BATCH, CONV_WIDTH - 1, CONV_DIM)),
        'cache_mem_kv': nrm(ks[4], (DEPTH, DEC_BATCH, N_MEM, 2, MEM_HEADS, HEAD_DIM)),
        'cache_cmp_kv': nrm(ks[5], (n_pool, PAGE_SIZE, 2, NSA_KV_HEADS, HEAD_DIM)),
        'cache_slc_kv': nrm(ks[6], (n_pool, PAGE_SIZE, 2, NSA_KV_HEADS, HEAD_DIM)),
        'state_win_kv': nrm(ks[7], (DEC_BATCH, win_len, 2, NSA_KV_HEADS, HEAD_DIM)),
        'page_table': page_table,
        'mem_prompt': nrm(ks[8], (BATCH, N_MEM, D_MODEL)),
        'g_mix': gain(ks[9], (DEPTH, D_MODEL)),
        'w_in_a': nrm(ks[10], (N_A_LAYERS, D_MODEL, 3 * CONV_DIM + MEM_DIM), D_MODEL ** -0.5),
        'conv_w': nrm(ks[11], (N_A_LAYERS, CONV_WIDTH, CONV_DIM), CONV_WIDTH ** -0.5),
        'w_in_b': nrm(ks[12], (N_B_LAYERS, D_MODEL, NSA_DIM + 3 * NSA_HEADS + MEM_DIM), D_MODEL ** -0.5),
        'w_o': nrm(ks[13], (DEPTH, MIX_WIDTH, D_MODEL), MIX_WIDTH ** -0.5),
        'w_mkv': nrm(ks[14], (DEPTH, D_MODEL, 2 * MEM_DIM), D_MODEL ** -0.5),
        'g_mem': gain(ks[15], (D_MODEL,)),
        'g_kv': gain(ks[16], (D_MODEL,)),
        'w_kv': nrm(ks[17], (D_MODEL, 6 * KV_DIM), D_MODEL ** -0.5),
        'pe_ck': nrm(ks[18], (CMP_LEN, HEAD_DIM), 0.1),
        'w1_ck': nrm(ks[19], (CMP_LEN, HEAD_DIM, CMP_HID), (CMP_LEN * HEAD_DIM) ** -0.5),
        'w2_ck': nrm(ks[20], (CMP_HID, HEAD_DIM), CMP_HID ** -0.5),
        'pe_cv': nrm(ks[21], (CMP_LEN, HEAD_DIM), 0.1),
        'w1_cv': nrm(ks[22], (CMP_LEN, HEAD_DIM, CMP_HID), (CMP_LEN * HEAD_DIM) ** -0.5),
        'w2_cv': nrm(ks[23], (CMP_HID, HEAD_DIM), CMP_HID ** -0.5),
        'g_ffn': gain(ks[24], (DEPTH, D_MODEL)),
        'w_gu': nrm(ks[25], (DEPTH, D_MODEL, 2 * D_FF), D_MODEL ** -0.5),
        'w_dn': nrm(ks[26], (DEPTH, D_FF, D_MODEL), D_FF ** -0.5),
        'g_final': gain(ks[27], (D_MODEL,)),
    }


def reference(x_prompt, x_sample, state_conv, cache_mem_kv, cache_cmp_kv, cache_slc_kv, state_win_kv,
              page_table, mem_prompt, g_mix, w_in_a, conv_w, w_in_b, w_o, w_mkv, g_mem, g_kv, w_kv,
              pe_ck, w1_ck, w2_ck, pe_cv, w1_cv, w2_cv, g_ffn, w_gu, w_dn, g_final):
    slopes = jnp.asarray(np.array(_alibi_list(NSA_HEADS), dtype=np.float32))
    bp, tp = x_prompt.shape[:2]
    bs, ts = x_sample.shape[:2]
    n_mem = mem_prompt.shape[1]
    win_len = state_win_kv.shape[1]
    mem_kv_p = jnp.einsum('bmd,lde->lbme', rmsnorm(mem_prompt, g_mem), w_mkv).reshape(
        DEPTH, bp, n_mem, 2, MEM_HEADS, HEAD_DIM)
    xp, xs = x_prompt, x_sample
    conv_p, conv_s = [], []
    for l in range(DEPTH):
        mk_p, mv_p = mem_kv_p[l, :, :, 0], mem_kv_p[l, :, :, 1]
        mk_s, mv_s = cache_mem_kv[l, :, :, 0], cache_mem_kv[l, :, :, 1]
        hp = rmsnorm(xp, g_mix[l])
        hs = rmsnorm(xs, g_mix[l])
        if l < N_A_LAYERS:
            zero_state = jnp.zeros((bp, CONV_WIDTH - 1, CONV_DIM), xp.dtype)
            yp, st_p = mix_a(hp, zero_state, mk_p, mv_p, w_in_a[l], conv_w[l])
            ys, st_s = mix_a(hs, state_conv[l], mk_s, mv_s, w_in_a[l], conv_w[l])
            conv_p.append(st_p)
            conv_s.append(st_s)
        else:
            if l == N_A_LAYERS:
                kv_p = shared_kv(xp, g_kv, w_kv)
                kv_s = shared_kv(xs, g_kv, w_kv)
                cmp_new_p, slc_new_p, win_new_p = kv_p[:, :, 0], kv_p[:, :, 1], kv_p[:, :, 2]
                cmp_new_s, slc_new_s, win_new_s = kv_s[:, :, 0], kv_s[:, :, 1], kv_s[:, :, 2]
                full_cmp_s = jnp.concatenate([gather_pages(cache_cmp_kv, page_table).astype(cmp_new_s.dtype), cmp_new_s], axis=1)
                full_slc_s = jnp.concatenate([gather_pages(cache_slc_kv, page_table).astype(slc_new_s.dtype), slc_new_s], axis=1)
                ctx_p = nsa_context(cmp_new_p, slc_new_p, pe_ck, w1_ck, w2_ck, pe_cv, w1_cv, w2_cv)
                ctx_s = nsa_context(full_cmp_s, full_slc_s, pe_ck, w1_ck, w2_ck, pe_cv, w1_cv, w2_cv)
                win_full_s = jnp.concatenate([state_win_kv.astype(win_new_s.dtype), win_new_s], axis=1)
                q_pos_s = PAST_LEN + jnp.arange(ts, dtype=jnp.int32)
                w_pos_s = PAST_LEN - win_len + jnp.arange(win_len + ts, dtype=jnp.int32)
            j = l - N_A_LAYERS
            qp, gp, mqp = project_b(hp, w_in_b[j])
            qs, gs, mqs = project_b(hs, w_in_b[j])
            yp = jnp.concatenate([nsa_prompt(qp, gp, ctx_p, win_new_p, slopes),
                                  mem_attention(mqp, mk_p, mv_p)], axis=-1)
            ys = jnp.concatenate([nsa_block(qs, gs, q_pos_s, *ctx_s, win_full_s[:, :, 0], win_full_s[:, :, 1], w_pos_s, slopes),
                                  mem_attention(mqs, mk_s, mv_s)], axis=-1)
        xp = xp + yp @ w_o[l]
        xs = xs + ys @ w_o[l]
        xp = xp + swiglu(rmsnorm(xp, g_ffn[l]), w_gu[l], w_dn[l])
        xs = xs + swiglu(rmsnorm(xs, g_ffn[l]), w_gu[l], w_dn[l])
    y_prompt = rmsnorm(xp, g_final)
    y_sample = rmsnorm(xs, g_final)
    conv_state_p = jnp.stack(conv_p)
    conv_state_s = jnp.stack(conv_s)
    win_kv_p = win_new_p[:, tp - min(WINDOW, tp):]
    win_kv_s = win_full_s[:, ts:]
    return (y_prompt, y_sample, conv_state_p, conv_state_s, mem_kv_p, cmp_new_p, slc_new_p, win_kv_p, cmp_new_s, slc_new_s, win_kv_s)
```

```python
import functools
import math

import numpy as np
import jax
import jax.numpy as jnp
from jax import lax
from jax.experimental import pallas as pl
from jax.experimental.pallas import tpu as pltpu

F32 = jnp.float32
BF16 = jnp.bfloat16

HEAD_DIM = 64
MEM_HEADS = 4
MEM_DIM = MEM_HEADS * HEAD_DIM
NSA_KV_HEADS = 4
NSA_GROUP = 3
NSA_HEADS = NSA_KV_HEADS * NSA_GROUP
NSA_DIM = NSA_HEADS * HEAD_DIM
KV_DIM = NSA_KV_HEADS * HEAD_DIM
CONV_WIDTH = 3
CMP_STRIDE = 16
CMP_LEN = 32
SEL_BLOCK = 64
N_SEL = 16
WINDOW = 512
Q_BLOCK = 64
RMS_EPS = 1e-6
NEG_INF = -1e30
FORCE_SCORE = 1e4
ATT_SCALE = HEAD_DIM ** -0.5

VMEM_LIMIT = 48 * 1024 * 1024


def _cparams(*sem):
    return pltpu.CompilerParams(dimension_semantics=sem, vmem_limit_bytes=VMEM_LIMIT)


def _row_tile(n, want):
    t = min(n, want)
    while n % t:
        t //= 2
    return t


def _rms_matmul_kernel(x_ref, g_ref, w_ref, o_ref, h_ref):
    @pl.when(pl.program_id(1) == 0)
    def _():
        x = x_ref[...]
        ms = jnp.mean(x * x, axis=-1, keepdims=True)
        h_ref[...] = (x * lax.rsqrt(ms + RMS_EPS) * g_ref[...]).astype(BF16)

    o_ref[...] = jnp.dot(h_ref[...], w_ref[...], preferred_element_type=F32)


def rms_matmul(x, g, w, *, tm=512, tn=512):
    n, d = x.shape
    c = w.shape[1]
    tm = _row_tile(n, tm)
    tn = _row_tile(c, tn)
    return pl.pallas_call(
        _rms_matmul_kernel,
        out_shape=jax.ShapeDtypeStruct((n, c), F32),
        grid=(n // tm, c // tn),
        in_specs=[
            pl.BlockSpec((tm, d), lambda i, j: (i, 0)),
            pl.BlockSpec((1, d), lambda i, j: (0, 0)),
            pl.BlockSpec((d, tn), lambda i, j: (0, j)),
        ],
        out_specs=pl.BlockSpec((tm, tn), lambda i, j: (i, j)),
        scratch_shapes=[pltpu.VMEM((tm, d), BF16)],
        compiler_params=_cparams("parallel", "arbitrary"),
        name="rms_matmul",
    )(x, g.reshape(1, d), w)


def _proj_residual_kernel(n_pairs, x_ref, *refs):
    a_refs = refs[:n_pairs]
    w_refs = refs[n_pairs:2 * n_pairs]
    o_ref = refs[2 * n_pairs]
    acc = x_ref[...]
    for a_ref, w_ref in zip(a_refs, w_refs):
        acc = acc + jnp.dot(a_ref[...].astype(BF16), w_ref[...], preferred_element_type=F32)
    o_ref[...] = acc


def proj_residual(x, pairs, *, tm=512):
    n, d = x.shape
    tm = _row_tile(n, tm)
    in_specs = [pl.BlockSpec((tm, d), lambda i: (i, 0))]
    args = [x]
    for a, blk, imap, _ in pairs:
        in_specs.append(pl.BlockSpec(blk, imap))
        args.append(a)
    for _, _, _, w in pairs:
        in_specs.append(pl.BlockSpec(w.shape, lambda i: (0, 0)))
        args.append(w)
    return pl.pallas_call(
        functools.partial(_proj_residual_kernel, len(pairs)),
        out_shape=jax.ShapeDtypeStruct((n, d), F32),
        grid=(n // tm,),
        in_specs=in_specs,
        out_specs=pl.BlockSpec((tm, d), lambda i: (i, 0)),
        compiler_params=_cparams("parallel"),
        name="proj_residual",
    )(*args)


def _ffn_kernel(x_ref, g_ref, wg_ref, wu_ref, wd_ref, o_ref, h_ref, acc_ref):
    f = pl.program_id(1)

    @pl.when(f == 0)
    def _():
        x = x_ref[...]
        ms = jnp.mean(x * x, axis=-1, keepdims=True)
        h_ref[...] = (x * lax.rsqrt(ms + RMS_EPS) * g_ref[...]).astype(BF16)
        acc_ref[...] = x

    h = h_ref[...]
    gate = jnp.dot(h, wg_ref[...], preferred_element_type=F32)
    up = jnp.dot(h, wu_ref[...], preferred_element_type=F32)
    act = (gate * jax.nn.sigmoid(gate) * up).astype(BF16)
    acc_ref[...] += jnp.dot(act, wd_ref[...], preferred_element_type=F32)

    @pl.when(f == pl.num_programs(1) - 1)
    def _():
        o_ref[...] = acc_ref[...]


def ffn(x, g, w_gu, w_dn, *, tm=512, tf=1408):
    n, d = x.shape
    dff = w_dn.shape[0]
    tm = _row_tile(n, tm)
    nf = dff // tf
    return pl.pallas_call(
        _ffn_kernel,
        out_shape=jax.ShapeDtypeStruct((n, d), F32),
        grid=(n // tm, nf),
        in_specs=[
            pl.BlockSpec((tm, d), lambda i, f: (i, 0)),
            pl.BlockSpec((1, d), lambda i, f: (0, 0)),
            pl.BlockSpec((d, tf), lambda i, f: (0, f)),
            pl.BlockSpec((d, tf), lambda i, f: (0, f + nf)),
            pl.BlockSpec((tf, d), lambda i, f: (f, 0)),
        ],
        out_specs=pl.BlockSpec((tm, d), lambda i, f: (i, 0)),
        scratch_shapes=[pltpu.VMEM((tm, d), BF16), pltpu.VMEM((tm, d), F32)],
        compiler_params=_cparams("parallel", "arbitrary"),
        name="ffn",
    )(x, g.reshape(1, d), w_gu, w_gu, w_dn)


def _rmsnorm_kernel(x_ref, g_ref, o_ref):
    x = x_ref[...]
    ms = jnp.mean(x * x, axis=-1, keepdims=True)
    o_ref[...] = x * lax.rsqrt(ms + RMS_EPS) * g_ref[...]


def rmsnorm_rows(x, g, *, tm=1024):
    n, d = x.shape
    tm = _row_tile(n, tm)
    return pl.pallas_call(
        _rmsnorm_kernel,
        out_shape=jax.ShapeDtypeStruct((n, d), F32),
        grid=(n // tm,),
        in_specs=[pl.BlockSpec((tm, d), lambda i: (i, 0)), pl.BlockSpec((1, d), lambda i: (0, 0))],
        out_specs=pl.BlockSpec((tm, d), lambda i: (i, 0)),
        compiler_params=_cparams("parallel"),
        name="rmsnorm",
    )(x, g.reshape(1, d))


def _conv_kernel(b_ref, c_ref, h_ref, cp_ref, hp_ref, st_ref, w_ref, y_ref, ns_ref):
    i = pl.program_id(1)
    u = c_ref[...] * h_ref[...]
    tt = u.shape[0]
    prev = cp_ref[...] * hp_ref[...]
    st = st_ref[0]
    first = i == 0
    p1 = jnp.where(first, st[1:2], prev[7:8])
    p2 = jnp.where(first, st[0:1], prev[6:7])
    row = lax.broadcasted_iota(jnp.int32, u.shape, 0)
    u1 = jnp.where(row == 0, p1, pltpu.roll(u, 1, 0))
    u2 = jnp.where(row == 0, p2, jnp.where(row == 1, p1, pltpu.roll(u, 2, 0)))
    w = w_ref[...]
    y = w[0:1] * u2 + w[1:2] * u1 + w[2:3] * u
    y_ref[...] = b_ref[...] * y
    ns_ref[0] = u[tt - 2:tt]


def gated_conv(proj, state, conv_w, batch, t, *, tt=512):
    c = conv_w.shape[1]
    tt = _row_tile(t, tt)
    nt = t // tt
    r8 = tt // 8

    def prev_map(col):
        return lambda b, i: (jnp.maximum((b * nt + i) * r8 - 1, 0), col)

    return pl.pallas_call(
        _conv_kernel,
        out_shape=(jax.ShapeDtypeStruct((batch * t, c), F32), jax.ShapeDtypeStruct((batch, 2, c), F32)),
        grid=(batch, nt),
        in_specs=[
            pl.BlockSpec((tt, c), lambda b, i: (b * nt + i, 0)),
            pl.BlockSpec((tt, c), lambda b, i: (b * nt + i, 1)),
            pl.BlockSpec((tt, c), lambda b, i: (b * nt + i, 2)),
            pl.BlockSpec((8, c), prev_map(1)),
            pl.BlockSpec((8, c), prev_map(2)),
            pl.BlockSpec((1, 2, c), lambda b, i: (b, 0, 0)),
            pl.BlockSpec((CONV_WIDTH, c), lambda b, i: (0, 0)),
        ],
        out_specs=(
            pl.BlockSpec((tt, c), lambda b, i: (b * nt + i, 0)),
            pl.BlockSpec((1, 2, c), lambda b, i: (b, 0, 0)),
        ),
        compiler_params=_cparams("parallel", "arbitrary"),
        name="gated_conv",
    )(proj, proj, proj, proj, proj, state, conv_w)


def _mem_attn_kernel(q_ref, kv_ref, o_ref):
    q = q_ref[...] * ATT_SCALE
    kv = kv_ref[0]
    outs = []
    for h in range(MEM_HEADS):
        qh = q[:, h * HEAD_DIM:(h + 1) * HEAD_DIM].astype(BF16)
        kh = kv[:, h * HEAD_DIM:(h + 1) * HEAD_DIM].astype(BF16)
        vh = kv[:, MEM_DIM + h * HEAD_DIM:MEM_DIM + (h + 1) * HEAD_DIM].astype(BF16)
        s = lax.dot_general(qh, kh, (((1,), (1,)), ((), ())), preferred_element_type=F32)
        m = jnp.max(s, axis=-1, keepdims=True)
        e = jnp.exp(s - m)
        p = e / jnp.sum(e, axis=-1, keepdims=True)
        outs.append(jnp.dot(p.astype(BF16), vh, preferred_element_type=F32))
    o_ref[...] = jnp.concatenate(outs, axis=-1)


def mem_attention(proj, col_block, mem_kv, batch, t, *, tt=512):
    tt = _row_tile(t, tt)
    nt = t // tt
    n_mem = mem_kv.shape[1]
    return pl.pallas_call(
        _mem_attn_kernel,
        out_shape=jax.ShapeDtypeStruct((batch * t, MEM_DIM), F32),
        grid=(batch, nt),
        in_specs=[
            pl.BlockSpec((tt, MEM_DIM), lambda b, i: (b * nt + i, col_block)),
            pl.BlockSpec((1, n_mem, 2 * MEM_DIM), lambda b, i: (b, 0, 0)),
        ],
        out_specs=pl.BlockSpec((tt, MEM_DIM), lambda b, i: (b * nt + i, 0)),
        compiler_params=_cparams("parallel", "arbitrary"),
        name="mem_attention",
    )(proj, mem_kv)


def _alibi_list(n):
    def pow2(m):
        start = 2.0 ** (-8.0 / m)
        return [start ** (i + 1) for i in range(m)]
    if n & (n - 1) == 0:
        return pow2(n)
    c = 2 ** int(math.floor(math.log2(n)))
    return pow2(c) + _alibi_list(2 * c)[0::2][: n - c]


def _t_gather_pages(pool, page_table):
    g = pool[page_table]
    return g.reshape(page_table.shape[0], page_table.shape[1] * pool.shape[1], *pool.shape[2:])


def _t_compress_blocks(k, pe, w1, w2):
    b, t = k.shape[:2]
    n_chunks = -(-t // CMP_STRIDE)
    k = jnp.pad(k, ((0, 0), (0, n_chunks * CMP_STRIDE - t), (0, 0), (0, 0)))
    ch = k.reshape(b, n_chunks, CMP_STRIDE, NSA_KV_HEADS, HEAD_DIM)
    lead = jnp.einsum('bcjkd,jde->bcke', ch + pe[None, None, :CMP_STRIDE, None, :], w1[:CMP_STRIDE])
    tail = jnp.einsum('bcjkd,jde->bcke', ch + pe[None, None, CMP_STRIDE:, None, :], w1[CMP_STRIDE:])
    out = jax.nn.silu(lead[:, :-1] + tail[:, 1:]) @ w2
    end_pos = jnp.arange(n_chunks - 1, dtype=jnp.int32) * CMP_STRIDE + (CMP_LEN - 1)
    return out, end_pos


def _t_sel_blocks(k):
    b, t = k.shape[:2]
    n_sb = max(-(-t // SEL_BLOCK), N_SEL)
    k = jnp.pad(k, ((0, 0), (0, n_sb * SEL_BLOCK - t), (0, 0), (0, 0)))
    return k.reshape(b, n_sb, SEL_BLOCK, NSA_KV_HEADS, HEAD_DIM).transpose(0, 3, 1, 2, 4)


def _t_nsa_context(cmp_rows, slc_rows, pe_ck, w1_ck, w2_ck, pe_cv, w1_cv, w2_cv):
    ck, c_end = _t_compress_blocks(cmp_rows[:, :, 0], pe_ck, w1_ck, w2_ck)
    cv, _ = _t_compress_blocks(cmp_rows[:, :, 1], pe_cv, w1_cv, w2_cv)
    return (ck, cv, c_end, _t_sel_blocks(slc_rows[:, :, 0]), _t_sel_blocks(slc_rows[:, :, 1]))


def _t_nsa_block(q, gates, q_pos, ck, cv, c_end, ks_blk, vs_blk, kw, vw, w_pos, slopes):
    b, tq = q.shape[:2]
    qg = q.reshape(b, tq, NSA_KV_HEADS, NSA_GROUP, HEAD_DIM)
    sl = slopes.reshape(NSA_KV_HEADS, NSA_GROUP)
    s_c = jnp.einsum('btkgd,bnkd->btkgn', qg, ck).astype(jnp.float32) * ATT_SCALE
    dist_c = q_pos[:, None] - c_end[None, :]
    vis_c = dist_c >= 0
    s_c = s_c - sl[None, None, :, :, None] * dist_c.astype(jnp.float32)[None, :, None, None, :]
    s_c = jnp.where(vis_c[None, :, None, None, :], s_c, NEG_INF)
    any_c = jnp.any(vis_c, axis=-1).astype(jnp.float32)
    p_c = jax.nn.softmax(s_c, axis=-1) * any_c[None, :, None, None, None]
    o_c = jnp.einsum('btkgn,bnkd->btkgd', p_c.astype(cv.dtype), cv)
    n_sb = ks_blk.shape[2]
    per = SEL_BLOCK // CMP_STRIDE
    imp = p_c.sum(axis=3)
    imp = jnp.pad(imp, ((0, 0), (0, 0), (0, 0), (0, n_sb * per - imp.shape[-1])))
    imp = imp.reshape(b, tq, NSA_KV_HEADS, n_sb, per).sum(-1)
    blk = jnp.arange(n_sb, dtype=jnp.int32)[None, :]
    cur = (q_pos // SEL_BLOCK)[:, None]
    causal_b = blk <= cur
    forced = (blk == 0) | (blk == cur) | (blk == cur - 1)
    score = jnp.where(forced[None, :, None, :], FORCE_SCORE,
                      jnp.where(causal_b[None, :, None, :], imp, -jnp.inf))
    _, idx = lax.top_k(score, N_SEL)
    idx = idx.transpose(0, 2, 1, 3)
    b_ix = jnp.arange(b)[:, None, None, None]
    k_ix = jnp.arange(NSA_KV_HEADS)[None, :, None, None]
    ks = ks_blk[b_ix, k_ix, idx]
    vs = vs_blk[b_ix, k_ix, idx]
    pos_s = idx[..., None] * SEL_BLOCK + jnp.arange(SEL_BLOCK, dtype=jnp.int32)
    dist_s = q_pos[None, None, :, None, None] - pos_s
    qt = qg.transpose(0, 2, 1, 3, 4)
    s_s = jnp.einsum('bktgd,bktsld->bktgsl', qt, ks).astype(jnp.float32) * ATT_SCALE
    s_s = s_s - sl[None, :, None, :, None, None] * dist_s[:, :, :, None].astype(jnp.float32)
    s_s = jnp.where((dist_s >= 0)[:, :, :, None], s_s, NEG_INF)
    p_s = jax.nn.softmax(s_s.reshape(*s_s.shape[:4], N_SEL * SEL_BLOCK), axis=-1).reshape(s_s.shape)
    o_s = jnp.einsum('bktgsl,bktsld->bktgd', p_s.astype(vs.dtype), vs).transpose(0, 2, 1, 3, 4)
    s_w = jnp.einsum('btkgd,blkd->btkgl', qg, kw).astype(jnp.float32) * ATT_SCALE
    dist_w = q_pos[:, None] - w_pos[None, :]
    vis_w = (dist_w >= 0) & (dist_w <= WINDOW) & (w_pos >= 0)[None, :]
    s_w = s_w - sl[None, None, :, :, None] * dist_w.astype(jnp.float32)[None, :, None, None, :]
    s_w = jnp.where(vis_w[None, :, None, None, :], s_w, NEG_INF)
    p_w = jax.nn.softmax(s_w, axis=-1)
    o_w = jnp.einsum('btkgl,blkd->btkgd', p_w.astype(vw.dtype), vw)
    gg = gates.reshape(b, tq, NSA_KV_HEADS, NSA_GROUP, 3)
    o = gg[..., 0:1] * o_c + gg[..., 1:2] * o_s + gg[..., 2:3] * o_w
    return o.reshape(b, tq, NSA_DIM)


def _t_nsa_prompt(q, gates, ctx, win_rows, slopes):
    ck, cv, c_end, ks_blk, vs_blk = ctx
    b, t = q.shape[:2]
    nqb = t // Q_BLOCK
    win_pad = jnp.pad(win_rows, ((0, 0), (WINDOW, 0), (0, 0), (0, 0), (0, 0)))
    q_b = q.reshape(b, nqb, Q_BLOCK, NSA_HEADS, HEAD_DIM).swapaxes(0, 1)
    g_b = gates.reshape(b, nqb, Q_BLOCK, NSA_HEADS, 3).swapaxes(0, 1)

    def one_block(args):
        i, qb, gb = args
        start = i * Q_BLOCK
        q_pos = start + jnp.arange(Q_BLOCK, dtype=jnp.int32)
        wb = lax.dynamic_slice_in_dim(win_pad, start, WINDOW + Q_BLOCK, axis=1)
        w_pos = start - WINDOW + jnp.arange(WINDOW + Q_BLOCK, dtype=jnp.int32)
        return _t_nsa_block(qb, gb, q_pos, ck, cv, c_end, ks_blk, vs_blk, wb[:, :, 0], wb[:, :, 1], w_pos, slopes)

    out = lax.map(one_block, (jnp.arange(nqb, dtype=jnp.int32), q_b, g_b))
    return out.swapaxes(0, 1).reshape(b, t, NSA_DIM)


Q_SLOT = 256
B_MQ_COL = NSA_KV_HEADS * Q_SLOT
B_GATE_COL = B_MQ_COL + MEM_DIM
B_COLS = B_GATE_COL + 128


def _layout_w_in_b(w):
    d = w.shape[0]
    qw = w[:, :NSA_DIM].reshape(d, NSA_KV_HEADS, NSA_GROUP * HEAD_DIM)
    qw = jnp.pad(qw, ((0, 0), (0, 0), (0, Q_SLOT - NSA_GROUP * HEAD_DIM))).reshape(d, NSA_KV_HEADS * Q_SLOT)
    gw = jnp.pad(w[:, NSA_DIM:NSA_DIM + 3 * NSA_HEADS], ((0, 0), (0, 128 - 3 * NSA_HEADS)))
    mw = w[:, NSA_DIM + 3 * NSA_HEADS:]
    return jnp.concatenate([qw, mw, gw], axis=1).astype(BF16)


def kernel(x_prompt, x_sample, state_conv, cache_mem_kv, cache_cmp_kv, cache_slc_kv, state_win_kv, page_table,
           mem_prompt, g_mix, w_in_a, conv_w, w_in_b, w_o, w_mkv, g_mem, g_kv, w_kv, pe_ck, w1_ck, w2_ck,
           pe_cv, w1_cv, w2_cv, g_ffn, w_gu, w_dn, g_final):
    bp, tp, d = x_prompt.shape
    bs, ts = x_sample.shape[:2]
    depth = g_mix.shape[0]
    n_a = w_in_a.shape[0]
    n_mem = mem_prompt.shape[1]
    win_len = state_win_kv.shape[1]
    past_len = page_table.shape[1] * cache_cmp_kv.shape[1]
    conv_dim = conv_w.shape[2]
    slopes = jnp.asarray(np.array(_alibi_list(NSA_HEADS), dtype=np.float32))

    w_in_a16 = w_in_a.astype(BF16)
    w_in_b16 = [_layout_w_in_b(w_in_b[j]) for j in range(depth - n_a)]
    w_o16 = w_o.astype(BF16)
    w_gu16 = w_gu.astype(BF16)
    w_dn16 = w_dn.astype(BF16)
    w_kv16 = w_kv.astype(BF16)
    w_mkv16 = w_mkv.transpose(1, 0, 2).reshape(d, depth * 2 * MEM_DIM).astype(BF16)

    mkv = rms_matmul(mem_prompt.reshape(bp * n_mem, d), g_mem, w_mkv16)
    mem_kv_p = mkv.reshape(bp, n_mem, depth, 2 * MEM_DIM).transpose(2, 0, 1, 3)
    mem_kv_s = cache_mem_kv.reshape(depth, bs, n_mem, 2 * MEM_DIM)

    groups = [
        dict(x=x_prompt.reshape(bp * tp, d), b=bp, t=tp, mem=mem_kv_p, st=jnp.zeros((n_a, bp, 2, conv_dim), F32)),
        dict(x=x_sample.reshape(bs * ts, d), b=bs, t=ts, mem=mem_kv_s, st=state_conv),
    ]
    conv_out = [[], []]
    kv_rows = [None, None]
    ctx = [None, None]

    for l in range(depth):
        for gi, gr in enumerate(groups):
            x, b, t = gr["x"], gr["b"], gr["t"]
            n = b * t
            if l < n_a:
                proj = rms_matmul(x, g_mix[l], w_in_a16[l])
                y_main, new_st = gated_conv(proj, gr["st"][l], conv_w[l], b, t)
                conv_out[gi].append(new_st)
                y_mem = mem_attention(proj, 3 * conv_dim // MEM_DIM, gr["mem"][l], b, t)
            else:
                if l == n_a:
                    kv_rows[gi] = rms_matmul(x, g_kv, w_kv16).reshape(b, t, 3, 2, NSA_KV_HEADS, HEAD_DIM)
                    kv = kv_rows[gi]
                    if gi == 0:
                        ctx[gi] = _t_nsa_context(kv[:, :, 0], kv[:, :, 1], pe_ck, w1_ck, w2_ck, pe_cv, w1_cv, w2_cv)
                    else:
                        full_cmp = jnp.concatenate([_t_gather_pages(cache_cmp_kv, page_table), kv[:, :, 0]], axis=1)
                        full_slc = jnp.concatenate([_t_gather_pages(cache_slc_kv, page_table), kv[:, :, 1]], axis=1)
                        ctx[gi] = _t_nsa_context(full_cmp, full_slc, pe_ck, w1_ck, w2_ck, pe_cv, w1_cv, w2_cv)
                j = l - n_a
                proj = rms_matmul(x, g_mix[l], w_in_b16[j], tn=B_COLS)
                q = proj[:, :B_MQ_COL].reshape(n, NSA_KV_HEADS, Q_SLOT)[:, :, :NSA_GROUP * HEAD_DIM]
                q = q.reshape(b, t, NSA_HEADS, HEAD_DIM)
                gates = jax.nn.sigmoid(proj[:, B_GATE_COL:B_GATE_COL + 3 * NSA_HEADS]).reshape(b, t, NSA_HEADS, 3)
                kv = kv_rows[gi]
                if gi == 0:
                    y_main = _t_nsa_prompt(q, gates, ctx[gi], kv[:, :, 2], slopes)
                else:
                    win_full = jnp.concatenate([state_win_kv, kv[:, :, 2]], axis=1)
                    q_pos = past_len + jnp.arange(t, dtype=jnp.int32)
                    w_pos = past_len - win_len + jnp.arange(win_len + t, dtype=jnp.int32)
                    y_main = _t_nsa_block(q, gates, q_pos, *ctx[gi], win_full[:, :, 0], win_full[:, :, 1], w_pos, slopes)
                y_main = y_main.reshape(n, NSA_DIM)
                y_mem = mem_attention(proj, B_MQ_COL // MEM_DIM, gr["mem"][l], b, t)
            km = y_main.shape[1]
            tm = _row_tile(n, 512)
            x = proj_residual(x, [
                (y_main, (tm, km), lambda i: (i, 0), w_o16[l, :km]),
                (y_mem, (tm, MEM_DIM), lambda i: (i, 0), w_o16[l, km:]),
            ])
            x = ffn(x, g_ffn[l], w_gu16[l], w_dn16[l])
            gr["x"] = x

    y_prompt = rmsnorm_rows(groups[0]["x"], g_final).reshape(bp, tp, d)
    y_sample = rmsnorm_rows(groups[1]["x"], g_final).reshape(bs, ts, d)
    conv_state_p = jnp.stack(conv_out[0])
    conv_state_s = jnp.stack(conv_out[1])
    mem_kv_out = mem_kv_p.reshape(depth, bp, n_mem, 2, MEM_HEADS, HEAD_DIM)
    kvp, kvs = kv_rows
    win_kv_p = kvp[:, tp - min(WINDOW, tp):, 2]
    win_kv_s = jnp.concatenate([state_win_kv, kvs[:, :, 2]], axis=1)[:, ts:]
    return (y_prompt, y_sample, conv_state_p, conv_state_s, mem_kv_out, kvp[:, :, 0], kvp[:, :, 1], win_kv_p,
            kvs[:, :, 0], kvs[:, :, 1], win_kv_s)
```

```python
import functools
import math

import numpy as np
import jax
import jax.numpy as jnp
from jax import lax
from jax.experimental import pallas as pl
from jax.experimental.pallas import tpu as pltpu

F32 = jnp.float32
BF16 = jnp.bfloat16

HEAD_DIM = 64
MEM_HEADS = 4
MEM_DIM = MEM_HEADS * HEAD_DIM
NSA_KV_HEADS = 4
NSA_GROUP = 3
NSA_HEADS = NSA_KV_HEADS * NSA_GROUP
NSA_DIM = NSA_HEADS * HEAD_DIM
KV_DIM = NSA_KV_HEADS * HEAD_DIM
CONV_WIDTH = 3
CMP_STRIDE = 16
CMP_LEN = 32
SEL_BLOCK = 64
N_SEL = 16
WINDOW = 512
Q_BLOCK = 64
RMS_EPS = 1e-6
NEG_INF = -1e30
FORCE_SCORE = 1e4
ATT_SCALE = HEAD_DIM ** -0.5

VMEM_LIMIT = 48 * 1024 * 1024


def _cparams(*sem):
    return pltpu.CompilerParams(dimension_semantics=sem, vmem_limit_bytes=VMEM_LIMIT)


def _row_tile(n, want):
    t = min(n, want)
    while n % t:
        t //= 2
    return t


def _rms_matmul_kernel(x_ref, g_ref, w_ref, o_ref, h_ref):
    @pl.when(pl.program_id(1) == 0)
    def _():
        x = x_ref[...]
        ms = jnp.mean(x * x, axis=-1, keepdims=True)
        h_ref[...] = (x * lax.rsqrt(ms + RMS_EPS) * g_ref[...]).astype(BF16)

    o_ref[...] = jnp.dot(h_ref[...], w_ref[...], preferred_element_type=F32)


def rms_matmul(x, g, w, *, tm=512, tn=512):
    n, d = x.shape
    c = w.shape[1]
    tm = _row_tile(n, tm)
    tn = _row_tile(c, tn)
    return pl.pallas_call(
        _rms_matmul_kernel,
        out_shape=jax.ShapeDtypeStruct((n, c), F32),
        grid=(n // tm, c // tn),
        in_specs=[
            pl.BlockSpec((tm, d), lambda i, j: (i, 0)),
            pl.BlockSpec((1, d), lambda i, j: (0, 0)),
            pl.BlockSpec((d, tn), lambda i, j: (0, j)),
        ],
        out_specs=pl.BlockSpec((tm, tn), lambda i, j: (i, j)),
        scratch_shapes=[pltpu.VMEM((tm, d), BF16)],
        compiler_params=_cparams("parallel", "arbitrary"),
        name="rms_matmul",
    )(x, g.reshape(1, d), w)


def _proj_residual_kernel(n_pairs, x_ref, *refs):
    a_refs = refs[:n_pairs]
    w_refs = refs[n_pairs:2 * n_pairs]
    o_ref = refs[2 * n_pairs]
    acc = x_ref[...]
    for a_ref, w_ref in zip(a_refs, w_refs):
        acc = acc + jnp.dot(a_ref[...].astype(BF16), w_ref[...], preferred_element_type=F32)
    o_ref[...] = acc


def proj_residual(x, pairs, *, tm=512):
    n, d = x.shape
    tm = _row_tile(n, tm)
    in_specs = [pl.BlockSpec((tm, d), lambda i: (i, 0))]
    args = [x]
    for a, blk, imap, _ in pairs:
        in_specs.append(pl.BlockSpec(blk, imap))
        args.append(a)
    for _, _, _, w in pairs:
        in_specs.append(pl.BlockSpec(w.shape, lambda i: (0, 0)))
        args.append(w)
    return pl.pallas_call(
        functools.partial(_proj_residual_kernel, len(pairs)),
        out_shape=jax.ShapeDtypeStruct((n, d), F32),
        grid=(n // tm,),
        in_specs=in_specs,
        out_specs=pl.BlockSpec((tm, d), lambda i: (i, 0)),
        compiler_params=_cparams("parallel"),
        name="proj_residual",
    )(*args)


def _ffn_kernel(x_ref, g_ref, wg_ref, wu_ref, wd_ref, o_ref, h_ref, acc_ref):
    f = pl.program_id(1)

    @pl.when(f == 0)
    def _():
        x = x_ref[...]
        ms = jnp.mean(x * x, axis=-1, keepdims=True)
        h_ref[...] = (x * lax.rsqrt(ms + RMS_EPS) * g_ref[...]).astype(BF16)
        acc_ref[...] = x

    h = h_ref[...]
    gate = jnp.dot(h, wg_ref[...], preferred_element_type=F32)
    up = jnp.dot(h, wu_ref[...], preferred_element_type=F32)
    act = (gate * jax.nn.sigmoid(gate) * up).astype(BF16)
    acc_ref[...] += jnp.dot(act, wd_ref[...], preferred_element_type=F32)

    @pl.when(f == pl.num_programs(1) - 1)
    def _():
        o_ref[...] = acc_ref[...]


def ffn(x, g, w_gu, w_dn, *, tm=512, tf=1408):
    n, d = x.shape
    dff = w_dn.shape[0]
    tm = _row_tile(n, tm)
    nf = dff // tf
    return pl.pallas_call(
        _ffn_kernel,
        out_shape=jax.ShapeDtypeStruct((n, d), F32),
        grid=(n // tm, nf),
        in_specs=[
            pl.BlockSpec((tm, d), lambda i, f: (i, 0)),
            pl.BlockSpec((1, d), lambda i, f: (0, 0)),
            pl.BlockSpec((d, tf), lambda i, f: (0, f)),
            pl.BlockSpec((d, tf), lambda i, f: (0, f + nf)),
            pl.BlockSpec((tf, d), lambda i, f: (f, 0)),
        ],
        out_specs=pl.BlockSpec((tm, d), lambda i, f: (i, 0)),
        scratch_shapes=[pltpu.VMEM((tm, d), BF16), pltpu.VMEM((tm, d), F32)],
        compiler_params=_cparams("parallel", "arbitrary"),
        name="ffn",
    )(x, g.reshape(1, d), w_gu, w_gu, w_dn)


def _rmsnorm_kernel(x_ref, g_ref, o_ref):
    x = x_ref[...]
    ms = jnp.mean(x * x, axis=-1, keepdims=True)
    o_ref[...] = x * lax.rsqrt(ms + RMS_EPS) * g_ref[...]


def rmsnorm_rows(x, g, *, tm=1024):
    n, d = x.shape
    tm = _row_tile(n, tm)
    return pl.pallas_call(
        _rmsnorm_kernel,
        out_shape=jax.ShapeDtypeStruct((n, d), F32),
        grid=(n // tm,),
        in_specs=[pl.BlockSpec((tm, d), lambda i: (i, 0)), pl.BlockSpec((1, d), lambda i: (0, 0))],
        out_specs=pl.BlockSpec((tm, d), lambda i: (i, 0)),
        compiler_params=_cparams("parallel"),
        name="rmsnorm",
    )(x, g.reshape(1, d))


def _conv_kernel(b_ref, c_ref, h_ref, cp_ref, hp_ref, st_ref, w_ref, y_ref, ns_ref):
    i = pl.program_id(1)
    u = c_ref[...] * h_ref[...]
    tt = u.shape[0]
    prev = cp_ref[...] * hp_ref[...]
    st = st_ref[0]
    first = i == 0
    p1 = jnp.where(first, st[1:2], prev[7:8])
    p2 = jnp.where(first, st[0:1], prev[6:7])
    row = lax.broadcasted_iota(jnp.int32, u.shape, 0)
    u1 = jnp.where(row == 0, p1, pltpu.roll(u, 1, 0))
    u2 = jnp.where(row == 0, p2, jnp.where(row == 1, p1, pltpu.roll(u, 2, 0)))
    w = w_ref[...]
    y = w[0:1] * u2 + w[1:2] * u1 + w[2:3] * u
    y_ref[...] = b_ref[...] * y
    ns_ref[0] = u[tt - 2:tt]


def gated_conv(proj, state, conv_w, batch, t, *, tt=512):
    c = conv_w.shape[1]
    tt = _row_tile(t, tt)
    nt = t // tt
    r8 = tt // 8

    def prev_map(col):
        return lambda b, i: (jnp.maximum((b * nt + i) * r8 - 1, 0), col)

    return pl.pallas_call(
        _conv_kernel,
        out_shape=(jax.ShapeDtypeStruct((batch * t, c), F32), jax.ShapeDtypeStruct((batch, 2, c), F32)),
        grid=(batch, nt),
        in_specs=[
            pl.BlockSpec((tt, c), lambda b, i: (b * nt + i, 0)),
            pl.BlockSpec((tt, c), lambda b, i: (b * nt + i, 1)),
            pl.BlockSpec((tt, c), lambda b, i: (b * nt + i, 2)),
            pl.BlockSpec((8, c), prev_map(1)),
            pl.BlockSpec((8, c), prev_map(2)),
            pl.BlockSpec((1, 2, c), lambda b, i: (b, 0, 0)),
            pl.BlockSpec((CONV_WIDTH, c), lambda b, i: (0, 0)),
        ],
        out_specs=(
            pl.BlockSpec((tt, c), lambda b, i: (b * nt + i, 0)),
            pl.BlockSpec((1, 2, c), lambda b, i: (b, 0, 0)),
        ),
        compiler_params=_cparams("parallel", "arbitrary"),
        name="gated_conv",
    )(proj, proj, proj, proj, proj, state, conv_w)


def _mem_attn_kernel(q_ref, kv_ref, o_ref):
    q = q_ref[...] * ATT_SCALE
    kv = kv_ref[0]
    outs = []
    for h in range(MEM_HEADS):
        qh = q[:, h * HEAD_DIM:(h + 1) * HEAD_DIM].astype(BF16)
        kh = kv[:, h * HEAD_DIM:(h + 1) * HEAD_DIM].astype(BF16)
        vh = kv[:, MEM_DIM + h * HEAD_DIM:MEM_DIM + (h + 1) * HEAD_DIM].astype(BF16)
        s = lax.dot_general(qh, kh, (((1,), (1,)), ((), ())), preferred_element_type=F32)
        m = jnp.max(s, axis=-1, keepdims=True)
        e = jnp.exp(s - m)
        p = e / jnp.sum(e, axis=-1, keepdims=True)
        outs.append(jnp.dot(p.astype(BF16), vh, preferred_element_type=F32))
    o_ref[...] = jnp.concatenate(outs, axis=-1)


def mem_attention(proj, col_block, mem_kv, batch, t, *, tt=512):
    tt = _row_tile(t, tt)
    nt = t // tt
    n_mem = mem_kv.shape[1]
    return pl.pallas_call(
        _mem_attn_kernel,
        out_shape=jax.ShapeDtypeStruct((batch * t, MEM_DIM), F32),
        grid=(batch, nt),
        in_specs=[
            pl.BlockSpec((tt, MEM_DIM), lambda b, i: (b * nt + i, col_block)),
            pl.BlockSpec((1, n_mem, 2 * MEM_DIM), lambda b, i: (b, 0, 0)),
        ],
        out_specs=pl.BlockSpec((tt, MEM_DIM), lambda b, i: (b * nt + i, 0)),
        compiler_params=_cparams("parallel", "arbitrary"),
        name="mem_attention",
    )(proj, mem_kv)


def _alibi_list(n):
    def pow2(m):
        start = 2.0 ** (-8.0 / m)
        return [start ** (i + 1) for i in range(m)]
    if n & (n - 1) == 0:
        return pow2(n)
    c = 2 ** int(math.floor(math.log2(n)))
    return pow2(c) + _alibi_list(2 * c)[0::2][: n - c]


LANES = 128
PAIR = LANES // HEAD_DIM


def _to_lane_columns(src, dst_ref, n_rows):
    for c in range(dst_ref.shape[0]):
        dst_ref[c, 0:n_rows, :] = src[0:n_rows, c * LANES:(c + 1) * LANES]


def _compress_tile(buf_ref, row0, m, pe_refs, w1_refs, w2_refs):
    outs = []
    for kv in range(2):
        pe = pe_refs[kv][...]
        lhs = []
        for p in range(NSA_KV_HEADS // PAIR):
            col = kv * (KV_DIM // LANES) + p
            pieces = [
                (buf_ref[col, pl.ds(row0 + j, m, stride=CMP_STRIDE), :] + pe[j:j + 1]).astype(BF16)
                for j in range(CMP_LEN)
            ]
            lhs.append(jnp.concatenate(pieces, axis=1))
        lhs = jnp.concatenate(lhs, axis=0)
        hid = jnp.dot(lhs, w1_refs[kv][...], preferred_element_type=F32)
        act = (hid * jax.nn.sigmoid(hid)).astype(BF16)
        out = jnp.dot(act, w2_refs[kv][...], preferred_element_type=F32)
        outs.append(jnp.concatenate([out[0:m], out[m:2 * m]], axis=1))
    return outs


def _compress_prompt_kernel(x_ref, pek_ref, pev_ref, w1k_ref, w1v_ref, w2k_ref, w2v_ref, ck_ref, cv_ref, buf_ref):
    t = x_ref.shape[0]
    _to_lane_columns(x_ref, buf_ref, t)
    buf_ref[:, t:t + CMP_STRIDE, :] = jnp.zeros((buf_ref.shape[0], CMP_STRIDE, LANES), F32)
    n_blk = t // CMP_STRIDE
    m = min(n_blk, 128)
    for s in range(n_blk // m):
        ck, cv = _compress_tile(buf_ref, s * m * CMP_STRIDE, m, (pek_ref, pev_ref), (w1k_ref, w1v_ref),
                                (w2k_ref, w2v_ref))
        ck_ref[0, s * m:(s + 1) * m, :] = ck
        cv_ref[0, s * m:(s + 1) * m, :] = cv


def _compress_weights(pe, w1, w2):
    eye = jnp.eye(PAIR, dtype=F32)
    w1p = jnp.einsum('jde,qr->jqdre', w1, eye).reshape(CMP_LEN * PAIR * HEAD_DIM, PAIR * w1.shape[2])
    w2p = jnp.einsum('ed,qr->qerd', w2, eye).reshape(PAIR * w2.shape[0], PAIR * HEAD_DIM)
    pe2 = jnp.tile(pe, (1, PAIR))
    return pe2, w1p.astype(BF16), w2p.astype(BF16)


def compress_prompt(kv_rows, batch, t, cw):
    pek, w1k, w2k, pev, w1v, w2v = cw
    n_blk = t // CMP_STRIDE
    full = lambda a: pl.BlockSpec(a.shape, lambda b: (0,) * a.ndim)
    return pl.pallas_call(
        _compress_prompt_kernel,
        out_shape=(jax.ShapeDtypeStruct((batch, n_blk, KV_DIM), F32),) * 2,
        grid=(batch,),
        in_specs=[pl.BlockSpec((t, 2 * KV_DIM), lambda b: (b, 0)), full(pek), full(pev), full(w1k), full(w1v),
                  full(w2k), full(w2v)],
        out_specs=(pl.BlockSpec((1, n_blk, KV_DIM), lambda b: (b, 0, 0)),) * 2,
        scratch_shapes=[pltpu.VMEM((2 * KV_DIM // LANES, t + CMP_STRIDE, LANES), F32)],
        compiler_params=_cparams("parallel"),
        name="compress_prompt",
    )(kv_rows, pek, pev, w1k, w1v, w2k, w2v)


N_CMP_POS = 256
N_SEL_POS = N_CMP_POS * CMP_STRIDE // SEL_BLOCK
MASK_BIG = 2.0 ** 100
M_INIT = -1e38


def _flash_init(m_ref, l_ref, acc_ref):
    m_ref[...] = jnp.full(m_ref.shape, M_INIT, F32)
    l_ref[...] = jnp.zeros(l_ref.shape, F32)
    acc_ref[...] = jnp.zeros(acc_ref.shape, F32)


def _flash_update(s, v, m_ref, l_ref, acc_ref):
    kb = s.shape[1]
    m_prev = m_ref[...]
    m_new = jnp.maximum(m_prev, jnp.max(s, axis=1, keepdims=True))
    alpha = jnp.exp(m_prev - m_new)
    p = jnp.exp(s - jnp.concatenate([m_new] * (kb // LANES), axis=1))
    l_ref[...] = alpha * l_ref[...] + jnp.sum(p, axis=1, keepdims=True)
    acc_ref[...] = acc_ref[...] * alpha[:, :HEAD_DIM] + jnp.dot(p.astype(BF16), v, preferred_element_type=F32)
    m_ref[...] = m_new


def _flash_result(l_ref, acc_ref):
    return acc_ref[...] / l_ref[...][:, :HEAD_DIM]


def _group_bias(s, slopes, kpos_f, tq):
    return jnp.concatenate([s[g * tq:(g + 1) * tq] + slopes[g] * kpos_f for g in range(NSA_GROUP)], axis=0)


def _topk_mask(score):
    tq, nb = score.shape
    pad = jnp.full((tq, LANES - nb), -jnp.inf, F32)
    st = jnp.concatenate([score, pad], axis=1).T[:nb]
    idx = lax.broadcasted_iota(jnp.int32, st.shape, 0)
    rank = jnp.zeros(st.shape, F32)
    for j in range(nb):
        rj = st[j:j + 1, :]
        rank = rank + jnp.where(idx > j, (rj >= st).astype(F32), (rj > st).astype(F32))
    sel_t = (rank < N_SEL).astype(F32)
    sel_t = jnp.concatenate([sel_t, jnp.zeros((LANES - nb, tq), F32)], axis=0)
    return sel_t.T[:, :nb]


def _nsa_prompt_kernel(sl_ref, q_ref, gt_ref, ck_ref, cv_ref, ks_ref, vs_ref, kw_ref, vw_ref, o_ref,
                       m_ref, l_ref, acc_ref, *, kb):
    kvh = pl.program_id(1)
    i = pl.program_id(2)
    tq = q_ref.shape[0]
    t0 = i * tq
    slopes = [sl_ref[kvh * NSA_GROUP + g] for g in range(NSA_GROUP)]
    q = q_ref[...] * ATT_SCALE
    q3 = jnp.concatenate([q[:, g * HEAD_DIM:(g + 1) * HEAD_DIM] for g in range(NSA_GROUP)], axis=0)
    q3b = q3.astype(BF16)
    nt_dims = (((1,), (1,)), ((), ()))
    t_col = t0 + lax.broadcasted_iota(jnp.int32, (tq, 1), 0)

    pos = lax.broadcasted_iota(jnp.int32, (1, N_CMP_POS), 1)
    blk_n = (pos % N_SEL_POS) * (N_CMP_POS // N_SEL_POS) + pos // N_SEL_POS
    c_end = blk_n * CMP_STRIDE + (CMP_LEN - 1)
    n_real = ks_ref.shape[2] // CMP_STRIDE - 1
    vis_c = (c_end <= t_col) & (blk_n < n_real)
    c_end_f = c_end.astype(F32)
    s_c = lax.dot_general(q3b, ck_ref[0, 0], nt_dims, preferred_element_type=F32)
    p_c = []
    for g in range(NSA_GROUP):
        sg = jnp.where(vis_c, s_c[g * tq:(g + 1) * tq] + slopes[g] * c_end_f, NEG_INF)
        mg = jnp.max(sg, axis=1, keepdims=True)
        eg = jnp.where(vis_c, jnp.exp(sg - mg), 0.0)
        dg = jnp.sum(eg, axis=1, keepdims=True)
        p_c.append(eg / jnp.where(dg > 0.0, dg, 1.0))
    o_c = jnp.dot(jnp.concatenate(p_c, axis=0).astype(BF16), cv_ref[0, 0], preferred_element_type=F32)
    imp = p_c[0] + p_c[1] + p_c[2]
    imp = imp[:, :LANES] + imp[:, LANES:]
    imp = imp[:, :N_SEL_POS] + imp[:, N_SEL_POS:]

    blk = lax.broadcasted_iota(jnp.int32, (1, N_SEL_POS), 1)
    cur = t_col // SEL_BLOCK
    forced = (blk == 0) | (blk == cur) | (blk == cur - 1)
    score = jnp.where(forced, FORCE_SCORE, jnp.where(blk <= cur, imp, -jnp.inf))
    notsel = (1.0 - _topk_mask(score)).astype(BF16)
    q3a = jnp.concatenate([q3b, jnp.concatenate([notsel] * NSA_GROUP, axis=0)], axis=1)

    lane_k = lax.broadcasted_iota(jnp.int32, (1, kb), 1)

    def sel_chunk(j, causal):
        k0 = pl.multiple_of(j * kb, kb)
        k = ks_ref[0, 0, pl.ds(k0, kb), :]
        v = vs_ref[0, 0, pl.ds(k0, kb), :]
        kpos = k0 + lane_k
        s = _group_bias(lax.dot_general(q3a, k, nt_dims, preferred_element_type=F32), slopes, kpos.astype(F32), tq)
        if causal:
            vis = jnp.concatenate([kpos <= t_col] * NSA_GROUP, axis=0)
            s = jnp.where(vis, s, -MASK_BIG)
        _flash_update(s, v, m_ref, l_ref, acc_ref)

    _flash_init(m_ref, l_ref, acc_ref)
    n_full = t0 // kb

    def sel_body(j, c):
        sel_chunk(j, False)
        return c

    lax.fori_loop(0, n_full, sel_body, 0)
    sel_chunk(n_full, True)
    o_s = _flash_result(l_ref, acc_ref)

    lane_w = lax.broadcasted_iota(jnp.int32, (1, tq), 1)
    _flash_init(m_ref, l_ref, acc_ref)
    n_wc = WINDOW // tq + 1
    for c in range(n_wc):
        @pl.when(i >= n_wc - 1 - c)
        def _():
            k0 = pl.multiple_of(t0 - (n_wc - 1 - c) * tq, tq)
            k = kw_ref[0, 0, pl.ds(k0, tq), :]
            v = vw_ref[0, 0, pl.ds(k0, tq), :]
            kpos = k0 + lane_w
            s = _group_bias(lax.dot_general(q3b, k, nt_dims, preferred_element_type=F32), slopes,
                            kpos.astype(F32), tq)
            if c == 0 or c == n_wc - 1:
                dist = t_col - kpos
                vis = jnp.concatenate([(dist >= 0) & (dist <= WINDOW)] * NSA_GROUP, axis=0)
                s = jnp.where(vis, s, -MASK_BIG)
            _flash_update(s, v, m_ref, l_ref, acc_ref)
    o_w = _flash_result(l_ref, acc_ref)

    gates = jax.nn.sigmoid(gt_ref[...])
    outs = []
    for g in range(NSA_GROUP):
        rows = slice(g * tq, (g + 1) * tq)
        outs.append(gates[:, 3 * g:3 * g + 1] * o_c[rows] + gates[:, 3 * g + 1:3 * g + 2] * o_s[rows]
                    + gates[:, 3 * g + 2:3 * g + 3] * o_w[rows])
    o_ref[0, 0] = jnp.concatenate(outs, axis=1)


def nsa_prompt(proj, slopes, ckp, cvp, ks_aug, vs, kw, vw, batch, t, *, tq=128, kb=256):
    nt = t // tq
    gcol = B_GATE_COL // LANES
    grid_spec = pltpu.PrefetchScalarGridSpec(
        num_scalar_prefetch=1,
        grid=(batch, NSA_KV_HEADS, nt),
        in_specs=[
            pl.BlockSpec((tq, Q_SLOT), lambda b, h, i, sl: (b * nt + i, h)),
            pl.BlockSpec((tq, LANES), lambda b, h, i, sl: (b * nt + i, gcol + h)),
            pl.BlockSpec((1, 1, N_CMP_POS, HEAD_DIM), lambda b, h, i, sl: (b, h, 0, 0)),
            pl.BlockSpec((1, 1, N_CMP_POS, HEAD_DIM), lambda b, h, i, sl: (b, h, 0, 0)),
            pl.BlockSpec((1, 1, t, LANES), lambda b, h, i, sl: (b, h, 0, 0)),
            pl.BlockSpec((1, 1, t, HEAD_DIM), lambda b, h, i, sl: (b, h, 0, 0)),
            pl.BlockSpec((1, 1, t, HEAD_DIM), lambda b, h, i, sl: (b, h, 0, 0)),
            pl.BlockSpec((1, 1, t, HEAD_DIM), lambda b, h, i, sl: (b, h, 0, 0)),
        ],
        out_specs=pl.BlockSpec((1, 1, tq, NSA_GROUP * HEAD_DIM), lambda b, h, i, sl: (b, h, i, 0)),
        scratch_shapes=[pltpu.VMEM((NSA_GROUP * tq, LANES), F32), pltpu.VMEM((NSA_GROUP * tq, LANES), F32),
                        pltpu.VMEM((NSA_GROUP * tq, HEAD_DIM), F32)],
    )
    return pl.pallas_call(
        functools.partial(_nsa_prompt_kernel, kb=kb),
        out_shape=jax.ShapeDtypeStruct((batch, NSA_KV_HEADS, t, NSA_GROUP * HEAD_DIM), F32),
        grid_spec=grid_spec,
        compiler_params=_cparams("parallel", "parallel", "arbitrary"),
        name="nsa_prompt",
    )(slopes, proj, proj, ckp, cvp, ks_aug, vs, kw, vw)


def _head_major(x, batch, t):
    return x.reshape(batch, t, NSA_KV_HEADS, HEAD_DIM).transpose(0, 2, 1, 3)


def _cmp_slots(c, batch):
    n_blk = c.shape[1]
    per = N_CMP_POS // N_SEL_POS
    c = jnp.pad(c, ((0, 0), (0, N_CMP_POS - n_blk), (0, 0)))
    c = c.reshape(batch, N_SEL_POS, per, NSA_KV_HEADS, HEAD_DIM).transpose(0, 3, 2, 1, 4)
    return c.reshape(batch, NSA_KV_HEADS, N_CMP_POS, HEAD_DIM).astype(BF16)


def _page_group_dma(pt_ref, pool_ref, buf_ref, sem_ref, n_pages, pg, page_rows):
    ng = n_pages // pg

    def run(step, slot, go):
        bb = step // ng
        gg = step % ng
        for k in range(pg):
            page = pt_ref[bb * n_pages + gg * pg + k]
            cp = pltpu.make_async_copy(pool_ref.at[page], buf_ref.at[slot, pl.ds(k * page_rows, page_rows)],
                                       sem_ref.at[slot])
            cp.start() if go == "start" else cp.wait()

    return run


def _compress_sample_kernel(pt_ref, pool_ref, new_ref, pek_ref, pev_ref, w1k_ref, w1v_ref, w2k_ref, w2v_ref,
                            ck_ref, cv_ref, buf_ref, sem_ref, col_ref, *, n_pages, pg, page_rows, ts):
    b = pl.program_id(0)
    g = pl.program_id(1)
    ng = n_pages // pg
    step = b * ng + g
    n_steps = pl.num_programs(0) * ng
    slot = step % 2
    rows = pg * page_rows
    pages = _page_group_dma(pt_ref, pool_ref, buf_ref, sem_ref, n_pages, pg, page_rows)

    def halo(st, sl, go):
        bb = st // ng
        gg = st % ng

        @pl.when(gg < ng - 1)
        def _():
            page = pt_ref[bb * n_pages + (gg + 1) * pg]
            cp = pltpu.make_async_copy(pool_ref.at[page, pl.ds(0, CMP_STRIDE)],
                                       buf_ref.at[sl, pl.ds(rows, CMP_STRIDE)], sem_ref.at[sl])
            cp.start() if go == "start" else cp.wait()

        @pl.when(gg == ng - 1)
        def _():
            cp = pltpu.make_async_copy(new_ref.at[bb, :, pl.ds(0, 2 * KV_DIM)],
                                       buf_ref.at[sl, pl.ds(rows, ts)], sem_ref.at[sl])
            cp.start() if go == "start" else cp.wait()

    @pl.when(step == 0)
    def _():
        pages(0, 0, "start")
        halo(0, 0, "start")

    @pl.when(step + 1 < n_steps)
    def _():
        pages(step + 1, 1 - slot, "start")
        halo(step + 1, 1 - slot, "start")

    pages(step, slot, "wait")
    halo(step, slot, "wait")

    @pl.when(g == ng - 1)
    def _():
        buf_ref[slot, rows + ts:rows + CMP_STRIDE, :] = jnp.zeros((CMP_STRIDE - ts, buf_ref.shape[2]), F32)

    m = rows // CMP_STRIDE
    _to_lane_columns(buf_ref.at[slot], col_ref, rows + CMP_STRIDE)
    ck, cv = _compress_tile(col_ref, 0, m, (pek_ref, pev_ref), (w1k_ref, w1v_ref), (w2k_ref, w2v_ref))
    ck_ref[0] = ck
    cv_ref[0] = cv


def compress_sample(page_table, pool, new_rows, cw, *, pg):
    pek, w1k, w2k, pev, w1v, w2v = cw
    batch, n_pages = page_table.shape
    page_rows = pool.shape[1]
    ts = new_rows.shape[1]
    ng = n_pages // pg
    m = pg * page_rows // CMP_STRIDE
    full = lambda a: pl.BlockSpec(a.shape, lambda b, g, pt: (0,) * a.ndim)
    grid_spec = pltpu.PrefetchScalarGridSpec(
        num_scalar_prefetch=1,
        grid=(batch, ng),
        in_specs=[pl.BlockSpec(memory_space=pl.ANY), pl.BlockSpec(memory_space=pl.ANY), full(pek), full(pev),
                  full(w1k), full(w1v), full(w2k), full(w2v)],
        out_specs=(pl.BlockSpec((1, m, KV_DIM), lambda b, g, pt: (b, g, 0)),) * 2,
        scratch_shapes=[pltpu.VMEM((2, pg * page_rows + CMP_STRIDE, 2 * KV_DIM), F32), pltpu.SemaphoreType.DMA((2,)),
                        pltpu.VMEM((2 * KV_DIM // LANES, pg * page_rows + CMP_STRIDE, LANES), F32)],
    )
    return pl.pallas_call(
        functools.partial(_compress_sample_kernel, n_pages=n_pages, pg=pg, page_rows=page_rows, ts=ts),
        out_shape=(jax.ShapeDtypeStruct((batch, ng * m, KV_DIM), F32),) * 2,
        grid_spec=grid_spec,
        compiler_params=_cparams("arbitrary", "arbitrary"),
        name="compress_sample",
    )(page_table.reshape(-1), pool, new_rows, pek, pev, w1k, w1v, w2k, w2v)


S_COL_G = 32


def _softmax_rows0(s, vis):
    s = jnp.where(vis, s, NEG_INF)
    m = jnp.max(s, axis=0, keepdims=True)
    e = jnp.where(vis, jnp.exp(s - m), 0.0)
    d = jnp.sum(e, axis=0, keepdims=True)
    return e / jnp.where(d > 0.0, d, 1.0)


def _nsa_sample_kernel(sl_ref, pt_ref, proj_ref, ck_ref, cv_ref, new_ref, win_ref, pool_ref, y_ref,
                       buf_ref, sem_ref, s_ref, v_ref, bdq_ref, oc_ref, ow_ref, sc_ref, sel_ref, imp_ref,
                       *, n_pages, pg, page_rows, past):
    b = pl.program_id(0)
    g = pl.program_id(1)
    ng = n_pages // pg
    step = b * ng + g
    n_steps = pl.num_programs(0) * ng
    slot = step % 2
    rows = pg * page_rows
    ts = proj_ref.shape[0]
    nt_dims = (((1,), (1,)), ((), ()))
    pages = _page_group_dma(pt_ref, pool_ref, buf_ref, sem_ref, n_pages, pg, page_rows)

    @pl.when(step == 0)
    def _():
        pages(0, 0, "start")

    @pl.when(step + 1 < n_steps)
    def _():
        pages(step + 1, 1 - slot, "start")

    col = lax.broadcasted_iota(jnp.int32, (1, LANES), 1)
    col_g = col // S_COL_G
    col_h = (col % S_COL_G) // ts
    t_row = past + col % ts
    slope_row = jnp.zeros((1, LANES), F32)
    for h in range(NSA_HEADS):
        slope_row = jnp.where((col_h == h // NSA_GROUP) & (col_g == h % NSA_GROUP), sl_ref[h], slope_row)
    n_sb = past // SEL_BLOCK + 1

    @pl.when(g == 0)
    def _():
        lane_h = lax.broadcasted_iota(jnp.int32, (ts, KV_DIM), 1) // HEAD_DIM
        tiles = []
        for gg in range(NSA_GROUP):
            qg = proj_ref[:, gg * KV_DIM:(gg + 1) * KV_DIM] * ATT_SCALE
            for h in range(NSA_KV_HEADS):
                tiles.append(jnp.where(lane_h == h, qg, 0.0))
        tiles.append(jnp.zeros((LANES - NSA_HEADS * ts, KV_DIM), F32))
        bdq = jnp.concatenate(tiles, axis=0).astype(BF16)
        bdq_ref[...] = bdq

        n_c = ck_ref.shape[1]
        s_c = lax.dot_general(ck_ref[0].astype(BF16), bdq, nt_dims, preferred_element_type=F32)
        c_end = lax.broadcasted_iota(jnp.int32, (n_c, 1), 0) * CMP_STRIDE + (CMP_LEN - 1)
        dist = t_row - c_end
        p_c = _softmax_rows0(s_c - slope_row * dist.astype(F32), dist >= 0)
        oc_ref[...] = jnp.dot(p_c.T.astype(BF16), cv_ref[0].astype(BF16), preferred_element_type=F32)
        pz = jnp.where(col < NSA_GROUP * S_COL_G, p_c, 0.0)
        imp_ref[...] = pz + pltpu.roll(pz, S_COL_G, 1) + pltpu.roll(pz, 2 * S_COL_G, 1) + pltpu.roll(pz, 3 * S_COL_G, 1)
        per = SEL_BLOCK // CMP_STRIDE
        n_imp = n_c // per
        imp = imp_ref[pl.ds(0, n_imp, stride=per), :]
        for r in range(1, per):
            imp = imp + imp_ref[pl.ds(r, n_imp, stride=per), :]
        n_pad = sc_ref.shape[0]
        imp = jnp.concatenate([imp, jnp.zeros((n_pad - n_imp, LANES), F32)], axis=0)
        blk = lax.broadcasted_iota(jnp.int32, (n_pad, 1), 0)
        cur = t_row // SEL_BLOCK
        forced = (blk == 0) | (blk == cur) | (blk == cur - 1)
        sc_ref[...] = jnp.where(forced, FORCE_SCORE, jnp.where((blk <= cur) & (blk < n_sb), imp, -jnp.inf))
        st = sc_ref[...]

        def rank_body(j, rank):
            rj = sc_ref[pl.ds(j, 1), :]
            return rank + jnp.where(blk > j, (rj >= st).astype(F32), (rj > st).astype(F32))

        rank = lax.fori_loop(0, n_sb, rank_body, jnp.zeros(st.shape, F32))
        sel_ref[...] = (rank < N_SEL).astype(F32)

        wl = win_ref.shape[1]
        zpad = jnp.zeros((LANES - ts, KV_DIM), F32)
        kw = jnp.concatenate([win_ref[0, :, 0:KV_DIM], new_ref[0, :, 4 * KV_DIM:5 * KV_DIM], zpad], axis=0)
        vw = jnp.concatenate([win_ref[0, :, KV_DIM:2 * KV_DIM], new_ref[0, :, 5 * KV_DIM:6 * KV_DIM], zpad], axis=0)
        s_w = lax.dot_general(kw.astype(BF16), bdq, nt_dims, preferred_element_type=F32)
        w_pos = past - wl + lax.broadcasted_iota(jnp.int32, (wl + LANES, 1), 0)
        dist_w = t_row - w_pos
        p_w = _softmax_rows0(s_w - slope_row * dist_w.astype(F32),
                             (dist_w >= 0) & (dist_w <= WINDOW) & (w_pos >= 0))
        ow_ref[...] = jnp.dot(p_w.T.astype(BF16), vw.astype(BF16), preferred_element_type=F32)

    pages(step, slot, "wait")
    bdq = bdq_ref[...]
    base = g * rows
    kpage = buf_ref[slot, :, 0:KV_DIM].astype(BF16)
    s_g = lax.dot_general(kpage, bdq, nt_dims, preferred_element_type=F32)
    kpos = base + lax.broadcasted_iota(jnp.int32, (rows, 1), 0)
    s_g = s_g - slope_row * (t_row - kpos).astype(F32)
    row0 = pl.multiple_of(base, rows)
    for bl in range(rows // SEL_BLOCK):
        on = sel_ref[pl.ds(g * (rows // SEL_BLOCK) + bl, 1), :] > 0.0
        s_ref[pl.ds(row0 + bl * SEL_BLOCK, SEL_BLOCK), :] = jnp.where(
            on, s_g[bl * SEL_BLOCK:(bl + 1) * SEL_BLOCK], NEG_INF)
    v_ref[pl.ds(row0, rows), :] = buf_ref[slot, :, KV_DIM:2 * KV_DIM].astype(BF16)

    @pl.when(g == ng - 1)
    def _():
        tail = s_ref.shape[0] - past
        zpad = jnp.zeros((tail - ts, KV_DIM), F32)
        k_new = jnp.concatenate([new_ref[0, :, 2 * KV_DIM:3 * KV_DIM], zpad], axis=0)
        v_new = jnp.concatenate([new_ref[0, :, 3 * KV_DIM:4 * KV_DIM], zpad], axis=0)
        s_n = lax.dot_general(k_new.astype(BF16), bdq, nt_dims, preferred_element_type=F32)
        kp = past + lax.broadcasted_iota(jnp.int32, (tail, 1), 0)
        dist_n = t_row - kp
        on = (sel_ref[pl.ds(n_sb - 1, 1), :] > 0.0) & (dist_n >= 0)
        s_ref[pl.ds(past, tail), :] = jnp.where(on, s_n - slope_row * dist_n.astype(F32), NEG_INF)
        v_ref[pl.ds(past, tail), :] = v_new.astype(BF16)
        nk = s_ref.shape[0]
        m = jnp.max(s_ref[...], axis=0, keepdims=True)
        ch = rows
        l = jnp.zeros((1, LANES), F32)
        o_s = jnp.zeros((LANES, KV_DIM), F32)
        for c0 in range(0, nk, ch):
            cl = min(ch, nk - c0)
            e = jnp.exp(s_ref[c0:c0 + cl, :] - m)
            l = l + jnp.sum(e, axis=0, keepdims=True)
            if cl % LANES:
                e = jnp.concatenate([e, jnp.zeros((LANES - cl % LANES, LANES), F32)], axis=0)
                vv = jnp.concatenate([v_ref[c0:c0 + cl, :], jnp.zeros((LANES - cl % LANES, KV_DIM), BF16)], axis=0)
            else:
                vv = v_ref[c0:c0 + cl, :]
            o_s = o_s + jnp.dot(e.T.astype(BF16), vv, preferred_element_type=F32)
        inv_l = (1.0 / jnp.broadcast_to(l, (8, LANES))).T[:, 0:1]
        o_s = o_s * inv_l
        o_c = oc_ref[...]
        o_w = ow_ref[...]
        gates = jax.nn.sigmoid(proj_ref[:, (NSA_GROUP + 1) * KV_DIM:(NSA_GROUP + 1) * KV_DIM + LANES])
        outs = []
        for h in range(NSA_HEADS):
            kvh, gg = h // NSA_GROUP, h % NSA_GROUP
            r0 = gg * S_COL_G + kvh * ts
            lanes = slice(kvh * HEAD_DIM, (kvh + 1) * HEAD_DIM)
            outs.append(gates[:, 3 * h:3 * h + 1] * o_c[r0:r0 + ts, lanes]
                        + gates[:, 3 * h + 1:3 * h + 2] * o_s[r0:r0 + ts, lanes]
                        + gates[:, 3 * h + 2:3 * h + 3] * o_w[r0:r0 + ts, lanes])
        y_ref[...] = jnp.concatenate(outs, axis=1)


def nsa_sample(proj, slopes, page_table, ck, cv, new_rows, win_state, pool, *, pg):
    batch, n_pages = page_table.shape
    page_rows = pool.shape[1]
    ts = new_rows.shape[1]
    past = n_pages * page_rows
    ng = n_pages // pg
    n_c = ck.shape[1]
    wl = win_state.shape[1]
    n_sb_pad = -(-(past // SEL_BLOCK + 1) // 8) * 8
    nk = past + 16
    grid_spec = pltpu.PrefetchScalarGridSpec(
        num_scalar_prefetch=2,
        grid=(batch, ng),
        in_specs=[
            pl.BlockSpec((ts, proj.shape[1]), lambda b, g, sl, pt: (b, 0)),
            pl.BlockSpec((1, n_c, KV_DIM), lambda b, g, sl, pt: (b, 0, 0)),
            pl.BlockSpec((1, n_c, KV_DIM), lambda b, g, sl, pt: (b, 0, 0)),
            pl.BlockSpec((1, ts, new_rows.shape[2]), lambda b, g, sl, pt: (b, 0, 0)),
            pl.BlockSpec((1, wl, 2 * KV_DIM), lambda b, g, sl, pt: (b, 0, 0)),
            pl.BlockSpec(memory_space=pl.ANY),
        ],
        out_specs=pl.BlockSpec((ts, NSA_DIM), lambda b, g, sl, pt: (b, 0)),
        scratch_shapes=[
            pltpu.VMEM((2, pg * page_rows, 2 * KV_DIM), F32), pltpu.SemaphoreType.DMA((2,)),
            pltpu.VMEM((nk, LANES), F32), pltpu.VMEM((nk, KV_DIM), BF16), pltpu.VMEM((LANES, KV_DIM), BF16),
            pltpu.VMEM((LANES, KV_DIM), F32), pltpu.VMEM((LANES, KV_DIM), F32),
            pltpu.VMEM((n_sb_pad, LANES), F32), pltpu.VMEM((n_sb_pad, LANES), F32), pltpu.VMEM((n_c, LANES), F32),
        ],
    )
    return pl.pallas_call(
        functools.partial(_nsa_sample_kernel, n_pages=n_pages, pg=pg, page_rows=page_rows, past=past),
        out_shape=jax.ShapeDtypeStruct((batch * ts, NSA_DIM), F32),
        grid_spec=grid_spec,
        compiler_params=_cparams("arbitrary", "arbitrary"),
        name="nsa_sample",
    )(slopes, page_table.reshape(-1), proj, ck, cv, new_rows, win_state, pool)


S_MQ_COL = NSA_GROUP * KV_DIM
S_COLS = S_MQ_COL + MEM_DIM + LANES


def _layout_w_in_b_sample(w):
    d = w.shape[0]
    qw = w[:, :NSA_DIM].reshape(d, NSA_KV_HEADS, NSA_GROUP, HEAD_DIM).transpose(0, 2, 1, 3).reshape(d, NSA_DIM)
    gw = jnp.pad(w[:, NSA_DIM:NSA_DIM + 3 * NSA_HEADS], ((0, 0), (0, LANES - 3 * NSA_HEADS)))
    mw = w[:, NSA_DIM + 3 * NSA_HEADS:]
    return jnp.concatenate([qw, mw, gw], axis=1).astype(BF16)


Q_SLOT = 256
B_MQ_COL = NSA_KV_HEADS * Q_SLOT
B_GATE_COL = B_MQ_COL + MEM_DIM
B_COLS = B_GATE_COL + NSA_KV_HEADS * LANES


def _layout_w_in_b(w):
    d = w.shape[0]
    qw = w[:, :NSA_DIM].reshape(d, NSA_KV_HEADS, NSA_GROUP * HEAD_DIM)
    qw = jnp.pad(qw, ((0, 0), (0, 0), (0, Q_SLOT - NSA_GROUP * HEAD_DIM))).reshape(d, NSA_KV_HEADS * Q_SLOT)
    gw = w[:, NSA_DIM:NSA_DIM + 3 * NSA_HEADS].reshape(d, NSA_KV_HEADS, 3 * NSA_GROUP)
    gw = jnp.pad(gw, ((0, 0), (0, 0), (0, LANES - 3 * NSA_GROUP))).reshape(d, NSA_KV_HEADS * LANES)
    mw = w[:, NSA_DIM + 3 * NSA_HEADS:]
    return jnp.concatenate([qw, mw, gw], axis=1).astype(BF16)


def kernel(x_prompt, x_sample, state_conv, cache_mem_kv, cache_cmp_kv, cache_slc_kv, state_win_kv, page_table,
           mem_prompt, g_mix, w_in_a, conv_w, w_in_b, w_o, w_mkv, g_mem, g_kv, w_kv, pe_ck, w1_ck, w2_ck,
           pe_cv, w1_cv, w2_cv, g_ffn, w_gu, w_dn, g_final):
    bp, tp, d = x_prompt.shape
    bs, ts = x_sample.shape[:2]
    depth = g_mix.shape[0]
    n_a = w_in_a.shape[0]
    n_mem = mem_prompt.shape[1]
    win_len = state_win_kv.shape[1]
    past_len = page_table.shape[1] * cache_cmp_kv.shape[1]
    conv_dim = conv_w.shape[2]
    slopes = jnp.asarray(np.array(_alibi_list(NSA_HEADS), dtype=np.float32))

    w_in_a16 = w_in_a.astype(BF16)
    w_in_b16 = [_layout_w_in_b(w_in_b[j]) for j in range(depth - n_a)]
    w_o16 = w_o.astype(BF16)
    w_gu16 = w_gu.astype(BF16)
    w_dn16 = w_dn.astype(BF16)
    w_kv16 = w_kv.astype(BF16)
    w_mkv16 = w_mkv.transpose(1, 0, 2).reshape(d, depth * 2 * MEM_DIM).astype(BF16)

    mkv = rms_matmul(mem_prompt.reshape(bp * n_mem, d), g_mem, w_mkv16)
    mem_kv_p = mkv.reshape(bp, n_mem, depth, 2 * MEM_DIM).transpose(2, 0, 1, 3)
    mem_kv_s = cache_mem_kv.reshape(depth, bs, n_mem, 2 * MEM_DIM)

    groups = [
        dict(x=x_prompt.reshape(bp * tp, d), b=bp, t=tp, mem=mem_kv_p, st=jnp.zeros((n_a, bp, 2, conv_dim), F32)),
        dict(x=x_sample.reshape(bs * ts, d), b=bs, t=ts, mem=mem_kv_s, st=state_conv),
    ]
    conv_out = [[], []]
    kv_rows = [None, None]

    cw = _compress_weights(pe_ck, w1_ck, w2_ck) + _compress_weights(pe_cv, w1_cv, w2_cv)
    nsa_in = None
    w_in_bs16 = [_layout_w_in_b_sample(w_in_b[j]) for j in range(depth - n_a)]
    n_pages = page_table.shape[1]
    page_rows = cache_cmp_kv.shape[1]
    pg = 16 if n_pages % 16 == 0 else n_pages
    pool_cmp = cache_cmp_kv.reshape(cache_cmp_kv.shape[0], page_rows, 2 * KV_DIM)
    pool_slc = cache_slc_kv.reshape(cache_slc_kv.shape[0], page_rows, 2 * KV_DIM)
    win_state = state_win_kv.reshape(bs, win_len, 2 * KV_DIM)
    assert ts == 8 and past_len % SEL_BLOCK == 0 and tp % 256 == 0 and tp // SEL_BLOCK <= N_SEL_POS

    def wo_pairs(l, n, y_main, y_mem, nsa_layout, b, t):
        tm = _row_tile(n, 512)
        if nsa_layout:
            hw = NSA_GROUP * HEAD_DIM
            nt = t // tm
            y2d = y_main.reshape(b * NSA_KV_HEADS * t, hw)
            pairs = [(y2d, (tm, hw), (lambda i, h=h: (((i // nt) * NSA_KV_HEADS + h) * nt + i % nt, 0)),
                      w_o16[l, h * hw:(h + 1) * hw]) for h in range(NSA_KV_HEADS)]
            km = NSA_DIM
        else:
            km = y_main.shape[1]
            pairs = [(y_main, (tm, km), lambda i: (i, 0), w_o16[l, :km])]
        return pairs + [(y_mem, (tm, MEM_DIM), lambda i: (i, 0), w_o16[l, km:])]

    for l in range(depth):
        for gi, gr in enumerate(groups):
            x, b, t = gr["x"], gr["b"], gr["t"]
            n = b * t
            nsa_layout = False
            if l < n_a:
                proj = rms_matmul(x, g_mix[l], w_in_a16[l])
                y_main, new_st = gated_conv(proj, gr["st"][l], conv_w[l], b, t)
                conv_out[gi].append(new_st)
                y_mem = mem_attention(proj, 3 * conv_dim // MEM_DIM, gr["mem"][l], b, t)
            else:
                if l == n_a:
                    kv2d = rms_matmul(x, g_kv, w_kv16)
                    kv_rows[gi] = kv2d.reshape(b, t, 3, 2, NSA_KV_HEADS, HEAD_DIM)
                    kv = kv_rows[gi]
                    if gi == 0:
                        ck, cv = compress_prompt(kv2d, b, t, cw)
                        onehot = jnp.where(jnp.arange(t)[:, None] // SEL_BLOCK == jnp.arange(N_SEL_POS)[None, :],
                                           -MASK_BIG, 0.0).astype(BF16)
                        hm = lambda c0: _head_major(kv2d[:, c0:c0 + KV_DIM], b, t).astype(BF16)
                        ks_aug = jnp.concatenate(
                            [hm(2 * KV_DIM), jnp.broadcast_to(onehot, (b, NSA_KV_HEADS, t, N_SEL_POS))], axis=-1)
                        nsa_in = (_cmp_slots(ck, b), _cmp_slots(cv, b), ks_aug, hm(3 * KV_DIM), hm(4 * KV_DIM),
                                  hm(5 * KV_DIM))
                    else:
                        new3 = kv2d.reshape(b, t, 6 * KV_DIM)
                        ck_s, cv_s = compress_sample(page_table, pool_cmp, new3, cw, pg=pg)
                j = l - n_a
                if gi == 0:
                    proj = rms_matmul(x, g_mix[l], w_in_b16[j], tn=B_COLS // 2)
                    y_main = nsa_prompt(proj, slopes, *nsa_in, b, t)
                    nsa_layout = True
                    y_mem = mem_attention(proj, B_MQ_COL // MEM_DIM, gr["mem"][l], b, t)
                else:
                    proj = rms_matmul(x, g_mix[l], w_in_bs16[j], tn=S_COLS)
                    y_main = nsa_sample(proj, slopes, page_table, ck_s, cv_s, new3, win_state, pool_slc, pg=pg)
                    y_mem = mem_attention(proj, S_MQ_COL // MEM_DIM, gr["mem"][l], b, t)
            x = proj_residual(x, wo_pairs(l, n, y_main, y_mem, nsa_layout, b, t))
            x = ffn(x, g_ffn[l], w_gu16[l], w_dn16[l])
            gr["x"] = x

    y_prompt = rmsnorm_rows(groups[0]["x"], g_final).reshape(bp, tp, d)
    y_sample = rmsnorm_rows(groups[1]["x"], g_final).reshape(bs, ts, d)
    conv_state_p = jnp.stack(conv_out[0])
    conv_state_s = jnp.stack(conv_out[1])
    mem_kv_out = mem_kv_p.reshape(depth, bp, n_mem, 2, MEM_HEADS, HEAD_DIM)
    kvp, kvs = kv_rows
    win_kv_p = kvp[:, tp - min(WINDOW, tp):, 2]
    win_kv_s = jnp.concatenate([state_win_kv, kvs[:, :, 2]], axis=1)[:, ts:]
    return (y_prompt, y_sample, conv_state_p, conv_state_s, mem_kv_out, kvp[:, :, 0], kvp[:, :, 1], win_kv_p,
            kvs[:, :, 0], kvs[:, :, 1], win_kv_s)
```

```python
import functools
import math

import numpy as np
import jax
import jax.numpy as jnp
from jax import lax
from jax.experimental import pallas as pl
from jax.experimental.pallas import tpu as pltpu

F32 = jnp.float32
BF16 = jnp.bfloat16

HEAD_DIM = 64
MEM_HEADS = 4
MEM_DIM = MEM_HEADS * HEAD_DIM
NSA_KV_HEADS = 4
NSA_GROUP = 3
NSA_HEADS = NSA_KV_HEADS * NSA_GROUP
NSA_DIM = NSA_HEADS * HEAD_DIM
KV_DIM = NSA_KV_HEADS * HEAD_DIM
CONV_WIDTH = 3
CMP_STRIDE = 16
CMP_LEN = 32
SEL_BLOCK = 64
N_SEL = 16
WINDOW = 512
Q_BLOCK = 64
RMS_EPS = 1e-6
NEG_INF = -1e30
FORCE_SCORE = 1e4
ATT_SCALE = HEAD_DIM ** -0.5

VMEM_LIMIT = 48 * 1024 * 1024


def _cparams(*sem):
    return pltpu.CompilerParams(dimension_semantics=sem, vmem_limit_bytes=VMEM_LIMIT)


def _row_tile(n, want):
    t = min(n, want)
    while n % t:
        t //= 2
    return t


def _rms_matmul_kernel(x_ref, g_ref, w_ref, o_ref, h_ref):
    @pl.when(pl.program_id(1) == 0)
    def _():
        x = x_ref[...]
        ms = jnp.mean(x * x, axis=-1, keepdims=True)
        h_ref[...] = (x * lax.rsqrt(ms + RMS_EPS) * g_ref[...]).astype(BF16)

    o_ref[...] = jnp.dot(h_ref[...], w_ref[...], preferred_element_type=F32)


def rms_matmul(x, g, w, *, tm=512, tn=512):
    n, d = x.shape
    c = w.shape[1]
    tm = _row_tile(n, tm)
    tn = _row_tile(c, tn)
    return pl.pallas_call(
        _rms_matmul_kernel,
        out_shape=jax.ShapeDtypeStruct((n, c), F32),
        grid=(n // tm, c // tn),
        in_specs=[
            pl.BlockSpec((tm, d), lambda i, j: (i, 0)),
            pl.BlockSpec((1, d), lambda i, j: (0, 0)),
            pl.BlockSpec((d, tn), lambda i, j: (0, j)),
        ],
        out_specs=pl.BlockSpec((tm, tn), lambda i, j: (i, j)),
        scratch_shapes=[pltpu.VMEM((tm, d), BF16)],
        compiler_params=_cparams("parallel", "arbitrary"),
        name="rms_matmul",
    )(x, g.reshape(1, d), w)


def _proj_residual_kernel(n_pairs, x_ref, *refs):
    a_refs = refs[:n_pairs]
    w_refs = refs[n_pairs:2 * n_pairs]
    o_ref = refs[2 * n_pairs]
    acc = x_ref[...]
    for a_ref, w_ref in zip(a_refs, w_refs):
        acc = acc + jnp.dot(a_ref[...].astype(BF16), w_ref[...], preferred_element_type=F32)
    o_ref[...] = acc


def proj_residual(x, pairs, *, tm=512):
    n, d = x.shape
    tm = _row_tile(n, tm)
    in_specs = [pl.BlockSpec((tm, d), lambda i: (i, 0))]
    args = [x]
    for a, blk, imap, _ in pairs:
        in_specs.append(pl.BlockSpec(blk, imap))
        args.append(a)
    for _, _, _, w in pairs:
        in_specs.append(pl.BlockSpec(w.shape, lambda i: (0, 0)))
        args.append(w)
    return pl.pallas_call(
        functools.partial(_proj_residual_kernel, len(pairs)),
        out_shape=jax.ShapeDtypeStruct((n, d), F32),
        grid=(n // tm,),
        in_specs=in_specs,
        out_specs=pl.BlockSpec((tm, d), lambda i: (i, 0)),
        compiler_params=_cparams("parallel"),
        name="proj_residual",
    )(*args)


def _ffn_kernel(x_ref, g_ref, wg_ref, wu_ref, wd_ref, o_ref, h_ref, acc_ref):
    f = pl.program_id(1)

    @pl.when(f == 0)
    def _():
        x = x_ref[...]
        ms = jnp.mean(x * x, axis=-1, keepdims=True)
        h_ref[...] = (x * lax.rsqrt(ms + RMS_EPS) * g_ref[...]).astype(BF16)
        acc_ref[...] = x

    h = h_ref[...]
    gate = jnp.dot(h, wg_ref[...], preferred_element_type=F32)
    up = jnp.dot(h, wu_ref[...], preferred_element_type=F32)
    act = (gate * jax.nn.sigmoid(gate) * up).astype(BF16)
    acc_ref[...] += jnp.dot(act, wd_ref[...], preferred_element_type=F32)

    @pl.when(f == pl.num_programs(1) - 1)
    def _():
        o_ref[...] = acc_ref[...]


def ffn(x, g, w_gu, w_dn, *, tm=512, tf=1408):
    n, d = x.shape
    dff = w_dn.shape[0]
    tm = _row_tile(n, tm)
    nf = dff // tf
    return pl.pallas_call(
        _ffn_kernel,
        out_shape=jax.ShapeDtypeStruct((n, d), F32),
        grid=(n // tm, nf),
        in_specs=[
            pl.BlockSpec((tm, d), lambda i, f: (i, 0)),
            pl.BlockSpec((1, d), lambda i, f: (0, 0)),
            pl.BlockSpec((d, tf), lambda i, f: (0, f)),
            pl.BlockSpec((d, tf), lambda i, f: (0, f + nf)),
            pl.BlockSpec((tf, d), lambda i, f: (f, 0)),
        ],
        out_specs=pl.BlockSpec((tm, d), lambda i, f: (i, 0)),
        scratch_shapes=[pltpu.VMEM((tm, d), BF16), pltpu.VMEM((tm, d), F32)],
        compiler_params=_cparams("parallel", "arbitrary"),
        name="ffn",
    )(x, g.reshape(1, d), w_gu, w_gu, w_dn)


def _rmsnorm_kernel(x_ref, g_ref, o_ref):
    x = x_ref[...]
    ms = jnp.mean(x * x, axis=-1, keepdims=True)
    o_ref[...] = x * lax.rsqrt(ms + RMS_EPS) * g_ref[...]


def rmsnorm_rows(x, g, *, tm=1024):
    n, d = x.shape
    tm = _row_tile(n, tm)
    return pl.pallas_call(
        _rmsnorm_kernel,
        out_shape=jax.ShapeDtypeStruct((n, d), F32),
        grid=(n // tm,),
        in_specs=[pl.BlockSpec((tm, d), lambda i: (i, 0)), pl.BlockSpec((1, d), lambda i: (0, 0))],
        out_specs=pl.BlockSpec((tm, d), lambda i: (i, 0)),
        compiler_params=_cparams("parallel"),
        name="rmsnorm",
    )(x, g.reshape(1, d))


def _conv_kernel(b_ref, c_ref, h_ref, cp_ref, hp_ref, st_ref, w_ref, y_ref, ns_ref):
    i = pl.program_id(1)
    u = c_ref[...] * h_ref[...]
    tt = u.shape[0]
    prev = cp_ref[...] * hp_ref[...]
    st = st_ref[0]
    first = i == 0
    p1 = jnp.where(first, st[1:2], prev[7:8])
    p2 = jnp.where(first, st[0:1], prev[6:7])
    row = lax.broadcasted_iota(jnp.int32, u.shape, 0)
    u1 = jnp.where(row == 0, p1, pltpu.roll(u, 1, 0))
    u2 = jnp.where(row == 0, p2, jnp.where(row == 1, p1, pltpu.roll(u, 2, 0)))
    w = w_ref[...]
    y = w[0:1] * u2 + w[1:2] * u1 + w[2:3] * u
    y_ref[...] = b_ref[...] * y
    ns_ref[0] = u[tt - 2:tt]


def gated_conv(proj, state, conv_w, batch, t, *, tt=512):
    c = conv_w.shape[1]
    tt = _row_tile(t, tt)
    nt = t // tt
    r8 = tt // 8

    def prev_map(col):
        return lambda b, i: (jnp.maximum((b * nt + i) * r8 - 1, 0), col)

    return pl.pallas_call(
        _conv_kernel,
        out_shape=(jax.ShapeDtypeStruct((batch * t, c), F32), jax.ShapeDtypeStruct((batch, 2, c), F32)),
        grid=(batch, nt),
        in_specs=[
            pl.BlockSpec((tt, c), lambda b, i: (b * nt + i, 0)),
            pl.BlockSpec((tt, c), lambda b, i: (b * nt + i, 1)),
            pl.BlockSpec((tt, c), lambda b, i: (b * nt + i, 2)),
            pl.BlockSpec((8, c), prev_map(1)),
            pl.BlockSpec((8, c), prev_map(2)),
            pl.BlockSpec((1, 2, c), lambda b, i: (b, 0, 0)),
            pl.BlockSpec((CONV_WIDTH, c), lambda b, i: (0, 0)),
        ],
        out_specs=(
            pl.BlockSpec((tt, c), lambda b, i: (b * nt + i, 0)),
            pl.BlockSpec((1, 2, c), lambda b, i: (b, 0, 0)),
        ),
        compiler_params=_cparams("parallel", "arbitrary"),
        name="gated_conv",
    )(proj, proj, proj, proj, proj, state, conv_w)


def _mem_attn_kernel(q_ref, kv_ref, o_ref):
    q = q_ref[...] * ATT_SCALE
    kv = kv_ref[0]
    outs = []
    for h in range(MEM_HEADS):
        qh = q[:, h * HEAD_DIM:(h + 1) * HEAD_DIM].astype(BF16)
        kh = kv[:, h * HEAD_DIM:(h + 1) * HEAD_DIM].astype(BF16)
        vh = kv[:, MEM_DIM + h * HEAD_DIM:MEM_DIM + (h + 1) * HEAD_DIM].astype(BF16)
        s = lax.dot_general(qh, kh, (((1,), (1,)), ((), ())), preferred_element_type=F32)
        m = jnp.max(s, axis=-1, keepdims=True)
        e = jnp.exp(s - m)
        p = e / jnp.sum(e, axis=-1, keepdims=True)
        outs.append(jnp.dot(p.astype(BF16), vh, preferred_element_type=F32))
    o_ref[...] = jnp.concatenate(outs, axis=-1)


def mem_attention(proj, col_block, mem_kv, batch, t, *, tt=512):
    tt = _row_tile(t, tt)
    nt = t // tt
    n_mem = mem_kv.shape[1]
    return pl.pallas_call(
        _mem_attn_kernel,
        out_shape=jax.ShapeDtypeStruct((batch * t, MEM_DIM), F32),
        grid=(batch, nt),
        in_specs=[
            pl.BlockSpec((tt, MEM_DIM), lambda b, i: (b * nt + i, col_block)),
            pl.BlockSpec((1, n_mem, 2 * MEM_DIM), lambda b, i: (b, 0, 0)),
        ],
        out_specs=pl.BlockSpec((tt, MEM_DIM), lambda b, i: (b * nt + i, 0)),
        compiler_params=_cparams("parallel", "arbitrary"),
        name="mem_attention",
    )(proj, mem_kv)


def _alibi_list(n):
    def pow2(m):
        start = 2.0 ** (-8.0 / m)
        return [start ** (i + 1) for i in range(m)]
    if n & (n - 1) == 0:
        return pow2(n)
    c = 2 ** int(math.floor(math.log2(n)))
    return pow2(c) + _alibi_list(2 * c)[0::2][: n - c]


LANES = 128
PAIR = LANES // HEAD_DIM


def _to_lane_columns(src, dst_ref, n_rows):
    for c in range(dst_ref.shape[0]):
        dst_ref[c, 0:n_rows, :] = src[0:n_rows, c * LANES:(c + 1) * LANES]


def _compress_tile(buf_ref, row0, m, pe_refs, w1_refs, w2_refs):
    outs = []
    for kv in range(2):
        pe = pe_refs[kv][...]
        lhs = []
        for p in range(NSA_KV_HEADS // PAIR):
            col = kv * (KV_DIM // LANES) + p
            pieces = [
                (buf_ref[col, pl.ds(row0 + j, m, stride=CMP_STRIDE), :] + pe[j:j + 1]).astype(BF16)
                for j in range(CMP_LEN)
            ]
            lhs.append(jnp.concatenate(pieces, axis=1))
        lhs = jnp.concatenate(lhs, axis=0)
        hid = jnp.dot(lhs, w1_refs[kv][...], preferred_element_type=F32)
        act = (hid * jax.nn.sigmoid(hid)).astype(BF16)
        out = jnp.dot(act, w2_refs[kv][...], preferred_element_type=F32)
        outs.append(jnp.concatenate([out[0:m], out[m:2 * m]], axis=1))
    return outs


def _compress_prompt_kernel(x_ref, pek_ref, pev_ref, w1k_ref, w1v_ref, w2k_ref, w2v_ref, ck_ref, cv_ref, buf_ref):
    t = x_ref.shape[0]
    _to_lane_columns(x_ref, buf_ref, t)
    buf_ref[:, t:t + CMP_STRIDE, :] = jnp.zeros((buf_ref.shape[0], CMP_STRIDE, LANES), F32)
    n_blk = t // CMP_STRIDE
    m = min(n_blk, 128)
    for s in range(n_blk // m):
        ck, cv = _compress_tile(buf_ref, s * m * CMP_STRIDE, m, (pek_ref, pev_ref), (w1k_ref, w1v_ref),
                                (w2k_ref, w2v_ref))
        ck_ref[0, s * m:(s + 1) * m, :] = ck
        cv_ref[0, s * m:(s + 1) * m, :] = cv


def _compress_weights(pe, w1, w2):
    eye = jnp.eye(PAIR, dtype=F32)
    w1p = jnp.einsum('jde,qr->jqdre', w1, eye).reshape(CMP_LEN * PAIR * HEAD_DIM, PAIR * w1.shape[2])
    w2p = jnp.einsum('ed,qr->qerd', w2, eye).reshape(PAIR * w2.shape[0], PAIR * HEAD_DIM)
    pe2 = jnp.tile(pe, (1, PAIR))
    return pe2, w1p.astype(BF16), w2p.astype(BF16)


def compress_prompt(kv_rows, batch, t, cw):
    pek, w1k, w2k, pev, w1v, w2v = cw
    n_blk = t // CMP_STRIDE
    full = lambda a: pl.BlockSpec(a.shape, lambda b: (0,) * a.ndim)
    return pl.pallas_call(
        _compress_prompt_kernel,
        out_shape=(jax.ShapeDtypeStruct((batch, n_blk, KV_DIM), F32),) * 2,
        grid=(batch,),
        in_specs=[pl.BlockSpec((t, 2 * KV_DIM), lambda b: (b, 0)), full(pek), full(pev), full(w1k), full(w1v),
                  full(w2k), full(w2v)],
        out_specs=(pl.BlockSpec((1, n_blk, KV_DIM), lambda b: (b, 0, 0)),) * 2,
        scratch_shapes=[pltpu.VMEM((2 * KV_DIM // LANES, t + CMP_STRIDE, LANES), F32)],
        compiler_params=_cparams("parallel"),
        name="compress_prompt",
    )(kv_rows, pek, pev, w1k, w1v, w2k, w2v)


N_CMP_POS = 256
N_SEL_POS = N_CMP_POS * CMP_STRIDE // SEL_BLOCK
MASK_BIG = 2.0 ** 100
M_INIT = -1e38


LOG2E = 1.4426950408889634
MASK_NONE, MASK_CAUSAL, MASK_BAND = 0, 1, 2


def _topk_mask(score):
    tq, nb = score.shape
    pad = jnp.full((tq, LANES - nb), -jnp.inf, F32)
    st = jnp.concatenate([score, pad], axis=1).T[:nb]
    sub = 8
    idx = lax.broadcasted_iota(jnp.int32, (sub, tq), 0)
    groups = [st[r:r + sub] for r in range(0, nb, sub)]
    ranks = [jnp.zeros((sub, tq), F32) for _ in groups]
    for j in range(nb):
        rj = st[j:j + 1, :]
        for gi, sg in enumerate(groups):
            if gi * sub > j:
                before = rj >= sg
            elif gi * sub + sub - 1 < j:
                before = rj > sg
            else:
                before = jnp.where(idx > j - gi * sub, (rj >= sg).astype(F32), (rj > sg).astype(F32)) > 0.0
            ranks[gi] = jnp.where(before, ranks[gi] + 1.0, ranks[gi])
    sel_t = (jnp.concatenate(ranks, axis=0) < N_SEL).astype(F32)
    sel_t = jnp.concatenate([sel_t, jnp.zeros((LANES - nb, tq), F32)], axis=0)
    return sel_t.T[:, :nb]


def _nsa_prompt_kernel(sl_ref, q_ref, gt_ref, ck_ref, cv_ref, kc_ref, vc_ref, o_ref,
                       q_scr, s0_scr, s1_scr, p0_scr, p1_scr, a0_scr, a1_scr, m_scr, l_scr, acc_scr, mb_scr,
                       *, mm_rows, sm_rows):
    kvh = pl.program_id(1)
    i = pl.program_id(2)
    s_scrs, p_scrs, a_scrs = (s0_scr, s1_scr), (p0_scr, p1_scr), (a0_scr, a1_scr)
    tq = q_ref.shape[0]
    kb = tq
    t_len = kc_ref.shape[2] // 2
    t0 = i * tq
    slopes = [sl_ref[kvh * NSA_GROUP + g] * LOG2E for g in range(NSA_GROUP)]
    q = q_ref[...] * (ATT_SCALE * LOG2E)
    q3 = jnp.concatenate([q[:, g * HEAD_DIM:(g + 1) * HEAD_DIM] for g in range(NSA_GROUP)], axis=0)
    q3b = q3.astype(BF16)
    nt_dims = (((1,), (1,)), ((), ()))
    t_col = t0 + lax.broadcasted_iota(jnp.int32, (tq, 1), 0)

    @pl.when(i == 0)
    def _():
        r = lax.broadcasted_iota(jnp.int32, (tq, kb), 0)
        c = lax.broadcasted_iota(jnp.int32, (tq, kb), 1)
        mb_scr[MASK_NONE] = jnp.zeros((tq, kb), F32)
        mb_scr[MASK_CAUSAL] = jnp.where(c <= r, 0.0, -MASK_BIG)
        mb_scr[MASK_BAND] = jnp.where(c >= r, 0.0, -MASK_BIG)

    m_scr[...] = jnp.full(m_scr.shape, M_INIT, F32)
    l_scr[...] = jnp.zeros(l_scr.shape, F32)
    acc_scr[...] = jnp.zeros(acc_scr.shape, F32)

    pos = lax.broadcasted_iota(jnp.int32, (1, N_CMP_POS), 1)
    blk_n = (pos % N_SEL_POS) * (N_CMP_POS // N_SEL_POS) + pos // N_SEL_POS
    c_end = blk_n * CMP_STRIDE + (CMP_LEN - 1)
    n_real = t_len // CMP_STRIDE - 1
    vis_c = (c_end <= t_col) & (blk_n < n_real)
    c_end_f = c_end.astype(F32)
    s_c = lax.dot_general(q3b, ck_ref[0, 0], nt_dims, preferred_element_type=F32)
    p_c = []
    for g in range(NSA_GROUP):
        sg = jnp.where(vis_c, s_c[g * tq:(g + 1) * tq] + slopes[g] * c_end_f, NEG_INF)
        mg = jnp.max(sg, axis=1, keepdims=True)
        eg = jnp.where(vis_c, jnp.exp2(sg - mg), 0.0)
        dg = jnp.sum(eg, axis=1, keepdims=True)
        p_c.append(eg / jnp.where(dg > 0.0, dg, 1.0))
    o_c = jnp.dot(jnp.concatenate(p_c, axis=0).astype(BF16), cv_ref[0, 0], preferred_element_type=F32)
    imp = p_c[0] + p_c[1] + p_c[2]
    imp = imp[:, :LANES] + imp[:, LANES:]
    imp = imp[:, :N_SEL_POS] + imp[:, N_SEL_POS:]

    blk = lax.broadcasted_iota(jnp.int32, (1, N_SEL_POS), 1)
    cur = t_col // SEL_BLOCK
    forced = (blk == 0) | (blk == cur) | (blk == cur - 1)
    score = jnp.where(forced, FORCE_SCORE, jnp.where(blk <= cur, imp, -jnp.inf))
    notsel = (1.0 - _topk_mask(score)).astype(BF16)
    q_scr[...] = jnp.concatenate([q3b, jnp.concatenate([notsel] * NSA_GROUP, axis=0)], axis=1)

    lane_k = lax.broadcasted_iota(jnp.int32, (1, kb), 1)
    n_win = WINDOW // kb + 1

    n_chunks = i + 1 + n_win

    def chunk(c):
        is_sel = c <= i
        is_pad = c >= n_chunks
        w = c - (i + 1)
        kpos0 = jnp.where(is_sel, c * kb, jnp.where(is_pad, -kb, t0 - WINDOW + w * kb))
        row0 = jnp.maximum(kpos0, 0) + jnp.where(is_sel | is_pad, 0, t_len)
        mtype = jnp.where(is_sel, jnp.where(c == i, MASK_CAUSAL, MASK_NONE),
                          jnp.where(w == 0, MASK_BAND, jnp.where(w == n_win - 1, MASK_CAUSAL, MASK_NONE)))
        return pl.multiple_of(row0, kb), kpos0, mtype, jnp.where(is_sel | is_pad, 0, 1)

    def stage_logits(c, par):
        s_scr = s_scrs[par]
        row0, kpos0, mtype, _ = chunk(c)
        k = kc_ref[0, 0, pl.ds(row0, kb), :]
        kpos_f = (kpos0 + lane_k).astype(F32)
        off = jnp.where(kpos0 >= 0, 0.0, -MASK_BIG)
        for g in range(NSA_GROUP):
            bias = slopes[g] * kpos_f + off
            for r in range(0, tq, mm_rows):
                rows = slice(g * tq + r, g * tq + r + mm_rows)
                s = lax.dot_general(q_scr[rows, :], k, nt_dims, preferred_element_type=F32)
                s_scr[rows, :] = (s + mb_scr[mtype, r:r + mm_rows, :]) + bias

    def stage_softmax(c, par):
        s_scr, p_scr, a_scr = s_scrs[par], p_scrs[par], a_scrs[par]
        st = chunk(c)[3]
        for r in range(0, NSA_GROUP * tq, sm_rows):
            rows = slice(r, r + sm_rows)
            s = s_scr[rows, :]
            m_prev = m_scr[st, rows, :]
            m_new = jnp.maximum(m_prev, jnp.max(s, axis=1, keepdims=True))
            alpha = jnp.exp2(m_prev - m_new)
            p = jnp.exp2(s - jnp.concatenate([m_new] * (kb // LANES), axis=1))
            l_scr[st, rows, :] = alpha * l_scr[st, rows, :] + jnp.sum(p, axis=1, keepdims=True)
            p_scr[rows, :] = p.astype(BF16)
            a_scr[rows, :] = alpha
            m_scr[st, rows, :] = m_new

    def stage_values(c, par):
        p_scr, a_scr = p_scrs[par], a_scrs[par]
        row0, _, _, st = chunk(c)
        v = vc_ref[0, 0, pl.ds(row0, kb), :]
        for r in range(0, NSA_GROUP * tq, mm_rows):
            rows = slice(r, r + mm_rows)
            pv = jnp.dot(p_scr[rows, :], v, preferred_element_type=F32)
            acc_scr[st, rows, :] = acc_scr[st, rows, :] * a_scr[rows, :HEAD_DIM] + pv

    stage_logits(0, 0)
    stage_softmax(0, 0)
    stage_logits(1, 1)

    def pipe_body(j, carry):
        c = 2 * j
        stage_values(c - 2, 0)
        stage_softmax(c - 1, 1)
        stage_logits(c, 0)
        stage_values(c - 1, 1)
        stage_softmax(c, 0)
        stage_logits(c + 1, 1)
        return carry

    n_even = n_chunks + n_chunks % 2
    lax.fori_loop(1, n_even // 2, pipe_body, 0)
    stage_values(n_even - 2, 0)
    stage_softmax(n_even - 1, 1)
    stage_values(n_even - 1, 1)
    o_s = acc_scr[0] / l_scr[0][:, :HEAD_DIM]
    o_w = acc_scr[1] / l_scr[1][:, :HEAD_DIM]

    gates = jax.nn.sigmoid(gt_ref[...])
    outs = []
    for g in range(NSA_GROUP):
        rows = slice(g * tq, (g + 1) * tq)
        outs.append(gates[:, 3 * g:3 * g + 1] * o_c[rows] + gates[:, 3 * g + 1:3 * g + 2] * o_s[rows]
                    + gates[:, 3 * g + 2:3 * g + 3] * o_w[rows])
    o_ref[0, 0] = jnp.concatenate(outs, axis=1)


def nsa_prompt(proj, slopes, ckp, cvp, kcat, vcat, batch, t, *, tq=256, mm_rows=256, sm_rows=64):
    assert t % tq == 0 and WINDOW % tq == 0 and WINDOW >= tq
    nt = t // tq
    rows = NSA_GROUP * tq
    gcol = B_GATE_COL // LANES
    grid_spec = pltpu.PrefetchScalarGridSpec(
        num_scalar_prefetch=1,
        grid=(batch, NSA_KV_HEADS, nt),
        in_specs=[
            pl.BlockSpec((tq, Q_SLOT), lambda b, h, i, sl: (b * nt + i, h)),
            pl.BlockSpec((tq, LANES), lambda b, h, i, sl: (b * nt + i, gcol + h)),
            pl.BlockSpec((1, 1, N_CMP_POS, HEAD_DIM), lambda b, h, i, sl: (b, h, 0, 0)),
            pl.BlockSpec((1, 1, N_CMP_POS, HEAD_DIM), lambda b, h, i, sl: (b, h, 0, 0)),
            pl.BlockSpec((1, 1, 2 * t, LANES), lambda b, h, i, sl: (b, h, 0, 0)),
            pl.BlockSpec((1, 1, 2 * t, HEAD_DIM), lambda b, h, i, sl: (b, h, 0, 0)),
        ],
        out_specs=pl.BlockSpec((1, 1, tq, NSA_GROUP * HEAD_DIM), lambda b, h, i, sl: (b, h, i, 0)),
        scratch_shapes=[
            pltpu.VMEM((rows, LANES), BF16),
            pltpu.VMEM((rows, tq), F32), pltpu.VMEM((rows, tq), F32),
            pltpu.VMEM((rows, tq), BF16), pltpu.VMEM((rows, tq), BF16),
            pltpu.VMEM((rows, LANES), F32), pltpu.VMEM((rows, LANES), F32),
            pltpu.VMEM((2, rows, LANES), F32),
            pltpu.VMEM((2, rows, LANES), F32),
            pltpu.VMEM((2, rows, HEAD_DIM), F32),
            pltpu.VMEM((3, tq, tq), F32),
        ],
    )
    return pl.pallas_call(
        functools.partial(_nsa_prompt_kernel, mm_rows=mm_rows, sm_rows=sm_rows),
        out_shape=jax.ShapeDtypeStruct((batch, NSA_KV_HEADS, t, NSA_GROUP * HEAD_DIM), F32),
        grid_spec=grid_spec,
        compiler_params=_cparams("parallel", "parallel", "arbitrary"),
        name="nsa_prompt",
    )(slopes, proj, proj, ckp, cvp, kcat, vcat)


def _head_major(x, batch, t):
    return x.reshape(batch, t, NSA_KV_HEADS, HEAD_DIM).transpose(0, 2, 1, 3)


def _cmp_slots(c, batch):
    n_blk = c.shape[1]
    per = N_CMP_POS // N_SEL_POS
    c = jnp.pad(c, ((0, 0), (0, N_CMP_POS - n_blk), (0, 0)))
    c = c.reshape(batch, N_SEL_POS, per, NSA_KV_HEADS, HEAD_DIM).transpose(0, 3, 2, 1, 4)
    return c.reshape(batch, NSA_KV_HEADS, N_CMP_POS, HEAD_DIM).astype(BF16)


def _page_group_dma(pt_ref, pool_ref, buf_ref, sem_ref, n_pages, pg, page_rows):
    ng = n_pages // pg

    def run(step, slot, go):
        bb = step // ng
        gg = step % ng
        for k in range(pg):
            page = pt_ref[bb * n_pages + gg * pg + k]
            cp = pltpu.make_async_copy(pool_ref.at[page], buf_ref.at[slot, pl.ds(k * page_rows, page_rows)],
                                       sem_ref.at[slot])
            cp.start() if go == "start" else cp.wait()

    return run


def _compress_sample_kernel(pt_ref, pool_ref, new_ref, pek_ref, pev_ref, w1k_ref, w1v_ref, w2k_ref, w2v_ref,
                            ck_ref, cv_ref, buf_ref, sem_ref, col_ref, *, n_pages, pg, page_rows, ts):
    b = pl.program_id(0)
    g = pl.program_id(1)
    ng = n_pages // pg
    step = b * ng + g
    n_steps = pl.num_programs(0) * ng
    slot = step % 2
    rows = pg * page_rows
    pages = _page_group_dma(pt_ref, pool_ref, buf_ref, sem_ref, n_pages, pg, page_rows)

    def halo(st, sl, go):
        bb = st // ng
        gg = st % ng

        @pl.when(gg < ng - 1)
        def _():
            page = pt_ref[bb * n_pages + (gg + 1) * pg]
            cp = pltpu.make_async_copy(pool_ref.at[page, pl.ds(0, CMP_STRIDE)],
                                       buf_ref.at[sl, pl.ds(rows, CMP_STRIDE)], sem_ref.at[sl])
            cp.start() if go == "start" else cp.wait()

        @pl.when(gg == ng - 1)
        def _():
            cp = pltpu.make_async_copy(new_ref.at[bb, :, pl.ds(0, 2 * KV_DIM)],
                                       buf_ref.at[sl, pl.ds(rows, ts)], sem_ref.at[sl])
            cp.start() if go == "start" else cp.wait()

    @pl.when(step == 0)
    def _():
        pages(0, 0, "start")
        halo(0, 0, "start")

    @pl.when(step + 1 < n_steps)
    def _():
        pages(step + 1, 1 - slot, "start")
        halo(step + 1, 1 - slot, "start")

    pages(step, slot, "wait")
    halo(step, slot, "wait")

    @pl.when(g == ng - 1)
    def _():
        buf_ref[slot, rows + ts:rows + CMP_STRIDE, :] = jnp.zeros((CMP_STRIDE - ts, buf_ref.shape[2]), F32)

    m = rows // CMP_STRIDE
    _to_lane_columns(buf_ref.at[slot], col_ref, rows + CMP_STRIDE)
    ck, cv = _compress_tile(col_ref, 0, m, (pek_ref, pev_ref), (w1k_ref, w1v_ref), (w2k_ref, w2v_ref))
    ck_ref[0] = ck
    cv_ref[0] = cv


def compress_sample(page_table, pool, new_rows, cw, *, pg):
    pek, w1k, w2k, pev, w1v, w2v = cw
    batch, n_pages = page_table.shape
    page_rows = pool.shape[1]
    ts = new_rows.shape[1]
    ng = n_pages // pg
    m = pg * page_rows // CMP_STRIDE
    full = lambda a: pl.BlockSpec(a.shape, lambda b, g, pt: (0,) * a.ndim)
    grid_spec = pltpu.PrefetchScalarGridSpec(
        num_scalar_prefetch=1,
        grid=(batch, ng),
        in_specs=[pl.BlockSpec(memory_space=pl.ANY), pl.BlockSpec(memory_space=pl.ANY), full(pek), full(pev),
                  full(w1k), full(w1v), full(w2k), full(w2v)],
        out_specs=(pl.BlockSpec((1, m, KV_DIM), lambda b, g, pt: (b, g, 0)),) * 2,
        scratch_shapes=[pltpu.VMEM((2, pg * page_rows + CMP_STRIDE, 2 * KV_DIM), F32), pltpu.SemaphoreType.DMA((2,)),
                        pltpu.VMEM((2 * KV_DIM // LANES, pg * page_rows + CMP_STRIDE, LANES), F32)],
    )
    return pl.pallas_call(
        functools.partial(_compress_sample_kernel, n_pages=n_pages, pg=pg, page_rows=page_rows, ts=ts),
        out_shape=(jax.ShapeDtypeStruct((batch, ng * m, KV_DIM), F32),) * 2,
        grid_spec=grid_spec,
        compiler_params=_cparams("arbitrary", "arbitrary"),
        name="compress_sample",
    )(page_table.reshape(-1), pool, new_rows, pek, pev, w1k, w1v, w2k, w2v)


S_COL_G = 32


def _softmax_rows0(s, vis):
    s = jnp.where(vis, s, NEG_INF)
    m = jnp.max(s, axis=0, keepdims=True)
    e = jnp.where(vis, jnp.exp(s - m), 0.0)
    d = jnp.sum(e, axis=0, keepdims=True)
    return e / jnp.where(d > 0.0, d, 1.0)


def _nsa_sample_kernel(sl_ref, pt_ref, proj_ref, ck_ref, cv_ref, new_ref, win_ref, pool_ref, y_ref,
                       buf_ref, sem_ref, s_ref, v_ref, bdq_ref, oc_ref, ow_ref, sc_ref, sel_ref, imp_ref,
                       *, n_pages, pg, page_rows, past):
    b = pl.program_id(0)
    g = pl.program_id(1)
    ng = n_pages // pg
    step = b * ng + g
    n_steps = pl.num_programs(0) * ng
    slot = step % 2
    rows = pg * page_rows
    ts = proj_ref.shape[0]
    nt_dims = (((1,), (1,)), ((), ()))
    pages = _page_group_dma(pt_ref, pool_ref, buf_ref, sem_ref, n_pages, pg, page_rows)

    @pl.when(step == 0)
    def _():
        pages(0, 0, "start")

    @pl.when(step + 1 < n_steps)
    def _():
        pages(step + 1, 1 - slot, "start")

    col = lax.broadcasted_iota(jnp.int32, (1, LANES), 1)
    col_g = col // S_COL_G
    col_h = (col % S_COL_G) // ts
    t_row = past + col % ts
    slope_row = jnp.zeros((1, LANES), F32)
    for h in range(NSA_HEADS):
        slope_row = jnp.where((col_h == h // NSA_GROUP) & (col_g == h % NSA_GROUP), sl_ref[h], slope_row)
    n_sb = past // SEL_BLOCK + 1

    @pl.when(g == 0)
    def _():
        lane_h = lax.broadcasted_iota(jnp.int32, (ts, KV_DIM), 1) // HEAD_DIM
        tiles = []
        for gg in range(NSA_GROUP):
            qg = proj_ref[:, gg * KV_DIM:(gg + 1) * KV_DIM] * ATT_SCALE
            for h in range(NSA_KV_HEADS):
                tiles.append(jnp.where(lane_h == h, qg, 0.0))
        tiles.append(jnp.zeros((LANES - NSA_HEADS * ts, KV_DIM), F32))
        bdq = jnp.concatenate(tiles, axis=0).astype(BF16)
        bdq_ref[...] = bdq

        n_c = ck_ref.shape[1]
        s_c = lax.dot_general(ck_ref[0].astype(BF16), bdq, nt_dims, preferred_element_type=F32)
        c_end = lax.broadcasted_iota(jnp.int32, (n_c, 1), 0) * CMP_STRIDE + (CMP_LEN - 1)
        dist = t_row - c_end
        p_c = _softmax_rows0(s_c - slope_row * dist.astype(F32), dist >= 0)
        oc_ref[...] = jnp.dot(p_c.T.astype(BF16), cv_ref[0].astype(BF16), preferred_element_type=F32)
        pz = jnp.where(col < NSA_GROUP * S_COL_G, p_c, 0.0)
        imp_ref[...] = pz + pltpu.roll(pz, S_COL_G, 1) + pltpu.roll(pz, 2 * S_COL_G, 1) + pltpu.roll(pz, 3 * S_COL_G, 1)
        per = SEL_BLOCK // CMP_STRIDE
        n_imp = n_c // per
        imp = imp_ref[pl.ds(0, n_imp, stride=per), :]
        for r in range(1, per):
            imp = imp + imp_ref[pl.ds(r, n_imp, stride=per), :]
        n_pad = sc_ref.shape[0]
        imp = jnp.concatenate([imp, jnp.zeros((n_pad - n_imp, LANES), F32)], axis=0)
        blk = lax.broadcasted_iota(jnp.int32, (n_pad, 1), 0)
        cur = t_row // SEL_BLOCK
        forced = (blk == 0) | (blk == cur) | (blk == cur - 1)
        sc_ref[...] = jnp.where(forced, FORCE_SCORE, jnp.where((blk <= cur) & (blk < n_sb), imp, -jnp.inf))
        st = sc_ref[...]

        def rank_body(j, rank):
            rj = sc_ref[pl.ds(j, 1), :]
            return rank + jnp.where(blk > j, (rj >= st).astype(F32), (rj > st).astype(F32))

        rank = lax.fori_loop(0, n_sb, rank_body, jnp.zeros(st.shape, F32))
        sel_ref[...] = (rank < N_SEL).astype(F32)

        wl = win_ref.shape[1]
        zpad = jnp.zeros((LANES - ts, KV_DIM), F32)
        kw = jnp.concatenate([win_ref[0, :, 0:KV_DIM], new_ref[0, :, 4 * KV_DIM:5 * KV_DIM], zpad], axis=0)
        vw = jnp.concatenate([win_ref[0, :, KV_DIM:2 * KV_DIM], new_ref[0, :, 5 * KV_DIM:6 * KV_DIM], zpad], axis=0)
        s_w = lax.dot_general(kw.astype(BF16), bdq, nt_dims, preferred_element_type=F32)
        w_pos = past - wl + lax.broadcasted_iota(jnp.int32, (wl + LANES, 1), 0)
        dist_w = t_row - w_pos
        p_w = _softmax_rows0(s_w - slope_row * dist_w.astype(F32),
                             (dist_w >= 0) & (dist_w <= WINDOW) & (w_pos >= 0))
        ow_ref[...] = jnp.dot(p_w.T.astype(BF16), vw.astype(BF16), preferred_element_type=F32)

    pages(step, slot, "wait")
    bdq = bdq_ref[...]
    base = g * rows
    kpage = buf_ref[slot, :, 0:KV_DIM].astype(BF16)
    s_g = lax.dot_general(kpage, bdq, nt_dims, preferred_element_type=F32)
    kpos = base + lax.broadcasted_iota(jnp.int32, (rows, 1), 0)
    s_g = s_g - slope_row * (t_row - kpos).astype(F32)
    row0 = pl.multiple_of(base, rows)
    for bl in range(rows // SEL_BLOCK):
        on = sel_ref[pl.ds(g * (rows // SEL_BLOCK) + bl, 1), :] > 0.0
        s_ref[pl.ds(row0 + bl * SEL_BLOCK, SEL_BLOCK), :] = jnp.where(
            on, s_g[bl * SEL_BLOCK:(bl + 1) * SEL_BLOCK], NEG_INF)
    v_ref[pl.ds(row0, rows), :] = buf_ref[slot, :, KV_DIM:2 * KV_DIM].astype(BF16)

    @pl.when(g == ng - 1)
    def _():
        tail = s_ref.shape[0] - past
        zpad = jnp.zeros((tail - ts, KV_DIM), F32)
        k_new = jnp.concatenate([new_ref[0, :, 2 * KV_DIM:3 * KV_DIM], zpad], axis=0)
        v_new = jnp.concatenate([new_ref[0, :, 3 * KV_DIM:4 * KV_DIM], zpad], axis=0)
        s_n = lax.dot_general(k_new.astype(BF16), bdq, nt_dims, preferred_element_type=F32)
        kp = past + lax.broadcasted_iota(jnp.int32, (tail, 1), 0)
        dist_n = t_row - kp
        on = (sel_ref[pl.ds(n_sb - 1, 1), :] > 0.0) & (dist_n >= 0)
        s_ref[pl.ds(past, tail), :] = jnp.where(on, s_n - slope_row * dist_n.astype(F32), NEG_INF)
        v_ref[pl.ds(past, tail), :] = v_new.astype(BF16)
        nk = s_ref.shape[0]
        m = jnp.max(s_ref[...], axis=0, keepdims=True)
        ch = rows
        l = jnp.zeros((1, LANES), F32)
        o_s = jnp.zeros((LANES, KV_DIM), F32)
        for c0 in range(0, nk, ch):
            cl = min(ch, nk - c0)
            e = jnp.exp(s_ref[c0:c0 + cl, :] - m)
            l = l + jnp.sum(e, axis=0, keepdims=True)
            if cl % LANES:
                e = jnp.concatenate([e, jnp.zeros((LANES - cl % LANES, LANES), F32)], axis=0)
                vv = jnp.concatenate([v_ref[c0:c0 + cl, :], jnp.zeros((LANES - cl % LANES, KV_DIM), BF16)], axis=0)
            else:
                vv = v_ref[c0:c0 + cl, :]
            o_s = o_s + jnp.dot(e.T.astype(BF16), vv, preferred_element_type=F32)
        inv_l = (1.0 / jnp.broadcast_to(l, (8, LANES))).T[:, 0:1]
        o_s = o_s * inv_l
        o_c = oc_ref[...]
        o_w = ow_ref[...]
        gates = jax.nn.sigmoid(proj_ref[:, (NSA_GROUP + 1) * KV_DIM:(NSA_GROUP + 1) * KV_DIM + LANES])
        outs = []
        for h in range(NSA_HEADS):
            kvh, gg = h // NSA_GROUP, h % NSA_GROUP
            r0 = gg * S_COL_G + kvh * ts
            lanes = slice(kvh * HEAD_DIM, (kvh + 1) * HEAD_DIM)
            outs.append(gates[:, 3 * h:3 * h + 1] * o_c[r0:r0 + ts, lanes]
                        + gates[:, 3 * h + 1:3 * h + 2] * o_s[r0:r0 + ts, lanes]
                        + gates[:, 3 * h + 2:3 * h + 3] * o_w[r0:r0 + ts, lanes])
        y_ref[...] = jnp.concatenate(outs, axis=1)


def nsa_sample(proj, slopes, page_table, ck, cv, new_rows, win_state, pool, *, pg):
    batch, n_pages = page_table.shape
    page_rows = pool.shape[1]
    ts = new_rows.shape[1]
    past = n_pages * page_rows
    ng = n_pages // pg
    n_c = ck.shape[1]
    wl = win_state.shape[1]
    n_sb_pad = -(-(past // SEL_BLOCK + 1) // 8) * 8
    nk = past + 16
    grid_spec = pltpu.PrefetchScalarGridSpec(
        num_scalar_prefetch=2,
        grid=(batch, ng),
        in_specs=[
            pl.BlockSpec((ts, proj.shape[1]), lambda b, g, sl, pt: (b, 0)),
            pl.BlockSpec((1, n_c, KV_DIM), lambda b, g, sl, pt: (b, 0, 0)),
            pl.BlockSpec((1, n_c, KV_DIM), lambda b, g, sl, pt: (b, 0, 0)),
            pl.BlockSpec((1, ts, new_rows.shape[2]), lambda b, g, sl, pt: (b, 0, 0)),
            pl.BlockSpec((1, wl, 2 * KV_DIM), lambda b, g, sl, pt: (b, 0, 0)),
            pl.BlockSpec(memory_space=pl.ANY),
        ],
        out_specs=pl.BlockSpec((ts, NSA_DIM), lambda b, g, sl, pt: (b, 0)),
        scratch_shapes=[
            pltpu.VMEM((2, pg * page_rows, 2 * KV_DIM), F32), pltpu.SemaphoreType.DMA((2,)),
            pltpu.VMEM((nk, LANES), F32), pltpu.VMEM((nk, KV_DIM), BF16), pltpu.VMEM((LANES, KV_DIM), BF16),
            pltpu.VMEM((LANES, KV_DIM), F32), pltpu.VMEM((LANES, KV_DIM), F32),
            pltpu.VMEM((n_sb_pad, LANES), F32), pltpu.VMEM((n_sb_pad, LANES), F32), pltpu.VMEM((n_c, LANES), F32),
        ],
    )
    return pl.pallas_call(
        functools.partial(_nsa_sample_kernel, n_pages=n_pages, pg=pg, page_rows=page_rows, past=past),
        out_shape=jax.ShapeDtypeStruct((batch * ts, NSA_DIM), F32),
        grid_spec=grid_spec,
        compiler_params=_cparams("arbitrary", "arbitrary"),
        name="nsa_sample",
    )(slopes, page_table.reshape(-1), proj, ck, cv, new_rows, win_state, pool)


S_MQ_COL = NSA_GROUP * KV_DIM
S_COLS = S_MQ_COL + MEM_DIM + LANES


def _layout_w_in_b_sample(w):
    d = w.shape[0]
    qw = w[:, :NSA_DIM].reshape(d, NSA_KV_HEADS, NSA_GROUP, HEAD_DIM).transpose(0, 2, 1, 3).reshape(d, NSA_DIM)
    gw = jnp.pad(w[:, NSA_DIM:NSA_DIM + 3 * NSA_HEADS], ((0, 0), (0, LANES - 3 * NSA_HEADS)))
    mw = w[:, NSA_DIM + 3 * NSA_HEADS:]
    return jnp.concatenate([qw, mw, gw], axis=1).astype(BF16)


Q_SLOT = 256
B_MQ_COL = NSA_KV_HEADS * Q_SLOT
B_GATE_COL = B_MQ_COL + MEM_DIM
B_COLS = B_GATE_COL + NSA_KV_HEADS * LANES


def _layout_w_in_b(w):
    d = w.shape[0]
    qw = w[:, :NSA_DIM].reshape(d, NSA_KV_HEADS, NSA_GROUP * HEAD_DIM)
    qw = jnp.pad(qw, ((0, 0), (0, 0), (0, Q_SLOT - NSA_GROUP * HEAD_DIM))).reshape(d, NSA_KV_HEADS * Q_SLOT)
    gw = w[:, NSA_DIM:NSA_DIM + 3 * NSA_HEADS].reshape(d, NSA_KV_HEADS, 3 * NSA_GROUP)
    gw = jnp.pad(gw, ((0, 0), (0, 0), (0, LANES - 3 * NSA_GROUP))).reshape(d, NSA_KV_HEADS * LANES)
    mw = w[:, NSA_DIM + 3 * NSA_HEADS:]
    return jnp.concatenate([qw, mw, gw], axis=1).astype(BF16)


def kernel(x_prompt, x_sample, state_conv, cache_mem_kv, cache_cmp_kv, cache_slc_kv, state_win_kv, page_table,
           mem_prompt, g_mix, w_in_a, conv_w, w_in_b, w_o, w_mkv, g_mem, g_kv, w_kv, pe_ck, w1_ck, w2_ck,
           pe_cv, w1_cv, w2_cv, g_ffn, w_gu, w_dn, g_final):
    bp, tp, d = x_prompt.shape
    bs, ts = x_sample.shape[:2]
    depth = g_mix.shape[0]
    n_a = w_in_a.shape[0]
    n_mem = mem_prompt.shape[1]
    win_len = state_win_kv.shape[1]
    past_len = page_table.shape[1] * cache_cmp_kv.shape[1]
    conv_dim = conv_w.shape[2]
    slopes = jnp.asarray(np.array(_alibi_list(NSA_HEADS), dtype=np.float32))

    w_in_a16 = w_in_a.astype(BF16)
    w_in_b16 = [_layout_w_in_b(w_in_b[j]) for j in range(depth - n_a)]
    w_o16 = w_o.astype(BF16)
    w_gu16 = w_gu.astype(BF16)
    w_dn16 = w_dn.astype(BF16)
    w_kv16 = w_kv.astype(BF16)
    w_mkv16 = w_mkv.transpose(1, 0, 2).reshape(d, depth * 2 * MEM_DIM).astype(BF16)

    mkv = rms_matmul(mem_prompt.reshape(bp * n_mem, d), g_mem, w_mkv16)
    mem_kv_p = mkv.reshape(bp, n_mem, depth, 2 * MEM_DIM).transpose(2, 0, 1, 3)
    mem_kv_s = cache_mem_kv.reshape(depth, bs, n_mem, 2 * MEM_DIM)

    groups = [
        dict(x=x_prompt.reshape(bp * tp, d), b=bp, t=tp, mem=mem_kv_p, st=jnp.zeros((n_a, bp, 2, conv_dim), F32)),
        dict(x=x_sample.reshape(bs * ts, d), b=bs, t=ts, mem=mem_kv_s, st=state_conv),
    ]
    conv_out = [[], []]
    kv_rows = [None, None]

    cw = _compress_weights(pe_ck, w1_ck, w2_ck) + _compress_weights(pe_cv, w1_cv, w2_cv)
    nsa_in = None
    w_in_bs16 = [_layout_w_in_b_sample(w_in_b[j]) for j in range(depth - n_a)]
    n_pages = page_table.shape[1]
    page_rows = cache_cmp_kv.shape[1]
    pg = 16 if n_pages % 16 == 0 else n_pages
    pool_cmp = cache_cmp_kv.reshape(cache_cmp_kv.shape[0], page_rows, 2 * KV_DIM)
    pool_slc = cache_slc_kv.reshape(cache_slc_kv.shape[0], page_rows, 2 * KV_DIM)
    win_state = state_win_kv.reshape(bs, win_len, 2 * KV_DIM)
    assert ts == 8 and past_len % SEL_BLOCK == 0 and tp % 256 == 0 and tp // SEL_BLOCK <= N_SEL_POS

    def wo_pairs(l, n, y_main, y_mem, nsa_layout, b, t):
        tm = _row_tile(n, 512)
        if nsa_layout:
            hw = NSA_GROUP * HEAD_DIM
            nt = t // tm
            y2d = y_main.reshape(b * NSA_KV_HEADS * t, hw)
            pairs = [(y2d, (tm, hw), (lambda i, h=h: (((i // nt) * NSA_KV_HEADS + h) * nt + i % nt, 0)),
                      w_o16[l, h * hw:(h + 1) * hw]) for h in range(NSA_KV_HEADS)]
            km = NSA_DIM
        else:
            km = y_main.shape[1]
            pairs = [(y_main, (tm, km), lambda i: (i, 0), w_o16[l, :km])]
        return pairs + [(y_mem, (tm, MEM_DIM), lambda i: (i, 0), w_o16[l, km:])]

    for l in range(depth):
        for gi, gr in enumerate(groups):
            x, b, t = gr["x"], gr["b"], gr["t"]
            n = b * t
            nsa_layout = False
            if l < n_a:
                proj = rms_matmul(x, g_mix[l], w_in_a16[l])
                y_main, new_st = gated_conv(proj, gr["st"][l], conv_w[l], b, t)
                conv_out[gi].append(new_st)
                y_mem = mem_attention(proj, 3 * conv_dim // MEM_DIM, gr["mem"][l], b, t)
            else:
                if l == n_a:
                    kv2d = rms_matmul(x, g_kv, w_kv16)
                    kv_rows[gi] = kv2d.reshape(b, t, 3, 2, NSA_KV_HEADS, HEAD_DIM)
                    kv = kv_rows[gi]
                    if gi == 0:
                        ck, cv = compress_prompt(kv2d, b, t, cw)
                        onehot = jnp.where(jnp.arange(t)[:, None] // SEL_BLOCK == jnp.arange(N_SEL_POS)[None, :],
                                           -MASK_BIG, 0.0).astype(BF16)
                        hm = lambda c0: _head_major(kv2d[:, c0:c0 + KV_DIM], b, t).astype(BF16)
                        aug = jnp.concatenate([jnp.broadcast_to(onehot, (b, NSA_KV_HEADS, t, N_SEL_POS)),
                                               jnp.zeros((b, NSA_KV_HEADS, t, N_SEL_POS), BF16)], axis=2)
                        kcat = jnp.concatenate([jnp.concatenate([hm(2 * KV_DIM), hm(4 * KV_DIM)], axis=2), aug],
                                               axis=-1)
                        vcat = jnp.concatenate([hm(3 * KV_DIM), hm(5 * KV_DIM)], axis=2)
                        nsa_in = (_cmp_slots(ck, b), _cmp_slots(cv, b), kcat, vcat)
                    else:
                        new3 = kv2d.reshape(b, t, 6 * KV_DIM)
                        ck_s, cv_s = compress_sample(page_table, pool_cmp, new3, cw, pg=pg)
                j = l - n_a
                if gi == 0:
                    proj = rms_matmul(x, g_mix[l], w_in_b16[j], tn=B_COLS // 2)
                    y_main = nsa_prompt(proj, slopes, *nsa_in, b, t)
                    nsa_layout = True
                    y_mem = mem_attention(proj, B_MQ_COL // MEM_DIM, gr["mem"][l], b, t)
                else:
                    proj = rms_matmul(x, g_mix[l], w_in_bs16[j], tn=S_COLS)
                    y_main = nsa_sample(proj, slopes, page_table, ck_s, cv_s, new3, win_state, pool_slc, pg=pg)
                    y_mem = mem_attention(proj, S_MQ_COL // MEM_DIM, gr["mem"][l], b, t)
            x = proj_residual(x, wo_pairs(l, n, y_main, y_mem, nsa_layout, b, t))
            x = ffn(x, g_ffn[l], w_gu16[l], w_dn16[l])
            gr["x"] = x

    y_prompt = rmsnorm_rows(groups[0]["x"], g_final).reshape(bp, tp, d)
    y_sample = rmsnorm_rows(groups[1]["x"], g_final).reshape(bs, ts, d)
    conv_state_p = jnp.stack(conv_out[0])
    conv_state_s = jnp.stack(conv_out[1])
    mem_kv_out = mem_kv_p.reshape(depth, bp, n_mem, 2, MEM_HEADS, HEAD_DIM)
    kvp, kvs = kv_rows
    win_kv_p = kvp[:, tp - min(WINDOW, tp):, 2]
    win_kv_s = jnp.concatenate([state_win_kv, kvs[:, :, 2]], axis=1)[:, ts:]
    return (y_prompt, y_sample, conv_state_p, conv_state_s, mem_kv_out, kvp[:, :, 0], kvp[:, :, 1], win_kv_p,
            kvs[:, :, 0], kvs[:, :, 1], win_kv_s)
```

```python
import functools
import math

import numpy as np
import jax
import jax.numpy as jnp
from jax import lax
from jax.experimental import pallas as pl
from jax.experimental.pallas import tpu as pltpu

F32 = jnp.float32
BF16 = jnp.bfloat16

HEAD_DIM = 64
MEM_HEADS = 4
MEM_DIM = MEM_HEADS * HEAD_DIM
NSA_KV_HEADS = 4
NSA_GROUP = 3
NSA_HEADS = NSA_KV_HEADS * NSA_GROUP
NSA_DIM = NSA_HEADS * HEAD_DIM
KV_DIM = NSA_KV_HEADS * HEAD_DIM
CONV_WIDTH = 3
CMP_STRIDE = 16
CMP_LEN = 32
SEL_BLOCK = 64
N_SEL = 16
WINDOW = 512
Q_BLOCK = 64
RMS_EPS = 1e-6
NEG_INF = -1e30
FORCE_SCORE = 1e4
ATT_SCALE = HEAD_DIM ** -0.5

VMEM_LIMIT = 48 * 1024 * 1024


def _cparams(*sem):
    return pltpu.CompilerParams(dimension_semantics=sem, vmem_limit_bytes=VMEM_LIMIT)


def _row_tile(n, want):
    t = min(n, want)
    while n % t:
        t //= 2
    return t


def _rms_matmul_kernel(x_ref, g_ref, w_ref, o_ref, h_ref):
    @pl.when(pl.program_id(1) == 0)
    def _():
        x = x_ref[...]
        ms = jnp.mean(x * x, axis=-1, keepdims=True)
        h_ref[...] = (x * lax.rsqrt(ms + RMS_EPS) * g_ref[...]).astype(BF16)

    o_ref[...] = jnp.dot(h_ref[...], w_ref[...], preferred_element_type=F32)


def rms_matmul(x, g, w, *, tm=1024, tn=512):
    n, d = x.shape
    c = w.shape[1]
    tm = _row_tile(n, tm)
    tn = _row_tile(c, tn)
    return pl.pallas_call(
        _rms_matmul_kernel,
        out_shape=jax.ShapeDtypeStruct((n, c), F32),
        grid=(n // tm, c // tn),
        in_specs=[
            pl.BlockSpec((tm, d), lambda i, j: (i, 0)),
            pl.BlockSpec((1, d), lambda i, j: (0, 0)),
            pl.BlockSpec((d, tn), lambda i, j: (0, j)),
        ],
        out_specs=pl.BlockSpec((tm, tn), lambda i, j: (i, j)),
        scratch_shapes=[pltpu.VMEM((tm, d), BF16)],
        compiler_params=_cparams("parallel", "arbitrary"),
        name="rms_matmul",
    )(x, g.reshape(1, d), w)


def _proj_residual_kernel(n_pairs, x_ref, *refs):
    a_refs = refs[:n_pairs]
    w_refs = refs[n_pairs:2 * n_pairs]
    o_ref = refs[2 * n_pairs]
    acc = x_ref[...]
    for a_ref, w_ref in zip(a_refs, w_refs):
        acc = acc + jnp.dot(a_ref[...].astype(BF16), w_ref[...], preferred_element_type=F32)
    o_ref[...] = acc


def proj_residual(x, pairs, *, tm=512):
    n, d = x.shape
    tm = _row_tile(n, tm)
    in_specs = [pl.BlockSpec((tm, d), lambda i: (i, 0))]
    args = [x]
    for a, blk, imap, _ in pairs:
        in_specs.append(pl.BlockSpec(blk, imap))
        args.append(a)
    for _, _, _, w in pairs:
        in_specs.append(pl.BlockSpec(w.shape, lambda i: (0, 0)))
        args.append(w)
    return pl.pallas_call(
        functools.partial(_proj_residual_kernel, len(pairs)),
        out_shape=jax.ShapeDtypeStruct((n, d), F32),
        grid=(n // tm,),
        in_specs=in_specs,
        out_specs=pl.BlockSpec((tm, d), lambda i: (i, 0)),
        compiler_params=_cparams("parallel"),
        name="proj_residual",
    )(*args)


def _ffn_kernel(x_ref, g_ref, wg_ref, wu_ref, wd_ref, o_ref, h_ref, acc_ref):
    f = pl.program_id(1)

    @pl.when(f == 0)
    def _():
        x = x_ref[...]
        ms = jnp.mean(x * x, axis=-1, keepdims=True)
        h_ref[...] = (x * lax.rsqrt(ms + RMS_EPS) * g_ref[...]).astype(BF16)
        acc_ref[...] = x

    h = h_ref[...]
    gate = jnp.dot(h, wg_ref[...], preferred_element_type=F32)
    up = jnp.dot(h, wu_ref[...], preferred_element_type=F32)
    act = (gate * jax.nn.sigmoid(gate) * up).astype(BF16)
    acc_ref[...] += jnp.dot(act, wd_ref[...], preferred_element_type=F32)

    @pl.when(f == pl.num_programs(1) - 1)
    def _():
        o_ref[...] = acc_ref[...]


def ffn(x, g, w_gu, w_dn, *, tm=512, tf=1408):
    n, d = x.shape
    dff = w_dn.shape[0]
    tm = _row_tile(n, tm)
    nf = dff // tf
    return pl.pallas_call(
        _ffn_kernel,
        out_shape=jax.ShapeDtypeStruct((n, d), F32),
        grid=(n // tm, nf),
        in_specs=[
            pl.BlockSpec((tm, d), lambda i, f: (i, 0)),
            pl.BlockSpec((1, d), lambda i, f: (0, 0)),
            pl.BlockSpec((d, tf), lambda i, f: (0, f)),
            pl.BlockSpec((d, tf), lambda i, f: (0, f + nf)),
            pl.BlockSpec((tf, d), lambda i, f: (f, 0)),
        ],
        out_specs=pl.BlockSpec((tm, d), lambda i, f: (i, 0)),
        scratch_shapes=[pltpu.VMEM((tm, d), BF16), pltpu.VMEM((tm, d), F32)],
        compiler_params=_cparams("parallel", "arbitrary"),
        name="ffn",
    )(x, g.reshape(1, d), w_gu, w_gu, w_dn)


def _rmsnorm_kernel(x_ref, g_ref, o_ref):
    x = x_ref[...]
    ms = jnp.mean(x * x, axis=-1, keepdims=True)
    o_ref[...] = x * lax.rsqrt(ms + RMS_EPS) * g_ref[...]


def rmsnorm_rows(x, g, *, tm=1024):
    n, d = x.shape
    tm = _row_tile(n, tm)
    return pl.pallas_call(
        _rmsnorm_kernel,
        out_shape=jax.ShapeDtypeStruct((n, d), F32),
        grid=(n // tm,),
        in_specs=[pl.BlockSpec((tm, d), lambda i: (i, 0)), pl.BlockSpec((1, d), lambda i: (0, 0))],
        out_specs=pl.BlockSpec((tm, d), lambda i: (i, 0)),
        compiler_params=_cparams("parallel"),
        name="rmsnorm",
    )(x, g.reshape(1, d))


def _conv_kernel(b_ref, c_ref, h_ref, cp_ref, hp_ref, st_ref, w_ref, y_ref, ns_ref):
    i = pl.program_id(1)
    u = c_ref[...] * h_ref[...]
    tt = u.shape[0]
    prev = cp_ref[...] * hp_ref[...]
    st = st_ref[0]
    first = i == 0
    p1 = jnp.where(first, st[1:2], prev[7:8])
    p2 = jnp.where(first, st[0:1], prev[6:7])
    row = lax.broadcasted_iota(jnp.int32, u.shape, 0)
    u1 = jnp.where(row == 0, p1, pltpu.roll(u, 1, 0))
    u2 = jnp.where(row == 0, p2, jnp.where(row == 1, p1, pltpu.roll(u, 2, 0)))
    w = w_ref[...]
    y = w[0:1] * u2 + w[1:2] * u1 + w[2:3] * u
    y_ref[...] = b_ref[...] * y
    ns_ref[0] = u[tt - 2:tt]


def gated_conv(proj, state, conv_w, batch, t, *, tt=512):
    c = conv_w.shape[1]
    tt = _row_tile(t, tt)
    nt = t // tt
    r8 = tt // 8

    def prev_map(col):
        return lambda b, i: (jnp.maximum((b * nt + i) * r8 - 1, 0), col)

    return pl.pallas_call(
        _conv_kernel,
        out_shape=(jax.ShapeDtypeStruct((batch * t, c), F32), jax.ShapeDtypeStruct((batch, 2, c), F32)),
        grid=(batch, nt),
        in_specs=[
            pl.BlockSpec((tt, c), lambda b, i: (b * nt + i, 0)),
            pl.BlockSpec((tt, c), lambda b, i: (b * nt + i, 1)),
            pl.BlockSpec((tt, c), lambda b, i: (b * nt + i, 2)),
            pl.BlockSpec((8, c), prev_map(1)),
            pl.BlockSpec((8, c), prev_map(2)),
            pl.BlockSpec((1, 2, c), lambda b, i: (b, 0, 0)),
            pl.BlockSpec((CONV_WIDTH, c), lambda b, i: (0, 0)),
        ],
        out_specs=(
            pl.BlockSpec((tt, c), lambda b, i: (b * nt + i, 0)),
            pl.BlockSpec((1, 2, c), lambda b, i: (b, 0, 0)),
        ),
        compiler_params=_cparams("parallel", "arbitrary"),
        name="gated_conv",
    )(proj, proj, proj, proj, proj, state, conv_w)


def _mem_attn_kernel(q_ref, kv_ref, o_ref):
    q = q_ref[...] * ATT_SCALE
    kv = kv_ref[0]
    outs = []
    for h in range(MEM_HEADS):
        qh = q[:, h * HEAD_DIM:(h + 1) * HEAD_DIM].astype(BF16)
        kh = kv[:, h * HEAD_DIM:(h + 1) * HEAD_DIM].astype(BF16)
        vh = kv[:, MEM_DIM + h * HEAD_DIM:MEM_DIM + (h + 1) * HEAD_DIM].astype(BF16)
        s = lax.dot_general(qh, kh, (((1,), (1,)), ((), ())), preferred_element_type=F32)
        m = jnp.max(s, axis=-1, keepdims=True)
        e = jnp.exp(s - m)
        p = e / jnp.sum(e, axis=-1, keepdims=True)
        outs.append(jnp.dot(p.astype(BF16), vh, preferred_element_type=F32))
    o_ref[...] = jnp.concatenate(outs, axis=-1)


def mem_attention(proj, col_block, mem_kv, batch, t, *, tt=512):
    tt = _row_tile(t, tt)
    nt = t // tt
    n_mem = mem_kv.shape[1]
    return pl.pallas_call(
        _mem_attn_kernel,
        out_shape=jax.ShapeDtypeStruct((batch * t, MEM_DIM), F32),
        grid=(batch, nt),
        in_specs=[
            pl.BlockSpec((tt, MEM_DIM), lambda b, i: (b * nt + i, col_block)),
            pl.BlockSpec((1, n_mem, 2 * MEM_DIM), lambda b, i: (b, 0, 0)),
        ],
        out_specs=pl.BlockSpec((tt, MEM_DIM), lambda b, i: (b * nt + i, 0)),
        compiler_params=_cparams("parallel", "arbitrary"),
        name="mem_attention",
    )(proj, mem_kv)


def _alibi_list(n):
    def pow2(m):
        start = 2.0 ** (-8.0 / m)
        return [start ** (i + 1) for i in range(m)]
    if n & (n - 1) == 0:
        return pow2(n)
    c = 2 ** int(math.floor(math.log2(n)))
    return pow2(c) + _alibi_list(2 * c)[0::2][: n - c]


LANES = 128
PAIR = LANES // HEAD_DIM


ROW_W = 2 * KV_DIM
CHUNK_W = CMP_STRIDE * ROW_W


def _compress_tile(buf_ref, row0, m, pe_refs, w1_refs, w2_refs):
    outs = []
    for kv in range(2):
        pe = pe_refs[kv][...]
        lhs = []
        for p in range(NSA_KV_HEADS // PAIR):
            lane0 = kv * KV_DIM + p * LANES
            pieces = []
            for j in range(CMP_LEN):
                rows = pl.ds(row0 + j // CMP_STRIDE, m)
                lanes = pl.ds((j % CMP_STRIDE) * ROW_W + lane0, LANES)
                pieces.append((buf_ref[rows, lanes] + pe[j:j + 1]).astype(BF16))
            lhs.append(jnp.concatenate(pieces, axis=1))
        lhs = jnp.concatenate(lhs, axis=0)
        hid = jnp.dot(lhs, w1_refs[kv][...], preferred_element_type=F32)
        act = (hid * jax.nn.sigmoid(hid)).astype(BF16)
        out = jnp.dot(act, w2_refs[kv][...], preferred_element_type=F32)
        outs.append(jnp.concatenate([out[0:m], out[m:2 * m]], axis=1))
    return outs


def _compress_prompt_kernel(x_ref, pek_ref, pev_ref, w1k_ref, w1v_ref, w2k_ref, w2v_ref, ck_ref, cv_ref, buf_ref):
    n_blk = x_ref.shape[1]
    buf_ref[0:n_blk, :] = x_ref[0]
    buf_ref[n_blk:n_blk + 8, :] = jnp.zeros((8, CHUNK_W), F32)
    m = min(n_blk, 128)
    for s in range(n_blk // m):
        ck, cv = _compress_tile(buf_ref, s * m, m, (pek_ref, pev_ref), (w1k_ref, w1v_ref), (w2k_ref, w2v_ref))
        ck_ref[0, s * m:(s + 1) * m, :] = ck
        cv_ref[0, s * m:(s + 1) * m, :] = cv


def _compress_weights(pe, w1, w2):
    eye = jnp.eye(PAIR, dtype=F32)
    w1p = jnp.einsum('jde,qr->jqdre', w1, eye).reshape(CMP_LEN * PAIR * HEAD_DIM, PAIR * w1.shape[2])
    w2p = jnp.einsum('ed,qr->qerd', w2, eye).reshape(PAIR * w2.shape[0], PAIR * HEAD_DIM)
    pe2 = jnp.tile(pe, (1, PAIR))
    return pe2, w1p.astype(BF16), w2p.astype(BF16)


def compress_prompt(chunks, cw):
    pek, w1k, w2k, pev, w1v, w2v = cw
    batch, n_blk = chunks.shape[:2]
    full = lambda a: pl.BlockSpec(a.shape, lambda b: (0,) * a.ndim)
    return pl.pallas_call(
        _compress_prompt_kernel,
        out_shape=(jax.ShapeDtypeStruct((batch, n_blk, KV_DIM), F32),) * 2,
        grid=(batch,),
        in_specs=[pl.BlockSpec((1, n_blk, CHUNK_W), lambda b: (b, 0, 0)), full(pek), full(pev), full(w1k),
                  full(w1v), full(w2k), full(w2v)],
        out_specs=(pl.BlockSpec((1, n_blk, KV_DIM), lambda b: (b, 0, 0)),) * 2,
        scratch_shapes=[pltpu.VMEM((n_blk + 8, CHUNK_W), F32)],
        compiler_params=_cparams("parallel"),
        name="compress_prompt",
    )(chunks, pek, pev, w1k, w1v, w2k, w2v)


N_CMP_POS = 256
N_SEL_POS = N_CMP_POS * CMP_STRIDE // SEL_BLOCK
MASK_BIG = 2.0 ** 100
M_INIT = -1e38


LOG2E = 1.4426950408889634
MASK_NONE, MASK_CAUSAL, MASK_BAND = 0, 1, 2


def _topk_mask(score):
    tq, nb = score.shape
    pad = jnp.full((tq, LANES - nb), -jnp.inf, F32)
    st = jnp.concatenate([score, pad], axis=1).T[:nb]
    sub = 8
    idx = lax.broadcasted_iota(jnp.int32, (sub, tq), 0)
    groups = [st[r:r + sub] for r in range(0, nb, sub)]
    ranks = [jnp.zeros((sub, tq), F32) for _ in groups]
    for j in range(nb):
        rj = st[j:j + 1, :]
        for gi, sg in enumerate(groups):
            if gi * sub > j:
                before = rj >= sg
            elif gi * sub + sub - 1 < j:
                before = rj > sg
            else:
                before = jnp.where(idx > j - gi * sub, (rj >= sg).astype(F32), (rj > sg).astype(F32)) > 0.0
            ranks[gi] = jnp.where(before, ranks[gi] + 1.0, ranks[gi])
    sel_t = (jnp.concatenate(ranks, axis=0) < N_SEL).astype(F32)
    blk_any = jnp.max(sel_t, axis=1, keepdims=True)
    sel_t = jnp.concatenate([sel_t, jnp.zeros((LANES - nb, tq), F32)], axis=0)
    return sel_t.T[:, :nb], blk_any


def _chunk_words(blk_any, per_chunk):
    nb = blk_any.shape[0]
    per_word = 8 * per_chunk
    bidx = lax.broadcasted_iota(jnp.int32, (nb, 1), 0)
    wgt = jnp.left_shift(1, 3 * ((bidx % per_word) // per_chunk)).astype(F32)
    words = []
    for w in range(nb // per_word):
        v = jnp.sum(jnp.where(bidx // per_word == w, blk_any * wgt, 0.0), axis=0, keepdims=True)
        words.append(v.astype(jnp.int32)[0, 0])
    return words


def _nsa_prompt_kernel(sl_ref, q_ref, gt_ref, ck_ref, cv_ref, kc_ref, vc_ref, o_ref,
                       q_scr, s0_scr, s1_scr, p0_scr, p1_scr, a0_scr, a1_scr, m_scr, l_scr, acc_scr, mb_scr, lst_ref,
                       *, mm_rows, sm_rows):
    kvh = pl.program_id(1)
    i = pl.program_id(2)
    s_scrs, p_scrs, a_scrs = (s0_scr, s1_scr), (p0_scr, p1_scr), (a0_scr, a1_scr)
    tq = q_ref.shape[0]
    kb = tq
    t_len = kc_ref.shape[2] // 2
    t0 = i * tq
    slopes = [sl_ref[kvh * NSA_GROUP + g] * LOG2E for g in range(NSA_GROUP)]
    q = q_ref[...] * (ATT_SCALE * LOG2E)
    q3 = jnp.concatenate([q[:, g * HEAD_DIM:(g + 1) * HEAD_DIM] for g in range(NSA_GROUP)], axis=0)
    q3b = q3.astype(BF16)
    nt_dims = (((1,), (1,)), ((), ()))
    t_col = t0 + lax.broadcasted_iota(jnp.int32, (tq, 1), 0)

    @pl.when(i == 0)
    def _():
        r = lax.broadcasted_iota(jnp.int32, (tq, kb), 0)
        c = lax.broadcasted_iota(jnp.int32, (tq, kb), 1)
        mb_scr[MASK_NONE] = jnp.zeros((tq, kb), F32)
        mb_scr[MASK_CAUSAL] = jnp.where(c <= r, 0.0, -MASK_BIG)
        mb_scr[MASK_BAND] = jnp.where(c >= r, 0.0, -MASK_BIG)

    m_scr[...] = jnp.full(m_scr.shape, M_INIT, F32)
    l_scr[...] = jnp.zeros(l_scr.shape, F32)
    acc_scr[...] = jnp.zeros(acc_scr.shape, F32)

    pos = lax.broadcasted_iota(jnp.int32, (1, N_CMP_POS), 1)
    blk_n = (pos % N_SEL_POS) * (N_CMP_POS // N_SEL_POS) + pos // N_SEL_POS
    c_end = blk_n * CMP_STRIDE + (CMP_LEN - 1)
    n_real = t_len // CMP_STRIDE - 1
    vis_c = (c_end <= t_col) & (blk_n < n_real)
    c_end_f = c_end.astype(F32)
    s_c = lax.dot_general(q3b, ck_ref[0, 0], nt_dims, preferred_element_type=F32)
    p_c = []
    for g in range(NSA_GROUP):
        sg = jnp.where(vis_c, s_c[g * tq:(g + 1) * tq] + slopes[g] * c_end_f, NEG_INF)
        mg = jnp.max(sg, axis=1, keepdims=True)
        eg = jnp.where(vis_c, jnp.exp2(sg - mg), 0.0)
        dg = jnp.sum(eg, axis=1, keepdims=True)
        p_c.append(eg / jnp.where(dg > 0.0, dg, 1.0))
    o_c = jnp.dot(jnp.concatenate(p_c, axis=0).astype(BF16), cv_ref[0, 0], preferred_element_type=F32)
    imp = p_c[0] + p_c[1] + p_c[2]
    imp = imp[:, :LANES] + imp[:, LANES:]
    imp = imp[:, :N_SEL_POS] + imp[:, N_SEL_POS:]

    blk = lax.broadcasted_iota(jnp.int32, (1, N_SEL_POS), 1)
    cur = t_col // SEL_BLOCK
    forced = (blk == 0) | (blk == cur) | (blk == cur - 1)
    score = jnp.where(forced, FORCE_SCORE, jnp.where(blk <= cur, imp, -jnp.inf))
    sel, blk_any = _topk_mask(score)
    notsel = (1.0 - sel).astype(BF16)
    q_scr[...] = jnp.concatenate([q3b, jnp.concatenate([notsel] * NSA_GROUP, axis=0)], axis=1)

    lane_k = lax.broadcasted_iota(jnp.int32, (1, kb), 1)
    n_win = WINDOW // kb + 1

    words = _chunk_words(blk_any, kb // SEL_BLOCK)
    n_sel = jnp.int32(0)
    for c in range(N_SEL_POS * SEL_BLOCK // kb):
        active = (((words[c // 8] >> (3 * (c % 8))) & 7) != 0) & (c <= i)
        lst_ref[n_sel] = c
        n_sel = n_sel + active.astype(jnp.int32)
    n_chunks = n_sel + n_win

    def chunk(p):
        is_sel = p < n_sel
        is_pad = p >= n_chunks
        c = lst_ref[jnp.minimum(p, n_sel - 1)]
        w = p - n_sel
        kpos0 = jnp.where(is_sel, c * kb, jnp.where(is_pad, -kb, t0 - WINDOW + w * kb))
        row0 = jnp.maximum(kpos0, 0) + jnp.where(is_sel | is_pad, 0, t_len)
        mtype = jnp.where(is_sel, jnp.where(c == i, MASK_CAUSAL, MASK_NONE),
                          jnp.where(w == 0, MASK_BAND, jnp.where(w == n_win - 1, MASK_CAUSAL, MASK_NONE)))
        return pl.multiple_of(row0, kb), kpos0, mtype, jnp.where(is_sel | is_pad, 0, 1)

    def stage_logits(c, par):
        s_scr = s_scrs[par]
        row0, kpos0, mtype, _ = chunk(c)
        k = kc_ref[0, 0, pl.ds(row0, kb), :]
        kpos_f = (kpos0 + lane_k).astype(F32)
        off = jnp.where(kpos0 >= 0, 0.0, -MASK_BIG)
        for g in range(NSA_GROUP):
            bias = slopes[g] * kpos_f + off
            for r in range(0, tq, mm_rows):
                rows = slice(g * tq + r, g * tq + r + mm_rows)
                s = lax.dot_general(q_scr[rows, :], k, nt_dims, preferred_element_type=F32)
                s_scr[rows, :] = (s + mb_scr[mtype, r:r + mm_rows, :]) + bias

    def stage_softmax(c, par):
        s_scr, p_scr, a_scr = s_scrs[par], p_scrs[par], a_scrs[par]
        st = chunk(c)[3]
        for r in range(0, NSA_GROUP * tq, sm_rows):
            rows = slice(r, r + sm_rows)
            s = s_scr[rows, :]
            m_prev = m_scr[st, rows, :]
            m_new = jnp.maximum(m_prev, jnp.max(s, axis=1, keepdims=True))
            alpha = jnp.exp2(m_prev - m_new)
            p = jnp.exp2(s - jnp.concatenate([m_new] * (kb // LANES), axis=1))
            l_scr[st, rows, :] = alpha * l_scr[st, rows, :] + jnp.sum(p, axis=1, keepdims=True)
            p_scr[rows, :] = p.astype(BF16)
            a_scr[rows, :] = alpha
            m_scr[st, rows, :] = m_new

    def stage_values(c, par):
        p_scr, a_scr = p_scrs[par], a_scrs[par]
        row0, _, _, st = chunk(c)
        v = vc_ref[0, 0, pl.ds(row0, kb), :]
        for r in range(0, NSA_GROUP * tq, mm_rows):
            rows = slice(r, r + mm_rows)
            pv = jnp.dot(p_scr[rows, :], v, preferred_element_type=F32)
            acc_scr[st, rows, :] = acc_scr[st, rows, :] * a_scr[rows, :HEAD_DIM] + pv

    stage_logits(0, 0)
    stage_softmax(0, 0)
    stage_logits(1, 1)

    def pipe_body(j, carry):
        c = 2 * j
        stage_values(c - 2, 0)
        stage_softmax(c - 1, 1)
        stage_logits(c, 0)
        stage_values(c - 1, 1)
        stage_softmax(c, 0)
        stage_logits(c + 1, 1)
        return carry

    n_even = n_chunks + n_chunks % 2
    lax.fori_loop(1, n_even // 2, pipe_body, 0)
    stage_values(n_even - 2, 0)
    stage_softmax(n_even - 1, 1)
    stage_values(n_even - 1, 1)
    o_s = acc_scr[0] / l_scr[0][:, :HEAD_DIM]
    o_w = acc_scr[1] / l_scr[1][:, :HEAD_DIM]

    gates = jax.nn.sigmoid(gt_ref[...])
    outs = []
    for g in range(NSA_GROUP):
        rows = slice(g * tq, (g + 1) * tq)
        outs.append(gates[:, 3 * g:3 * g + 1] * o_c[rows] + gates[:, 3 * g + 1:3 * g + 2] * o_s[rows]
                    + gates[:, 3 * g + 2:3 * g + 3] * o_w[rows])
    o_ref[0, 0] = jnp.concatenate(outs, axis=1)


def nsa_prompt(proj, slopes, ckp, cvp, kcat, vcat, batch, t, *, tq=256, mm_rows=256, sm_rows=64):
    assert t % tq == 0 and WINDOW % tq == 0 and WINDOW >= tq
    nt = t // tq
    rows = NSA_GROUP * tq
    gcol = B_GATE_COL // LANES
    grid_spec = pltpu.PrefetchScalarGridSpec(
        num_scalar_prefetch=1,
        grid=(batch, NSA_KV_HEADS, nt),
        in_specs=[
            pl.BlockSpec((tq, Q_SLOT), lambda b, h, i, sl: (b * nt + i, h)),
            pl.BlockSpec((tq, LANES), lambda b, h, i, sl: (b * nt + i, gcol + h)),
            pl.BlockSpec((1, 1, N_CMP_POS, HEAD_DIM), lambda b, h, i, sl: (b, h, 0, 0)),
            pl.BlockSpec((1, 1, N_CMP_POS, HEAD_DIM), lambda b, h, i, sl: (b, h, 0, 0)),
            pl.BlockSpec((1, 1, 2 * t, LANES), lambda b, h, i, sl: (b, h, 0, 0)),
            pl.BlockSpec((1, 1, 2 * t, HEAD_DIM), lambda b, h, i, sl: (b, h, 0, 0)),
        ],
        out_specs=pl.BlockSpec((1, 1, tq, NSA_GROUP * HEAD_DIM), lambda b, h, i, sl: (b, h, i, 0)),
        scratch_shapes=[
            pltpu.VMEM((rows, LANES), BF16),
            pltpu.VMEM((rows, tq), F32), pltpu.VMEM((rows, tq), F32),
            pltpu.VMEM((rows, tq), BF16), pltpu.VMEM((rows, tq), BF16),
            pltpu.VMEM((rows, LANES), F32), pltpu.VMEM((rows, LANES), F32),
            pltpu.VMEM((2, rows, LANES), F32),
            pltpu.VMEM((2, rows, LANES), F32),
            pltpu.VMEM((2, rows, HEAD_DIM), F32),
            pltpu.VMEM((3, tq, tq), F32),
            pltpu.SMEM((N_SEL_POS * SEL_BLOCK // tq,), jnp.int32),
        ],
    )
    return pl.pallas_call(
        functools.partial(_nsa_prompt_kernel, mm_rows=mm_rows, sm_rows=sm_rows),
        out_shape=jax.ShapeDtypeStruct((batch, NSA_KV_HEADS, t, NSA_GROUP * HEAD_DIM), F32),
        grid_spec=grid_spec,
        compiler_params=_cparams("parallel", "parallel", "arbitrary"),
        name="nsa_prompt",
    )(slopes, proj, proj, ckp, cvp, kcat, vcat)


def _head_major(x, batch, t):
    return x.reshape(batch, t, NSA_KV_HEADS, HEAD_DIM).transpose(0, 2, 1, 3)


def _cmp_slots(c, batch):
    n_blk = c.shape[1]
    per = N_CMP_POS // N_SEL_POS
    c = jnp.pad(c, ((0, 0), (0, N_CMP_POS - n_blk), (0, 0)))
    c = c.reshape(batch, N_SEL_POS, per, NSA_KV_HEADS, HEAD_DIM).transpose(0, 3, 2, 1, 4)
    return c.reshape(batch, NSA_KV_HEADS, N_CMP_POS, HEAD_DIM).astype(BF16)


def _page_group_dma(pt_ref, pool_ref, buf_ref, sem_ref, n_pages, pg, page_rows):
    ng = n_pages // pg

    def run(step, slot, go):
        bb = step // ng
        gg = step % ng
        for k in range(pg):
            page = pt_ref[bb * n_pages + gg * pg + k]
            cp = pltpu.make_async_copy(pool_ref.at[page], buf_ref.at[slot, pl.ds(k * page_rows, page_rows)],
                                       sem_ref.at[slot])
            cp.start() if go == "start" else cp.wait()

    return run


def _compress_sample_kernel(pt_ref, pool_ref, new_ref, pek_ref, pev_ref, w1k_ref, w1v_ref, w2k_ref, w2v_ref,
                            ck_ref, cv_ref, buf_ref, sem_ref, *, n_pages, pg, page_rows):
    b = pl.program_id(0)
    g = pl.program_id(1)
    ng = n_pages // pg
    step = b * ng + g
    n_steps = pl.num_programs(0) * ng
    slot = step % 2
    rows = pg * page_rows
    pages = _page_group_dma(pt_ref, pool_ref, buf_ref, sem_ref, n_pages, pg, page_rows)

    def halo(st, sl, go):
        bb = st // ng
        gg = st % ng

        @pl.when(gg < ng - 1)
        def _():
            page = pt_ref[bb * n_pages + (gg + 1) * pg]
            cp = pltpu.make_async_copy(pool_ref.at[page, pl.ds(0, 1)], buf_ref.at[sl, pl.ds(rows, 1)],
                                       sem_ref.at[sl])
            cp.start() if go == "start" else cp.wait()

        @pl.when(gg == ng - 1)
        def _():
            cp = pltpu.make_async_copy(new_ref.at[bb], buf_ref.at[sl, pl.ds(rows, 1)], sem_ref.at[sl])
            cp.start() if go == "start" else cp.wait()

    @pl.when(step == 0)
    def _():
        pages(0, 0, "start")
        halo(0, 0, "start")

    @pl.when(step + 1 < n_steps)
    def _():
        pages(step + 1, 1 - slot, "start")
        halo(step + 1, 1 - slot, "start")

    pages(step, slot, "wait")
    halo(step, slot, "wait")
    ck, cv = _compress_tile(buf_ref.at[slot], 0, rows, (pek_ref, pev_ref), (w1k_ref, w1v_ref), (w2k_ref, w2v_ref))
    ck_ref[0] = ck
    cv_ref[0] = cv


def compress_sample(page_table, pool, new_chunk, cw, *, pg):
    pek, w1k, w2k, pev, w1v, w2v = cw
    batch, n_pages = page_table.shape
    page_rows = pool.shape[1]
    ng = n_pages // pg
    m = pg * page_rows
    full = lambda a: pl.BlockSpec(a.shape, lambda b, g, pt: (0,) * a.ndim)
    grid_spec = pltpu.PrefetchScalarGridSpec(
        num_scalar_prefetch=1,
        grid=(batch, ng),
        in_specs=[pl.BlockSpec(memory_space=pl.ANY), pl.BlockSpec(memory_space=pl.ANY), full(pek), full(pev),
                  full(w1k), full(w1v), full(w2k), full(w2v)],
        out_specs=(pl.BlockSpec((1, m, KV_DIM), lambda b, g, pt: (b, g, 0)),) * 2,
        scratch_shapes=[pltpu.VMEM((2, m + 8, CHUNK_W), F32), pltpu.SemaphoreType.DMA((2,))],
    )
    return pl.pallas_call(
        functools.partial(_compress_sample_kernel, n_pages=n_pages, pg=pg, page_rows=page_rows),
        out_shape=(jax.ShapeDtypeStruct((batch, ng * m, KV_DIM), F32),) * 2,
        grid_spec=grid_spec,
        compiler_params=_cparams("arbitrary", "arbitrary"),
        name="compress_sample",
    )(page_table.reshape(-1), pool, new_chunk, pek, pev, w1k, w1v, w2k, w2v)


S_COL_G = 32


def _softmax_rows0(s, vis):
    s = jnp.where(vis, s, NEG_INF)
    m = jnp.max(s, axis=0, keepdims=True)
    e = jnp.where(vis, jnp.exp(s - m), 0.0)
    d = jnp.sum(e, axis=0, keepdims=True)
    return e / jnp.where(d > 0.0, d, 1.0)


def _nsa_sample_kernel(sl_ref, pt_ref, proj_ref, ck_ref, cv_ref, new_ref, win_ref, pool_ref, y_ref,
                       buf_ref, sem_ref, s_ref, v_ref, bdq_ref, oc_ref, ow_ref, sc_ref, sel_ref, imp_ref,
                       *, n_pages, pg, page_rows, past):
    b = pl.program_id(0)
    g = pl.program_id(1)
    ng = n_pages // pg
    step = b * ng + g
    n_steps = pl.num_programs(0) * ng
    slot = step % 2
    rows = pg * page_rows
    ts = proj_ref.shape[0]
    nt_dims = (((1,), (1,)), ((), ()))
    pages = _page_group_dma(pt_ref, pool_ref, buf_ref, sem_ref, n_pages, pg, page_rows)

    @pl.when(step == 0)
    def _():
        pages(0, 0, "start")

    @pl.when(step + 1 < n_steps)
    def _():
        pages(step + 1, 1 - slot, "start")

    col = lax.broadcasted_iota(jnp.int32, (1, LANES), 1)
    col_g = col // S_COL_G
    col_h = (col % S_COL_G) // ts
    t_row = past + col % ts
    slope_row = jnp.zeros((1, LANES), F32)
    for h in range(NSA_HEADS):
        slope_row = jnp.where((col_h == h // NSA_GROUP) & (col_g == h % NSA_GROUP), sl_ref[h], slope_row)
    n_sb = past // SEL_BLOCK + 1

    @pl.when(g == 0)
    def _():
        lane_h = lax.broadcasted_iota(jnp.int32, (ts, KV_DIM), 1) // HEAD_DIM
        tiles = []
        for gg in range(NSA_GROUP):
            qg = proj_ref[:, gg * KV_DIM:(gg + 1) * KV_DIM] * ATT_SCALE
            for h in range(NSA_KV_HEADS):
                tiles.append(jnp.where(lane_h == h, qg, 0.0))
        tiles.append(jnp.zeros((LANES - NSA_HEADS * ts, KV_DIM), F32))
        bdq = jnp.concatenate(tiles, axis=0).astype(BF16)
        bdq_ref[...] = bdq

        n_c = ck_ref.shape[1]
        s_c = lax.dot_general(ck_ref[0].astype(BF16), bdq, nt_dims, preferred_element_type=F32)
        c_end = lax.broadcasted_iota(jnp.int32, (n_c, 1), 0) * CMP_STRIDE + (CMP_LEN - 1)
        dist = t_row - c_end
        p_c = _softmax_rows0(s_c - slope_row * dist.astype(F32), dist >= 0)
        oc_ref[...] = jnp.dot(p_c.T.astype(BF16), cv_ref[0].astype(BF16), preferred_element_type=F32)
        pz = jnp.where(col < NSA_GROUP * S_COL_G, p_c, 0.0)
        imp_ref[...] = pz + pltpu.roll(pz, S_COL_G, 1) + pltpu.roll(pz, 2 * S_COL_G, 1) + pltpu.roll(pz, 3 * S_COL_G, 1)
        per = SEL_BLOCK // CMP_STRIDE
        n_imp = n_c // per
        imp = imp_ref[pl.ds(0, n_imp, stride=per), :]
        for r in range(1, per):
            imp = imp + imp_ref[pl.ds(r, n_imp, stride=per), :]
        n_pad = sc_ref.shape[0]
        imp = jnp.concatenate([imp, jnp.zeros((n_pad - n_imp, LANES), F32)], axis=0)
        blk = lax.broadcasted_iota(jnp.int32, (n_pad, 1), 0)
        cur = t_row // SEL_BLOCK
        forced = (blk == 0) | (blk == cur) | (blk == cur - 1)
        sc_ref[...] = jnp.where(forced, FORCE_SCORE, jnp.where((blk <= cur) & (blk < n_sb), imp, -jnp.inf))
        st = sc_ref[...]

        def rank_body(j, rank):
            rj = sc_ref[pl.ds(j, 1), :]
            return rank + jnp.where(blk > j, (rj >= st).astype(F32), (rj > st).astype(F32))

        rank = lax.fori_loop(0, n_sb, rank_body, jnp.zeros(st.shape, F32))
        sel_ref[...] = (rank < N_SEL).astype(F32)

        wl = win_ref.shape[1]
        zpad = jnp.zeros((LANES - ts, KV_DIM), F32)
        kw = jnp.concatenate([win_ref[0, :, 0:KV_DIM], new_ref[0, :, 4 * KV_DIM:5 * KV_DIM], zpad], axis=0)
        vw = jnp.concatenate([win_ref[0, :, KV_DIM:2 * KV_DIM], new_ref[0, :, 5 * KV_DIM:6 * KV_DIM], zpad], axis=0)
        s_w = lax.dot_general(kw.astype(BF16), bdq, nt_dims, preferred_element_type=F32)
        w_pos = past - wl + lax.broadcasted_iota(jnp.int32, (wl + LANES, 1), 0)
        dist_w = t_row - w_pos
        p_w = _softmax_rows0(s_w - slope_row * dist_w.astype(F32),
                             (dist_w >= 0) & (dist_w <= WINDOW) & (w_pos >= 0))
        ow_ref[...] = jnp.dot(p_w.T.astype(BF16), vw.astype(BF16), preferred_element_type=F32)

    pages(step, slot, "wait")
    bdq = bdq_ref[...]
    base = g * rows
    kpage = buf_ref[slot, :, 0:KV_DIM].astype(BF16)
    s_g = lax.dot_general(kpage, bdq, nt_dims, preferred_element_type=F32)
    kpos = base + lax.broadcasted_iota(jnp.int32, (rows, 1), 0)
    s_g = s_g - slope_row * (t_row - kpos).astype(F32)
    row0 = pl.multiple_of(base, rows)
    for bl in range(rows // SEL_BLOCK):
        on = sel_ref[pl.ds(g * (rows // SEL_BLOCK) + bl, 1), :] > 0.0
        s_ref[pl.ds(row0 + bl * SEL_BLOCK, SEL_BLOCK), :] = jnp.where(
            on, s_g[bl * SEL_BLOCK:(bl + 1) * SEL_BLOCK], NEG_INF)
    v_ref[pl.ds(row0, rows), :] = buf_ref[slot, :, KV_DIM:2 * KV_DIM].astype(BF16)

    @pl.when(g == ng - 1)
    def _():
        tail = s_ref.shape[0] - past
        zpad = jnp.zeros((tail - ts, KV_DIM), F32)
        k_new = jnp.concatenate([new_ref[0, :, 2 * KV_DIM:3 * KV_DIM], zpad], axis=0)
        v_new = jnp.concatenate([new_ref[0, :, 3 * KV_DIM:4 * KV_DIM], zpad], axis=0)
        s_n = lax.dot_general(k_new.astype(BF16), bdq, nt_dims, preferred_element_type=F32)
        kp = past + lax.broadcasted_iota(jnp.int32, (tail, 1), 0)
        dist_n = t_row - kp
        on = (sel_ref[pl.ds(n_sb - 1, 1), :] > 0.0) & (dist_n >= 0)
        s_ref[pl.ds(past, tail), :] = jnp.where(on, s_n - slope_row * dist_n.astype(F32), NEG_INF)
        v_ref[pl.ds(past, tail), :] = v_new.astype(BF16)
        nk = s_ref.shape[0]
        m = jnp.max(s_ref[...], axis=0, keepdims=True)
        ch = rows
        l = jnp.zeros((1, LANES), F32)
        o_s = jnp.zeros((LANES, KV_DIM), F32)
        for c0 in range(0, nk, ch):
            cl = min(ch, nk - c0)
            e = jnp.exp(s_ref[c0:c0 + cl, :] - m)
            l = l + jnp.sum(e, axis=0, keepdims=True)
            if cl % LANES:
                e = jnp.concatenate([e, jnp.zeros((LANES - cl % LANES, LANES), F32)], axis=0)
                vv = jnp.concatenate([v_ref[c0:c0 + cl, :], jnp.zeros((LANES - cl % LANES, KV_DIM), BF16)], axis=0)
            else:
                vv = v_ref[c0:c0 + cl, :]
            o_s = o_s + jnp.dot(e.T.astype(BF16), vv, preferred_element_type=F32)
        inv_l = (1.0 / jnp.broadcast_to(l, (8, LANES))).T[:, 0:1]
        o_s = o_s * inv_l
        o_c = oc_ref[...]
        o_w = ow_ref[...]
        gates = jax.nn.sigmoid(proj_ref[:, (NSA_GROUP + 1) * KV_DIM:(NSA_GROUP + 1) * KV_DIM + LANES])
        outs = []
        for h in range(NSA_HEADS):
            kvh, gg = h // NSA_GROUP, h % NSA_GROUP
            r0 = gg * S_COL_G + kvh * ts
            lanes = slice(kvh * HEAD_DIM, (kvh + 1) * HEAD_DIM)
            outs.append(gates[:, 3 * h:3 * h + 1] * o_c[r0:r0 + ts, lanes]
                        + gates[:, 3 * h + 1:3 * h + 2] * o_s[r0:r0 + ts, lanes]
                        + gates[:, 3 * h + 2:3 * h + 3] * o_w[r0:r0 + ts, lanes])
        y_ref[...] = jnp.concatenate(outs, axis=1)


def nsa_sample(proj, slopes, page_table, ck, cv, new_rows, win_state, pool, *, pg):
    batch, n_pages = page_table.shape
    page_rows = pool.shape[1]
    ts = new_rows.shape[1]
    past = n_pages * page_rows
    ng = n_pages // pg
    n_c = ck.shape[1]
    wl = win_state.shape[1]
    n_sb_pad = -(-(past // SEL_BLOCK + 1) // 8) * 8
    nk = past + 16
    grid_spec = pltpu.PrefetchScalarGridSpec(
        num_scalar_prefetch=2,
        grid=(batch, ng),
        in_specs=[
            pl.BlockSpec((ts, proj.shape[1]), lambda b, g, sl, pt: (b, 0)),
            pl.BlockSpec((1, n_c, KV_DIM), lambda b, g, sl, pt: (b, 0, 0)),
            pl.BlockSpec((1, n_c, KV_DIM), lambda b, g, sl, pt: (b, 0, 0)),
            pl.BlockSpec((1, ts, new_rows.shape[2]), lambda b, g, sl, pt: (b, 0, 0)),
            pl.BlockSpec((1, wl, 2 * KV_DIM), lambda b, g, sl, pt: (b, 0, 0)),
            pl.BlockSpec(memory_space=pl.ANY),
        ],
        out_specs=pl.BlockSpec((ts, NSA_DIM), lambda b, g, sl, pt: (b, 0)),
        scratch_shapes=[
            pltpu.VMEM((2, pg * page_rows, 2 * KV_DIM), F32), pltpu.SemaphoreType.DMA((2,)),
            pltpu.VMEM((nk, LANES), F32), pltpu.VMEM((nk, KV_DIM), BF16), pltpu.VMEM((LANES, KV_DIM), BF16),
            pltpu.VMEM((LANES, KV_DIM), F32), pltpu.VMEM((LANES, KV_DIM), F32),
            pltpu.VMEM((n_sb_pad, LANES), F32), pltpu.VMEM((n_sb_pad, LANES), F32), pltpu.VMEM((n_c, LANES), F32),
        ],
    )
    return pl.pallas_call(
        functools.partial(_nsa_sample_kernel, n_pages=n_pages, pg=pg, page_rows=page_rows, past=past),
        out_shape=jax.ShapeDtypeStruct((batch * ts, NSA_DIM), F32),
        grid_spec=grid_spec,
        compiler_params=_cparams("arbitrary", "arbitrary"),
        name="nsa_sample",
    )(slopes, page_table.reshape(-1), proj, ck, cv, new_rows, win_state, pool)


S_MQ_COL = NSA_GROUP * KV_DIM
S_COLS = S_MQ_COL + MEM_DIM + LANES


def _layout_w_in_b_sample(w):
    d = w.shape[0]
    qw = w[:, :NSA_DIM].reshape(d, NSA_KV_HEADS, NSA_GROUP, HEAD_DIM).transpose(0, 2, 1, 3).reshape(d, NSA_DIM)
    gw = jnp.pad(w[:, NSA_DIM:NSA_DIM + 3 * NSA_HEADS], ((0, 0), (0, LANES - 3 * NSA_HEADS)))
    mw = w[:, NSA_DIM + 3 * NSA_HEADS:]
    return jnp.concatenate([qw, mw, gw], axis=1).astype(BF16)


Q_SLOT = 256
B_MQ_COL = NSA_KV_HEADS * Q_SLOT
B_GATE_COL = B_MQ_COL + MEM_DIM
B_COLS = B_GATE_COL + NSA_KV_HEADS * LANES


def _layout_w_in_b(w):
    d = w.shape[0]
    qw = w[:, :NSA_DIM].reshape(d, NSA_KV_HEADS, NSA_GROUP * HEAD_DIM)
    qw = jnp.pad(qw, ((0, 0), (0, 0), (0, Q_SLOT - NSA_GROUP * HEAD_DIM))).reshape(d, NSA_KV_HEADS * Q_SLOT)
    gw = w[:, NSA_DIM:NSA_DIM + 3 * NSA_HEADS].reshape(d, NSA_KV_HEADS, 3 * NSA_GROUP)
    gw = jnp.pad(gw, ((0, 0), (0, 0), (0, LANES - 3 * NSA_GROUP))).reshape(d, NSA_KV_HEADS * LANES)
    mw = w[:, NSA_DIM + 3 * NSA_HEADS:]
    return jnp.concatenate([qw, mw, gw], axis=1).astype(BF16)


def kernel(x_prompt, x_sample, state_conv, cache_mem_kv, cache_cmp_kv, cache_slc_kv, state_win_kv, page_table,
           mem_prompt, g_mix, w_in_a, conv_w, w_in_b, w_o, w_mkv, g_mem, g_kv, w_kv, pe_ck, w1_ck, w2_ck,
           pe_cv, w1_cv, w2_cv, g_ffn, w_gu, w_dn, g_final):
    bp, tp, d = x_prompt.shape
    bs, ts = x_sample.shape[:2]
    depth = g_mix.shape[0]
    n_a = w_in_a.shape[0]
    n_mem = mem_prompt.shape[1]
    win_len = state_win_kv.shape[1]
    past_len = page_table.shape[1] * cache_cmp_kv.shape[1]
    conv_dim = conv_w.shape[2]
    slopes = jnp.asarray(np.array(_alibi_list(NSA_HEADS), dtype=np.float32))

    w_in_a16 = w_in_a.astype(BF16)
    w_in_b16 = [_layout_w_in_b(w_in_b[j]) for j in range(depth - n_a)]
    w_o16 = w_o.astype(BF16)
    w_gu16 = w_gu.astype(BF16)
    w_dn16 = w_dn.astype(BF16)
    w_kv16 = w_kv.astype(BF16)
    w_mkv16 = w_mkv.transpose(1, 0, 2).reshape(d, depth * 2 * MEM_DIM).astype(BF16)

    mkv = rms_matmul(mem_prompt.reshape(bp * n_mem, d), g_mem, w_mkv16)
    mem_kv_p = mkv.reshape(bp, n_mem, depth, 2 * MEM_DIM).transpose(2, 0, 1, 3)
    mem_kv_s = cache_mem_kv.reshape(depth, bs, n_mem, 2 * MEM_DIM)

    groups = [
        dict(x=x_prompt.reshape(bp * tp, d), b=bp, t=tp, mem=mem_kv_p, st=jnp.zeros((n_a, bp, 2, conv_dim), F32)),
        dict(x=x_sample.reshape(bs * ts, d), b=bs, t=ts, mem=mem_kv_s, st=state_conv),
    ]
    conv_out = [[], []]
    kv_rows = [None, None]

    cw = _compress_weights(pe_ck, w1_ck, w2_ck) + _compress_weights(pe_cv, w1_cv, w2_cv)
    nsa_in = None
    w_in_bs16 = [_layout_w_in_b_sample(w_in_b[j]) for j in range(depth - n_a)]
    n_pages = page_table.shape[1]
    page_rows = cache_cmp_kv.shape[1]
    pg = 16 if n_pages % 16 == 0 else n_pages
    pool_cmp = cache_cmp_kv.reshape(cache_cmp_kv.shape[0], page_rows // CMP_STRIDE, CHUNK_W)
    pool_slc = cache_slc_kv.reshape(cache_slc_kv.shape[0], page_rows, 2 * KV_DIM)
    win_state = state_win_kv.reshape(bs, win_len, 2 * KV_DIM)
    assert ts == 8 and past_len % SEL_BLOCK == 0 and tp % 256 == 0 and tp // SEL_BLOCK <= N_SEL_POS

    def wo_pairs(l, n, y_main, y_mem, nsa_layout, b, t):
        tm = _row_tile(n, 512)
        if nsa_layout:
            hw = NSA_GROUP * HEAD_DIM
            nt = t // tm
            y2d = y_main.reshape(b * NSA_KV_HEADS * t, hw)
            pairs = [(y2d, (tm, hw), (lambda i, h=h: (((i // nt) * NSA_KV_HEADS + h) * nt + i % nt, 0)),
                      w_o16[l, h * hw:(h + 1) * hw]) for h in range(NSA_KV_HEADS)]
            km = NSA_DIM
        else:
            km = y_main.shape[1]
            pairs = [(y_main, (tm, km), lambda i: (i, 0), w_o16[l, :km])]
        return pairs + [(y_mem, (tm, MEM_DIM), lambda i: (i, 0), w_o16[l, km:])]

    for l in range(depth):
        for gi, gr in enumerate(groups):
            x, b, t = gr["x"], gr["b"], gr["t"]
            n = b * t
            nsa_layout = False
            if l < n_a:
                proj = rms_matmul(x, g_mix[l], w_in_a16[l], tn=w_in_a16.shape[2] // 2)
                y_main, new_st = gated_conv(proj, gr["st"][l], conv_w[l], b, t)
                conv_out[gi].append(new_st)
                y_mem = mem_attention(proj, 3 * conv_dim // MEM_DIM, gr["mem"][l], b, t)
            else:
                if l == n_a:
                    kv2d = rms_matmul(x, g_kv, w_kv16, tn=3 * KV_DIM)
                    kv_rows[gi] = kv2d.reshape(b, t, 3, 2, NSA_KV_HEADS, HEAD_DIM)
                    kv = kv_rows[gi]
                    if gi == 0:
                        ck, cv = compress_prompt(kv2d[:, :ROW_W].reshape(b, t // CMP_STRIDE, CHUNK_W), cw)
                        onehot = jnp.where(jnp.arange(t)[:, None] // SEL_BLOCK == jnp.arange(N_SEL_POS)[None, :],
                                           -MASK_BIG, 0.0).astype(BF16)
                        hm = lambda c0: _head_major(kv2d[:, c0:c0 + KV_DIM], b, t).astype(BF16)
                        aug = jnp.concatenate([jnp.broadcast_to(onehot, (b, NSA_KV_HEADS, t, N_SEL_POS)),
                                               jnp.zeros((b, NSA_KV_HEADS, t, N_SEL_POS), BF16)], axis=2)
                        kcat = jnp.concatenate([jnp.concatenate([hm(2 * KV_DIM), hm(4 * KV_DIM)], axis=2), aug],
                                               axis=-1)
                        vcat = jnp.concatenate([hm(3 * KV_DIM), hm(5 * KV_DIM)], axis=2)
                        nsa_in = (_cmp_slots(ck, b), _cmp_slots(cv, b), kcat, vcat)
                    else:
                        new3 = kv2d.reshape(b, t, 6 * KV_DIM)
                        new_chunk = jnp.pad(new3[:, :, :ROW_W], ((0, 0), (0, CMP_STRIDE - t), (0, 0)))
                        ck_s, cv_s = compress_sample(page_table, pool_cmp, new_chunk.reshape(b, 1, CHUNK_W), cw,
                                                     pg=pg)
                j = l - n_a
                if gi == 0:
                    proj = rms_matmul(x, g_mix[l], w_in_b16[j], tn=B_COLS // 2)
                    y_main = nsa_prompt(proj, slopes, *nsa_in, b, t)
                    nsa_layout = True
                    y_mem = mem_attention(proj, B_MQ_COL // MEM_DIM, gr["mem"][l], b, t)
                else:
                    proj = rms_matmul(x, g_mix[l], w_in_bs16[j], tn=S_COLS)
                    y_main = nsa_sample(proj, slopes, page_table, ck_s, cv_s, new3, win_state, pool_slc, pg=pg)
                    y_mem = mem_attention(proj, S_MQ_COL // MEM_DIM, gr["mem"][l], b, t)
            x = proj_residual(x, wo_pairs(l, n, y_main, y_mem, nsa_layout, b, t))
            x = ffn(x, g_ffn[l], w_gu16[l], w_dn16[l])
            gr["x"] = x

    y_prompt = rmsnorm_rows(groups[0]["x"], g_final).reshape(bp, tp, d)
    y_sample = rmsnorm_rows(groups[1]["x"], g_final).reshape(bs, ts, d)
    conv_state_p = jnp.stack(conv_out[0])
    conv_state_s = jnp.stack(conv_out[1])
    mem_kv_out = mem_kv_p.reshape(depth, bp, n_mem, 2, MEM_HEADS, HEAD_DIM)
    kvp, kvs = kv_rows
    win_kv_p = kvp[:, tp - min(WINDOW, tp):, 2]
    win_kv_s = jnp.concatenate([state_win_kv, kvs[:, :, 2]], axis=1)[:, ts:]
    return (y_prompt, y_sample, conv_state_p, conv_state_s, mem_kv_out, kvp[:, :, 0], kvp[:, :, 1], win_kv_p,
            kvs[:, :, 0], kvs[:, :, 1], win_kv_s)
```

```python
import functools
import math

import numpy as np
import jax
import jax.numpy as jnp
from jax import lax
from jax.experimental import pallas as pl
from jax.experimental.pallas import tpu as pltpu

F32 = jnp.float32
BF16 = jnp.bfloat16

HEAD_DIM = 64
MEM_HEADS = 4
MEM_DIM = MEM_HEADS * HEAD_DIM
NSA_KV_HEADS = 4
NSA_GROUP = 3
NSA_HEADS = NSA_KV_HEADS * NSA_GROUP
NSA_DIM = NSA_HEADS * HEAD_DIM
KV_DIM = NSA_KV_HEADS * HEAD_DIM
CONV_WIDTH = 3
CMP_STRIDE = 16
CMP_LEN = 32
SEL_BLOCK = 64
N_SEL = 16
WINDOW = 512
Q_BLOCK = 64
RMS_EPS = 1e-6
NEG_INF = -1e30
FORCE_SCORE = 1e4
ATT_SCALE = HEAD_DIM ** -0.5

VMEM_LIMIT = 48 * 1024 * 1024


def _cparams(*sem):
    return pltpu.CompilerParams(dimension_semantics=sem, vmem_limit_bytes=VMEM_LIMIT)


def _row_tile(n, want):
    t = min(n, want)
    while n % t:
        t //= 2
    return t


def _rms_matmul_kernel(x_ref, g_ref, w_ref, o_ref, h_ref):
    @pl.when(pl.program_id(1) == 0)
    def _():
        x = x_ref[...]
        ms = jnp.mean(x * x, axis=-1, keepdims=True)
        h_ref[...] = (x * lax.rsqrt(ms + RMS_EPS) * g_ref[...]).astype(BF16)

    o_ref[...] = jnp.dot(h_ref[...], w_ref[...], preferred_element_type=F32)


def rms_matmul(x, g, w, *, tm=1024, tn=512):
    n, d = x.shape
    c = w.shape[1]
    tm = _row_tile(n, tm)
    tn = _row_tile(c, tn)
    return pl.pallas_call(
        _rms_matmul_kernel,
        out_shape=jax.ShapeDtypeStruct((n, c), F32),
        grid=(n // tm, c // tn),
        in_specs=[
            pl.BlockSpec((tm, d), lambda i, j: (i, 0)),
            pl.BlockSpec((1, d), lambda i, j: (0, 0)),
            pl.BlockSpec((d, tn), lambda i, j: (0, j)),
        ],
        out_specs=pl.BlockSpec((tm, tn), lambda i, j: (i, j)),
        scratch_shapes=[pltpu.VMEM((tm, d), BF16)],
        compiler_params=_cparams("parallel", "arbitrary"),
        name="rms_matmul",
    )(x, g.reshape(1, d), w)


def _proj_residual_kernel(n_pairs, x_ref, *refs):
    a_refs = refs[:n_pairs]
    w_refs = refs[n_pairs:2 * n_pairs]
    o_ref = refs[2 * n_pairs]
    acc = x_ref[...]
    for a_ref, w_ref in zip(a_refs, w_refs):
        acc = acc + jnp.dot(a_ref[...].astype(BF16), w_ref[...], preferred_element_type=F32)
    o_ref[...] = acc


def proj_residual(x, pairs, *, tm=512):
    n, d = x.shape
    tm = _row_tile(n, tm)
    in_specs = [pl.BlockSpec((tm, d), lambda i: (i, 0))]
    args = [x]
    for a, blk, imap, _ in pairs:
        in_specs.append(pl.BlockSpec(blk, imap))
        args.append(a)
    for _, _, _, w in pairs:
        in_specs.append(pl.BlockSpec(w.shape, lambda i: (0, 0)))
        args.append(w)
    return pl.pallas_call(
        functools.partial(_proj_residual_kernel, len(pairs)),
        out_shape=jax.ShapeDtypeStruct((n, d), F32),
        grid=(n // tm,),
        in_specs=in_specs,
        out_specs=pl.BlockSpec((tm, d), lambda i: (i, 0)),
        compiler_params=_cparams("parallel"),
        name="proj_residual",
    )(*args)


def _ffn_kernel(x_ref, g_ref, wg_ref, wu_ref, wd_ref, o_ref, h_ref, acc_ref):
    f = pl.program_id(1)

    @pl.when(f == 0)
    def _():
        x = x_ref[...]
        ms = jnp.mean(x * x, axis=-1, keepdims=True)
        h_ref[...] = (x * lax.rsqrt(ms + RMS_EPS) * g_ref[...]).astype(BF16)
        acc_ref[...] = x

    h = h_ref[...]
    gate = jnp.dot(h, wg_ref[...], preferred_element_type=F32)
    up = jnp.dot(h, wu_ref[...], preferred_element_type=F32)
    act = (gate * jax.nn.sigmoid(gate) * up).astype(BF16)
    acc_ref[...] += jnp.dot(act, wd_ref[...], preferred_element_type=F32)

    @pl.when(f == pl.num_programs(1) - 1)
    def _():
        o_ref[...] = acc_ref[...]


def ffn(x, g, w_gu, w_dn, *, tm=512, tf=1408):
    n, d = x.shape
    dff = w_dn.shape[0]
    tm = _row_tile(n, tm)
    nf = dff // tf
    return pl.pallas_call(
        _ffn_kernel,
        out_shape=jax.ShapeDtypeStruct((n, d), F32),
        grid=(n // tm, nf),
        in_specs=[
            pl.BlockSpec((tm, d), lambda i, f: (i, 0)),
            pl.BlockSpec((1, d), lambda i, f: (0, 0)),
            pl.BlockSpec((d, tf), lambda i, f: (0, f)),
            pl.BlockSpec((d, tf), lambda i, f: (0, f + nf)),
            pl.BlockSpec((tf, d), lambda i, f: (f, 0)),
        ],
        out_specs=pl.BlockSpec((tm, d), lambda i, f: (i, 0)),
        scratch_shapes=[pltpu.VMEM((tm, d), BF16), pltpu.VMEM((tm, d), F32)],
        compiler_params=_cparams("parallel", "arbitrary"),
        name="ffn",
    )(x, g.reshape(1, d), w_gu, w_gu, w_dn)


def _rmsnorm_kernel(x_ref, g_ref, o_ref):
    x = x_ref[...]
    ms = jnp.mean(x * x, axis=-1, keepdims=True)
    o_ref[...] = x * lax.rsqrt(ms + RMS_EPS) * g_ref[...]


def rmsnorm_rows(x, g, *, tm=1024):
    n, d = x.shape
    tm = _row_tile(n, tm)
    return pl.pallas_call(
        _rmsnorm_kernel,
        out_shape=jax.ShapeDtypeStruct((n, d), F32),
        grid=(n // tm,),
        in_specs=[pl.BlockSpec((tm, d), lambda i: (i, 0)), pl.BlockSpec((1, d), lambda i: (0, 0))],
        out_specs=pl.BlockSpec((tm, d), lambda i: (i, 0)),
        compiler_params=_cparams("parallel"),
        name="rmsnorm",
    )(x, g.reshape(1, d))


def _conv_kernel(b_ref, c_ref, h_ref, cp_ref, hp_ref, st_ref, w_ref, y_ref, ns_ref):
    i = pl.program_id(1)
    u = c_ref[...] * h_ref[...]
    tt = u.shape[0]
    prev = cp_ref[...] * hp_ref[...]
    st = st_ref[0]
    first = i == 0
    p1 = jnp.where(first, st[1:2], prev[7:8])
    p2 = jnp.where(first, st[0:1], prev[6:7])
    row = lax.broadcasted_iota(jnp.int32, u.shape, 0)
    u1 = jnp.where(row == 0, p1, pltpu.roll(u, 1, 0))
    u2 = jnp.where(row == 0, p2, jnp.where(row == 1, p1, pltpu.roll(u, 2, 0)))
    w = w_ref[...]
    y = w[0:1] * u2 + w[1:2] * u1 + w[2:3] * u
    y_ref[...] = b_ref[...] * y
    ns_ref[0] = u[tt - 2:tt]


def gated_conv(proj, state, conv_w, batch, t, *, tt=512):
    c = conv_w.shape[1]
    tt = _row_tile(t, tt)
    nt = t // tt
    r8 = tt // 8

    def prev_map(col):
        return lambda b, i: (jnp.maximum((b * nt + i) * r8 - 1, 0), col)

    return pl.pallas_call(
        _conv_kernel,
        out_shape=(jax.ShapeDtypeStruct((batch * t, c), F32), jax.ShapeDtypeStruct((batch, 2, c), F32)),
        grid=(batch, nt),
        in_specs=[
            pl.BlockSpec((tt, c), lambda b, i: (b * nt + i, 0)),
            pl.BlockSpec((tt, c), lambda b, i: (b * nt + i, 1)),
            pl.BlockSpec((tt, c), lambda b, i: (b * nt + i, 2)),
            pl.BlockSpec((8, c), prev_map(1)),
            pl.BlockSpec((8, c), prev_map(2)),
            pl.BlockSpec((1, 2, c), lambda b, i: (b, 0, 0)),
            pl.BlockSpec((CONV_WIDTH, c), lambda b, i: (0, 0)),
        ],
        out_specs=(
            pl.BlockSpec((tt, c), lambda b, i: (b * nt + i, 0)),
            pl.BlockSpec((1, 2, c), lambda b, i: (b, 0, 0)),
        ),
        compiler_params=_cparams("parallel", "arbitrary"),
        name="gated_conv",
    )(proj, proj, proj, proj, proj, state, conv_w)


def _mem_attn_kernel(q_ref, kv_ref, o_ref):
    q = q_ref[...] * ATT_SCALE
    kv = kv_ref[0]
    outs = []
    for h in range(MEM_HEADS):
        qh = q[:, h * HEAD_DIM:(h + 1) * HEAD_DIM].astype(BF16)
        kh = kv[:, h * HEAD_DIM:(h + 1) * HEAD_DIM].astype(BF16)
        vh = kv[:, MEM_DIM + h * HEAD_DIM:MEM_DIM + (h + 1) * HEAD_DIM].astype(BF16)
        s = lax.dot_general(qh, kh, (((1,), (1,)), ((), ())), preferred_element_type=F32)
        m = jnp.max(s, axis=-1, keepdims=True)
        e = jnp.exp(s - m)
        p = e / jnp.sum(e, axis=-1, keepdims=True)
        outs.append(jnp.dot(p.astype(BF16), vh, preferred_element_type=F32))
    o_ref[...] = jnp.concatenate(outs, axis=-1)


def mem_attention(proj, col_block, mem_kv, batch, t, *, tt=512):
    tt = _row_tile(t, tt)
    nt = t // tt
    n_mem = mem_kv.shape[1]
    return pl.pallas_call(
        _mem_attn_kernel,
        out_shape=jax.ShapeDtypeStruct((batch * t, MEM_DIM), F32),
        grid=(batch, nt),
        in_specs=[
            pl.BlockSpec((tt, MEM_DIM), lambda b, i: (b * nt + i, col_block)),
            pl.BlockSpec((1, n_mem, 2 * MEM_DIM), lambda b, i: (b, 0, 0)),
        ],
        out_specs=pl.BlockSpec((tt, MEM_DIM), lambda b, i: (b * nt + i, 0)),
        compiler_params=_cparams("parallel", "arbitrary"),
        name="mem_attention",
    )(proj, mem_kv)


def _alibi_list(n):
    def pow2(m):
        start = 2.0 ** (-8.0 / m)
        return [start ** (i + 1) for i in range(m)]
    if n & (n - 1) == 0:
        return pow2(n)
    c = 2 ** int(math.floor(math.log2(n)))
    return pow2(c) + _alibi_list(2 * c)[0::2][: n - c]


LANES = 128
PAIR = LANES // HEAD_DIM


def _to_lane_columns(src, dst_ref, n_rows):
    for c in range(dst_ref.shape[0]):
        dst_ref[c, 0:n_rows, :] = src[0:n_rows, c * LANES:(c + 1) * LANES]


def _compress_tile(buf_ref, row0, m, pe_refs, w1_refs, w2_refs):
    outs = []
    for kv in range(2):
        pe = pe_refs[kv][...]
        lhs = []
        for p in range(NSA_KV_HEADS // PAIR):
            col = kv * (KV_DIM // LANES) + p
            pieces = [
                (buf_ref[col, pl.ds(row0 + j, m, stride=CMP_STRIDE), :] + pe[j:j + 1]).astype(BF16)
                for j in range(CMP_LEN)
            ]
            lhs.append(jnp.concatenate(pieces, axis=1))
        lhs = jnp.concatenate(lhs, axis=0)
        hid = jnp.dot(lhs, w1_refs[kv][...], preferred_element_type=F32)
        act = (hid * jax.nn.sigmoid(hid)).astype(BF16)
        out = jnp.dot(act, w2_refs[kv][...], preferred_element_type=F32)
        outs.append(jnp.concatenate([out[0:m], out[m:2 * m]], axis=1))
    return outs


def _compress_prompt_kernel(x_ref, pek_ref, pev_ref, w1k_ref, w1v_ref, w2k_ref, w2v_ref, ck_ref, cv_ref, buf_ref):
    t = x_ref.shape[0]
    _to_lane_columns(x_ref, buf_ref, t)
    buf_ref[:, t:t + CMP_STRIDE, :] = jnp.zeros((buf_ref.shape[0], CMP_STRIDE, LANES), F32)
    n_blk = t // CMP_STRIDE
    m = min(n_blk, 128)
    for s in range(n_blk // m):
        ck, cv = _compress_tile(buf_ref, s * m * CMP_STRIDE, m, (pek_ref, pev_ref), (w1k_ref, w1v_ref),
                                (w2k_ref, w2v_ref))
        ck_ref[0, s * m:(s + 1) * m, :] = ck
        cv_ref[0, s * m:(s + 1) * m, :] = cv


def _compress_weights(pe, w1, w2):
    eye = jnp.eye(PAIR, dtype=F32)
    w1p = jnp.einsum('jde,qr->jqdre', w1, eye).reshape(CMP_LEN * PAIR * HEAD_DIM, PAIR * w1.shape[2])
    w2p = jnp.einsum('ed,qr->qerd', w2, eye).reshape(PAIR * w2.shape[0], PAIR * HEAD_DIM)
    pe2 = jnp.tile(pe, (1, PAIR))
    return pe2, w1p.astype(BF16), w2p.astype(BF16)


def compress_prompt(kv_rows, batch, t, cw):
    pek, w1k, w2k, pev, w1v, w2v = cw
    n_blk = t // CMP_STRIDE
    full = lambda a: pl.BlockSpec(a.shape, lambda b: (0,) * a.ndim)
    return pl.pallas_call(
        _compress_prompt_kernel,
        out_shape=(jax.ShapeDtypeStruct((batch, n_blk, KV_DIM), F32),) * 2,
        grid=(batch,),
        in_specs=[pl.BlockSpec((t, 2 * KV_DIM), lambda b: (b, 0)), full(pek), full(pev), full(w1k), full(w1v),
                  full(w2k), full(w2v)],
        out_specs=(pl.BlockSpec((1, n_blk, KV_DIM), lambda b: (b, 0, 0)),) * 2,
        scratch_shapes=[pltpu.VMEM((2 * KV_DIM // LANES, t + CMP_STRIDE, LANES), F32)],
        compiler_params=_cparams("parallel"),
        name="compress_prompt",
    )(kv_rows, pek, pev, w1k, w1v, w2k, w2v)


N_CMP_POS = 256
N_SEL_POS = N_CMP_POS * CMP_STRIDE // SEL_BLOCK
MASK_BIG = 2.0 ** 100
M_INIT = -1e38


LOG2E = 1.4426950408889634
MASK_NONE, MASK_CAUSAL, MASK_BAND = 0, 1, 2


def _topk_mask(score):
    tq, nb = score.shape
    pad = jnp.full((tq, LANES - nb), -jnp.inf, F32)
    st = jnp.concatenate([score, pad], axis=1).T[:nb]
    sub = 8
    idx = lax.broadcasted_iota(jnp.int32, (sub, tq), 0)
    groups = [st[r:r + sub] for r in range(0, nb, sub)]
    ranks = [jnp.zeros((sub, tq), F32) for _ in groups]
    for j in range(nb):
        rj = st[j:j + 1, :]
        for gi, sg in enumerate(groups):
            if gi * sub > j:
                before = rj >= sg
            elif gi * sub + sub - 1 < j:
                before = rj > sg
            else:
                before = jnp.where(idx > j - gi * sub, (rj >= sg).astype(F32), (rj > sg).astype(F32)) > 0.0
            ranks[gi] = jnp.where(before, ranks[gi] + 1.0, ranks[gi])
    sel_t = (jnp.concatenate(ranks, axis=0) < N_SEL).astype(F32)
    blk_any = jnp.max(sel_t, axis=1, keepdims=True)
    sel_t = jnp.concatenate([sel_t, jnp.zeros((LANES - nb, tq), F32)], axis=0)
    return sel_t.T[:, :nb], blk_any


def _chunk_words(blk_any, per_chunk):
    nb = blk_any.shape[0]
    per_word = 8 * per_chunk
    bidx = lax.broadcasted_iota(jnp.int32, (nb, 1), 0)
    wgt = jnp.left_shift(1, 3 * ((bidx % per_word) // per_chunk)).astype(F32)
    words = []
    for w in range(nb // per_word):
        v = jnp.sum(jnp.where(bidx // per_word == w, blk_any * wgt, 0.0), axis=0, keepdims=True)
        words.append(v.astype(jnp.int32)[0, 0])
    return words


def _nsa_prompt_kernel(sl_ref, q_ref, gt_ref, ck_ref, cv_ref, kc_ref, vc_ref, o_ref,
                       q_scr, s0_scr, s1_scr, p0_scr, p1_scr, a0_scr, a1_scr, m_scr, l_scr, acc_scr, mb_scr, lst_ref,
                       *, mm_rows, sm_rows):
    kvh = pl.program_id(1)
    i = pl.program_id(2)
    s_scrs, p_scrs, a_scrs = (s0_scr, s1_scr), (p0_scr, p1_scr), (a0_scr, a1_scr)
    tq = q_ref.shape[0]
    kb = tq
    t_len = kc_ref.shape[2] // 2
    t0 = i * tq
    slopes = [sl_ref[kvh * NSA_GROUP + g] * LOG2E for g in range(NSA_GROUP)]
    q = q_ref[...] * (ATT_SCALE * LOG2E)
    q3 = jnp.concatenate([q[:, g * HEAD_DIM:(g + 1) * HEAD_DIM] for g in range(NSA_GROUP)], axis=0)
    q3b = q3.astype(BF16)
    nt_dims = (((1,), (1,)), ((), ()))
    t_col = t0 + lax.broadcasted_iota(jnp.int32, (tq, 1), 0)

    @pl.when(i == 0)
    def _():
        r = lax.broadcasted_iota(jnp.int32, (tq, kb), 0)
        c = lax.broadcasted_iota(jnp.int32, (tq, kb), 1)
        mb_scr[MASK_NONE] = jnp.zeros((tq, kb), F32)
        mb_scr[MASK_CAUSAL] = jnp.where(c <= r, 0.0, -MASK_BIG)
        mb_scr[MASK_BAND] = jnp.where(c >= r, 0.0, -MASK_BIG)

    m_scr[...] = jnp.full(m_scr.shape, M_INIT, F32)
    l_scr[...] = jnp.zeros(l_scr.shape, F32)
    acc_scr[...] = jnp.zeros(acc_scr.shape, F32)

    pos = lax.broadcasted_iota(jnp.int32, (1, N_CMP_POS), 1)
    blk_n = (pos % N_SEL_POS) * (N_CMP_POS // N_SEL_POS) + pos // N_SEL_POS
    c_end = blk_n * CMP_STRIDE + (CMP_LEN - 1)
    n_real = t_len // CMP_STRIDE - 1
    vis_c = (c_end <= t_col) & (blk_n < n_real)
    c_end_f = c_end.astype(F32)
    s_c = lax.dot_general(q3b, ck_ref[0, 0], nt_dims, preferred_element_type=F32)
    p_c = []
    for g in range(NSA_GROUP):
        sg = jnp.where(vis_c, s_c[g * tq:(g + 1) * tq] + slopes[g] * c_end_f, NEG_INF)
        mg = jnp.max(sg, axis=1, keepdims=True)
        eg = jnp.where(vis_c, jnp.exp2(sg - mg), 0.0)
        dg = jnp.sum(eg, axis=1, keepdims=True)
        p_c.append(eg / jnp.where(dg > 0.0, dg, 1.0))
    o_c = jnp.dot(jnp.concatenate(p_c, axis=0).astype(BF16), cv_ref[0, 0], preferred_element_type=F32)
    imp = p_c[0] + p_c[1] + p_c[2]
    imp = imp[:, :LANES] + imp[:, LANES:]
    imp = imp[:, :N_SEL_POS] + imp[:, N_SEL_POS:]

    blk = lax.broadcasted_iota(jnp.int32, (1, N_SEL_POS), 1)
    cur = t_col // SEL_BLOCK
    forced = (blk == 0) | (blk == cur) | (blk == cur - 1)
    score = jnp.where(forced, FORCE_SCORE, jnp.where(blk <= cur, imp, -jnp.inf))
    sel, blk_any = _topk_mask(score)
    notsel = (1.0 - sel).astype(BF16)
    q_scr[...] = jnp.concatenate([q3b, jnp.concatenate([notsel] * NSA_GROUP, axis=0)], axis=1)

    lane_k = lax.broadcasted_iota(jnp.int32, (1, kb), 1)
    n_win = WINDOW // kb + 1

    words = _chunk_words(blk_any, kb // SEL_BLOCK)
    n_sel = jnp.int32(0)
    for c in range(N_SEL_POS * SEL_BLOCK // kb):
        active = (((words[c // 8] >> (3 * (c % 8))) & 7) != 0) & (c <= i)
        lst_ref[n_sel] = c
        n_sel = n_sel + active.astype(jnp.int32)
    n_chunks = n_sel + n_win

    def chunk(p):
        is_sel = p < n_sel
        is_pad = p >= n_chunks
        c = lst_ref[jnp.minimum(p, n_sel - 1)]
        w = p - n_sel
        kpos0 = jnp.where(is_sel, c * kb, jnp.where(is_pad, -kb, t0 - WINDOW + w * kb))
        row0 = jnp.maximum(kpos0, 0) + jnp.where(is_sel | is_pad, 0, t_len)
        mtype = jnp.where(is_sel, jnp.where(c == i, MASK_CAUSAL, MASK_NONE),
                          jnp.where(w == 0, MASK_BAND, jnp.where(w == n_win - 1, MASK_CAUSAL, MASK_NONE)))
        return pl.multiple_of(row0, kb), kpos0, mtype, jnp.where(is_sel | is_pad, 0, 1)

    def stage_logits(c, par):
        s_scr = s_scrs[par]
        row0, kpos0, mtype, _ = chunk(c)
        k = kc_ref[0, 0, pl.ds(row0, kb), :]
        kpos_f = (kpos0 + lane_k).astype(F32)
        off = jnp.where(kpos0 >= 0, 0.0, -MASK_BIG)
        for g in range(NSA_GROUP):
            bias = slopes[g] * kpos_f + off
            for r in range(0, tq, mm_rows):
                rows = slice(g * tq + r, g * tq + r + mm_rows)
                s = lax.dot_general(q_scr[rows, :], k, nt_dims, preferred_element_type=F32)
                s_scr[rows, :] = (s + mb_scr[mtype, r:r + mm_rows, :]) + bias

    def stage_softmax(c, par):
        s_scr, p_scr, a_scr = s_scrs[par], p_scrs[par], a_scrs[par]
        st = chunk(c)[3]
        for r in range(0, NSA_GROUP * tq, sm_rows):
            rows = slice(r, r + sm_rows)
            s = s_scr[rows, :]
            m_prev = m_scr[st, rows, :]
            m_new = jnp.maximum(m_prev, jnp.max(s, axis=1, keepdims=True))
            alpha = jnp.exp2(m_prev - m_new)
            p = jnp.exp2(s - jnp.concatenate([m_new] * (kb // LANES), axis=1))
            l_scr[st, rows, :] = alpha * l_scr[st, rows, :] + jnp.sum(p, axis=1, keepdims=True)
            p_scr[rows, :] = p.astype(BF16)
            a_scr[rows, :] = alpha
            m_scr[st, rows, :] = m_new

    def stage_values(c, par):
        p_scr, a_scr = p_scrs[par], a_scrs[par]
        row0, _, _, st = chunk(c)
        v = vc_ref[0, 0, pl.ds(row0, kb), :]
        for r in range(0, NSA_GROUP * tq, mm_rows):
            rows = slice(r, r + mm_rows)
            pv = jnp.dot(p_scr[rows, :], v, preferred_element_type=F32)
            acc_scr[st, rows, :] = acc_scr[st, rows, :] * a_scr[rows, :HEAD_DIM] + pv

    stage_logits(0, 0)
    stage_softmax(0, 0)
    stage_logits(1, 1)

    def pipe_body(j, carry):
        c = 2 * j
        stage_values(c - 2, 0)
        stage_softmax(c - 1, 1)
        stage_logits(c, 0)
        stage_values(c - 1, 1)
        stage_softmax(c, 0)
        stage_logits(c + 1, 1)
        return carry

    n_even = n_chunks + n_chunks % 2
    lax.fori_loop(1, n_even // 2, pipe_body, 0)
    stage_values(n_even - 2, 0)
    stage_softmax(n_even - 1, 1)
    stage_values(n_even - 1, 1)
    o_s = acc_scr[0] / l_scr[0][:, :HEAD_DIM]
    o_w = acc_scr[1] / l_scr[1][:, :HEAD_DIM]

    gates = jax.nn.sigmoid(gt_ref[...])
    outs = []
    for g in range(NSA_GROUP):
        rows = slice(g * tq, (g + 1) * tq)
        outs.append(gates[:, 3 * g:3 * g + 1] * o_c[rows] + gates[:, 3 * g + 1:3 * g + 2] * o_s[rows]
                    + gates[:, 3 * g + 2:3 * g + 3] * o_w[rows])
    o_ref[0, 0] = jnp.concatenate(outs, axis=1)


def nsa_prompt(proj, slopes, ckp, cvp, kcat, vcat, batch, t, *, tq=256, mm_rows=256, sm_rows=256):
    assert t % tq == 0 and WINDOW % tq == 0 and WINDOW >= tq
    nt = t // tq
    rows = NSA_GROUP * tq
    gcol = B_GATE_COL // LANES
    grid_spec = pltpu.PrefetchScalarGridSpec(
        num_scalar_prefetch=1,
        grid=(batch, NSA_KV_HEADS, nt),
        in_specs=[
            pl.BlockSpec((tq, Q_SLOT), lambda b, h, i, sl: (b * nt + i, h)),
            pl.BlockSpec((tq, LANES), lambda b, h, i, sl: (b * nt + i, gcol + h)),
            pl.BlockSpec((1, 1, N_CMP_POS, HEAD_DIM), lambda b, h, i, sl: (b, h, 0, 0)),
            pl.BlockSpec((1, 1, N_CMP_POS, HEAD_DIM), lambda b, h, i, sl: (b, h, 0, 0)),
            pl.BlockSpec((1, 1, 2 * t, LANES), lambda b, h, i, sl: (b, h, 0, 0)),
            pl.BlockSpec((1, 1, 2 * t, HEAD_DIM), lambda b, h, i, sl: (b, h, 0, 0)),
        ],
        out_specs=pl.BlockSpec((1, 1, tq, NSA_GROUP * HEAD_DIM), lambda b, h, i, sl: (b, h, i, 0)),
        scratch_shapes=[
            pltpu.VMEM((rows, LANES), BF16),
            pltpu.VMEM((rows, tq), F32), pltpu.VMEM((rows, tq), F32),
            pltpu.VMEM((rows, tq), BF16), pltpu.VMEM((rows, tq), BF16),
            pltpu.VMEM((rows, LANES), F32), pltpu.VMEM((rows, LANES), F32),
            pltpu.VMEM((2, rows, LANES), F32),
            pltpu.VMEM((2, rows, LANES), F32),
            pltpu.VMEM((2, rows, HEAD_DIM), F32),
            pltpu.VMEM((3, tq, tq), F32),
            pltpu.SMEM((N_SEL_POS * SEL_BLOCK // tq,), jnp.int32),
        ],
    )
    return pl.pallas_call(
        functools.partial(_nsa_prompt_kernel, mm_rows=mm_rows, sm_rows=sm_rows),
        out_shape=jax.ShapeDtypeStruct((batch, NSA_KV_HEADS, t, NSA_GROUP * HEAD_DIM), F32),
        grid_spec=grid_spec,
        compiler_params=_cparams("parallel", "parallel", "arbitrary"),
        name="nsa_prompt",
    )(slopes, proj, proj, ckp, cvp, kcat, vcat)


def _kv_layout_kernel(x_ref, kc_ref, vc_ref):
    tm = x_ref.shape[0]
    pos = pl.program_id(1) * tm + lax.broadcasted_iota(jnp.int32, (tm, N_SEL_POS), 0)
    blk = lax.broadcasted_iota(jnp.int32, (tm, N_SEL_POS), 1)
    onehot = jnp.where(pos // SEL_BLOCK == blk, -MASK_BIG, 0.0).astype(BF16)
    zeros = jnp.zeros((tm, N_SEL_POS), BF16)
    for h in range(NSA_KV_HEADS):
        head = lambda c0: x_ref[:, c0 + h * HEAD_DIM:c0 + (h + 1) * HEAD_DIM].astype(BF16)
        kc_ref[0, h, 0] = jnp.concatenate([head(2 * KV_DIM), onehot], axis=1)
        kc_ref[0, h, 1] = jnp.concatenate([head(4 * KV_DIM), zeros], axis=1)
        vc_ref[0, h, 0] = head(3 * KV_DIM)
        vc_ref[0, h, 1] = head(5 * KV_DIM)


def kv_layout(kv_rows, batch, t, *, tm=512):
    tm = _row_tile(t, tm)
    nt = t // tm
    kc, vc = pl.pallas_call(
        _kv_layout_kernel,
        out_shape=(jax.ShapeDtypeStruct((batch, NSA_KV_HEADS, 2, t, LANES), BF16),
                   jax.ShapeDtypeStruct((batch, NSA_KV_HEADS, 2, t, HEAD_DIM), BF16)),
        grid=(batch, nt),
        in_specs=[pl.BlockSpec((tm, kv_rows.shape[1]), lambda b, i: (b * nt + i, 0))],
        out_specs=(pl.BlockSpec((1, NSA_KV_HEADS, 2, tm, LANES), lambda b, i: (b, 0, 0, i, 0)),
                   pl.BlockSpec((1, NSA_KV_HEADS, 2, tm, HEAD_DIM), lambda b, i: (b, 0, 0, i, 0))),
        compiler_params=_cparams("parallel", "parallel"),
        name="kv_layout",
    )(kv_rows)
    return (kc.reshape(batch, NSA_KV_HEADS, 2 * t, LANES), vc.reshape(batch, NSA_KV_HEADS, 2 * t, HEAD_DIM))


def _cmp_slots(c, batch):
    n_blk = c.shape[1]
    per = N_CMP_POS // N_SEL_POS
    c = jnp.pad(c, ((0, 0), (0, N_CMP_POS - n_blk), (0, 0)))
    c = c.reshape(batch, N_SEL_POS, per, NSA_KV_HEADS, HEAD_DIM).transpose(0, 3, 2, 1, 4)
    return c.reshape(batch, NSA_KV_HEADS, N_CMP_POS, HEAD_DIM).astype(BF16)


def _page_group_dma(pt_ref, pool_ref, buf_ref, sem_ref, n_pages, pg, page_rows):
    ng = n_pages // pg

    def run(step, slot, go):
        bb = step // ng
        gg = step % ng
        for k in range(pg):
            page = pt_ref[bb * n_pages + gg * pg + k]
            cp = pltpu.make_async_copy(pool_ref.at[page], buf_ref.at[slot, pl.ds(k * page_rows, page_rows)],
                                       sem_ref.at[slot])
            cp.start() if go == "start" else cp.wait()

    return run


def _compress_sample_kernel(pt_ref, pool_ref, new_ref, pek_ref, pev_ref, w1k_ref, w1v_ref, w2k_ref, w2v_ref,
                            ck_ref, cv_ref, buf_ref, sem_ref, col_ref, *, n_pages, pg, page_rows, ts):
    b = pl.program_id(0)
    g = pl.program_id(1)
    ng = n_pages // pg
    step = b * ng + g
    n_steps = pl.num_programs(0) * ng
    slot = step % 2
    rows = pg * page_rows
    pages = _page_group_dma(pt_ref, pool_ref, buf_ref, sem_ref, n_pages, pg, page_rows)

    def halo(st, sl, go):
        bb = st // ng
        gg = st % ng

        @pl.when(gg < ng - 1)
        def _():
            page = pt_ref[bb * n_pages + (gg + 1) * pg]
            cp = pltpu.make_async_copy(pool_ref.at[page, pl.ds(0, CMP_STRIDE)],
                                       buf_ref.at[sl, pl.ds(rows, CMP_STRIDE)], sem_ref.at[sl])
            cp.start() if go == "start" else cp.wait()

        @pl.when(gg == ng - 1)
        def _():
            cp = pltpu.make_async_copy(new_ref.at[bb, :, pl.ds(0, 2 * KV_DIM)],
                                       buf_ref.at[sl, pl.ds(rows, ts)], sem_ref.at[sl])
            cp.start() if go == "start" else cp.wait()

    @pl.when(step == 0)
    def _():
        pages(0, 0, "start")
        halo(0, 0, "start")

    @pl.when(step + 1 < n_steps)
    def _():
        pages(step + 1, 1 - slot, "start")
        halo(step + 1, 1 - slot, "start")

    pages(step, slot, "wait")
    halo(step, slot, "wait")

    @pl.when(g == ng - 1)
    def _():
        buf_ref[slot, rows + ts:rows + CMP_STRIDE, :] = jnp.zeros((CMP_STRIDE - ts, buf_ref.shape[2]), F32)

    m = rows // CMP_STRIDE
    _to_lane_columns(buf_ref.at[slot], col_ref, rows + CMP_STRIDE)
    ck, cv = _compress_tile(col_ref, 0, m, (pek_ref, pev_ref), (w1k_ref, w1v_ref), (w2k_ref, w2v_ref))
    ck_ref[0] = ck
    cv_ref[0] = cv


def compress_sample(page_table, pool, new_rows, cw, *, pg):
    pek, w1k, w2k, pev, w1v, w2v = cw
    batch, n_pages = page_table.shape
    page_rows = pool.shape[1]
    ts = new_rows.shape[1]
    ng = n_pages // pg
    m = pg * page_rows // CMP_STRIDE
    full = lambda a: pl.BlockSpec(a.shape, lambda b, g, pt: (0,) * a.ndim)
    grid_spec = pltpu.PrefetchScalarGridSpec(
        num_scalar_prefetch=1,
        grid=(batch, ng),
        in_specs=[pl.BlockSpec(memory_space=pl.ANY), pl.BlockSpec(memory_space=pl.ANY), full(pek), full(pev),
                  full(w1k), full(w1v), full(w2k), full(w2v)],
        out_specs=(pl.BlockSpec((1, m, KV_DIM), lambda b, g, pt: (b, g, 0)),) * 2,
        scratch_shapes=[pltpu.VMEM((2, pg * page_rows + CMP_STRIDE, 2 * KV_DIM), F32), pltpu.SemaphoreType.DMA((2,)),
                        pltpu.VMEM((2 * KV_DIM // LANES, pg * page_rows + CMP_STRIDE, LANES), F32)],
    )
    return pl.pallas_call(
        functools.partial(_compress_sample_kernel, n_pages=n_pages, pg=pg, page_rows=page_rows, ts=ts),
        out_shape=(jax.ShapeDtypeStruct((batch, ng * m, KV_DIM), F32),) * 2,
        grid_spec=grid_spec,
        compiler_params=_cparams("arbitrary", "arbitrary"),
        name="compress_sample",
    )(page_table.reshape(-1), pool, new_rows, pek, pev, w1k, w1v, w2k, w2v)


S_COL_G = 32


def _softmax_rows0(s, vis):
    s = jnp.where(vis, s, NEG_INF)
    m = jnp.max(s, axis=0, keepdims=True)
    e = jnp.where(vis, jnp.exp(s - m), 0.0)
    d = jnp.sum(e, axis=0, keepdims=True)
    return e / jnp.where(d > 0.0, d, 1.0)


def _nsa_sample_kernel(sl_ref, pt_ref, proj_ref, ck_ref, cv_ref, new_ref, win_ref, pool_ref, y_ref,
                       buf_ref, sem_ref, s_ref, v_ref, bdq_ref, oc_ref, ow_ref, sc_ref, sel_ref, imp_ref,
                       *, n_pages, pg, page_rows, past):
    b = pl.program_id(0)
    g = pl.program_id(1)
    ng = n_pages // pg
    step = b * ng + g
    n_steps = pl.num_programs(0) * ng
    slot = step % 2
    rows = pg * page_rows
    ts = proj_ref.shape[0]
    nt_dims = (((1,), (1,)), ((), ()))
    pages = _page_group_dma(pt_ref, pool_ref, buf_ref, sem_ref, n_pages, pg, page_rows)

    @pl.when(step == 0)
    def _():
        pages(0, 0, "start")

    @pl.when(step + 1 < n_steps)
    def _():
        pages(step + 1, 1 - slot, "start")

    col = lax.broadcasted_iota(jnp.int32, (1, LANES), 1)
    col_g = col // S_COL_G
    col_h = (col % S_COL_G) // ts
    t_row = past + col % ts
    slope_row = jnp.zeros((1, LANES), F32)
    for h in range(NSA_HEADS):
        slope_row = jnp.where((col_h == h // NSA_GROUP) & (col_g == h % NSA_GROUP), sl_ref[h], slope_row)
    n_sb = past // SEL_BLOCK + 1

    @pl.when(g == 0)
    def _():
        lane_h = lax.broadcasted_iota(jnp.int32, (ts, KV_DIM), 1) // HEAD_DIM
        tiles = []
        for gg in range(NSA_GROUP):
            qg = proj_ref[:, gg * KV_DIM:(gg + 1) * KV_DIM] * ATT_SCALE
            for h in range(NSA_KV_HEADS):
                tiles.append(jnp.where(lane_h == h, qg, 0.0))
        tiles.append(jnp.zeros((LANES - NSA_HEADS * ts, KV_DIM), F32))
        bdq = jnp.concatenate(tiles, axis=0).astype(BF16)
        bdq_ref[...] = bdq

        n_c = ck_ref.shape[1]
        s_c = lax.dot_general(ck_ref[0].astype(BF16), bdq, nt_dims, preferred_element_type=F32)
        c_end = lax.broadcasted_iota(jnp.int32, (n_c, 1), 0) * CMP_STRIDE + (CMP_LEN - 1)
        dist = t_row - c_end
        p_c = _softmax_rows0(s_c - slope_row * dist.astype(F32), dist >= 0)
        oc_ref[...] = jnp.dot(p_c.T.astype(BF16), cv_ref[0].astype(BF16), preferred_element_type=F32)
        pz = jnp.where(col < NSA_GROUP * S_COL_G, p_c, 0.0)
        imp_ref[...] = pz + pltpu.roll(pz, S_COL_G, 1) + pltpu.roll(pz, 2 * S_COL_G, 1) + pltpu.roll(pz, 3 * S_COL_G, 1)
        per = SEL_BLOCK // CMP_STRIDE
        n_imp = n_c // per
        imp = imp_ref[pl.ds(0, n_imp, stride=per), :]
        for r in range(1, per):
            imp = imp + imp_ref[pl.ds(r, n_imp, stride=per), :]
        n_pad = sc_ref.shape[0]
        imp = jnp.concatenate([imp, jnp.zeros((n_pad - n_imp, LANES), F32)], axis=0)
        blk = lax.broadcasted_iota(jnp.int32, (n_pad, 1), 0)
        cur = t_row // SEL_BLOCK
        forced = (blk == 0) | (blk == cur) | (blk == cur - 1)
        sc_ref[...] = jnp.where(forced, FORCE_SCORE, jnp.where((blk <= cur) & (blk < n_sb), imp, -jnp.inf))
        st = sc_ref[...]

        def rank_body(j, rank):
            rj = sc_ref[pl.ds(j, 1), :]
            return rank + jnp.where(blk > j, (rj >= st).astype(F32), (rj > st).astype(F32))

        rank = lax.fori_loop(0, n_sb, rank_body, jnp.zeros(st.shape, F32))
        sel_ref[...] = (rank < N_SEL).astype(F32)

        wl = win_ref.shape[1]
        zpad = jnp.zeros((LANES - ts, KV_DIM), F32)
        kw = jnp.concatenate([win_ref[0, :, 0:KV_DIM], new_ref[0, :, 4 * KV_DIM:5 * KV_DIM], zpad], axis=0)
        vw = jnp.concatenate([win_ref[0, :, KV_DIM:2 * KV_DIM], new_ref[0, :, 5 * KV_DIM:6 * KV_DIM], zpad], axis=0)
        s_w = lax.dot_general(kw.astype(BF16), bdq, nt_dims, preferred_element_type=F32)
        w_pos = past - wl + lax.broadcasted_iota(jnp.int32, (wl + LANES, 1), 0)
        dist_w = t_row - w_pos
        p_w = _softmax_rows0(s_w - slope_row * dist_w.astype(F32),
                             (dist_w >= 0) & (dist_w <= WINDOW) & (w_pos >= 0))
        ow_ref[...] = jnp.dot(p_w.T.astype(BF16), vw.astype(BF16), preferred_element_type=F32)

    pages(step, slot, "wait")
    bdq = bdq_ref[...]
    base = g * rows
    kpage = buf_ref[slot, :, 0:KV_DIM].astype(BF16)
    s_g = lax.dot_general(kpage, bdq, nt_dims, preferred_element_type=F32)
    kpos = base + lax.broadcasted_iota(jnp.int32, (rows, 1), 0)
    s_g = s_g - slope_row * (t_row - kpos).astype(F32)
    row0 = pl.multiple_of(base, rows)
    for bl in range(rows // SEL_BLOCK):
        on = sel_ref[pl.ds(g * (rows // SEL_BLOCK) + bl, 1), :] > 0.0
        s_ref[pl.ds(row0 + bl * SEL_BLOCK, SEL_BLOCK), :] = jnp.where(
            on, s_g[bl * SEL_BLOCK:(bl + 1) * SEL_BLOCK], NEG_INF)
    v_ref[pl.ds(row0, rows), :] = buf_ref[slot, :, KV_DIM:2 * KV_DIM].astype(BF16)

    @pl.when(g == ng - 1)
    def _():
        tail = s_ref.shape[0] - past
        zpad = jnp.zeros((tail - ts, KV_DIM), F32)
        k_new = jnp.concatenate([new_ref[0, :, 2 * KV_DIM:3 * KV_DIM], zpad], axis=0)
        v_new = jnp.concatenate([new_ref[0, :, 3 * KV_DIM:4 * KV_DIM], zpad], axis=0)
        s_n = lax.dot_general(k_new.astype(BF16), bdq, nt_dims, preferred_element_type=F32)
        kp = past + lax.broadcasted_iota(jnp.int32, (tail, 1), 0)
        dist_n = t_row - kp
        on = (sel_ref[pl.ds(n_sb - 1, 1), :] > 0.0) & (dist_n >= 0)
        s_ref[pl.ds(past, tail), :] = jnp.where(on, s_n - slope_row * dist_n.astype(F32), NEG_INF)
        v_ref[pl.ds(past, tail), :] = v_new.astype(BF16)
        nk = s_ref.shape[0]
        m = jnp.max(s_ref[...], axis=0, keepdims=True)
        ch = rows
        l = jnp.zeros((1, LANES), F32)
        o_s = jnp.zeros((LANES, KV_DIM), F32)
        for c0 in range(0, nk, ch):
            cl = min(ch, nk - c0)
            e = jnp.exp(s_ref[c0:c0 + cl, :] - m)
            l = l + jnp.sum(e, axis=0, keepdims=True)
            if cl % LANES:
                e = jnp.concatenate([e, jnp.zeros((LANES - cl % LANES, LANES), F32)], axis=0)
                vv = jnp.concatenate([v_ref[c0:c0 + cl, :], jnp.zeros((LANES - cl % LANES, KV_DIM), BF16)], axis=0)
            else:
                vv = v_ref[c0:c0 + cl, :]
            o_s = o_s + jnp.dot(e.T.astype(BF16), vv, preferred_element_type=F32)
        inv_l = (1.0 / jnp.broadcast_to(l, (8, LANES))).T[:, 0:1]
        o_s = o_s * inv_l
        o_c = oc_ref[...]
        o_w = ow_ref[...]
        gates = jax.nn.sigmoid(proj_ref[:, (NSA_GROUP + 1) * KV_DIM:(NSA_GROUP + 1) * KV_DIM + LANES])
        outs = []
        for h in range(NSA_HEADS):
            kvh, gg = h // NSA_GROUP, h % NSA_GROUP
            r0 = gg * S_COL_G + kvh * ts
            lanes = slice(kvh * HEAD_DIM, (kvh + 1) * HEAD_DIM)
            outs.append(gates[:, 3 * h:3 * h + 1] * o_c[r0:r0 + ts, lanes]
                        + gates[:, 3 * h + 1:3 * h + 2] * o_s[r0:r0 + ts, lanes]
                        + gates[:, 3 * h + 2:3 * h + 3] * o_w[r0:r0 + ts, lanes])
        y_ref[...] = jnp.concatenate(outs, axis=1)


def nsa_sample(proj, slopes, page_table, ck, cv, new_rows, win_state, pool, *, pg):
    batch, n_pages = page_table.shape
    page_rows = pool.shape[1]
    ts = new_rows.shape[1]
    past = n_pages * page_rows
    ng = n_pages // pg
    n_c = ck.shape[1]
    wl = win_state.shape[1]
    n_sb_pad = -(-(past // SEL_BLOCK + 1) // 8) * 8
    nk = past + 16
    grid_spec = pltpu.PrefetchScalarGridSpec(
        num_scalar_prefetch=2,
        grid=(batch, ng),
        in_specs=[
            pl.BlockSpec((ts, proj.shape[1]), lambda b, g, sl, pt: (b, 0)),
            pl.BlockSpec((1, n_c, KV_DIM), lambda b, g, sl, pt: (b, 0, 0)),
            pl.BlockSpec((1, n_c, KV_DIM), lambda b, g, sl, pt: (b, 0, 0)),
            pl.BlockSpec((1, ts, new_rows.shape[2]), lambda b, g, sl, pt: (b, 0, 0)),
            pl.BlockSpec((1, wl, 2 * KV_DIM), lambda b, g, sl, pt: (b, 0, 0)),
            pl.BlockSpec(memory_space=pl.ANY),
        ],
        out_specs=pl.BlockSpec((ts, NSA_DIM), lambda b, g, sl, pt: (b, 0)),
        scratch_shapes=[
            pltpu.VMEM((2, pg * page_rows, 2 * KV_DIM), F32), pltpu.SemaphoreType.DMA((2,)),
            pltpu.VMEM((nk, LANES), F32), pltpu.VMEM((nk, KV_DIM), BF16), pltpu.VMEM((LANES, KV_DIM), BF16),
            pltpu.VMEM((LANES, KV_DIM), F32), pltpu.VMEM((LANES, KV_DIM), F32),
            pltpu.VMEM((n_sb_pad, LANES), F32), pltpu.VMEM((n_sb_pad, LANES), F32), pltpu.VMEM((n_c, LANES), F32),
        ],
    )
    return pl.pallas_call(
        functools.partial(_nsa_sample_kernel, n_pages=n_pages, pg=pg, page_rows=page_rows, past=past),
        out_shape=jax.ShapeDtypeStruct((batch * ts, NSA_DIM), F32),
        grid_spec=grid_spec,
        compiler_params=_cparams("arbitrary", "arbitrary"),
        name="nsa_sample",
    )(slopes, page_table.reshape(-1), proj, ck, cv, new_rows, win_state, pool)


S_MQ_COL = NSA_GROUP * KV_DIM
S_COLS = S_MQ_COL + MEM_DIM + LANES


def _layout_w_in_b_sample(w):
    d = w.shape[0]
    qw = w[:, :NSA_DIM].reshape(d, NSA_KV_HEADS, NSA_GROUP, HEAD_DIM).transpose(0, 2, 1, 3).reshape(d, NSA_DIM)
    gw = jnp.pad(w[:, NSA_DIM:NSA_DIM + 3 * NSA_HEADS], ((0, 0), (0, LANES - 3 * NSA_HEADS)))
    mw = w[:, NSA_DIM + 3 * NSA_HEADS:]
    return jnp.concatenate([qw, mw, gw], axis=1).astype(BF16)


Q_SLOT = 256
B_MQ_COL = NSA_KV_HEADS * Q_SLOT
B_GATE_COL = B_MQ_COL + MEM_DIM
B_COLS = B_GATE_COL + NSA_KV_HEADS * LANES


def _layout_w_in_b(w):
    d = w.shape[0]
    qw = w[:, :NSA_DIM].reshape(d, NSA_KV_HEADS, NSA_GROUP * HEAD_DIM)
    qw = jnp.pad(qw, ((0, 0), (0, 0), (0, Q_SLOT - NSA_GROUP * HEAD_DIM))).reshape(d, NSA_KV_HEADS * Q_SLOT)
    gw = w[:, NSA_DIM:NSA_DIM + 3 * NSA_HEADS].reshape(d, NSA_KV_HEADS, 3 * NSA_GROUP)
    gw = jnp.pad(gw, ((0, 0), (0, 0), (0, LANES - 3 * NSA_GROUP))).reshape(d, NSA_KV_HEADS * LANES)
    mw = w[:, NSA_DIM + 3 * NSA_HEADS:]
    return jnp.concatenate([qw, mw, gw], axis=1).astype(BF16)


def kernel(x_prompt, x_sample, state_conv, cache_mem_kv, cache_cmp_kv, cache_slc_kv, state_win_kv, page_table,
           mem_prompt, g_mix, w_in_a, conv_w, w_in_b, w_o, w_mkv, g_mem, g_kv, w_kv, pe_ck, w1_ck, w2_ck,
           pe_cv, w1_cv, w2_cv, g_ffn, w_gu, w_dn, g_final):
    bp, tp, d = x_prompt.shape
    bs, ts = x_sample.shape[:2]
    depth = g_mix.shape[0]
    n_a = w_in_a.shape[0]
    n_mem = mem_prompt.shape[1]
    win_len = state_win_kv.shape[1]
    past_len = page_table.shape[1] * cache_cmp_kv.shape[1]
    conv_dim = conv_w.shape[2]
    slopes = jnp.asarray(np.array(_alibi_list(NSA_HEADS), dtype=np.float32))

    w_in_a16 = w_in_a.astype(BF16)
    w_in_b16 = [_layout_w_in_b(w_in_b[j]) for j in range(depth - n_a)]
    w_o16 = w_o.astype(BF16)
    w_gu16 = w_gu.astype(BF16)
    w_dn16 = w_dn.astype(BF16)
    w_kv16 = w_kv.astype(BF16)
    w_mkv16 = w_mkv.transpose(1, 0, 2).reshape(d, depth * 2 * MEM_DIM).astype(BF16)

    mkv = rms_matmul(mem_prompt.reshape(bp * n_mem, d), g_mem, w_mkv16)
    mem_kv_p = mkv.reshape(bp, n_mem, depth, 2 * MEM_DIM).transpose(2, 0, 1, 3)
    mem_kv_s = cache_mem_kv.reshape(depth, bs, n_mem, 2 * MEM_DIM)

    groups = [
        dict(x=x_prompt.reshape(bp * tp, d), b=bp, t=tp, mem=mem_kv_p, st=jnp.zeros((n_a, bp, 2, conv_dim), F32)),
        dict(x=x_sample.reshape(bs * ts, d), b=bs, t=ts, mem=mem_kv_s, st=state_conv),
    ]
    conv_out = [[], []]
    kv_rows = [None, None]

    cw = _compress_weights(pe_ck, w1_ck, w2_ck) + _compress_weights(pe_cv, w1_cv, w2_cv)
    nsa_in = None
    w_in_bs16 = [_layout_w_in_b_sample(w_in_b[j]) for j in range(depth - n_a)]
    n_pages = page_table.shape[1]
    page_rows = cache_cmp_kv.shape[1]
    pg = 16 if n_pages % 16 == 0 else n_pages
    pool_cmp = cache_cmp_kv.reshape(cache_cmp_kv.shape[0], page_rows, 2 * KV_DIM)
    pool_slc = cache_slc_kv.reshape(cache_slc_kv.shape[0], page_rows, 2 * KV_DIM)
    win_state = state_win_kv.reshape(bs, win_len, 2 * KV_DIM)
    assert ts == 8 and past_len % SEL_BLOCK == 0 and tp % 256 == 0 and tp // SEL_BLOCK <= N_SEL_POS

    def wo_pairs(l, n, y_main, y_mem, nsa_layout, b, t):
        tm = _row_tile(n, 512)
        if nsa_layout:
            hw = NSA_GROUP * HEAD_DIM
            nt = t // tm
            y2d = y_main.reshape(b * NSA_KV_HEADS * t, hw)
            pairs = [(y2d, (tm, hw), (lambda i, h=h: (((i // nt) * NSA_KV_HEADS + h) * nt + i % nt, 0)),
                      w_o16[l, h * hw:(h + 1) * hw]) for h in range(NSA_KV_HEADS)]
            km = NSA_DIM
        else:
            km = y_main.shape[1]
            pairs = [(y_main, (tm, km), lambda i: (i, 0), w_o16[l, :km])]
        return pairs + [(y_mem, (tm, MEM_DIM), lambda i: (i, 0), w_o16[l, km:])]

    for l in range(depth):
        for gi, gr in enumerate(groups):
            x, b, t = gr["x"], gr["b"], gr["t"]
            n = b * t
            nsa_layout = False
            if l < n_a:
                proj = rms_matmul(x, g_mix[l], w_in_a16[l], tn=w_in_a16.shape[2] // 2)
                y_main, new_st = gated_conv(proj, gr["st"][l], conv_w[l], b, t)
                conv_out[gi].append(new_st)
                y_mem = mem_attention(proj, 3 * conv_dim // MEM_DIM, gr["mem"][l], b, t)
            else:
                if l == n_a:
                    kv2d = rms_matmul(x, g_kv, w_kv16, tn=3 * KV_DIM)
                    kv_rows[gi] = kv2d.reshape(b, t, 3, 2, NSA_KV_HEADS, HEAD_DIM)
                    kv = kv_rows[gi]
                    if gi == 0:
                        ck, cv = compress_prompt(kv2d, b, t, cw)
                        nsa_in = (_cmp_slots(ck, b), _cmp_slots(cv, b)) + kv_layout(kv2d, b, t)
                    else:
                        new3 = kv2d.reshape(b, t, 6 * KV_DIM)
                        ck_s, cv_s = compress_sample(page_table, pool_cmp, new3, cw, pg=pg)
                j = l - n_a
                if gi == 0:
                    proj = rms_matmul(x, g_mix[l], w_in_b16[j], tn=B_COLS // 2)
                    y_main = nsa_prompt(proj, slopes, *nsa_in, b, t)
                    nsa_layout = True
                    y_mem = mem_attention(proj, B_MQ_COL // MEM_DIM, gr["mem"][l], b, t)
                else:
                    proj = rms_matmul(x, g_mix[l], w_in_bs16[j], tn=S_COLS)
                    y_main = nsa_sample(proj, slopes, page_table, ck_s, cv_s, new3, win_state, pool_slc, pg=pg)
                    y_mem = mem_attention(proj, S_MQ_COL // MEM_DIM, gr["mem"][l], b, t)
            x = proj_residual(x, wo_pairs(l, n, y_main, y_mem, nsa_layout, b, t))
            x = ffn(x, g_ffn[l], w_gu16[l], w_dn16[l])
            gr["x"] = x

    y_prompt = rmsnorm_rows(groups[0]["x"], g_final).reshape(bp, tp, d)
    y_sample = rmsnorm_rows(groups[1]["x"], g_final).reshape(bs, ts, d)
    conv_state_p = jnp.stack(conv_out[0])
    conv_state_s = jnp.stack(conv_out[1])
    mem_kv_out = mem_kv_p.reshape(depth, bp, n_mem, 2, MEM_HEADS, HEAD_DIM)
    kvp, kvs = kv_rows
    win_kv_p = kvp[:, tp - min(WINDOW, tp):, 2]
    win_kv_s = jnp.concatenate([state_win_kv, kvs[:, :, 2]], axis=1)[:, ts:]
    return (y_prompt, y_sample, conv_state_p, conv_state_s, mem_kv_out, kvp[:, :, 0], kvp[:, :, 1], win_kv_p,
            kvs[:, :, 0], kvs[:, :, 1], win_kv_s)
```

```python
import functools
import math

import numpy as np
import jax
import jax.numpy as jnp
from jax import lax
from jax.experimental import pallas as pl
from jax.experimental.pallas import tpu as pltpu

F32 = jnp.float32
BF16 = jnp.bfloat16

HEAD_DIM = 64
MEM_HEADS = 4
MEM_DIM = MEM_HEADS * HEAD_DIM
NSA_KV_HEADS = 4
NSA_GROUP = 3
NSA_HEADS = NSA_KV_HEADS * NSA_GROUP
NSA_DIM = NSA_HEADS * HEAD_DIM
KV_DIM = NSA_KV_HEADS * HEAD_DIM
CONV_WIDTH = 3
CMP_STRIDE = 16
CMP_LEN = 32
SEL_BLOCK = 64
N_SEL = 16
WINDOW = 512
Q_BLOCK = 64
RMS_EPS = 1e-6
NEG_INF = -1e30
FORCE_SCORE = 1e4
ATT_SCALE = HEAD_DIM ** -0.5

VMEM_LIMIT = 48 * 1024 * 1024


def _cparams(*sem):
    return pltpu.CompilerParams(dimension_semantics=sem, vmem_limit_bytes=VMEM_LIMIT)


def _row_tile(n, want):
    t = min(n, want)
    while n % t:
        t //= 2
    return t


def _rms_matmul_kernel(x_ref, g_ref, w_ref, o_ref, h_ref):
    @pl.when(pl.program_id(1) == 0)
    def _():
        x = x_ref[...]
        ms = jnp.mean(x * x, axis=-1, keepdims=True)
        h_ref[...] = (x * lax.rsqrt(ms + RMS_EPS) * g_ref[...]).astype(BF16)

    o_ref[...] = jnp.dot(h_ref[...], w_ref[...], preferred_element_type=F32)


def rms_matmul(x, g, w, *, tm=1024, tn=512):
    n, d = x.shape
    c = w.shape[1]
    tm = _row_tile(n, tm)
    tn = _row_tile(c, tn)
    return pl.pallas_call(
        _rms_matmul_kernel,
        out_shape=jax.ShapeDtypeStruct((n, c), F32),
        grid=(n // tm, c // tn),
        in_specs=[
            pl.BlockSpec((tm, d), lambda i, j: (i, 0)),
            pl.BlockSpec((1, d), lambda i, j: (0, 0)),
            pl.BlockSpec((d, tn), lambda i, j: (0, j)),
        ],
        out_specs=pl.BlockSpec((tm, tn), lambda i, j: (i, j)),
        scratch_shapes=[pltpu.VMEM((tm, d), BF16)],
        compiler_params=_cparams("parallel", "arbitrary"),
        name="rms_matmul",
    )(x, g.reshape(1, d), w)


def _proj_residual_kernel(n_pairs, x_ref, *refs):
    a_refs = refs[:n_pairs]
    w_refs = refs[n_pairs:2 * n_pairs]
    o_ref = refs[2 * n_pairs]
    acc = x_ref[...]
    for a_ref, w_ref in zip(a_refs, w_refs):
        acc = acc + jnp.dot(a_ref[...].astype(BF16), w_ref[...], preferred_element_type=F32)
    o_ref[...] = acc


def proj_residual(x, pairs, *, tm=512):
    n, d = x.shape
    tm = _row_tile(n, tm)
    in_specs = [pl.BlockSpec((tm, d), lambda i: (i, 0))]
    args = [x]
    for a, blk, imap, _ in pairs:
        in_specs.append(pl.BlockSpec(blk, imap))
        args.append(a)
    for _, _, _, w in pairs:
        in_specs.append(pl.BlockSpec(w.shape, lambda i: (0, 0)))
        args.append(w)
    return pl.pallas_call(
        functools.partial(_proj_residual_kernel, len(pairs)),
        out_shape=jax.ShapeDtypeStruct((n, d), F32),
        grid=(n // tm,),
        in_specs=in_specs,
        out_specs=pl.BlockSpec((tm, d), lambda i: (i, 0)),
        compiler_params=_cparams("parallel"),
        name="proj_residual",
    )(*args)


def _ffn_kernel(x_ref, g_ref, wg_ref, wu_ref, wd_ref, o_ref, h_ref, acc_ref):
    f = pl.program_id(1)

    @pl.when(f == 0)
    def _():
        x = x_ref[...]
        ms = jnp.mean(x * x, axis=-1, keepdims=True)
        h_ref[...] = (x * lax.rsqrt(ms + RMS_EPS) * g_ref[...]).astype(BF16)
        acc_ref[...] = x

    h = h_ref[...]
    gate = jnp.dot(h, wg_ref[...], preferred_element_type=F32)
    up = jnp.dot(h, wu_ref[...], preferred_element_type=F32)
    act = (gate * jax.nn.sigmoid(gate) * up).astype(BF16)
    acc_ref[...] += jnp.dot(act, wd_ref[...], preferred_element_type=F32)

    @pl.when(f == pl.num_programs(1) - 1)
    def _():
        o_ref[...] = acc_ref[...]


def ffn(x, g, w_gu, w_dn, *, tm=512, tf=1408):
    n, d = x.shape
    dff = w_dn.shape[0]
    tm = _row_tile(n, tm)
    nf = dff // tf
    return pl.pallas_call(
        _ffn_kernel,
        out_shape=jax.ShapeDtypeStruct((n, d), F32),
        grid=(n // tm, nf),
        in_specs=[
            pl.BlockSpec((tm, d), lambda i, f: (i, 0)),
            pl.BlockSpec((1, d), lambda i, f: (0, 0)),
            pl.BlockSpec((d, tf), lambda i, f: (0, f)),
            pl.BlockSpec((d, tf), lambda i, f: (0, f + nf)),
            pl.BlockSpec((tf, d), lambda i, f: (f, 0)),
        ],
        out_specs=pl.BlockSpec((tm, d), lambda i, f: (i, 0)),
        scratch_shapes=[pltpu.VMEM((tm, d), BF16), pltpu.VMEM((tm, d), F32)],
        compiler_params=_cparams("parallel", "arbitrary"),
        name="ffn",
    )(x, g.reshape(1, d), w_gu, w_gu, w_dn)


def _rmsnorm_kernel(x_ref, g_ref, o_ref):
    x = x_ref[...]
    ms = jnp.mean(x * x, axis=-1, keepdims=True)
    o_ref[...] = x * lax.rsqrt(ms + RMS_EPS) * g_ref[...]


def rmsnorm_rows(x, g, *, tm=1024):
    n, d = x.shape
    tm = _row_tile(n, tm)
    return pl.pallas_call(
        _rmsnorm_kernel,
        out_shape=jax.ShapeDtypeStruct((n, d), F32),
        grid=(n // tm,),
        in_specs=[pl.BlockSpec((tm, d), lambda i: (i, 0)), pl.BlockSpec((1, d), lambda i: (0, 0))],
        out_specs=pl.BlockSpec((tm, d), lambda i: (i, 0)),
        compiler_params=_cparams("parallel"),
        name="rmsnorm",
    )(x, g.reshape(1, d))


def _conv_kernel(b_ref, c_ref, h_ref, cp_ref, hp_ref, st_ref, w_ref, y_ref, ns_ref):
    i = pl.program_id(1)
    u = c_ref[...] * h_ref[...]
    tt = u.shape[0]
    prev = cp_ref[...] * hp_ref[...]
    st = st_ref[0]
    first = i == 0
    p1 = jnp.where(first, st[1:2], prev[7:8])
    p2 = jnp.where(first, st[0:1], prev[6:7])
    row = lax.broadcasted_iota(jnp.int32, u.shape, 0)
    u1 = jnp.where(row == 0, p1, pltpu.roll(u, 1, 0))
    u2 = jnp.where(row == 0, p2, jnp.where(row == 1, p1, pltpu.roll(u, 2, 0)))
    w = w_ref[...]
    y = w[0:1] * u2 + w[1:2] * u1 + w[2:3] * u
    y_ref[...] = b_ref[...] * y
    ns_ref[0] = u[tt - 2:tt]


def gated_conv(proj, state, conv_w, batch, t, *, tt=512):
    c = conv_w.shape[1]
    tt = _row_tile(t, tt)
    nt = t // tt
    r8 = tt // 8

    def prev_map(col):
        return lambda b, i: (jnp.maximum((b * nt + i) * r8 - 1, 0), col)

    return pl.pallas_call(
        _conv_kernel,
        out_shape=(jax.ShapeDtypeStruct((batch * t, c), F32), jax.ShapeDtypeStruct((batch, 2, c), F32)),
        grid=(batch, nt),
        in_specs=[
            pl.BlockSpec((tt, c), lambda b, i: (b * nt + i, 0)),
            pl.BlockSpec((tt, c), lambda b, i: (b * nt + i, 1)),
            pl.BlockSpec((tt, c), lambda b, i: (b * nt + i, 2)),
            pl.BlockSpec((8, c), prev_map(1)),
            pl.BlockSpec((8, c), prev_map(2)),
            pl.BlockSpec((1, 2, c), lambda b, i: (b, 0, 0)),
            pl.BlockSpec((CONV_WIDTH, c), lambda b, i: (0, 0)),
        ],
        out_specs=(
            pl.BlockSpec((tt, c), lambda b, i: (b * nt + i, 0)),
            pl.BlockSpec((1, 2, c), lambda b, i: (b, 0, 0)),
        ),
        compiler_params=_cparams("parallel", "arbitrary"),
        name="gated_conv",
    )(proj, proj, proj, proj, proj, state, conv_w)


def _mem_attn_kernel(q_ref, kv_ref, o_ref):
    q = q_ref[...] * ATT_SCALE
    kv = kv_ref[0]
    outs = []
    for h in range(MEM_HEADS):
        qh = q[:, h * HEAD_DIM:(h + 1) * HEAD_DIM].astype(BF16)
        kh = kv[:, h * HEAD_DIM:(h + 1) * HEAD_DIM].astype(BF16)
        vh = kv[:, MEM_DIM + h * HEAD_DIM:MEM_DIM + (h + 1) * HEAD_DIM].astype(BF16)
        s = lax.dot_general(qh, kh, (((1,), (1,)), ((), ())), preferred_element_type=F32)
        m = jnp.max(s, axis=-1, keepdims=True)
        e = jnp.exp(s - m)
        p = e / jnp.sum(e, axis=-1, keepdims=True)
        outs.append(jnp.dot(p.astype(BF16), vh, preferred_element_type=F32))
    o_ref[...] = jnp.concatenate(outs, axis=-1)


def mem_attention(proj, col_block, mem_kv, batch, t, *, tt=512):
    tt = _row_tile(t, tt)
    nt = t // tt
    n_mem = mem_kv.shape[1]
    return pl.pallas_call(
        _mem_attn_kernel,
        out_shape=jax.ShapeDtypeStruct((batch * t, MEM_DIM), F32),
        grid=(batch, nt),
        in_specs=[
            pl.BlockSpec((tt, MEM_DIM), lambda b, i: (b * nt + i, col_block)),
            pl.BlockSpec((1, n_mem, 2 * MEM_DIM), lambda b, i: (b, 0, 0)),
        ],
        out_specs=pl.BlockSpec((tt, MEM_DIM), lambda b, i: (b * nt + i, 0)),
        compiler_params=_cparams("parallel", "arbitrary"),
        name="mem_attention",
    )(proj, mem_kv)


def _alibi_list(n):
    def pow2(m):
        start = 2.0 ** (-8.0 / m)
        return [start ** (i + 1) for i in range(m)]
    if n & (n - 1) == 0:
        return pow2(n)
    c = 2 ** int(math.floor(math.log2(n)))
    return pow2(c) + _alibi_list(2 * c)[0::2][: n - c]


LANES = 128
PAIR = LANES // HEAD_DIM


def _to_lane_columns(src, dst_ref, n_rows):
    for c in range(dst_ref.shape[0]):
        dst_ref[c, 0:n_rows, :] = src[0:n_rows, c * LANES:(c + 1) * LANES]


def _compress_tile(buf_ref, row0, m, pe_refs, w1_refs, w2_refs):
    outs = []
    for kv in range(2):
        pe = pe_refs[kv][...]
        lhs = []
        for p in range(NSA_KV_HEADS // PAIR):
            col = kv * (KV_DIM // LANES) + p
            pieces = [
                (buf_ref[col, pl.ds(row0 + j, m, stride=CMP_STRIDE), :] + pe[j:j + 1]).astype(BF16)
                for j in range(CMP_LEN)
            ]
            lhs.append(jnp.concatenate(pieces, axis=1))
        lhs = jnp.concatenate(lhs, axis=0)
        hid = jnp.dot(lhs, w1_refs[kv][...], preferred_element_type=F32)
        act = (hid * jax.nn.sigmoid(hid)).astype(BF16)
        out = jnp.dot(act, w2_refs[kv][...], preferred_element_type=F32)
        outs.append(jnp.concatenate([out[0:m], out[m:2 * m]], axis=1))
    return outs


def _compress_prompt_kernel(x_ref, pek_ref, pev_ref, w1k_ref, w1v_ref, w2k_ref, w2v_ref, ck_ref, cv_ref, buf_ref):
    t = x_ref.shape[0]
    _to_lane_columns(x_ref, buf_ref, t)
    buf_ref[:, t:t + CMP_STRIDE, :] = jnp.zeros((buf_ref.shape[0], CMP_STRIDE, LANES), F32)
    n_blk = t // CMP_STRIDE
    m = min(n_blk, 128)
    for s in range(n_blk // m):
        ck, cv = _compress_tile(buf_ref, s * m * CMP_STRIDE, m, (pek_ref, pev_ref), (w1k_ref, w1v_ref),
                                (w2k_ref, w2v_ref))
        ck_ref[0, s * m:(s + 1) * m, :] = ck
        cv_ref[0, s * m:(s + 1) * m, :] = cv


def _compress_weights(pe, w1, w2):
    eye = jnp.eye(PAIR, dtype=F32)
    w1p = jnp.einsum('jde,qr->jqdre', w1, eye).reshape(CMP_LEN * PAIR * HEAD_DIM, PAIR * w1.shape[2])
    w2p = jnp.einsum('ed,qr->qerd', w2, eye).reshape(PAIR * w2.shape[0], PAIR * HEAD_DIM)
    pe2 = jnp.tile(pe, (1, PAIR))
    return pe2, w1p.astype(BF16), w2p.astype(BF16)


def compress_prompt(kv_rows, batch, t, cw):
    pek, w1k, w2k, pev, w1v, w2v = cw
    n_blk = t // CMP_STRIDE
    full = lambda a: pl.BlockSpec(a.shape, lambda b: (0,) * a.ndim)
    return pl.pallas_call(
        _compress_prompt_kernel,
        out_shape=(jax.ShapeDtypeStruct((batch, n_blk, KV_DIM), F32),) * 2,
        grid=(batch,),
        in_specs=[pl.BlockSpec((t, 2 * KV_DIM), lambda b: (b, 0)), full(pek), full(pev), full(w1k), full(w1v),
                  full(w2k), full(w2v)],
        out_specs=(pl.BlockSpec((1, n_blk, KV_DIM), lambda b: (b, 0, 0)),) * 2,
        scratch_shapes=[pltpu.VMEM((2 * KV_DIM // LANES, t + CMP_STRIDE, LANES), F32)],
        compiler_params=_cparams("parallel"),
        name="compress_prompt",
    )(kv_rows, pek, pev, w1k, w1v, w2k, w2v)


N_CMP_POS = 256
N_SEL_POS = N_CMP_POS * CMP_STRIDE // SEL_BLOCK
MASK_BIG = 2.0 ** 100
M_INIT = -1e38


LOG2E = 1.4426950408889634
MASK_NONE, MASK_CAUSAL, MASK_BAND = 0, 1, 2


def _topk_mask(score):
    tq, nb = score.shape
    pad = jnp.full((tq, LANES - nb), -jnp.inf, F32)
    st = jnp.concatenate([score, pad], axis=1).T[:nb]
    sub = 8
    idx = lax.broadcasted_iota(jnp.int32, (sub, tq), 0)
    groups = [st[r:r + sub] for r in range(0, nb, sub)]
    ranks = [jnp.zeros((sub, tq), F32) for _ in groups]
    for j in range(nb):
        rj = st[j:j + 1, :]
        for gi, sg in enumerate(groups):
            if gi * sub > j:
                before = rj >= sg
            elif gi * sub + sub - 1 < j:
                before = rj > sg
            else:
                before = jnp.where(idx > j - gi * sub, (rj >= sg).astype(F32), (rj > sg).astype(F32)) > 0.0
            ranks[gi] = jnp.where(before, ranks[gi] + 1.0, ranks[gi])
    sel_t = (jnp.concatenate(ranks, axis=0) < N_SEL).astype(F32)
    blk_any = jnp.max(sel_t, axis=1, keepdims=True)
    sel_t = jnp.concatenate([sel_t, jnp.zeros((LANES - nb, tq), F32)], axis=0)
    return sel_t.T[:, :nb], blk_any


def _chunk_words(blk_any, per_chunk):
    nb = blk_any.shape[0]
    per_word = 8 * per_chunk
    bidx = lax.broadcasted_iota(jnp.int32, (nb, 1), 0)
    wgt = jnp.left_shift(1, 3 * ((bidx % per_word) // per_chunk)).astype(F32)
    words = []
    for w in range(nb // per_word):
        v = jnp.sum(jnp.where(bidx // per_word == w, blk_any * wgt, 0.0), axis=0, keepdims=True)
        words.append(v.astype(jnp.int32)[0, 0])
    return words


def _nsa_prompt_kernel(sl_ref, q_ref, gt_ref, ck_ref, cv_ref, kc_ref, vc_ref, o_ref,
                       q_scr, s0_scr, s1_scr, p0_scr, p1_scr, a0_scr, a1_scr, m_scr, l_scr, acc_scr, mb_scr, lst_ref,
                       *, mm_rows, sm_rows):
    kvh = pl.program_id(1)
    i = pl.program_id(2)
    s_scrs, p_scrs, a_scrs = (s0_scr, s1_scr), (p0_scr, p1_scr), (a0_scr, a1_scr)
    tq = q_ref.shape[0]
    kb = tq
    t_len = kc_ref.shape[2] // 2
    t0 = i * tq
    slopes = [sl_ref[kvh * NSA_GROUP + g] * LOG2E for g in range(NSA_GROUP)]
    q = q_ref[...] * (ATT_SCALE * LOG2E)
    q3 = jnp.concatenate([q[:, g * HEAD_DIM:(g + 1) * HEAD_DIM] for g in range(NSA_GROUP)], axis=0)
    q3b = q3.astype(BF16)
    nt_dims = (((1,), (1,)), ((), ()))
    t_col = t0 + lax.broadcasted_iota(jnp.int32, (tq, 1), 0)

    @pl.when(i == 0)
    def _():
        r = lax.broadcasted_iota(jnp.int32, (tq, kb), 0)
        c = lax.broadcasted_iota(jnp.int32, (tq, kb), 1)
        mb_scr[MASK_NONE] = jnp.zeros((tq, kb), F32)
        mb_scr[MASK_CAUSAL] = jnp.where(c <= r, 0.0, -MASK_BIG)
        mb_scr[MASK_BAND] = jnp.where(c >= r, 0.0, -MASK_BIG)

    m_scr[...] = jnp.full(m_scr.shape, M_INIT, F32)
    l_scr[...] = jnp.zeros(l_scr.shape, F32)
    acc_scr[...] = jnp.zeros(acc_scr.shape, F32)

    pos = lax.broadcasted_iota(jnp.int32, (1, N_CMP_POS), 1)
    blk_n = (pos % N_SEL_POS) * (N_CMP_POS // N_SEL_POS) + pos // N_SEL_POS
    c_end = blk_n * CMP_STRIDE + (CMP_LEN - 1)
    n_real = t_len // CMP_STRIDE - 1
    vis_c = (c_end <= t_col) & (blk_n < n_real)
    c_end_f = c_end.astype(F32)
    s_c = lax.dot_general(q3b, ck_ref[0, 0], nt_dims, preferred_element_type=F32)
    p_c = []
    for g in range(NSA_GROUP):
        sg = jnp.where(vis_c, s_c[g * tq:(g + 1) * tq] + slopes[g] * c_end_f, NEG_INF)
        mg = jnp.max(sg, axis=1, keepdims=True)
        eg = jnp.where(vis_c, jnp.exp2(sg - mg), 0.0)
        dg = jnp.sum(eg, axis=1, keepdims=True)
        p_c.append(eg / jnp.where(dg > 0.0, dg, 1.0))
    o_c = jnp.dot(jnp.concatenate(p_c, axis=0).astype(BF16), cv_ref[0, 0], preferred_element_type=F32)
    imp = p_c[0] + p_c[1] + p_c[2]
    imp = imp[:, :LANES] + imp[:, LANES:]
    imp = imp[:, :N_SEL_POS] + imp[:, N_SEL_POS:]

    blk = lax.broadcasted_iota(jnp.int32, (1, N_SEL_POS), 1)
    cur = t_col // SEL_BLOCK
    forced = (blk == 0) | (blk == cur) | (blk == cur - 1)
    score = jnp.where(forced, FORCE_SCORE, jnp.where(blk <= cur, imp, -jnp.inf))
    sel, blk_any = _topk_mask(score)
    notsel = (1.0 - sel).astype(BF16)
    q_scr[...] = jnp.concatenate([q3b, jnp.concatenate([notsel] * NSA_GROUP, axis=0)], axis=1)

    lane_k = lax.broadcasted_iota(jnp.int32, (1, kb), 1)
    n_win = WINDOW // kb + 1

    words = _chunk_words(blk_any, kb // SEL_BLOCK)
    n_sel = jnp.int32(0)
    for c in range(N_SEL_POS * SEL_BLOCK // kb):
        active = (((words[c // 8] >> (3 * (c % 8))) & 7) != 0) & (c <= i)
        lst_ref[n_sel] = c
        n_sel = n_sel + active.astype(jnp.int32)
    n_chunks = n_sel + n_win

    def chunk(p):
        is_sel = p < n_sel
        is_pad = p >= n_chunks
        c = lst_ref[jnp.minimum(p, n_sel - 1)]
        w = p - n_sel
        kpos0 = jnp.where(is_sel, c * kb, jnp.where(is_pad, -kb, t0 - WINDOW + w * kb))
        row0 = jnp.maximum(kpos0, 0) + jnp.where(is_sel | is_pad, 0, t_len)
        mtype = jnp.where(is_sel, jnp.where(c == i, MASK_CAUSAL, MASK_NONE),
                          jnp.where(w == 0, MASK_BAND, jnp.where(w == n_win - 1, MASK_CAUSAL, MASK_NONE)))
        return pl.multiple_of(row0, kb), kpos0, mtype, jnp.where(is_sel | is_pad, 0, 1)

    def stage_logits(c, par):
        s_scr = s_scrs[par]
        row0, kpos0, mtype, _ = chunk(c)
        k = kc_ref[0, 0, pl.ds(row0, kb), :]
        kpos_f = (kpos0 + lane_k).astype(F32)
        off = jnp.where(kpos0 >= 0, 0.0, -MASK_BIG)
        for g in range(NSA_GROUP):
            bias = slopes[g] * kpos_f + off
            for r in range(0, tq, mm_rows):
                rows = slice(g * tq + r, g * tq + r + mm_rows)
                s = lax.dot_general(q_scr[rows, :], k, nt_dims, preferred_element_type=F32)
                s_scr[rows, :] = (s + mb_scr[mtype, r:r + mm_rows, :]) + bias

    def stage_softmax(c, par):
        s_scr, p_scr, a_scr = s_scrs[par], p_scrs[par], a_scrs[par]
        st = chunk(c)[3]
        for r in range(0, NSA_GROUP * tq, sm_rows):
            rows = slice(r, r + sm_rows)
            s = s_scr[rows, :]
            m_prev = m_scr[st, rows, :]
            m_new = jnp.maximum(m_prev, jnp.max(s, axis=1, keepdims=True))
            alpha = jnp.exp2(m_prev - m_new)
            p = jnp.exp2(s - jnp.concatenate([m_new] * (kb // LANES), axis=1))
            l_scr[st, rows, :] = alpha * l_scr[st, rows, :] + jnp.sum(p, axis=1, keepdims=True)
            p_scr[rows, :] = p.astype(BF16)
            a_scr[rows, :] = alpha
            m_scr[st, rows, :] = m_new

    def stage_values(c, par):
        p_scr, a_scr = p_scrs[par], a_scrs[par]
        row0, _, _, st = chunk(c)
        v = vc_ref[0, 0, pl.ds(row0, kb), :]
        for r in range(0, NSA_GROUP * tq, mm_rows):
            rows = slice(r, r + mm_rows)
            pv = jnp.dot(p_scr[rows, :], v, preferred_element_type=F32)
            acc_scr[st, rows, :] = acc_scr[st, rows, :] * a_scr[rows, :HEAD_DIM] + pv

    stage_logits(0, 0)
    stage_softmax(0, 0)
    stage_logits(1, 1)

    def pipe_body(j, carry):
        c = 2 * j
        stage_values(c - 2, 0)
        stage_softmax(c - 1, 1)
        stage_logits(c, 0)
        stage_values(c - 1, 1)
        stage_softmax(c, 0)
        stage_logits(c + 1, 1)
        return carry

    n_even = n_chunks + n_chunks % 2
    lax.fori_loop(1, n_even // 2, pipe_body, 0)
    stage_values(n_even - 2, 0)
    stage_softmax(n_even - 1, 1)
    stage_values(n_even - 1, 1)
    o_s = acc_scr[0] / l_scr[0][:, :HEAD_DIM]
    o_w = acc_scr[1] / l_scr[1][:, :HEAD_DIM]

    gates = jax.nn.sigmoid(gt_ref[...])
    outs = []
    for g in range(NSA_GROUP):
        rows = slice(g * tq, (g + 1) * tq)
        outs.append(gates[:, 3 * g:3 * g + 1] * o_c[rows] + gates[:, 3 * g + 1:3 * g + 2] * o_s[rows]
                    + gates[:, 3 * g + 2:3 * g + 3] * o_w[rows])
    o_ref[0, 0] = jnp.concatenate(outs, axis=1)


def nsa_prompt(proj, slopes, ckp, cvp, kcat, vcat, batch, t, *, tq=256, mm_rows=256, sm_rows=256):
    assert t % tq == 0 and WINDOW % tq == 0 and WINDOW >= tq
    nt = t // tq
    rows = NSA_GROUP * tq
    gcol = B_GATE_COL // LANES
    grid_spec = pltpu.PrefetchScalarGridSpec(
        num_scalar_prefetch=1,
        grid=(batch, NSA_KV_HEADS, nt),
        in_specs=[
            pl.BlockSpec((tq, Q_SLOT), lambda b, h, i, sl: (b * nt + i, h)),
            pl.BlockSpec((tq, LANES), lambda b, h, i, sl: (b * nt + i, gcol + h)),
            pl.BlockSpec((1, 1, N_CMP_POS, HEAD_DIM), lambda b, h, i, sl: (b, h, 0, 0)),
            pl.BlockSpec((1, 1, N_CMP_POS, HEAD_DIM), lambda b, h, i, sl: (b, h, 0, 0)),
            pl.BlockSpec((1, 1, 2 * t, LANES), lambda b, h, i, sl: (b, h, 0, 0)),
            pl.BlockSpec((1, 1, 2 * t, HEAD_DIM), lambda b, h, i, sl: (b, h, 0, 0)),
        ],
        out_specs=pl.BlockSpec((1, 1, tq, NSA_GROUP * HEAD_DIM), lambda b, h, i, sl: (b, h, i, 0)),
        scratch_shapes=[
            pltpu.VMEM((rows, LANES), BF16),
            pltpu.VMEM((rows, tq), F32), pltpu.VMEM((rows, tq), F32),
            pltpu.VMEM((rows, tq), BF16), pltpu.VMEM((rows, tq), BF16),
            pltpu.VMEM((rows, LANES), F32), pltpu.VMEM((rows, LANES), F32),
            pltpu.VMEM((2, rows, LANES), F32),
            pltpu.VMEM((2, rows, LANES), F32),
            pltpu.VMEM((2, rows, HEAD_DIM), F32),
            pltpu.VMEM((3, tq, tq), F32),
            pltpu.SMEM((N_SEL_POS * SEL_BLOCK // tq,), jnp.int32),
        ],
    )
    return pl.pallas_call(
        functools.partial(_nsa_prompt_kernel, mm_rows=mm_rows, sm_rows=sm_rows),
        out_shape=jax.ShapeDtypeStruct((batch, NSA_KV_HEADS, t, NSA_GROUP * HEAD_DIM), F32),
        grid_spec=grid_spec,
        compiler_params=_cparams("parallel", "parallel", "arbitrary"),
        name="nsa_prompt",
    )(slopes, proj, proj, ckp, cvp, kcat, vcat)


def _kv_layout_kernel(x_ref, kc_ref, vc_ref):
    tm = x_ref.shape[0]
    pos = pl.program_id(1) * tm + lax.broadcasted_iota(jnp.int32, (tm, N_SEL_POS), 0)
    blk = lax.broadcasted_iota(jnp.int32, (tm, N_SEL_POS), 1)
    onehot = jnp.where(pos // SEL_BLOCK == blk, -MASK_BIG, 0.0).astype(BF16)
    zeros = jnp.zeros((tm, N_SEL_POS), BF16)
    for h in range(NSA_KV_HEADS):
        head = lambda c0: x_ref[:, c0 + h * HEAD_DIM:c0 + (h + 1) * HEAD_DIM].astype(BF16)
        kc_ref[0, h, 0] = jnp.concatenate([head(2 * KV_DIM), onehot], axis=1)
        kc_ref[0, h, 1] = jnp.concatenate([head(4 * KV_DIM), zeros], axis=1)
        vc_ref[0, h, 0] = head(3 * KV_DIM)
        vc_ref[0, h, 1] = head(5 * KV_DIM)


def kv_layout(kv_rows, batch, t, *, tm=512):
    tm = _row_tile(t, tm)
    nt = t // tm
    kc, vc = pl.pallas_call(
        _kv_layout_kernel,
        out_shape=(jax.ShapeDtypeStruct((batch, NSA_KV_HEADS, 2, t, LANES), BF16),
                   jax.ShapeDtypeStruct((batch, NSA_KV_HEADS, 2, t, HEAD_DIM), BF16)),
        grid=(batch, nt),
        in_specs=[pl.BlockSpec((tm, kv_rows.shape[1]), lambda b, i: (b * nt + i, 0))],
        out_specs=(pl.BlockSpec((1, NSA_KV_HEADS, 2, tm, LANES), lambda b, i: (b, 0, 0, i, 0)),
                   pl.BlockSpec((1, NSA_KV_HEADS, 2, tm, HEAD_DIM), lambda b, i: (b, 0, 0, i, 0))),
        compiler_params=_cparams("parallel", "parallel"),
        name="kv_layout",
    )(kv_rows)
    return (kc.reshape(batch, NSA_KV_HEADS, 2 * t, LANES), vc.reshape(batch, NSA_KV_HEADS, 2 * t, HEAD_DIM))


def _cmp_slots(c, batch):
    n_blk = c.shape[1]
    per = N_CMP_POS // N_SEL_POS
    c = jnp.pad(c, ((0, 0), (0, N_CMP_POS - n_blk), (0, 0)))
    c = c.reshape(batch, N_SEL_POS, per, NSA_KV_HEADS, HEAD_DIM).transpose(0, 3, 2, 1, 4)
    return c.reshape(batch, NSA_KV_HEADS, N_CMP_POS, HEAD_DIM).astype(BF16)


def _compress_sample_kernel(pt_ref, pool_ref, new_ref, pek_ref, pev_ref, w1k_ref, w1v_ref, w2k_ref, w2v_ref,
                            ck_ref, cv_ref, buf_ref, sem_ref, col_ref, *, n_pages, pg):
    b = pl.program_id(0)
    g = pl.program_id(1)
    ng = n_pages // pg
    step = b * ng + g
    n_steps = pl.num_programs(0) * ng
    slot = step % 2
    page_rows = buf_ref.shape[3]
    rows = pg * page_rows
    ts = new_ref.shape[1]

    def pages(st, sl, go):
        bb = st // ng
        gg = st % ng

        def copy(k, dst):
            cp = pltpu.make_async_copy(pool_ref.at[pt_ref[bb * n_pages + gg * pg + k]], buf_ref.at[sl, dst],
                                       sem_ref.at[sl])
            cp.start() if go == "start" else cp.wait()

        for k in range(pg):
            copy(k, k)

        @pl.when(gg < ng - 1)
        def _():
            copy(pg, pg)

    @pl.when(step == 0)
    def _():
        pages(0, 0, "start")

    @pl.when(step + 1 < n_steps)
    def _():
        pages(step + 1, 1 - slot, "start")

    pages(step, slot, "wait")
    n_col = col_ref.shape[0]
    for k in range(pg):
        for c in range(n_col):
            col_ref[c, k * page_rows:(k + 1) * page_rows, :] = buf_ref[slot, k, c * LANES:(c + 1) * LANES, :].T

    @pl.when(g < ng - 1)
    def _():
        for c in range(n_col):
            col_ref[c, rows:rows + CMP_STRIDE, :] = buf_ref[slot, pg, c * LANES:(c + 1) * LANES, :].T[0:CMP_STRIDE]

    @pl.when(g == ng - 1)
    def _():
        for c in range(n_col):
            col_ref[c, rows:rows + ts, :] = new_ref[0, :, c * LANES:(c + 1) * LANES]
            col_ref[c, rows + ts:rows + CMP_STRIDE, :] = jnp.zeros((CMP_STRIDE - ts, LANES), F32)

    ck, cv = _compress_tile(col_ref, 0, rows // CMP_STRIDE, (pek_ref, pev_ref), (w1k_ref, w1v_ref),
                            (w2k_ref, w2v_ref))
    ck_ref[0] = ck
    cv_ref[0] = cv


def compress_sample(page_table, pool_t, new_rows, cw, *, pg):
    pek, w1k, w2k, pev, w1v, w2v = cw
    batch, n_pages = page_table.shape
    page_rows = pool_t.shape[2]
    ts = new_rows.shape[1]
    ng = n_pages // pg
    m = pg * page_rows // CMP_STRIDE
    full = lambda a: pl.BlockSpec(a.shape, lambda b, g, pt: (0,) * a.ndim)
    grid_spec = pltpu.PrefetchScalarGridSpec(
        num_scalar_prefetch=1,
        grid=(batch, ng),
        in_specs=[pl.BlockSpec(memory_space=pl.ANY),
                  pl.BlockSpec((1, ts, new_rows.shape[2]), lambda b, g, pt: (b, 0, 0)),
                  full(pek), full(pev), full(w1k), full(w1v), full(w2k), full(w2v)],
        out_specs=(pl.BlockSpec((1, m, KV_DIM), lambda b, g, pt: (b, g, 0)),) * 2,
        scratch_shapes=[pltpu.VMEM((2, pg + 1, 2 * KV_DIM, page_rows), F32), pltpu.SemaphoreType.DMA((2,)),
                        pltpu.VMEM((2 * KV_DIM // LANES, pg * page_rows + CMP_STRIDE, LANES), F32)],
    )
    return pl.pallas_call(
        functools.partial(_compress_sample_kernel, n_pages=n_pages, pg=pg),
        out_shape=(jax.ShapeDtypeStruct((batch, ng * m, KV_DIM), F32),) * 2,
        grid_spec=grid_spec,
        compiler_params=_cparams("arbitrary", "arbitrary"),
        name="compress_sample",
    )(page_table.reshape(-1), pool_t, new_rows, pek, pev, w1k, w1v, w2k, w2v)


S_COL_G = 32


def _softmax_rows0(s, vis):
    s = jnp.where(vis, s, NEG_INF)
    m = jnp.max(s, axis=0, keepdims=True)
    e = jnp.where(vis, jnp.exp(s - m), 0.0)
    d = jnp.sum(e, axis=0, keepdims=True)
    return e / jnp.where(d > 0.0, d, 1.0)


def _nsa_sample_kernel(sl_ref, pt_ref, proj_ref, ck_ref, cv_ref, new_ref, win_ref, pool_ref, y_ref,
                       buf_ref, sem_ref, s_ref, v_ref, bdq_ref, oc_ref, ow_ref, sc_ref, sel_ref, imp_ref,
                       *, n_pages, pg, page_rows, past):
    b = pl.program_id(0)
    g = pl.program_id(1)
    ng = n_pages // pg
    step = b * ng + g
    n_steps = pl.num_programs(0) * ng
    slot = step % 2
    rows = pg * page_rows
    ts = proj_ref.shape[0]
    nt_dims = (((1,), (1,)), ((), ()))
    pages = _page_group_dma(pt_ref, pool_ref, buf_ref, sem_ref, n_pages, pg, page_rows)

    @pl.when(step == 0)
    def _():
        pages(0, 0, "start")

    @pl.when(step + 1 < n_steps)
    def _():
        pages(step + 1, 1 - slot, "start")

    col = lax.broadcasted_iota(jnp.int32, (1, LANES), 1)
    col_g = col // S_COL_G
    col_h = (col % S_COL_G) // ts
    t_row = past + col % ts
    slope_row = jnp.zeros((1, LANES), F32)
    for h in range(NSA_HEADS):
        slope_row = jnp.where((col_h == h // NSA_GROUP) & (col_g == h % NSA_GROUP), sl_ref[h], slope_row)
    n_sb = past // SEL_BLOCK + 1

    @pl.when(g == 0)
    def _():
        lane_h = lax.broadcasted_iota(jnp.int32, (ts, KV_DIM), 1) // HEAD_DIM
        tiles = []
        for gg in range(NSA_GROUP):
            qg = proj_ref[:, gg * KV_DIM:(gg + 1) * KV_DIM] * ATT_SCALE
            for h in range(NSA_KV_HEADS):
                tiles.append(jnp.where(lane_h == h, qg, 0.0))
        tiles.append(jnp.zeros((LANES - NSA_HEADS * ts, KV_DIM), F32))
        bdq = jnp.concatenate(tiles, axis=0).astype(BF16)
        bdq_ref[...] = bdq

        n_c = ck_ref.shape[1]
        s_c = lax.dot_general(ck_ref[0].astype(BF16), bdq, nt_dims, preferred_element_type=F32)
        c_end = lax.broadcasted_iota(jnp.int32, (n_c, 1), 0) * CMP_STRIDE + (CMP_LEN - 1)
        dist = t_row - c_end
        p_c = _softmax_rows0(s_c - slope_row * dist.astype(F32), dist >= 0)
        oc_ref[...] = jnp.dot(p_c.T.astype(BF16), cv_ref[0].astype(BF16), preferred_element_type=F32)
        pz = jnp.where(col < NSA_GROUP * S_COL_G, p_c, 0.0)
        imp_ref[...] = pz + pltpu.roll(pz, S_COL_G, 1) + pltpu.roll(pz, 2 * S_COL_G, 1) + pltpu.roll(pz, 3 * S_COL_G, 1)
        per = SEL_BLOCK // CMP_STRIDE
        n_imp = n_c // per
        imp = imp_ref[pl.ds(0, n_imp, stride=per), :]
        for r in range(1, per):
            imp = imp + imp_ref[pl.ds(r, n_imp, stride=per), :]
        n_pad = sc_ref.shape[0]
        imp = jnp.concatenate([imp, jnp.zeros((n_pad - n_imp, LANES), F32)], axis=0)
        blk = lax.broadcasted_iota(jnp.int32, (n_pad, 1), 0)
        cur = t_row // SEL_BLOCK
        forced = (blk == 0) | (blk == cur) | (blk == cur - 1)
        sc_ref[...] = jnp.where(forced, FORCE_SCORE, jnp.where((blk <= cur) & (blk < n_sb), imp, -jnp.inf))
        st = sc_ref[...]

        def rank_body(j, rank):
            rj = sc_ref[pl.ds(j, 1), :]
            return rank + jnp.where(blk > j, (rj >= st).astype(F32), (rj > st).astype(F32))

        rank = lax.fori_loop(0, n_sb, rank_body, jnp.zeros(st.shape, F32))
        sel_ref[...] = (rank < N_SEL).astype(F32)

        wl = win_ref.shape[1]
        zpad = jnp.zeros((LANES - ts, KV_DIM), F32)
        kw = jnp.concatenate([win_ref[0, :, 0:KV_DIM], new_ref[0, :, 4 * KV_DIM:5 * KV_DIM], zpad], axis=0)
        vw = jnp.concatenate([win_ref[0, :, KV_DIM:2 * KV_DIM], new_ref[0, :, 5 * KV_DIM:6 * KV_DIM], zpad], axis=0)
        s_w = lax.dot_general(kw.astype(BF16), bdq, nt_dims, preferred_element_type=F32)
        w_pos = past - wl + lax.broadcasted_iota(jnp.int32, (wl + LANES, 1), 0)
        dist_w = t_row - w_pos
        p_w = _softmax_rows0(s_w - slope_row * dist_w.astype(F32),
                             (dist_w >= 0) & (dist_w <= WINDOW) & (w_pos >= 0))
        ow_ref[...] = jnp.dot(p_w.T.astype(BF16), vw.astype(BF16), preferred_element_type=F32)

    pages(step, slot, "wait")
    bdq = bdq_ref[...]
    base = g * rows
    kpage = buf_ref[slot, :, 0:KV_DIM].astype(BF16)
    s_g = lax.dot_general(kpage, bdq, nt_dims, preferred_element_type=F32)
    kpos = base + lax.broadcasted_iota(jnp.int32, (rows, 1), 0)
    s_g = s_g - slope_row * (t_row - kpos).astype(F32)
    row0 = pl.multiple_of(base, rows)
    for bl in range(rows // SEL_BLOCK):
        on = sel_ref[pl.ds(g * (rows // SEL_BLOCK) + bl, 1), :] > 0.0
        s_ref[pl.ds(row0 + bl * SEL_BLOCK, SEL_BLOCK), :] = jnp.where(
            on, s_g[bl * SEL_BLOCK:(bl + 1) * SEL_BLOCK], NEG_INF)
    v_ref[pl.ds(row0, rows), :] = buf_ref[slot, :, KV_DIM:2 * KV_DIM].astype(BF16)

    @pl.when(g == ng - 1)
    def _():
        tail = s_ref.shape[0] - past
        zpad = jnp.zeros((tail - ts, KV_DIM), F32)
        k_new = jnp.concatenate([new_ref[0, :, 2 * KV_DIM:3 * KV_DIM], zpad], axis=0)
        v_new = jnp.concatenate([new_ref[0, :, 3 * KV_DIM:4 * KV_DIM], zpad], axis=0)
        s_n = lax.dot_general(k_new.astype(BF16), bdq, nt_dims, preferred_element_type=F32)
        kp = past + lax.broadcasted_iota(jnp.int32, (tail, 1), 0)
        dist_n = t_row - kp
        on = (sel_ref[pl.ds(n_sb - 1, 1), :] > 0.0) & (dist_n >= 0)
        s_ref[pl.ds(past, tail), :] = jnp.where(on, s_n - slope_row * dist_n.astype(F32), NEG_INF)
        v_ref[pl.ds(past, tail), :] = v_new.astype(BF16)
        nk = s_ref.shape[0]
        m = jnp.max(s_ref[...], axis=0, keepdims=True)
        ch = rows
        l = jnp.zeros((1, LANES), F32)
        o_s = jnp.zeros((LANES, KV_DIM), F32)
        for c0 in range(0, nk, ch):
            cl = min(ch, nk - c0)
            e = jnp.exp(s_ref[c0:c0 + cl, :] - m)
            l = l + jnp.sum(e, axis=0, keepdims=True)
            if cl % LANES:
                e = jnp.concatenate([e, jnp.zeros((LANES - cl % LANES, LANES), F32)], axis=0)
                vv = jnp.concatenate([v_ref[c0:c0 + cl, :], jnp.zeros((LANES - cl % LANES, KV_DIM), BF16)], axis=0)
            else:
                vv = v_ref[c0:c0 + cl, :]
            o_s = o_s + jnp.dot(e.T.astype(BF16), vv, preferred_element_type=F32)
        inv_l = (1.0 / jnp.broadcast_to(l, (8, LANES))).T[:, 0:1]
        o_s = o_s * inv_l
        o_c = oc_ref[...]
        o_w = ow_ref[...]
        gates = jax.nn.sigmoid(proj_ref[:, (NSA_GROUP + 1) * KV_DIM:(NSA_GROUP + 1) * KV_DIM + LANES])
        outs = []
        for h in range(NSA_HEADS):
            kvh, gg = h // NSA_GROUP, h % NSA_GROUP
            r0 = gg * S_COL_G + kvh * ts
            lanes = slice(kvh * HEAD_DIM, (kvh + 1) * HEAD_DIM)
            outs.append(gates[:, 3 * h:3 * h + 1] * o_c[r0:r0 + ts, lanes]
                        + gates[:, 3 * h + 1:3 * h + 2] * o_s[r0:r0 + ts, lanes]
                        + gates[:, 3 * h + 2:3 * h + 3] * o_w[r0:r0 + ts, lanes])
        y_ref[...] = jnp.concatenate(outs, axis=1)


def nsa_sample(proj, slopes, page_table, ck, cv, new_rows, win_state, pool, *, pg):
    batch, n_pages = page_table.shape
    page_rows = pool.shape[1]
    ts = new_rows.shape[1]
    past = n_pages * page_rows
    ng = n_pages // pg
    n_c = ck.shape[1]
    wl = win_state.shape[1]
    n_sb_pad = -(-(past // SEL_BLOCK + 1) // 8) * 8
    nk = past + 16
    grid_spec = pltpu.PrefetchScalarGridSpec(
        num_scalar_prefetch=2,
        grid=(batch, ng),
        in_specs=[
            pl.BlockSpec((ts, proj.shape[1]), lambda b, g, sl, pt: (b, 0)),
            pl.BlockSpec((1, n_c, KV_DIM), lambda b, g, sl, pt: (b, 0, 0)),
            pl.BlockSpec((1, n_c, KV_DIM), lambda b, g, sl, pt: (b, 0, 0)),
            pl.BlockSpec((1, ts, new_rows.shape[2]), lambda b, g, sl, pt: (b, 0, 0)),
            pl.BlockSpec((1, wl, 2 * KV_DIM), lambda b, g, sl, pt: (b, 0, 0)),
            pl.BlockSpec(memory_space=pl.ANY),
        ],
        out_specs=pl.BlockSpec((ts, NSA_DIM), lambda b, g, sl, pt: (b, 0)),
        scratch_shapes=[
            pltpu.VMEM((2, pg * page_rows, 2 * KV_DIM), F32), pltpu.SemaphoreType.DMA((2,)),
            pltpu.VMEM((nk, LANES), F32), pltpu.VMEM((nk, KV_DIM), BF16), pltpu.VMEM((LANES, KV_DIM), BF16),
            pltpu.VMEM((LANES, KV_DIM), F32), pltpu.VMEM((LANES, KV_DIM), F32),
            pltpu.VMEM((n_sb_pad, LANES), F32), pltpu.VMEM((n_sb_pad, LANES), F32), pltpu.VMEM((n_c, LANES), F32),
        ],
    )
    return pl.pallas_call(
        functools.partial(_nsa_sample_kernel, n_pages=n_pages, pg=pg, page_rows=page_rows, past=past),
        out_shape=jax.ShapeDtypeStruct((batch * ts, NSA_DIM), F32),
        grid_spec=grid_spec,
        compiler_params=_cparams("arbitrary", "arbitrary"),
        name="nsa_sample",
    )(slopes, page_table.reshape(-1), proj, ck, cv, new_rows, win_state, pool)


def _softmax_lanes(pieces):
    m = functools.reduce(jnp.maximum, [jnp.max(s, axis=1, keepdims=True) for s in pieces])
    es = [jnp.where(s > 0.5 * NEG_INF, jnp.exp(s - m), 0.0) for s in pieces]
    d = functools.reduce(jnp.add, [jnp.sum(e, axis=1, keepdims=True) for e in es])
    inv = 1.0 / jnp.where(d > 0.0, d, 1.0)
    return [e * inv for e in es]


def _nsa_decode_kernel(sl_ref, pt_ref, proj_ref, ck_ref, cv_ref, new_ref, win_ref, e_ref, pool_ref, y_ref,
                       buf_ref, sem_ref, q_scr, m_scr, l_scr, acc_scr, oc_scr, ow_scr, sc_scr,
                       *, n_pages, pg, past):
    b = pl.program_id(0)
    g = pl.program_id(1)
    ng = n_pages // pg
    step = b * ng + g
    n_steps = pl.num_programs(0) * ng
    slot = step % 2
    page_rows = buf_ref.shape[3]
    rows = pg * page_rows
    ts = proj_ref.shape[0]
    nt_dims = (((1,), (1,)), ((), ()))

    def pages(st, sl, go):
        bb = st // ng
        gg = st % ng
        for k in range(pg):
            cp = pltpu.make_async_copy(pool_ref.at[pt_ref[bb * n_pages + gg * pg + k]], buf_ref.at[sl, k],
                                       sem_ref.at[sl])
            cp.start() if go == "start" else cp.wait()

    @pl.when(step == 0)
    def _():
        pages(0, 0, "start")

    @pl.when(step + 1 < n_steps)
    def _():
        pages(step + 1, 1 - slot, "start")

    row = lax.broadcasted_iota(jnp.int32, (LANES, 1), 0)
    row_g = row // S_COL_G
    row_h = (row % S_COL_G) // ts
    t_col = past + row % ts
    slope_col = jnp.zeros((LANES, 1), F32)
    for h in range(NSA_HEADS):
        slope_col = jnp.where((row_h == h // NSA_GROUP) & (row_g == h % NSA_GROUP), sl_ref[h], slope_col)
    n_sb = past // SEL_BLOCK + 1
    zpad = jnp.zeros((LANES - ts, KV_DIM), F32)
    lane_new = lax.broadcasted_iota(jnp.int32, (1, LANES), 1)
    kp_new = past + lane_new
    vis_new = (kp_new <= t_col) & (lane_new < ts)

    @pl.when(g == 0)
    def _():
        lane_h = lax.broadcasted_iota(jnp.int32, (ts, KV_DIM), 1) // HEAD_DIM
        tiles = []
        for gg in range(NSA_GROUP):
            qg = proj_ref[:, gg * KV_DIM:(gg + 1) * KV_DIM] * ATT_SCALE
            for h in range(NSA_KV_HEADS):
                tiles.append(jnp.where(lane_h == h, qg, 0.0))
        tiles.append(jnp.zeros((LANES - NSA_HEADS * ts, KV_DIM), F32))
        bdq = jnp.concatenate(tiles, axis=0).astype(BF16)
        q_scr[:, 0:KV_DIM] = bdq

        n_c = ck_ref.shape[1]
        per = SEL_BLOCK // CMP_STRIDE
        n_q = n_c // per
        s_c = lax.dot_general(bdq, ck_ref[0].astype(BF16), nt_dims, preferred_element_type=F32)
        slot_c = lax.broadcasted_iota(jnp.int32, (1, n_c), 1)
        c_end = ((slot_c % n_q) * per + slot_c // n_q) * CMP_STRIDE + (CMP_LEN - 1)
        dist = t_col - c_end
        s_c = jnp.where(dist >= 0, s_c - slope_col * dist.astype(F32), NEG_INF)
        (p_c,) = _softmax_lanes([s_c])
        oc_scr[...] = jnp.dot(p_c.astype(BF16), cv_ref[0].astype(BF16), preferred_element_type=F32)
        imp = p_c[0:S_COL_G] + p_c[S_COL_G:2 * S_COL_G] + p_c[2 * S_COL_G:3 * S_COL_G]
        imp = functools.reduce(jnp.add, [imp[:, r * n_q:(r + 1) * n_q] for r in range(per)])
        n_blk = sc_scr.shape[0]
        imp = jnp.concatenate([imp, jnp.zeros((S_COL_G, n_blk - n_q), F32)], axis=1)
        imp = jnp.concatenate([imp, jnp.zeros((LANES - S_COL_G, n_blk), F32)], axis=0)
        blk = lax.broadcasted_iota(jnp.int32, (1, n_blk), 1)
        cur = t_col // SEL_BLOCK
        forced = (blk == 0) | (blk == cur) | (blk == cur - 1)
        score = jnp.where(forced, FORCE_SCORE, jnp.where((blk <= cur) & (blk < n_sb), imp, -jnp.inf))
        sc_scr[...] = score.T
        st = sc_scr[...]
        blk_s = lax.broadcasted_iota(jnp.int32, (n_blk, 1), 0)

        def rank_body(j, rank):
            rj = sc_scr[pl.ds(j, 1), :]
            return rank + jnp.where(blk_s > j, (rj >= st).astype(F32), (rj > st).astype(F32))

        rank = lax.fori_loop(0, n_sb, rank_body, jnp.zeros(st.shape, F32))
        notsel = (rank >= N_SEL).astype(F32).T[0:S_COL_G]
        notsel = jnp.concatenate([notsel] * NSA_GROUP + [jnp.ones((LANES - NSA_GROUP * S_COL_G, n_blk), F32)], axis=0)
        q_scr[:, KV_DIM:KV_DIM + n_blk] = notsel.astype(BF16)

        wl = win_ref.shape[2]
        kw_new = jnp.concatenate([new_ref[0, :, 4 * KV_DIM:5 * KV_DIM], zpad], axis=0).astype(BF16)
        vw_new = jnp.concatenate([new_ref[0, :, 5 * KV_DIM:6 * KV_DIM], zpad], axis=0).astype(BF16)
        s_w = jnp.dot(bdq, win_ref[0, 0:KV_DIM, :].astype(BF16), preferred_element_type=F32)
        w_pos = past - wl + lax.broadcasted_iota(jnp.int32, (1, wl), 1)
        dist_w = t_col - w_pos
        s_w = jnp.where((dist_w >= 0) & (dist_w <= WINDOW) & (w_pos >= 0), s_w - slope_col * dist_w.astype(F32),
                        NEG_INF)
        s_wn = lax.dot_general(bdq, kw_new, nt_dims, preferred_element_type=F32)
        dist_n = t_col - kp_new
        s_wn = jnp.where(vis_new & (dist_n <= WINDOW), s_wn - slope_col * dist_n.astype(F32), NEG_INF)
        p_w, p_wn = _softmax_lanes([s_w, s_wn])
        ow_scr[...] = (lax.dot_general(p_w.astype(BF16), win_ref[0, KV_DIM:2 * KV_DIM, :].astype(BF16), nt_dims,
                                       preferred_element_type=F32)
                       + jnp.dot(p_wn.astype(BF16), vw_new, preferred_element_type=F32))
        m_scr[...] = jnp.full(m_scr.shape, M_INIT, F32)
        l_scr[...] = jnp.zeros(l_scr.shape, F32)
        acc_scr[...] = jnp.zeros(acc_scr.shape, F32)

    def online_update(s, pv_fn):
        n = s.shape[1]
        m_prev = m_scr[...]
        m_new = jnp.maximum(m_prev, jnp.max(s, axis=1, keepdims=True))
        alpha = jnp.exp(m_prev - m_new)
        p = jnp.exp(s - jnp.concatenate([m_new] * (n // LANES), axis=1))
        l_scr[...] = alpha * l_scr[...] + jnp.sum(p, axis=1, keepdims=True)
        acc_scr[...] = acc_scr[...] * jnp.concatenate([alpha] * (KV_DIM // LANES), axis=1) + pv_fn(p.astype(BF16))
        m_scr[...] = m_new

    pages(step, slot, "wait")
    kt = jnp.concatenate([buf_ref[slot, k, 0:KV_DIM, :] for k in range(pg)], axis=1).astype(BF16)
    vt = jnp.concatenate([buf_ref[slot, k, KV_DIM:2 * KV_DIM, :] for k in range(pg)], axis=1).astype(BF16)
    n_blk = sc_scr.shape[0]
    s_g = (jnp.dot(q_scr[:, 0:KV_DIM], kt, preferred_element_type=F32)
           + jnp.dot(q_scr[:, KV_DIM:KV_DIM + n_blk], e_ref[0], preferred_element_type=F32))
    kpos = g * rows + lax.broadcasted_iota(jnp.int32, (1, rows), 1)
    s_g = s_g - slope_col * (t_col - kpos).astype(F32)
    online_update(s_g, lambda p: lax.dot_general(p, vt, nt_dims, preferred_element_type=F32))

    @pl.when(g == ng - 1)
    def _():
        k_new = jnp.concatenate([new_ref[0, :, 2 * KV_DIM:3 * KV_DIM], zpad], axis=0).astype(BF16)
        v_new = jnp.concatenate([new_ref[0, :, 3 * KV_DIM:4 * KV_DIM], zpad], axis=0).astype(BF16)
        s_n = lax.dot_general(q_scr[:, 0:KV_DIM], k_new, nt_dims, preferred_element_type=F32)
        unsel_last = q_scr[:, KV_DIM + n_sb - 1:KV_DIM + n_sb].astype(F32)
        dist_n = t_col - kp_new
        s_n = jnp.where(vis_new & (unsel_last < 0.5), s_n - slope_col * dist_n.astype(F32), -MASK_BIG)
        online_update(s_n, lambda p: jnp.dot(p, v_new, preferred_element_type=F32))
        o_s = acc_scr[...] / jnp.concatenate([l_scr[...]] * (KV_DIM // LANES), axis=1)
        o_c = oc_scr[...]
        o_w = ow_scr[...]
        gates = jax.nn.sigmoid(proj_ref[:, (NSA_GROUP + 1) * KV_DIM:(NSA_GROUP + 1) * KV_DIM + LANES])
        outs = []
        for h in range(NSA_HEADS):
            kvh, gg = h // NSA_GROUP, h % NSA_GROUP
            r0 = gg * S_COL_G + kvh * ts
            lanes = slice(kvh * HEAD_DIM, (kvh + 1) * HEAD_DIM)
            outs.append(gates[:, 3 * h:3 * h + 1] * o_c[r0:r0 + ts, lanes]
                        + gates[:, 3 * h + 1:3 * h + 2] * o_s[r0:r0 + ts, lanes]
                        + gates[:, 3 * h + 2:3 * h + 3] * o_w[r0:r0 + ts, lanes])
        y_ref[...] = jnp.concatenate(outs, axis=1)


def nsa_decode(proj, slopes, page_table, ckp, cvp, new_rows, win_t, pool_t, *, pg):
    batch, n_pages = page_table.shape
    page_rows = pool_t.shape[2]
    ts = new_rows.shape[1]
    past = n_pages * page_rows
    ng = n_pages // pg
    rows = pg * page_rows
    n_c = ckp.shape[1]
    wl = win_t.shape[2]
    n_blk = -(-(past // SEL_BLOCK + 1) // LANES) * LANES
    key_blk = jnp.arange(past, dtype=jnp.int32).reshape(ng, 1, rows) // SEL_BLOCK
    e_tab = jnp.where(key_blk == jnp.arange(n_blk, dtype=jnp.int32)[None, :, None], -MASK_BIG, 0.0).astype(BF16)
    grid_spec = pltpu.PrefetchScalarGridSpec(
        num_scalar_prefetch=2,
        grid=(batch, ng),
        in_specs=[
            pl.BlockSpec((ts, proj.shape[1]), lambda b, g, sl, pt: (b, 0)),
            pl.BlockSpec((1, n_c, KV_DIM), lambda b, g, sl, pt: (b, 0, 0)),
            pl.BlockSpec((1, n_c, KV_DIM), lambda b, g, sl, pt: (b, 0, 0)),
            pl.BlockSpec((1, ts, new_rows.shape[2]), lambda b, g, sl, pt: (b, 0, 0)),
            pl.BlockSpec((1, 2 * KV_DIM, wl), lambda b, g, sl, pt: (b, 0, 0)),
            pl.BlockSpec((1, n_blk, rows), lambda b, g, sl, pt: (g, 0, 0)),
            pl.BlockSpec(memory_space=pl.ANY),
        ],
        out_specs=pl.BlockSpec((ts, NSA_DIM), lambda b, g, sl, pt: (b, 0)),
        scratch_shapes=[
            pltpu.VMEM((2, pg, 2 * KV_DIM, page_rows), F32), pltpu.SemaphoreType.DMA((2,)),
            pltpu.VMEM((LANES, KV_DIM + n_blk), BF16),
            pltpu.VMEM((LANES, LANES), F32), pltpu.VMEM((LANES, LANES), F32), pltpu.VMEM((LANES, KV_DIM), F32),
            pltpu.VMEM((LANES, KV_DIM), F32), pltpu.VMEM((LANES, KV_DIM), F32),
            pltpu.VMEM((n_blk, LANES), F32),
        ],
    )
    return pl.pallas_call(
        functools.partial(_nsa_decode_kernel, n_pages=n_pages, pg=pg, past=past),
        out_shape=jax.ShapeDtypeStruct((batch * ts, NSA_DIM), F32),
        grid_spec=grid_spec,
        compiler_params=_cparams("arbitrary", "arbitrary"),
        name="nsa_decode",
    )(slopes, page_table.reshape(-1), proj, ckp, cvp, new_rows, win_t, e_tab, pool_t)


S_MQ_COL =NSA_GROUP * KV_DIM
S_COLS = S_MQ_COL + MEM_DIM + LANES


def _layout_w_in_b_sample(w):
    d = w.shape[0]
    qw = w[:, :NSA_DIM].reshape(d, NSA_KV_HEADS, NSA_GROUP, HEAD_DIM).transpose(0, 2, 1, 3).reshape(d, NSA_DIM)
    gw = jnp.pad(w[:, NSA_DIM:NSA_DIM + 3 * NSA_HEADS], ((0, 0), (0, LANES - 3 * NSA_HEADS)))
    mw = w[:, NSA_DIM + 3 * NSA_HEADS:]
    return jnp.concatenate([qw, mw, gw], axis=1).astype(BF16)


Q_SLOT = 256
B_MQ_COL = NSA_KV_HEADS * Q_SLOT
B_GATE_COL = B_MQ_COL + MEM_DIM
B_COLS = B_GATE_COL + NSA_KV_HEADS * LANES


def _layout_w_in_b(w):
    d = w.shape[0]
    qw = w[:, :NSA_DIM].reshape(d, NSA_KV_HEADS, NSA_GROUP * HEAD_DIM)
    qw = jnp.pad(qw, ((0, 0), (0, 0), (0, Q_SLOT - NSA_GROUP * HEAD_DIM))).reshape(d, NSA_KV_HEADS * Q_SLOT)
    gw = w[:, NSA_DIM:NSA_DIM + 3 * NSA_HEADS].reshape(d, NSA_KV_HEADS, 3 * NSA_GROUP)
    gw = jnp.pad(gw, ((0, 0), (0, 0), (0, LANES - 3 * NSA_GROUP))).reshape(d, NSA_KV_HEADS * LANES)
    mw = w[:, NSA_DIM + 3 * NSA_HEADS:]
    return jnp.concatenate([qw, mw, gw], axis=1).astype(BF16)


def kernel(x_prompt, x_sample, state_conv, cache_mem_kv, cache_cmp_kv, cache_slc_kv, state_win_kv, page_table,
           mem_prompt, g_mix, w_in_a, conv_w, w_in_b, w_o, w_mkv, g_mem, g_kv, w_kv, pe_ck, w1_ck, w2_ck,
           pe_cv, w1_cv, w2_cv, g_ffn, w_gu, w_dn, g_final):
    bp, tp, d = x_prompt.shape
    bs, ts = x_sample.shape[:2]
    depth = g_mix.shape[0]
    n_a = w_in_a.shape[0]
    n_mem = mem_prompt.shape[1]
    win_len = state_win_kv.shape[1]
    past_len = page_table.shape[1] * cache_cmp_kv.shape[1]
    conv_dim = conv_w.shape[2]
    slopes = jnp.asarray(np.array(_alibi_list(NSA_HEADS), dtype=np.float32))

    w_in_a16 = w_in_a.astype(BF16)
    w_in_b16 = [_layout_w_in_b(w_in_b[j]) for j in range(depth - n_a)]
    w_o16 = w_o.astype(BF16)
    w_gu16 = w_gu.astype(BF16)
    w_dn16 = w_dn.astype(BF16)
    w_kv16 = w_kv.astype(BF16)
    w_mkv16 = w_mkv.transpose(1, 0, 2).reshape(d, depth * 2 * MEM_DIM).astype(BF16)

    mkv = rms_matmul(mem_prompt.reshape(bp * n_mem, d), g_mem, w_mkv16)
    mem_kv_p = mkv.reshape(bp, n_mem, depth, 2 * MEM_DIM).transpose(2, 0, 1, 3)
    mem_kv_s = cache_mem_kv.reshape(depth, bs, n_mem, 2 * MEM_DIM)

    groups = [
        dict(x=x_prompt.reshape(bp * tp, d), b=bp, t=tp, mem=mem_kv_p, st=jnp.zeros((n_a, bp, 2, conv_dim), F32)),
        dict(x=x_sample.reshape(bs * ts, d), b=bs, t=ts, mem=mem_kv_s, st=state_conv),
    ]
    conv_out = [[], []]
    kv_rows = [None, None]

    cw = _compress_weights(pe_ck, w1_ck, w2_ck) + _compress_weights(pe_cv, w1_cv, w2_cv)
    nsa_in = None
    w_in_bs16 = [_layout_w_in_b_sample(w_in_b[j]) for j in range(depth - n_a)]
    n_pages = page_table.shape[1]
    page_rows = cache_cmp_kv.shape[1]
    pg = 16 if n_pages % 16 == 0 else n_pages
    assert page_rows == LANES
    pool_cmp = cache_cmp_kv.transpose(0, 2, 3, 4, 1).reshape(cache_cmp_kv.shape[0], 2 * KV_DIM, page_rows)
    pool_slc = cache_slc_kv.transpose(0, 2, 3, 4, 1).reshape(cache_slc_kv.shape[0], 2 * KV_DIM, page_rows)
    win_state = state_win_kv.transpose(0, 2, 3, 4, 1).reshape(bs, 2 * KV_DIM, win_len)
    assert ts == 8 and past_len % SEL_BLOCK == 0 and tp % 256 == 0 and tp // SEL_BLOCK <= N_SEL_POS

    def wo_pairs(l, n, y_main, y_mem, nsa_layout, b, t):
        tm = _row_tile(n, 512)
        if nsa_layout:
            hw = NSA_GROUP * HEAD_DIM
            nt = t // tm
            y2d = y_main.reshape(b * NSA_KV_HEADS * t, hw)
            pairs = [(y2d, (tm, hw), (lambda i, h=h: (((i // nt) * NSA_KV_HEADS + h) * nt + i % nt, 0)),
                      w_o16[l, h * hw:(h + 1) * hw]) for h in range(NSA_KV_HEADS)]
            km = NSA_DIM
        else:
            km = y_main.shape[1]
            pairs = [(y_main, (tm, km), lambda i: (i, 0), w_o16[l, :km])]
        return pairs + [(y_mem, (tm, MEM_DIM), lambda i: (i, 0), w_o16[l, km:])]

    for l in range(depth):
        for gi, gr in enumerate(groups):
            x, b, t = gr["x"], gr["b"], gr["t"]
            n = b * t
            nsa_layout = False
            if l < n_a:
                proj = rms_matmul(x, g_mix[l], w_in_a16[l], tn=w_in_a16.shape[2] // 2)
                y_main, new_st = gated_conv(proj, gr["st"][l], conv_w[l], b, t)
                conv_out[gi].append(new_st)
                y_mem = mem_attention(proj, 3 * conv_dim // MEM_DIM, gr["mem"][l], b, t)
            else:
                if l == n_a:
                    kv2d = rms_matmul(x, g_kv, w_kv16, tn=3 * KV_DIM)
                    kv_rows[gi] = kv2d
                    if gi == 0:
                        ck, cv = compress_prompt(kv2d, b, t, cw)
                        nsa_in = (_cmp_slots(ck, b), _cmp_slots(cv, b)) + kv_layout(kv2d, b, t)
                    else:
                        new3 = kv2d.reshape(b, t, 6 * KV_DIM)
                        ck_s, cv_s = compress_sample(page_table, pool_cmp, new3, cw, pg=pg)
                        per = SEL_BLOCK // CMP_STRIDE
                        slots = lambda c: c.reshape(b, c.shape[1] // per, per, KV_DIM).transpose(0, 2, 1, 3).reshape(
                            c.shape)
                        ck_s, cv_s = slots(ck_s), slots(cv_s)
                j = l - n_a
                if gi == 0:
                    proj = rms_matmul(x, g_mix[l], w_in_b16[j], tn=B_COLS // 2)
                    y_main = nsa_prompt(proj, slopes, *nsa_in, b, t)
                    nsa_layout = True
                    y_mem = mem_attention(proj, B_MQ_COL // MEM_DIM, gr["mem"][l], b, t)
                else:
                    proj = rms_matmul(x, g_mix[l], w_in_bs16[j], tn=S_COLS)
                    y_main = nsa_decode(proj, slopes, page_table, ck_s, cv_s, new3, win_state, pool_slc, pg=pg)
                    y_mem = mem_attention(proj, S_MQ_COL // MEM_DIM, gr["mem"][l], b, t)
            x = proj_residual(x, wo_pairs(l, n, y_main, y_mem, nsa_layout, b, t))
            x = ffn(x, g_ffn[l], w_gu16[l], w_dn16[l])
            gr["x"] = x

    y_prompt = rmsnorm_rows(groups[0]["x"], g_final).reshape(bp, tp, d)
    y_sample = rmsnorm_rows(groups[1]["x"], g_final).reshape(bs, ts, d)
    conv_state_p = jnp.stack(conv_out[0])
    conv_state_s = jnp.stack(conv_out[1])
    mem_kv_out = mem_kv_p.reshape(depth, bp, n_mem, 2, MEM_HEADS, HEAD_DIM)
    def branch_rows(kv2d, br, b, t):
        return kv2d[:, br * 2 * KV_DIM:(br + 1) * 2 * KV_DIM].reshape(b, t, 2, NSA_KV_HEADS, HEAD_DIM)

    kvp, kvs = kv_rows
    win_kv_p = branch_rows(kvp, 2, bp, tp)[:, tp - min(WINDOW, tp):]
    win_kv_s = jnp.concatenate([state_win_kv, branch_rows(kvs, 2, bs, ts)], axis=1)[:, ts:]
    return (y_prompt, y_sample, conv_state_p, conv_state_s, mem_kv_out, branch_rows(kvp, 0, bp, tp),
            branch_rows(kvp, 1, bp, tp), win_kv_p, branch_rows(kvs, 0, bs, ts), branch_rows(kvs, 1, bs, ts), win_kv_s)
```

```python
import functools
import math

import numpy as np
import jax
import jax.numpy as jnp
from jax import lax
from jax.experimental import pallas as pl
from jax.experimental.pallas import tpu as pltpu

F32 = jnp.float32
BF16 = jnp.bfloat16

HEAD_DIM = 64
MEM_HEADS = 4
MEM_DIM = MEM_HEADS * HEAD_DIM
NSA_KV_HEADS = 4
NSA_GROUP = 3
NSA_HEADS = NSA_KV_HEADS * NSA_GROUP
NSA_DIM = NSA_HEADS * HEAD_DIM
KV_DIM = NSA_KV_HEADS * HEAD_DIM
CONV_WIDTH = 3
CMP_STRIDE = 16
CMP_LEN = 32
SEL_BLOCK = 64
N_SEL = 16
WINDOW = 512
Q_BLOCK = 64
RMS_EPS = 1e-6
NEG_INF = -1e30
FORCE_SCORE = 1e4
ATT_SCALE = HEAD_DIM ** -0.5

VMEM_LIMIT = 48 * 1024 * 1024


def _cparams(*sem):
    return pltpu.CompilerParams(dimension_semantics=sem, vmem_limit_bytes=VMEM_LIMIT)


def _row_tile(n, want):
    t = min(n, want)
    while n % t:
        t //= 2
    return t


def _rms_matmul_kernel(x_ref, g_ref, w_ref, o_ref, h_ref):
    @pl.when(pl.program_id(1) == 0)
    def _():
        x = x_ref[...]
        ms = jnp.mean(x * x, axis=-1, keepdims=True)
        h_ref[...] = (x * lax.rsqrt(ms + RMS_EPS) * g_ref[...]).astype(BF16)

    o_ref[...] = jnp.dot(h_ref[...], w_ref[...], preferred_element_type=F32)


def rms_matmul(x, g, w, *, tm=1024, tn=512):
    n, d = x.shape
    c = w.shape[1]
    tm = _row_tile(n, tm)
    tn = _row_tile(c, tn)
    return pl.pallas_call(
        _rms_matmul_kernel,
        out_shape=jax.ShapeDtypeStruct((n, c), F32),
        grid=(n // tm, c // tn),
        in_specs=[
            pl.BlockSpec((tm, d), lambda i, j: (i, 0)),
            pl.BlockSpec((1, d), lambda i, j: (0, 0)),
            pl.BlockSpec((d, tn), lambda i, j: (0, j)),
        ],
        out_specs=pl.BlockSpec((tm, tn), lambda i, j: (i, j)),
        scratch_shapes=[pltpu.VMEM((tm, d), BF16)],
        compiler_params=_cparams("parallel", "arbitrary"),
        name="rms_matmul",
    )(x, g.reshape(1, d), w)


def _proj_residual_kernel(n_pairs, x_ref, *refs):
    a_refs = refs[:n_pairs]
    w_refs = refs[n_pairs:2 * n_pairs]
    o_ref = refs[2 * n_pairs]
    acc = x_ref[...]
    for a_ref, w_ref in zip(a_refs, w_refs):
        acc = acc + jnp.dot(a_ref[...].astype(BF16), w_ref[...], preferred_element_type=F32)
    o_ref[...] = acc


def proj_residual(x, pairs, *, tm=512):
    n, d = x.shape
    tm = _row_tile(n, tm)
    in_specs = [pl.BlockSpec((tm, d), lambda i: (i, 0))]
    args = [x]
    for a, blk, imap, _ in pairs:
        in_specs.append(pl.BlockSpec(blk, imap))
        args.append(a)
    for _, _, _, w in pairs:
        in_specs.append(pl.BlockSpec(w.shape, lambda i: (0, 0)))
        args.append(w)
    return pl.pallas_call(
        functools.partial(_proj_residual_kernel, len(pairs)),
        out_shape=jax.ShapeDtypeStruct((n, d), F32),
        grid=(n // tm,),
        in_specs=in_specs,
        out_specs=pl.BlockSpec((tm, d), lambda i: (i, 0)),
        compiler_params=_cparams("parallel"),
        name="proj_residual",
    )(*args)


def _ffn_kernel(x_ref, g_ref, wg_ref, wu_ref, wd_ref, o_ref, h_ref, acc_ref):
    f = pl.program_id(1)

    @pl.when(f == 0)
    def _():
        x = x_ref[...]
        ms = jnp.mean(x * x, axis=-1, keepdims=True)
        h_ref[...] = (x * lax.rsqrt(ms + RMS_EPS) * g_ref[...]).astype(BF16)
        acc_ref[...] = x

    h = h_ref[...]
    gate = jnp.dot(h, wg_ref[...], preferred_element_type=F32)
    up = jnp.dot(h, wu_ref[...], preferred_element_type=F32)
    act = (gate * jax.nn.sigmoid(gate) * up).astype(BF16)
    acc_ref[...] += jnp.dot(act, wd_ref[...], preferred_element_type=F32)

    @pl.when(f == pl.num_programs(1) - 1)
    def _():
        o_ref[...] = acc_ref[...]


def ffn(x, g, w_gu, w_dn, *, tm=512, tf=1408):
    n, d = x.shape
    dff = w_dn.shape[0]
    tm = _row_tile(n, tm)
    nf = dff // tf
    return pl.pallas_call(
        _ffn_kernel,
        out_shape=jax.ShapeDtypeStruct((n, d), F32),
        grid=(n // tm, nf),
        in_specs=[
            pl.BlockSpec((tm, d), lambda i, f: (i, 0)),
            pl.BlockSpec((1, d), lambda i, f: (0, 0)),
            pl.BlockSpec((d, tf), lambda i, f: (0, f)),
            pl.BlockSpec((d, tf), lambda i, f: (0, f + nf)),
            pl.BlockSpec((tf, d), lambda i, f: (f, 0)),
        ],
        out_specs=pl.BlockSpec((tm, d), lambda i, f: (i, 0)),
        scratch_shapes=[pltpu.VMEM((tm, d), BF16), pltpu.VMEM((tm, d), F32)],
        compiler_params=_cparams("parallel", "arbitrary"),
        name="ffn",
    )(x, g.reshape(1, d), w_gu, w_gu, w_dn)


def _rmsnorm_kernel(x_ref, g_ref, o_ref):
    x = x_ref[...]
    ms = jnp.mean(x * x, axis=-1, keepdims=True)
    o_ref[...] = x * lax.rsqrt(ms + RMS_EPS) * g_ref[...]


def rmsnorm_rows(x, g, *, tm=1024):
    n, d = x.shape
    tm = _row_tile(n, tm)
    return pl.pallas_call(
        _rmsnorm_kernel,
        out_shape=jax.ShapeDtypeStruct((n, d), F32),
        grid=(n // tm,),
        in_specs=[pl.BlockSpec((tm, d), lambda i: (i, 0)), pl.BlockSpec((1, d), lambda i: (0, 0))],
        out_specs=pl.BlockSpec((tm, d), lambda i: (i, 0)),
        compiler_params=_cparams("parallel"),
        name="rmsnorm",
    )(x, g.reshape(1, d))


def _conv_kernel(b_ref, c_ref, h_ref, cp_ref, hp_ref, st_ref, w_ref, y_ref, ns_ref):
    i = pl.program_id(1)
    u = c_ref[...] * h_ref[...]
    tt = u.shape[0]
    prev = cp_ref[...] * hp_ref[...]
    st = st_ref[0]
    first = i == 0
    p1 = jnp.where(first, st[1:2], prev[7:8])
    p2 = jnp.where(first, st[0:1], prev[6:7])
    row = lax.broadcasted_iota(jnp.int32, u.shape, 0)
    u1 = jnp.where(row == 0, p1, pltpu.roll(u, 1, 0))
    u2 = jnp.where(row == 0, p2, jnp.where(row == 1, p1, pltpu.roll(u, 2, 0)))
    w = w_ref[...]
    y = w[0:1] * u2 + w[1:2] * u1 + w[2:3] * u
    y_ref[...] = b_ref[...] * y
    ns_ref[0] = u[tt - 2:tt]


def gated_conv(proj, state, conv_w, batch, t, *, tt=512):
    c = conv_w.shape[1]
    tt = _row_tile(t, tt)
    nt = t // tt
    r8 = tt // 8

    def prev_map(col):
        return lambda b, i: (jnp.maximum((b * nt + i) * r8 - 1, 0), col)

    return pl.pallas_call(
        _conv_kernel,
        out_shape=(jax.ShapeDtypeStruct((batch * t, c), F32), jax.ShapeDtypeStruct((batch, 2, c), F32)),
        grid=(batch, nt),
        in_specs=[
            pl.BlockSpec((tt, c), lambda b, i: (b * nt + i, 0)),
            pl.BlockSpec((tt, c), lambda b, i: (b * nt + i, 1)),
            pl.BlockSpec((tt, c), lambda b, i: (b * nt + i, 2)),
            pl.BlockSpec((8, c), prev_map(1)),
            pl.BlockSpec((8, c), prev_map(2)),
            pl.BlockSpec((1, 2, c), lambda b, i: (b, 0, 0)),
            pl.BlockSpec((CONV_WIDTH, c), lambda b, i: (0, 0)),
        ],
        out_specs=(
            pl.BlockSpec((tt, c), lambda b, i: (b * nt + i, 0)),
            pl.BlockSpec((1, 2, c), lambda b, i: (b, 0, 0)),
        ),
        compiler_params=_cparams("parallel", "arbitrary"),
        name="gated_conv",
    )(proj, proj, proj, proj, proj, state, conv_w)


def _mem_attn_kernel(q_ref, kv_ref, o_ref):
    q = q_ref[...] * ATT_SCALE
    kv = kv_ref[0]
    outs = []
    for h in range(MEM_HEADS):
        qh = q[:, h * HEAD_DIM:(h + 1) * HEAD_DIM].astype(BF16)
        kh = kv[:, h * HEAD_DIM:(h + 1) * HEAD_DIM].astype(BF16)
        vh = kv[:, MEM_DIM + h * HEAD_DIM:MEM_DIM + (h + 1) * HEAD_DIM].astype(BF16)
        s = lax.dot_general(qh, kh, (((1,), (1,)), ((), ())), preferred_element_type=F32)
        m = jnp.max(s, axis=-1, keepdims=True)
        e = jnp.exp(s - m)
        p = e / jnp.sum(e, axis=-1, keepdims=True)
        outs.append(jnp.dot(p.astype(BF16), vh, preferred_element_type=F32))
    o_ref[...] = jnp.concatenate(outs, axis=-1)


def mem_attention(proj, col_block, mem_kv, batch, t, *, tt=512):
    tt = _row_tile(t, tt)
    nt = t // tt
    n_mem = mem_kv.shape[1]
    return pl.pallas_call(
        _mem_attn_kernel,
        out_shape=jax.ShapeDtypeStruct((batch * t, MEM_DIM), F32),
        grid=(batch, nt),
        in_specs=[
            pl.BlockSpec((tt, MEM_DIM), lambda b, i: (b * nt + i, col_block)),
            pl.BlockSpec((1, n_mem, 2 * MEM_DIM), lambda b, i: (b, 0, 0)),
        ],
        out_specs=pl.BlockSpec((tt, MEM_DIM), lambda b, i: (b * nt + i, 0)),
        compiler_params=_cparams("parallel", "arbitrary"),
        name="mem_attention",
    )(proj, mem_kv)


def _alibi_list(n):
    def pow2(m):
        start = 2.0 ** (-8.0 / m)
        return [start ** (i + 1) for i in range(m)]
    if n & (n - 1) == 0:
        return pow2(n)
    c = 2 ** int(math.floor(math.log2(n)))
    return pow2(c) + _alibi_list(2 * c)[0::2][: n - c]


LANES = 128
PAIR = LANES // HEAD_DIM


def _to_lane_columns(src, dst_ref, n_rows):
    for c in range(dst_ref.shape[0]):
        dst_ref[c, 0:n_rows, :] = src[0:n_rows, c * LANES:(c + 1) * LANES]


def _compress_tile(buf_ref, row0, m, pe_refs, w1_refs, w2_refs):
    outs = []
    for kv in range(2):
        pe = pe_refs[kv][...]
        lhs = []
        for p in range(NSA_KV_HEADS // PAIR):
            col = kv * (KV_DIM // LANES) + p
            pieces = [
                (buf_ref[col, pl.ds(row0 + j, m, stride=CMP_STRIDE), :] + pe[j:j + 1]).astype(BF16)
                for j in range(CMP_LEN)
            ]
            lhs.append(jnp.concatenate(pieces, axis=1))
        lhs = jnp.concatenate(lhs, axis=0)
        hid = jnp.dot(lhs, w1_refs[kv][...], preferred_element_type=F32)
        act = (hid * jax.nn.sigmoid(hid)).astype(BF16)
        out = jnp.dot(act, w2_refs[kv][...], preferred_element_type=F32)
        outs.append(jnp.concatenate([out[0:m], out[m:2 * m]], axis=1))
    return outs


def _compress_prompt_kernel(x_ref, pek_ref, pev_ref, w1k_ref, w1v_ref, w2k_ref, w2v_ref, ck_ref, cv_ref, buf_ref):
    t = x_ref.shape[0]
    _to_lane_columns(x_ref, buf_ref, t)
    buf_ref[:, t:t + CMP_STRIDE, :] = jnp.zeros((buf_ref.shape[0], CMP_STRIDE, LANES), F32)
    n_blk = t // CMP_STRIDE
    m = min(n_blk, 128)
    for s in range(n_blk // m):
        ck, cv = _compress_tile(buf_ref, s * m * CMP_STRIDE, m, (pek_ref, pev_ref), (w1k_ref, w1v_ref),
                                (w2k_ref, w2v_ref))
        ck_ref[0, s * m:(s + 1) * m, :] = ck
        cv_ref[0, s * m:(s + 1) * m, :] = cv


def _compress_weights(pe, w1, w2):
    eye = jnp.eye(PAIR, dtype=F32)
    w1p = jnp.einsum('jde,qr->jqdre', w1, eye).reshape(CMP_LEN * PAIR * HEAD_DIM, PAIR * w1.shape[2])
    w2p = jnp.einsum('ed,qr->qerd', w2, eye).reshape(PAIR * w2.shape[0], PAIR * HEAD_DIM)
    pe2 = jnp.tile(pe, (1, PAIR))
    return pe2, w1p.astype(BF16), w2p.astype(BF16)


def compress_prompt(kv_rows, batch, t, cw):
    pek, w1k, w2k, pev, w1v, w2v = cw
    n_blk = t // CMP_STRIDE
    full = lambda a: pl.BlockSpec(a.shape, lambda b: (0,) * a.ndim)
    return pl.pallas_call(
        _compress_prompt_kernel,
        out_shape=(jax.ShapeDtypeStruct((batch, n_blk, KV_DIM), F32),) * 2,
        grid=(batch,),
        in_specs=[pl.BlockSpec((t, 2 * KV_DIM), lambda b: (b, 0)), full(pek), full(pev), full(w1k), full(w1v),
                  full(w2k), full(w2v)],
        out_specs=(pl.BlockSpec((1, n_blk, KV_DIM), lambda b: (b, 0, 0)),) * 2,
        scratch_shapes=[pltpu.VMEM((2 * KV_DIM // LANES, t + CMP_STRIDE, LANES), F32)],
        compiler_params=_cparams("parallel"),
        name="compress_prompt",
    )(kv_rows, pek, pev, w1k, w1v, w2k, w2v)


N_CMP_POS = 256
N_SEL_POS = N_CMP_POS * CMP_STRIDE // SEL_BLOCK
MASK_BIG = 2.0 ** 100
M_INIT = -1e38


LOG2E = 1.4426950408889634
MASK_NONE, MASK_CAUSAL, MASK_BAND = 0, 1, 2


def _topk_mask(score):
    tq, nb = score.shape
    pad = jnp.full((tq, LANES - nb), -jnp.inf, F32)
    st = jnp.concatenate([score, pad], axis=1).T[:nb]
    sub = 8
    idx = lax.broadcasted_iota(jnp.int32, (sub, tq), 0)
    groups = [st[r:r + sub] for r in range(0, nb, sub)]
    ranks = [jnp.zeros((sub, tq), F32) for _ in groups]
    for j in range(nb):
        rj = st[j:j + 1, :]
        for gi, sg in enumerate(groups):
            if gi * sub > j:
                before = rj >= sg
            elif gi * sub + sub - 1 < j:
                before = rj > sg
            else:
                before = jnp.where(idx > j - gi * sub, (rj >= sg).astype(F32), (rj > sg).astype(F32)) > 0.0
            ranks[gi] = jnp.where(before, ranks[gi] + 1.0, ranks[gi])
    sel_t = (jnp.concatenate(ranks, axis=0) < N_SEL).astype(F32)
    blk_any = jnp.max(sel_t, axis=1, keepdims=True)
    sel_t = jnp.concatenate([sel_t, jnp.zeros((LANES - nb, tq), F32)], axis=0)
    return sel_t.T[:, :nb], blk_any


def _chunk_words(blk_any, per_chunk):
    nb = blk_any.shape[0]
    per_word = 8 * per_chunk
    bidx = lax.broadcasted_iota(jnp.int32, (nb, 1), 0)
    wgt = jnp.left_shift(1, 3 * ((bidx % per_word) // per_chunk)).astype(F32)
    words = []
    for w in range(nb // per_word):
        v = jnp.sum(jnp.where(bidx // per_word == w, blk_any * wgt, 0.0), axis=0, keepdims=True)
        words.append(v.astype(jnp.int32)[0, 0])
    return words


def _nsa_prompt_kernel(sl_ref, q_ref, gt_ref, ck_ref, cv_ref, kc_ref, vc_ref, o_ref,
                       q_scr, s0_scr, s1_scr, p0_scr, p1_scr, a0_scr, a1_scr, m_scr, l_scr, acc_scr, mb_scr, lst_ref,
                       *, mm_rows, sm_rows):
    kvh = pl.program_id(1)
    i = pl.program_id(2)
    s_scrs, p_scrs, a_scrs = (s0_scr, s1_scr), (p0_scr, p1_scr), (a0_scr, a1_scr)
    tq = q_ref.shape[0]
    kb = tq
    t_len = kc_ref.shape[2] // 2
    t0 = i * tq
    slopes = [sl_ref[kvh * NSA_GROUP + g] * LOG2E for g in range(NSA_GROUP)]
    q = q_ref[...] * (ATT_SCALE * LOG2E)
    q3 = jnp.concatenate([q[:, g * HEAD_DIM:(g + 1) * HEAD_DIM] for g in range(NSA_GROUP)], axis=0)
    q3b = q3.astype(BF16)
    nt_dims = (((1,), (1,)), ((), ()))
    t_col = t0 + lax.broadcasted_iota(jnp.int32, (tq, 1), 0)

    @pl.when(i == 0)
    def _():
        r = lax.broadcasted_iota(jnp.int32, (tq, kb), 0)
        c = lax.broadcasted_iota(jnp.int32, (tq, kb), 1)
        mb_scr[MASK_NONE] = jnp.zeros((tq, kb), F32)
        mb_scr[MASK_CAUSAL] = jnp.where(c <= r, 0.0, -MASK_BIG)
        mb_scr[MASK_BAND] = jnp.where(c >= r, 0.0, -MASK_BIG)

    m_scr[...] = jnp.full(m_scr.shape, M_INIT, F32)
    l_scr[...] = jnp.zeros(l_scr.shape, F32)
    acc_scr[...] = jnp.zeros(acc_scr.shape, F32)

    pos = lax.broadcasted_iota(jnp.int32, (1, N_CMP_POS), 1)
    blk_n = (pos % N_SEL_POS) * (N_CMP_POS // N_SEL_POS) + pos // N_SEL_POS
    c_end = blk_n * CMP_STRIDE + (CMP_LEN - 1)
    n_real = t_len // CMP_STRIDE - 1
    vis_c = (c_end <= t_col) & (blk_n < n_real)
    c_end_f = c_end.astype(F32)
    s_c = lax.dot_general(q3b, ck_ref[0, 0], nt_dims, preferred_element_type=F32)
    p_c = []
    for g in range(NSA_GROUP):
        sg = jnp.where(vis_c, s_c[g * tq:(g + 1) * tq] + slopes[g] * c_end_f, NEG_INF)
        mg = jnp.max(sg, axis=1, keepdims=True)
        eg = jnp.where(vis_c, jnp.exp2(sg - mg), 0.0)
        dg = jnp.sum(eg, axis=1, keepdims=True)
        p_c.append(eg / jnp.where(dg > 0.0, dg, 1.0))
    o_c = jnp.dot(jnp.concatenate(p_c, axis=0).astype(BF16), cv_ref[0, 0], preferred_element_type=F32)
    imp = p_c[0] + p_c[1] + p_c[2]
    imp = imp[:, :LANES] + imp[:, LANES:]
    imp = imp[:, :N_SEL_POS] + imp[:, N_SEL_POS:]

    blk = lax.broadcasted_iota(jnp.int32, (1, N_SEL_POS), 1)
    cur = t_col // SEL_BLOCK
    forced = (blk == 0) | (blk == cur) | (blk == cur - 1)
    score = jnp.where(forced, FORCE_SCORE, jnp.where(blk <= cur, imp, -jnp.inf))
    sel, blk_any = _topk_mask(score)
    notsel = (1.0 - sel).astype(BF16)
    q_scr[...] = jnp.concatenate([q3b, jnp.concatenate([notsel] * NSA_GROUP, axis=0)], axis=1)

    lane_k = lax.broadcasted_iota(jnp.int32, (1, kb), 1)
    n_win = WINDOW // kb + 1

    words = _chunk_words(blk_any, kb // SEL_BLOCK)
    n_sel = jnp.int32(0)
    for c in range(N_SEL_POS * SEL_BLOCK // kb):
        active = (((words[c // 8] >> (3 * (c % 8))) & 7) != 0) & (c <= i)
        lst_ref[n_sel] = c
        n_sel = n_sel + active.astype(jnp.int32)
    n_chunks = n_sel + n_win

    def chunk(p):
        is_sel = p < n_sel
        is_pad = p >= n_chunks
        c = lst_ref[jnp.minimum(p, n_sel - 1)]
        w = p - n_sel
        kpos0 = jnp.where(is_sel, c * kb, jnp.where(is_pad, -kb, t0 - WINDOW + w * kb))
        row0 = jnp.maximum(kpos0, 0) + jnp.where(is_sel | is_pad, 0, t_len)
        mtype = jnp.where(is_sel, jnp.where(c == i, MASK_CAUSAL, MASK_NONE),
                          jnp.where(w == 0, MASK_BAND, jnp.where(w == n_win - 1, MASK_CAUSAL, MASK_NONE)))
        return pl.multiple_of(row0, kb), kpos0, mtype, jnp.where(is_sel | is_pad, 0, 1)

    def stage_logits(c, par):
        s_scr = s_scrs[par]
        row0, kpos0, mtype, _ = chunk(c)
        k = kc_ref[0, 0, pl.ds(row0, kb), :]
        kpos_f = (kpos0 + lane_k).astype(F32)
        off = jnp.where(kpos0 >= 0, 0.0, -MASK_BIG)
        for g in range(NSA_GROUP):
            bias = slopes[g] * kpos_f + off
            for r in range(0, tq, mm_rows):
                rows = slice(g * tq + r, g * tq + r + mm_rows)
                s = lax.dot_general(q_scr[rows, :], k, nt_dims, preferred_element_type=F32)
                s_scr[rows, :] = (s + mb_scr[mtype, r:r + mm_rows, :]) + bias

    def stage_softmax(c, par):
        s_scr, p_scr, a_scr = s_scrs[par], p_scrs[par], a_scrs[par]
        st = chunk(c)[3]
        for r in range(0, NSA_GROUP * tq, sm_rows):
            rows = slice(r, r + sm_rows)
            s = s_scr[rows, :]
            m_prev = m_scr[st, rows, :]
            m_new = jnp.maximum(m_prev, jnp.max(s, axis=1, keepdims=True))
            alpha = jnp.exp2(m_prev - m_new)
            p = jnp.exp2(s - jnp.concatenate([m_new] * (kb // LANES), axis=1))
            l_scr[st, rows, :] = alpha * l_scr[st, rows, :] + jnp.sum(p, axis=1, keepdims=True)
            p_scr[rows, :] = p.astype(BF16)
            a_scr[rows, :] = alpha
            m_scr[st, rows, :] = m_new

    def stage_values(c, par):
        p_scr, a_scr = p_scrs[par], a_scrs[par]
        row0, _, _, st = chunk(c)
        v = vc_ref[0, 0, pl.ds(row0, kb), :]
        for r in range(0, NSA_GROUP * tq, mm_rows):
            rows = slice(r, r + mm_rows)
            pv = jnp.dot(p_scr[rows, :], v, preferred_element_type=F32)
            acc_scr[st, rows, :] = acc_scr[st, rows, :] * a_scr[rows, :HEAD_DIM] + pv

    stage_logits(0, 0)
    stage_softmax(0, 0)
    stage_logits(1, 1)

    def pipe_body(j, carry):
        c = 2 * j
        stage_values(c - 2, 0)
        stage_softmax(c - 1, 1)
        stage_logits(c, 0)
        stage_values(c - 1, 1)
        stage_softmax(c, 0)
        stage_logits(c + 1, 1)
        return carry

    n_even = n_chunks + n_chunks % 2
    lax.fori_loop(1, n_even // 2, pipe_body, 0)
    stage_values(n_even - 2, 0)
    stage_softmax(n_even - 1, 1)
    stage_values(n_even - 1, 1)
    o_s = acc_scr[0] / l_scr[0][:, :HEAD_DIM]
    o_w = acc_scr[1] / l_scr[1][:, :HEAD_DIM]

    gates = jax.nn.sigmoid(gt_ref[...])
    outs = []
    for g in range(NSA_GROUP):
        rows = slice(g * tq, (g + 1) * tq)
        outs.append(gates[:, 3 * g:3 * g + 1] * o_c[rows] + gates[:, 3 * g + 1:3 * g + 2] * o_s[rows]
                    + gates[:, 3 * g + 2:3 * g + 3] * o_w[rows])
    o_ref[0, 0] = jnp.concatenate(outs, axis=1)


def nsa_prompt(proj, slopes, ckp, cvp, kcat, vcat, batch, t, *, tq=256, mm_rows=256, sm_rows=256):
    assert t % tq == 0 and WINDOW % tq == 0 and WINDOW >= tq
    nt = t // tq
    rows = NSA_GROUP * tq
    gcol = B_GATE_COL // LANES
    grid_spec = pltpu.PrefetchScalarGridSpec(
        num_scalar_prefetch=1,
        grid=(batch, NSA_KV_HEADS, nt),
        in_specs=[
            pl.BlockSpec((tq, Q_SLOT), lambda b, h, i, sl: (b * nt + i, h)),
            pl.BlockSpec((tq, LANES), lambda b, h, i, sl: (b * nt + i, gcol + h)),
            pl.BlockSpec((1, 1, N_CMP_POS, HEAD_DIM), lambda b, h, i, sl: (b, h, 0, 0)),
            pl.BlockSpec((1, 1, N_CMP_POS, HEAD_DIM), lambda b, h, i, sl: (b, h, 0, 0)),
            pl.BlockSpec((1, 1, 2 * t, LANES), lambda b, h, i, sl: (b, h, 0, 0)),
            pl.BlockSpec((1, 1, 2 * t, HEAD_DIM), lambda b, h, i, sl: (b, h, 0, 0)),
        ],
        out_specs=pl.BlockSpec((1, 1, tq, NSA_GROUP * HEAD_DIM), lambda b, h, i, sl: (b, h, i, 0)),
        scratch_shapes=[
            pltpu.VMEM((rows, LANES), BF16),
            pltpu.VMEM((rows, tq), F32), pltpu.VMEM((rows, tq), F32),
            pltpu.VMEM((rows, tq), BF16), pltpu.VMEM((rows, tq), BF16),
            pltpu.VMEM((rows, LANES), F32), pltpu.VMEM((rows, LANES), F32),
            pltpu.VMEM((2, rows, LANES), F32),
            pltpu.VMEM((2, rows, LANES), F32),
            pltpu.VMEM((2, rows, HEAD_DIM), F32),
            pltpu.VMEM((3, tq, tq), F32),
            pltpu.SMEM((N_SEL_POS * SEL_BLOCK // tq,), jnp.int32),
        ],
    )
    return pl.pallas_call(
        functools.partial(_nsa_prompt_kernel, mm_rows=mm_rows, sm_rows=sm_rows),
        out_shape=jax.ShapeDtypeStruct((batch, NSA_KV_HEADS, t, NSA_GROUP * HEAD_DIM), F32),
        grid_spec=grid_spec,
        compiler_params=_cparams("parallel", "parallel", "arbitrary"),
        name="nsa_prompt",
    )(slopes, proj, proj, ckp, cvp, kcat, vcat)


def _kv_layout_kernel(x_ref, kc_ref, vc_ref):
    tm = x_ref.shape[0]
    pos = pl.program_id(1) * tm + lax.broadcasted_iota(jnp.int32, (tm, N_SEL_POS), 0)
    blk = lax.broadcasted_iota(jnp.int32, (tm, N_SEL_POS), 1)
    onehot = jnp.where(pos // SEL_BLOCK == blk, -MASK_BIG, 0.0).astype(BF16)
    zeros = jnp.zeros((tm, N_SEL_POS), BF16)
    for h in range(NSA_KV_HEADS):
        head = lambda c0: x_ref[:, c0 + h * HEAD_DIM:c0 + (h + 1) * HEAD_DIM].astype(BF16)
        kc_ref[0, h, 0] = jnp.concatenate([head(2 * KV_DIM), onehot], axis=1)
        kc_ref[0, h, 1] = jnp.concatenate([head(4 * KV_DIM), zeros], axis=1)
        vc_ref[0, h, 0] = head(3 * KV_DIM)
        vc_ref[0, h, 1] = head(5 * KV_DIM)


def kv_layout(kv_rows, batch, t, *, tm=512):
    tm = _row_tile(t, tm)
    nt = t // tm
    kc, vc = pl.pallas_call(
        _kv_layout_kernel,
        out_shape=(jax.ShapeDtypeStruct((batch, NSA_KV_HEADS, 2, t, LANES), BF16),
                   jax.ShapeDtypeStruct((batch, NSA_KV_HEADS, 2, t, HEAD_DIM), BF16)),
        grid=(batch, nt),
        in_specs=[pl.BlockSpec((tm, kv_rows.shape[1]), lambda b, i: (b * nt + i, 0))],
        out_specs=(pl.BlockSpec((1, NSA_KV_HEADS, 2, tm, LANES), lambda b, i: (b, 0, 0, i, 0)),
                   pl.BlockSpec((1, NSA_KV_HEADS, 2, tm, HEAD_DIM), lambda b, i: (b, 0, 0, i, 0))),
        compiler_params=_cparams("parallel", "parallel"),
        name="kv_layout",
    )(kv_rows)
    return (kc.reshape(batch, NSA_KV_HEADS, 2 * t, LANES), vc.reshape(batch, NSA_KV_HEADS, 2 * t, HEAD_DIM))


def _cmp_slots(c, batch):
    n_blk = c.shape[1]
    per = N_CMP_POS // N_SEL_POS
    c = jnp.pad(c, ((0, 0), (0, N_CMP_POS - n_blk), (0, 0)))
    c = c.reshape(batch, N_SEL_POS, per, NSA_KV_HEADS, HEAD_DIM).transpose(0, 3, 2, 1, 4)
    return c.reshape(batch, NSA_KV_HEADS, N_CMP_POS, HEAD_DIM).astype(BF16)


def _compress_sample_kernel(pt_ref, pool_ref, new_ref, pek_ref, pev_ref, w1k_ref, w1v_ref, w2k_ref, w2v_ref,
                            ck_ref, cv_ref, buf_ref, sem_ref, col_ref, *, n_pages, pg):
    b = pl.program_id(0)
    g = pl.program_id(1)
    ng = n_pages // pg
    step = b * ng + g
    n_steps = pl.num_programs(0) * ng
    slot = step % 2
    page_rows = buf_ref.shape[3]
    rows = pg * page_rows
    ts = new_ref.shape[1]

    def pages(st, sl, go):
        bb = st // ng
        gg = st % ng

        def copy(k, dst):
            cp = pltpu.make_async_copy(pool_ref.at[pt_ref[bb * n_pages + gg * pg + k]], buf_ref.at[sl, dst],
                                       sem_ref.at[sl])
            cp.start() if go == "start" else cp.wait()

        for k in range(pg):
            copy(k, k)

        @pl.when(gg < ng - 1)
        def _():
            copy(pg, pg)

    @pl.when(step == 0)
    def _():
        pages(0, 0, "start")

    @pl.when(step + 1 < n_steps)
    def _():
        pages(step + 1, 1 - slot, "start")

    pages(step, slot, "wait")
    n_col = col_ref.shape[0]
    for k in range(pg):
        for c in range(n_col):
            col_ref[c, k * page_rows:(k + 1) * page_rows, :] = buf_ref[slot, k, c * LANES:(c + 1) * LANES, :].T

    @pl.when(g < ng - 1)
    def _():
        for c in range(n_col):
            col_ref[c, rows:rows + CMP_STRIDE, :] = buf_ref[slot, pg, c * LANES:(c + 1) * LANES, :].T[0:CMP_STRIDE]

    @pl.when(g == ng - 1)
    def _():
        for c in range(n_col):
            col_ref[c, rows:rows + ts, :] = new_ref[0, :, c * LANES:(c + 1) * LANES]
            col_ref[c, rows + ts:rows + CMP_STRIDE, :] = jnp.zeros((CMP_STRIDE - ts, LANES), F32)

    ck, cv = _compress_tile(col_ref, 0, rows // CMP_STRIDE, (pek_ref, pev_ref), (w1k_ref, w1v_ref),
                            (w2k_ref, w2v_ref))
    ck_ref[0] = ck
    cv_ref[0] = cv


def compress_sample(page_table, pool_t, new_rows, cw, *, pg):
    pek, w1k, w2k, pev, w1v, w2v = cw
    batch, n_pages = page_table.shape
    page_rows = pool_t.shape[2]
    ts = new_rows.shape[1]
    ng = n_pages // pg
    m = pg * page_rows // CMP_STRIDE
    full = lambda a: pl.BlockSpec(a.shape, lambda b, g, pt: (0,) * a.ndim)
    grid_spec = pltpu.PrefetchScalarGridSpec(
        num_scalar_prefetch=1,
        grid=(batch, ng),
        in_specs=[pl.BlockSpec(memory_space=pl.ANY),
                  pl.BlockSpec((1, ts, new_rows.shape[2]), lambda b, g, pt: (b, 0, 0)),
                  full(pek), full(pev), full(w1k), full(w1v), full(w2k), full(w2v)],
        out_specs=(pl.BlockSpec((1, m, KV_DIM), lambda b, g, pt: (b, g, 0)),) * 2,
        scratch_shapes=[pltpu.VMEM((2, pg + 1, 2 * KV_DIM, page_rows), F32), pltpu.SemaphoreType.DMA((2,)),
                        pltpu.VMEM((2 * KV_DIM // LANES, pg * page_rows + CMP_STRIDE, LANES), F32)],
    )
    return pl.pallas_call(
        functools.partial(_compress_sample_kernel, n_pages=n_pages, pg=pg),
        out_shape=(jax.ShapeDtypeStruct((batch, ng * m, KV_DIM), F32),) * 2,
        grid_spec=grid_spec,
        compiler_params=_cparams("arbitrary", "arbitrary"),
        name="compress_sample",
    )(page_table.reshape(-1), pool_t, new_rows, pek, pev, w1k, w1v, w2k, w2v)


S_COL_G = 32


def _softmax_lanes(pieces):
    m = functools.reduce(jnp.maximum, [jnp.max(s, axis=1, keepdims=True) for s in pieces])
    es = [jnp.where(s > 0.5 * NEG_INF, jnp.exp(s - m), 0.0) for s in pieces]
    d = functools.reduce(jnp.add, [jnp.sum(e, axis=1, keepdims=True) for e in es])
    inv = 1.0 / jnp.where(d > 0.0, d, 1.0)
    return [e * inv for e in es]


def _nsa_decode_kernel(sl_ref, pt_ref, proj_ref, ck_ref, cv_ref, new_ref, win_ref, e_ref, pool_ref, y_ref,
                       buf_ref, sem_ref, q_scr, m_scr, l_scr, acc_scr, oc_scr, ow_scr,
                       *, n_pages, pg, past):
    b = pl.program_id(0)
    g = pl.program_id(1)
    ng = n_pages // pg
    step = b * ng + g
    n_steps = pl.num_programs(0) * ng
    slot = step % 2
    page_rows = buf_ref.shape[3]
    rows = pg * page_rows
    ts = proj_ref.shape[0]
    nt_dims = (((1,), (1,)), ((), ()))

    def pages(st, sl, go):
        bb = st // ng
        gg = st % ng
        for k in range(pg):
            cp = pltpu.make_async_copy(pool_ref.at[pt_ref[bb * n_pages + gg * pg + k]], buf_ref.at[sl, k],
                                       sem_ref.at[sl])
            cp.start() if go == "start" else cp.wait()

    @pl.when(step == 0)
    def _():
        pages(0, 0, "start")

    @pl.when(step + 1 < n_steps)
    def _():
        pages(step + 1, 1 - slot, "start")

    row = lax.broadcasted_iota(jnp.int32, (LANES, 1), 0)
    row_g = row // S_COL_G
    row_h = (row % S_COL_G) // ts
    t_col = past + row % ts
    slope_col = jnp.zeros((LANES, 1), F32)
    for h in range(NSA_HEADS):
        slope_col = jnp.where((row_h == h // NSA_GROUP) & (row_g == h % NSA_GROUP), sl_ref[h], slope_col)
    n_sb = past // SEL_BLOCK + 1
    zpad = jnp.zeros((LANES - ts, KV_DIM), F32)
    lane_new = lax.broadcasted_iota(jnp.int32, (1, LANES), 1)
    kp_new = past + lane_new
    vis_new = (kp_new <= t_col) & (lane_new < ts)

    @pl.when(g == 0)
    def _():
        lane_h = lax.broadcasted_iota(jnp.int32, (ts, KV_DIM), 1) // HEAD_DIM
        tiles = []
        for gg in range(NSA_GROUP):
            qg = proj_ref[:, gg * KV_DIM:(gg + 1) * KV_DIM] * ATT_SCALE
            for h in range(NSA_KV_HEADS):
                tiles.append(jnp.where(lane_h == h, qg, 0.0))
        tiles.append(jnp.zeros((LANES - NSA_HEADS * ts, KV_DIM), F32))
        bdq = jnp.concatenate(tiles, axis=0).astype(BF16)
        q_scr[:, 0:KV_DIM] = bdq

        n_c = ck_ref.shape[1]
        per = SEL_BLOCK // CMP_STRIDE
        n_q = n_c // per
        s_c = lax.dot_general(bdq, ck_ref[0].astype(BF16), nt_dims, preferred_element_type=F32)
        slot_c = lax.broadcasted_iota(jnp.int32, (1, n_c), 1)
        c_end = ((slot_c % n_q) * per + slot_c // n_q) * CMP_STRIDE + (CMP_LEN - 1)
        dist = t_col - c_end
        s_c = jnp.where(dist >= 0, s_c - slope_col * dist.astype(F32), NEG_INF)
        (p_c,) = _softmax_lanes([s_c])
        oc_scr[...] = jnp.dot(p_c.astype(BF16), cv_ref[0].astype(BF16), preferred_element_type=F32)
        imp = p_c[0:S_COL_G] + p_c[S_COL_G:2 * S_COL_G] + p_c[2 * S_COL_G:3 * S_COL_G]
        imp = functools.reduce(jnp.add, [imp[:, r * n_q:(r + 1) * n_q] for r in range(per)])
        n_blk = q_scr.shape[1] - KV_DIM
        imp = jnp.concatenate([imp, jnp.zeros((S_COL_G, n_blk - n_q), F32)], axis=1)
        blk = lax.broadcasted_iota(jnp.int32, (1, n_blk), 1)
        cur = t_col[0:S_COL_G] // SEL_BLOCK
        forced = (blk == 0) | (blk == cur) | (blk == cur - 1)
        score = jnp.where(forced, FORCE_SCORE, jnp.where((blk <= cur) & (blk < n_sb), imp, -jnp.inf))
        rank = jnp.zeros(score.shape, F32)
        for j in range(n_sb):
            rj = score[:, j:j + 1]
            rank = rank + jnp.where(blk > j, (rj >= score).astype(F32), (rj > score).astype(F32))
        notsel = (rank >= N_SEL).astype(F32)
        notsel = jnp.concatenate([notsel] * NSA_GROUP + [jnp.ones((LANES - NSA_GROUP * S_COL_G, n_blk), F32)], axis=0)
        q_scr[:, KV_DIM:KV_DIM + n_blk] = notsel.astype(BF16)

        wl = win_ref.shape[2]
        kw_new = jnp.concatenate([new_ref[0, :, 4 * KV_DIM:5 * KV_DIM], zpad], axis=0).astype(BF16)
        vw_new = jnp.concatenate([new_ref[0, :, 5 * KV_DIM:6 * KV_DIM], zpad], axis=0).astype(BF16)
        s_w = jnp.dot(bdq, win_ref[0, 0:KV_DIM, :].astype(BF16), preferred_element_type=F32)
        w_pos = past - wl + lax.broadcasted_iota(jnp.int32, (1, wl), 1)
        dist_w = t_col - w_pos
        s_w = jnp.where((dist_w >= 0) & (dist_w <= WINDOW) & (w_pos >= 0), s_w - slope_col * dist_w.astype(F32),
                        NEG_INF)
        s_wn = lax.dot_general(bdq, kw_new, nt_dims, preferred_element_type=F32)
        dist_n = t_col - kp_new
        s_wn = jnp.where(vis_new & (dist_n <= WINDOW), s_wn - slope_col * dist_n.astype(F32), NEG_INF)
        p_w, p_wn = _softmax_lanes([s_w, s_wn])
        ow_scr[...] = (lax.dot_general(p_w.astype(BF16), win_ref[0, KV_DIM:2 * KV_DIM, :].astype(BF16), nt_dims,
                                       preferred_element_type=F32)
                       + jnp.dot(p_wn.astype(BF16), vw_new, preferred_element_type=F32))
        m_scr[...] = jnp.full(m_scr.shape, M_INIT, F32)
        l_scr[...] = jnp.zeros(l_scr.shape, F32)
        acc_scr[...] = jnp.zeros(acc_scr.shape, F32)

    def online_update(s, pv_fn):
        n = s.shape[1]
        m_prev = m_scr[...]
        m_new = jnp.maximum(m_prev, jnp.max(s, axis=1, keepdims=True))
        alpha = jnp.exp(m_prev - m_new)
        p = jnp.exp(s - jnp.concatenate([m_new] * (n // LANES), axis=1))
        l_scr[...] = alpha * l_scr[...] + jnp.sum(p, axis=1, keepdims=True)
        acc_scr[...] = acc_scr[...] * jnp.concatenate([alpha] * (KV_DIM // LANES), axis=1) + pv_fn(p.astype(BF16))
        m_scr[...] = m_new

    pages(step, slot, "wait")
    kt = jnp.concatenate([buf_ref[slot, k, 0:KV_DIM, :] for k in range(pg)], axis=1).astype(BF16)
    vt = jnp.concatenate([buf_ref[slot, k, KV_DIM:2 * KV_DIM, :] for k in range(pg)], axis=1).astype(BF16)
    n_blk = q_scr.shape[1] - KV_DIM
    s_g = (jnp.dot(q_scr[:, 0:KV_DIM], kt, preferred_element_type=F32)
           + jnp.dot(q_scr[:, KV_DIM:KV_DIM + n_blk], e_ref[0], preferred_element_type=F32))
    kpos = g * rows + lax.broadcasted_iota(jnp.int32, (1, rows), 1)
    s_g = s_g - slope_col * (t_col - kpos).astype(F32)
    online_update(s_g, lambda p: lax.dot_general(p, vt, nt_dims, preferred_element_type=F32))

    @pl.when(g == ng - 1)
    def _():
        k_new = jnp.concatenate([new_ref[0, :, 2 * KV_DIM:3 * KV_DIM], zpad], axis=0).astype(BF16)
        v_new = jnp.concatenate([new_ref[0, :, 3 * KV_DIM:4 * KV_DIM], zpad], axis=0).astype(BF16)
        s_n = lax.dot_general(q_scr[:, 0:KV_DIM], k_new, nt_dims, preferred_element_type=F32)
        unsel_last = q_scr[:, KV_DIM + n_sb - 1:KV_DIM + n_sb].astype(F32)
        dist_n = t_col - kp_new
        s_n = jnp.where(vis_new & (unsel_last < 0.5), s_n - slope_col * dist_n.astype(F32), -MASK_BIG)
        online_update(s_n, lambda p: jnp.dot(p, v_new, preferred_element_type=F32))
        o_s = acc_scr[...] / jnp.concatenate([l_scr[...]] * (KV_DIM // LANES), axis=1)
        o_c = oc_scr[...]
        o_w = ow_scr[...]
        gates = jax.nn.sigmoid(proj_ref[:, (NSA_GROUP + 1) * KV_DIM:(NSA_GROUP + 1) * KV_DIM + LANES])
        outs = []
        for h in range(NSA_HEADS):
            kvh, gg = h // NSA_GROUP, h % NSA_GROUP
            r0 = gg * S_COL_G + kvh * ts
            lanes = slice(kvh * HEAD_DIM, (kvh + 1) * HEAD_DIM)
            outs.append(gates[:, 3 * h:3 * h + 1] * o_c[r0:r0 + ts, lanes]
                        + gates[:, 3 * h + 1:3 * h + 2] * o_s[r0:r0 + ts, lanes]
                        + gates[:, 3 * h + 2:3 * h + 3] * o_w[r0:r0 + ts, lanes])
        y_ref[...] = jnp.concatenate(outs, axis=1)


def nsa_decode(proj, slopes, page_table, ckp, cvp, new_rows, win_t, pool_t, *, pg):
    batch, n_pages = page_table.shape
    page_rows = pool_t.shape[2]
    ts = new_rows.shape[1]
    past = n_pages * page_rows
    ng = n_pages // pg
    rows = pg * page_rows
    n_c = ckp.shape[1]
    wl = win_t.shape[2]
    n_blk = -(-(past // SEL_BLOCK + 1) // LANES) * LANES
    key_blk = jnp.arange(past, dtype=jnp.int32).reshape(ng, 1, rows) // SEL_BLOCK
    e_tab = jnp.where(key_blk == jnp.arange(n_blk, dtype=jnp.int32)[None, :, None], -MASK_BIG, 0.0).astype(BF16)
    grid_spec = pltpu.PrefetchScalarGridSpec(
        num_scalar_prefetch=2,
        grid=(batch, ng),
        in_specs=[
            pl.BlockSpec((ts, proj.shape[1]), lambda b, g, sl, pt: (b, 0)),
            pl.BlockSpec((1, n_c, KV_DIM), lambda b, g, sl, pt: (b, 0, 0)),
            pl.BlockSpec((1, n_c, KV_DIM), lambda b, g, sl, pt: (b, 0, 0)),
            pl.BlockSpec((1, ts, new_rows.shape[2]), lambda b, g, sl, pt: (b, 0, 0)),
            pl.BlockSpec((1, 2 * KV_DIM, wl), lambda b, g, sl, pt: (b, 0, 0)),
            pl.BlockSpec((1, n_blk, rows), lambda b, g, sl, pt: (g, 0, 0)),
            pl.BlockSpec(memory_space=pl.ANY),
        ],
        out_specs=pl.BlockSpec((ts, NSA_DIM), lambda b, g, sl, pt: (b, 0)),
        scratch_shapes=[
            pltpu.VMEM((2, pg, 2 * KV_DIM, page_rows), F32), pltpu.SemaphoreType.DMA((2,)),
            pltpu.VMEM((LANES, KV_DIM + n_blk), BF16),
            pltpu.VMEM((LANES, LANES), F32), pltpu.VMEM((LANES, LANES), F32), pltpu.VMEM((LANES, KV_DIM), F32),
            pltpu.VMEM((LANES, KV_DIM), F32), pltpu.VMEM((LANES, KV_DIM), F32),
        ],
    )
    return pl.pallas_call(
        functools.partial(_nsa_decode_kernel, n_pages=n_pages, pg=pg, past=past),
        out_shape=jax.ShapeDtypeStruct((batch * ts, NSA_DIM), F32),
        grid_spec=grid_spec,
        compiler_params=_cparams("arbitrary", "arbitrary"),
        name="nsa_decode",
    )(slopes, page_table.reshape(-1), proj, ckp, cvp, new_rows, win_t, e_tab, pool_t)


S_MQ_COL =NSA_GROUP * KV_DIM
S_COLS = S_MQ_COL + MEM_DIM + LANES


def _layout_w_in_b_sample(w):
    d = w.shape[0]
    qw = w[:, :NSA_DIM].reshape(d, NSA_KV_HEADS, NSA_GROUP, HEAD_DIM).transpose(0, 2, 1, 3).reshape(d, NSA_DIM)
    gw = jnp.pad(w[:, NSA_DIM:NSA_DIM + 3 * NSA_HEADS], ((0, 0), (0, LANES - 3 * NSA_HEADS)))
    mw = w[:, NSA_DIM + 3 * NSA_HEADS:]
    return jnp.concatenate([qw, mw, gw], axis=1).astype(BF16)


Q_SLOT = 256
B_MQ_COL = NSA_KV_HEADS * Q_SLOT
B_GATE_COL = B_MQ_COL + MEM_DIM
B_COLS = B_GATE_COL + NSA_KV_HEADS * LANES


def _layout_w_in_b(w):
    d = w.shape[0]
    qw = w[:, :NSA_DIM].reshape(d, NSA_KV_HEADS, NSA_GROUP * HEAD_DIM)
    qw = jnp.pad(qw, ((0, 0), (0, 0), (0, Q_SLOT - NSA_GROUP * HEAD_DIM))).reshape(d, NSA_KV_HEADS * Q_SLOT)
    gw = w[:, NSA_DIM:NSA_DIM + 3 * NSA_HEADS].reshape(d, NSA_KV_HEADS, 3 * NSA_GROUP)
    gw = jnp.pad(gw, ((0, 0), (0, 0), (0, LANES - 3 * NSA_GROUP))).reshape(d, NSA_KV_HEADS * LANES)
    mw = w[:, NSA_DIM + 3 * NSA_HEADS:]
    return jnp.concatenate([qw, mw, gw], axis=1).astype(BF16)


def kernel(x_prompt, x_sample, state_conv, cache_mem_kv, cache_cmp_kv, cache_slc_kv, state_win_kv, page_table,
           mem_prompt, g_mix, w_in_a, conv_w, w_in_b, w_o, w_mkv, g_mem, g_kv, w_kv, pe_ck, w1_ck, w2_ck,
           pe_cv, w1_cv, w2_cv, g_ffn, w_gu, w_dn, g_final):
    bp, tp, d = x_prompt.shape
    bs, ts = x_sample.shape[:2]
    depth = g_mix.shape[0]
    n_a = w_in_a.shape[0]
    n_mem = mem_prompt.shape[1]
    win_len = state_win_kv.shape[1]
    past_len = page_table.shape[1] * cache_cmp_kv.shape[1]
    conv_dim = conv_w.shape[2]
    slopes = jnp.asarray(np.array(_alibi_list(NSA_HEADS), dtype=np.float32))

    w_in_a16 = w_in_a.astype(BF16)
    w_in_b16 = [_layout_w_in_b(w_in_b[j]) for j in range(depth - n_a)]
    w_o16 = w_o.astype(BF16)
    w_gu16 = w_gu.astype(BF16)
    w_dn16 = w_dn.astype(BF16)
    w_kv16 = w_kv.astype(BF16)
    w_mkv16 = w_mkv.transpose(1, 0, 2).reshape(d, depth * 2 * MEM_DIM).astype(BF16)

    mkv = rms_matmul(mem_prompt.reshape(bp * n_mem, d), g_mem, w_mkv16)
    mem_kv_p = mkv.reshape(bp, n_mem, depth, 2 * MEM_DIM).transpose(2, 0, 1, 3)
    mem_kv_s = cache_mem_kv.reshape(depth, bs, n_mem, 2 * MEM_DIM)

    groups = [
        dict(x=x_prompt.reshape(bp * tp, d), b=bp, t=tp, mem=mem_kv_p, st=jnp.zeros((n_a, bp, 2, conv_dim), F32)),
        dict(x=x_sample.reshape(bs * ts, d), b=bs, t=ts, mem=mem_kv_s, st=state_conv),
    ]
    conv_out = [[], []]
    kv_rows = [None, None]

    cw = _compress_weights(pe_ck, w1_ck, w2_ck) + _compress_weights(pe_cv, w1_cv, w2_cv)
    nsa_in = None
    w_in_bs16 = [_layout_w_in_b_sample(w_in_b[j]) for j in range(depth - n_a)]
    n_pages = page_table.shape[1]
    page_rows = cache_cmp_kv.shape[1]
    pg = 16 if n_pages % 16 == 0 else n_pages
    assert page_rows == LANES
    pool_cmp = cache_cmp_kv.transpose(0, 2, 3, 4, 1).reshape(cache_cmp_kv.shape[0], 2 * KV_DIM, page_rows)
    pool_slc = cache_slc_kv.transpose(0, 2, 3, 4, 1).reshape(cache_slc_kv.shape[0], 2 * KV_DIM, page_rows)
    win_state = state_win_kv.transpose(0, 2, 3, 4, 1).reshape(bs, 2 * KV_DIM, win_len)
    assert ts == 8 and past_len % SEL_BLOCK == 0 and tp % 256 == 0 and tp // SEL_BLOCK <= N_SEL_POS

    def wo_pairs(l, n, y_main, y_mem, nsa_layout, b, t):
        tm = _row_tile(n, 512)
        if nsa_layout:
            hw = NSA_GROUP * HEAD_DIM
            nt = t // tm
            y2d = y_main.reshape(b * NSA_KV_HEADS * t, hw)
            pairs = [(y2d, (tm, hw), (lambda i, h=h: (((i // nt) * NSA_KV_HEADS + h) * nt + i % nt, 0)),
                      w_o16[l, h * hw:(h + 1) * hw]) for h in range(NSA_KV_HEADS)]
            km = NSA_DIM
        else:
            km = y_main.shape[1]
            pairs = [(y_main, (tm, km), lambda i: (i, 0), w_o16[l, :km])]
        return pairs + [(y_mem, (tm, MEM_DIM), lambda i: (i, 0), w_o16[l, km:])]

    for l in range(depth):
        for gi, gr in enumerate(groups):
            x, b, t = gr["x"], gr["b"], gr["t"]
            n = b * t
            nsa_layout = False
            if l < n_a:
                proj = rms_matmul(x, g_mix[l], w_in_a16[l], tn=w_in_a16.shape[2] // 2)
                y_main, new_st = gated_conv(proj, gr["st"][l], conv_w[l], b, t)
                conv_out[gi].append(new_st)
                y_mem = mem_attention(proj, 3 * conv_dim // MEM_DIM, gr["mem"][l], b, t)
            else:
                if l == n_a:
                    kv2d = rms_matmul(x, g_kv, w_kv16, tn=3 * KV_DIM)
                    kv_rows[gi] = kv2d
                    if gi == 0:
                        ck, cv = compress_prompt(kv2d, b, t, cw)
                        nsa_in = (_cmp_slots(ck, b), _cmp_slots(cv, b)) + kv_layout(kv2d, b, t)
                    else:
                        new3 = kv2d.reshape(b, t, 6 * KV_DIM)
                        ck_s, cv_s = compress_sample(page_table, pool_cmp, new3, cw, pg=pg)
                        per = SEL_BLOCK // CMP_STRIDE
                        slots = lambda c: c.reshape(b, c.shape[1] // per, per, KV_DIM).transpose(0, 2, 1, 3).reshape(
                            c.shape)
                        ck_s, cv_s = slots(ck_s), slots(cv_s)
                j = l - n_a
                if gi == 0:
                    proj = rms_matmul(x, g_mix[l], w_in_b16[j], tn=B_COLS // 2)
                    y_main = nsa_prompt(proj, slopes, *nsa_in, b, t)
                    nsa_layout = True
                    y_mem = mem_attention(proj, B_MQ_COL // MEM_DIM, gr["mem"][l], b, t)
                else:
                    proj = rms_matmul(x, g_mix[l], w_in_bs16[j], tn=S_COLS)
                    y_main = nsa_decode(proj, slopes, page_table, ck_s, cv_s, new3, win_state, pool_slc, pg=pg)
                    y_mem = mem_attention(proj, S_MQ_COL // MEM_DIM, gr["mem"][l], b, t)
            x = proj_residual(x, wo_pairs(l, n, y_main, y_mem, nsa_layout, b, t))
            x = ffn(x, g_ffn[l], w_gu16[l], w_dn16[l])
            gr["x"] = x

    y_prompt = rmsnorm_rows(groups[0]["x"], g_final).reshape(bp, tp, d)
    y_sample = rmsnorm_rows(groups[1]["x"], g_final).reshape(bs, ts, d)
    conv_state_p = jnp.stack(conv_out[0])
    conv_state_s = jnp.stack(conv_out[1])
    mem_kv_out = mem_kv_p.reshape(depth, bp, n_mem, 2, MEM_HEADS, HEAD_DIM)
    def branch_rows(kv2d, br, b, t):
        return kv2d[:, br * 2 * KV_DIM:(br + 1) * 2 * KV_DIM].reshape(b, t, 2, NSA_KV_HEADS, HEAD_DIM)

    kvp, kvs = kv_rows
    win_kv_p = branch_rows(kvp, 2, bp, tp)[:, tp - min(WINDOW, tp):]
    win_kv_s = jnp.concatenate([state_win_kv, branch_rows(kvs, 2, bs, ts)], axis=1)[:, ts:]
    return (y_prompt, y_sample, conv_state_p, conv_state_s, mem_kv_out, branch_rows(kvp, 0, bp, tp),
            branch_rows(kvp, 1, bp, tp), win_kv_p, branch_rows(kvs, 0, bs, ts), branch_rows(kvs, 1, bs, ts), win_kv_s)
```

```python
import functools
import math

import numpy as np
import jax
import jax.numpy as jnp
from jax import lax
from jax.experimental import pallas as pl
from jax.experimental.pallas import tpu as pltpu

F32 = jnp.float32
BF16 = jnp.bfloat16

HEAD_DIM = 64
MEM_HEADS = 4
MEM_DIM = MEM_HEADS * HEAD_DIM
NSA_KV_HEADS = 4
NSA_GROUP = 3
NSA_HEADS = NSA_KV_HEADS * NSA_GROUP
NSA_DIM = NSA_HEADS * HEAD_DIM
KV_DIM = NSA_KV_HEADS * HEAD_DIM
CONV_WIDTH = 3
CMP_STRIDE = 16
CMP_LEN = 32
SEL_BLOCK = 64
N_SEL = 16
WINDOW = 512
Q_BLOCK = 64
RMS_EPS = 1e-6
NEG_INF = -1e30
FORCE_SCORE = 1e4
ATT_SCALE = HEAD_DIM ** -0.5

VMEM_LIMIT = 48 * 1024 * 1024


def _cparams(*sem):
    return pltpu.CompilerParams(dimension_semantics=sem, vmem_limit_bytes=VMEM_LIMIT)


def _row_tile(n, want):
    t = min(n, want)
    while n % t:
        t //= 2
    return t


def _rms_matmul_kernel(x_ref, g_ref, w_ref, o_ref, h_ref):
    @pl.when(pl.program_id(1) == 0)
    def _():
        x = x_ref[...]
        ms = jnp.mean(x * x, axis=-1, keepdims=True)
        h_ref[...] = (x * lax.rsqrt(ms + RMS_EPS) * g_ref[...]).astype(BF16)

    o_ref[...] = jnp.dot(h_ref[...], w_ref[...], preferred_element_type=F32)


def rms_matmul(x, g, w, *, tm=1024, tn=512):
    n, d = x.shape
    c = w.shape[1]
    tm = _row_tile(n, tm)
    tn = _row_tile(c, tn)
    return pl.pallas_call(
        _rms_matmul_kernel,
        out_shape=jax.ShapeDtypeStruct((n, c), F32),
        grid=(n // tm, c // tn),
        in_specs=[
            pl.BlockSpec((tm, d), lambda i, j: (i, 0)),
            pl.BlockSpec((1, d), lambda i, j: (0, 0)),
            pl.BlockSpec((d, tn), lambda i, j: (0, j)),
        ],
        out_specs=pl.BlockSpec((tm, tn), lambda i, j: (i, j)),
        scratch_shapes=[pltpu.VMEM((tm, d), BF16)],
        compiler_params=_cparams("parallel", "arbitrary"),
        name="rms_matmul",
    )(x, g.reshape(1, d), w)


def _proj_residual_kernel(n_pairs, x_ref, *refs):
    a_refs = refs[:n_pairs]
    w_refs = refs[n_pairs:2 * n_pairs]
    o_ref = refs[2 * n_pairs]
    acc = x_ref[...]
    for a_ref, w_ref in zip(a_refs, w_refs):
        acc = acc + jnp.dot(a_ref[...].astype(BF16), w_ref[...], preferred_element_type=F32)
    o_ref[...] = acc


def proj_residual(x, pairs, *, tm=1024):
    n, d = x.shape
    tm = _row_tile(n, tm)
    in_specs = [pl.BlockSpec((tm, d), lambda i: (i, 0))]
    args = [x]
    for a, blk, imap, _ in pairs:
        in_specs.append(pl.BlockSpec(blk, imap))
        args.append(a)
    for _, _, _, w in pairs:
        in_specs.append(pl.BlockSpec(w.shape, lambda i: (0, 0)))
        args.append(w)
    return pl.pallas_call(
        functools.partial(_proj_residual_kernel, len(pairs)),
        out_shape=jax.ShapeDtypeStruct((n, d), F32),
        grid=(n // tm,),
        in_specs=in_specs,
        out_specs=pl.BlockSpec((tm, d), lambda i: (i, 0)),
        compiler_params=_cparams("parallel"),
        name="proj_residual",
    )(*args)


def _ffn_kernel(x_ref, g_ref, wg_ref, wu_ref, wd_ref, o_ref, h_ref, acc_ref):
    f = pl.program_id(1)

    @pl.when(f == 0)
    def _():
        x = x_ref[...]
        ms = jnp.mean(x * x, axis=-1, keepdims=True)
        h_ref[...] = (x * lax.rsqrt(ms + RMS_EPS) * g_ref[...]).astype(BF16)
        acc_ref[...] = x

    h = h_ref[...]
    gate = jnp.dot(h, wg_ref[...], preferred_element_type=F32)
    up = jnp.dot(h, wu_ref[...], preferred_element_type=F32)
    act = (gate * jax.nn.sigmoid(gate) * up).astype(BF16)
    acc_ref[...] += jnp.dot(act, wd_ref[...], preferred_element_type=F32)

    @pl.when(f == pl.num_programs(1) - 1)
    def _():
        o_ref[...] = acc_ref[...]


def ffn(x, g, w_gu, w_dn, *, tm=512, tf=1408):
    n, d = x.shape
    dff = w_dn.shape[0]
    tm = _row_tile(n, tm)
    nf = dff // tf
    return pl.pallas_call(
        _ffn_kernel,
        out_shape=jax.ShapeDtypeStruct((n, d), F32),
        grid=(n // tm, nf),
        in_specs=[
            pl.BlockSpec((tm, d), lambda i, f: (i, 0)),
            pl.BlockSpec((1, d), lambda i, f: (0, 0)),
            pl.BlockSpec((d, tf), lambda i, f: (0, f)),
            pl.BlockSpec((d, tf), lambda i, f: (0, f + nf)),
            pl.BlockSpec((tf, d), lambda i, f: (f, 0)),
        ],
        out_specs=pl.BlockSpec((tm, d), lambda i, f: (i, 0)),
        scratch_shapes=[pltpu.VMEM((tm, d), BF16), pltpu.VMEM((tm, d), F32)],
        compiler_params=_cparams("parallel", "arbitrary"),
        name="ffn",
    )(x, g.reshape(1, d), w_gu, w_gu, w_dn)


def _rmsnorm_kernel(x_ref, g_ref, o_ref):
    x = x_ref[...]
    ms = jnp.mean(x * x, axis=-1, keepdims=True)
    o_ref[...] = x * lax.rsqrt(ms + RMS_EPS) * g_ref[...]


def rmsnorm_rows(x, g, *, tm=1024):
    n, d = x.shape
    tm = _row_tile(n, tm)
    return pl.pallas_call(
        _rmsnorm_kernel,
        out_shape=jax.ShapeDtypeStruct((n, d), F32),
        grid=(n // tm,),
        in_specs=[pl.BlockSpec((tm, d), lambda i: (i, 0)), pl.BlockSpec((1, d), lambda i: (0, 0))],
        out_specs=pl.BlockSpec((tm, d), lambda i: (i, 0)),
        compiler_params=_cparams("parallel"),
        name="rmsnorm",
    )(x, g.reshape(1, d))


def _conv_kernel(b_ref, c_ref, h_ref, cp_ref, hp_ref, st_ref, w_ref, y_ref, ns_ref):
    i = pl.program_id(1)
    u = c_ref[...] * h_ref[...]
    tt = u.shape[0]
    prev = cp_ref[...] * hp_ref[...]
    st = st_ref[0]
    first = i == 0
    p1 = jnp.where(first, st[1:2], prev[7:8])
    p2 = jnp.where(first, st[0:1], prev[6:7])
    row = lax.broadcasted_iota(jnp.int32, u.shape, 0)
    u1 = jnp.where(row == 0, p1, pltpu.roll(u, 1, 0))
    u2 = jnp.where(row == 0, p2, jnp.where(row == 1, p1, pltpu.roll(u, 2, 0)))
    w = w_ref[...]
    y = w[0:1] * u2 + w[1:2] * u1 + w[2:3] * u
    y_ref[...] = b_ref[...] * y
    ns_ref[0] = u[tt - 2:tt]


def gated_conv(proj, state, conv_w, batch, t, *, tt=512):
    c = conv_w.shape[1]
    tt = _row_tile(t, tt)
    nt = t // tt
    r8 = tt // 8

    def prev_map(col):
        return lambda b, i: (jnp.maximum((b * nt + i) * r8 - 1, 0), col)

    return pl.pallas_call(
        _conv_kernel,
        out_shape=(jax.ShapeDtypeStruct((batch * t, c), F32), jax.ShapeDtypeStruct((batch, 2, c), F32)),
        grid=(batch, nt),
        in_specs=[
            pl.BlockSpec((tt, c), lambda b, i: (b * nt + i, 0)),
            pl.BlockSpec((tt, c), lambda b, i: (b * nt + i, 1)),
            pl.BlockSpec((tt, c), lambda b, i: (b * nt + i, 2)),
            pl.BlockSpec((8, c), prev_map(1)),
            pl.BlockSpec((8, c), prev_map(2)),
            pl.BlockSpec((1, 2, c), lambda b, i: (b, 0, 0)),
            pl.BlockSpec((CONV_WIDTH, c), lambda b, i: (0, 0)),
        ],
        out_specs=(
            pl.BlockSpec((tt, c), lambda b, i: (b * nt + i, 0)),
            pl.BlockSpec((1, 2, c), lambda b, i: (b, 0, 0)),
        ),
        compiler_params=_cparams("parallel", "arbitrary"),
        name="gated_conv",
    )(proj, proj, proj, proj, proj, state, conv_w)


def _mem_attn_kernel(q_ref, kv_ref, o_ref):
    nb = kv_ref.shape[0]
    tt = q_ref.shape[0] // nb
    for i in range(nb):
        q = q_ref[i * tt:(i + 1) * tt, :] * ATT_SCALE
        kv = kv_ref[i]
        outs = []
        for h in range(MEM_HEADS):
            qh = q[:, h * HEAD_DIM:(h + 1) * HEAD_DIM].astype(BF16)
            kh = kv[:, h * HEAD_DIM:(h + 1) * HEAD_DIM].astype(BF16)
            vh = kv[:, MEM_DIM + h * HEAD_DIM:MEM_DIM + (h + 1) * HEAD_DIM].astype(BF16)
            s = lax.dot_general(qh, kh, (((1,), (1,)), ((), ())), preferred_element_type=F32)
            m = jnp.max(s, axis=-1, keepdims=True)
            e = jnp.exp(s - m)
            p = e / jnp.sum(e, axis=-1, keepdims=True)
            outs.append(jnp.dot(p.astype(BF16), vh, preferred_element_type=F32))
        o_ref[i * tt:(i + 1) * tt, :] = jnp.concatenate(outs, axis=-1)


def mem_attention(proj, col_block, mem_kv, batch, t, *, tt=1024, group=4):
    tt = _row_tile(t, tt)
    nt = t // tt
    nb = group if (nt == 1 and batch % group == 0) else 1
    n_mem = mem_kv.shape[1]
    return pl.pallas_call(
        _mem_attn_kernel,
        out_shape=jax.ShapeDtypeStruct((batch * t, MEM_DIM), F32),
        grid=(batch // nb, nt),
        in_specs=[
            pl.BlockSpec((nb * tt, MEM_DIM), lambda b, i: (b * nt + i, col_block)),
            pl.BlockSpec((nb, n_mem, 2 * MEM_DIM), lambda b, i: (b, 0, 0)),
        ],
        out_specs=pl.BlockSpec((nb * tt, MEM_DIM), lambda b, i: (b * nt + i, 0)),
        compiler_params=_cparams("parallel", "arbitrary"),
        name="mem_attention",
    )(proj, mem_kv)


def _alibi_list(n):
    def pow2(m):
        start = 2.0 ** (-8.0 / m)
        return [start ** (i + 1) for i in range(m)]
    if n & (n - 1) == 0:
        return pow2(n)
    c = 2 ** int(math.floor(math.log2(n)))
    return pow2(c) + _alibi_list(2 * c)[0::2][: n - c]


LANES = 128
PAIR = LANES // HEAD_DIM


CMP_PITCH = CMP_STRIDE + 1


def _to_lane_columns(src, dst_ref, n_rows):
    for c in range(dst_ref.shape[0]):
        dst_ref[c, 0:n_rows, :] = src[0:n_rows, c * LANES:(c + 1) * LANES]


def _compress_tile(buf_ref, row0, m, pe_refs, w1_refs, w2_refs, pitch=CMP_STRIDE):
    outs = []
    base = row0 // CMP_STRIDE * pitch
    for kv in range(2):
        pe = pe_refs[kv][...]
        lhs = []
        for p in range(NSA_KV_HEADS // PAIR):
            col = kv * (KV_DIM // LANES) + p
            pieces = [
                (buf_ref[col, pl.ds(base + j // CMP_STRIDE * pitch + j % CMP_STRIDE, m, stride=pitch), :]
                 + pe[j:j + 1]).astype(BF16)
                for j in range(CMP_LEN)
            ]
            lhs.append(jnp.concatenate(pieces, axis=1))
        lhs = jnp.concatenate(lhs, axis=0)
        hid = jnp.dot(lhs, w1_refs[kv][...], preferred_element_type=F32)
        act = (hid * jax.nn.sigmoid(hid)).astype(BF16)
        out = jnp.dot(act, w2_refs[kv][...], preferred_element_type=F32)
        outs.append(jnp.concatenate([out[0:m], out[m:2 * m]], axis=1))
    return outs


def _compress_prompt_kernel(x_ref, pek_ref, pev_ref, w1k_ref, w1v_ref, w2k_ref, w2v_ref, ck_ref, cv_ref, buf_ref):
    t = x_ref.shape[0]
    _to_lane_columns(x_ref, buf_ref, t)
    buf_ref[:, t:t + CMP_STRIDE, :] = jnp.zeros((buf_ref.shape[0], CMP_STRIDE, LANES), F32)
    n_blk = t // CMP_STRIDE
    m = min(n_blk, 128)
    for s in range(n_blk // m):
        ck, cv = _compress_tile(buf_ref, s * m * CMP_STRIDE, m, (pek_ref, pev_ref), (w1k_ref, w1v_ref),
                                (w2k_ref, w2v_ref))
        ck_ref[0, s * m:(s + 1) * m, :] = ck
        cv_ref[0, s * m:(s + 1) * m, :] = cv


def _compress_weights(pe, w1, w2):
    eye = jnp.eye(PAIR, dtype=F32)
    w1p = jnp.einsum('jde,qr->jqdre', w1, eye).reshape(CMP_LEN * PAIR * HEAD_DIM, PAIR * w1.shape[2])
    w2p = jnp.einsum('ed,qr->qerd', w2, eye).reshape(PAIR * w2.shape[0], PAIR * HEAD_DIM)
    pe2 = jnp.tile(pe, (1, PAIR))
    return pe2, w1p.astype(BF16), w2p.astype(BF16)


def compress_prompt(kv_rows, batch, t, cw):
    pek, w1k, w2k, pev, w1v, w2v = cw
    n_blk = t // CMP_STRIDE
    full = lambda a: pl.BlockSpec(a.shape, lambda b: (0,) * a.ndim)
    return pl.pallas_call(
        _compress_prompt_kernel,
        out_shape=(jax.ShapeDtypeStruct((batch, n_blk, KV_DIM), F32),) * 2,
        grid=(batch,),
        in_specs=[pl.BlockSpec((t, 2 * KV_DIM), lambda b: (b, 0)), full(pek), full(pev), full(w1k), full(w1v),
                  full(w2k), full(w2v)],
        out_specs=(pl.BlockSpec((1, n_blk, KV_DIM), lambda b: (b, 0, 0)),) * 2,
        scratch_shapes=[pltpu.VMEM((2 * KV_DIM // LANES, t + CMP_STRIDE, LANES), F32)],
        compiler_params=_cparams("parallel"),
        name="compress_prompt",
    )(kv_rows, pek, pev, w1k, w1v, w2k, w2v)


N_CMP_POS = 256
N_SEL_POS = N_CMP_POS * CMP_STRIDE // SEL_BLOCK
MASK_BIG = 2.0 ** 100
M_INIT = -1e38


LOG2E = 1.4426950408889634
MASK_NONE, MASK_CAUSAL, MASK_BAND = 0, 1, 2


def _topk_mask(score):
    tq, nb = score.shape
    pad = jnp.full((tq, LANES - nb), -jnp.inf, F32)
    st = jnp.concatenate([score, pad], axis=1).T[:nb]
    sub = 8
    idx = lax.broadcasted_iota(jnp.int32, (sub, tq), 0)
    groups = [st[r:r + sub] for r in range(0, nb, sub)]
    ranks = [jnp.zeros((sub, tq), F32) for _ in groups]
    for j in range(nb):
        rj = st[j:j + 1, :]
        for gi, sg in enumerate(groups):
            if gi * sub > j:
                before = rj >= sg
            elif gi * sub + sub - 1 < j:
                before = rj > sg
            else:
                before = jnp.where(idx > j - gi * sub, (rj >= sg).astype(F32), (rj > sg).astype(F32)) > 0.0
            ranks[gi] = jnp.where(before, ranks[gi] + 1.0, ranks[gi])
    sel_t = (jnp.concatenate(ranks, axis=0) < N_SEL).astype(F32)
    blk_any = jnp.max(sel_t, axis=1, keepdims=True)
    sel_t = jnp.concatenate([sel_t, jnp.zeros((LANES - nb, tq), F32)], axis=0)
    return sel_t.T[:, :nb], blk_any


def _chunk_words(blk_any, per_chunk):
    nb = blk_any.shape[0]
    per_word = 8 * per_chunk
    bidx = lax.broadcasted_iota(jnp.int32, (nb, 1), 0)
    wgt = jnp.left_shift(1, 3 * ((bidx % per_word) // per_chunk)).astype(F32)
    words = []
    for w in range(nb // per_word):
        v = jnp.sum(jnp.where(bidx // per_word == w, blk_any * wgt, 0.0), axis=0, keepdims=True)
        words.append(v.astype(jnp.int32)[0, 0])
    return words


def _nsa_prompt_kernel(sl_ref, q_ref, gt_ref, ck_ref, cv_ref, kc_ref, vc_ref, o_ref,
                       q_scr, s0_scr, s1_scr, p0_scr, p1_scr, a0_scr, a1_scr, m_scr, l_scr, acc_scr, mb_scr, lst_ref,
                       *, mm_rows, sm_rows):
    kvh = pl.program_id(1)
    i = pl.program_id(2)
    s_scrs, p_scrs, a_scrs = (s0_scr, s1_scr), (p0_scr, p1_scr), (a0_scr, a1_scr)
    tq = q_ref.shape[0]
    kb = tq
    t_len = kc_ref.shape[2] // 2
    t0 = i * tq
    slopes = [sl_ref[kvh * NSA_GROUP + g] * LOG2E for g in range(NSA_GROUP)]
    q = q_ref[...] * (ATT_SCALE * LOG2E)
    q3 = jnp.concatenate([q[:, g * HEAD_DIM:(g + 1) * HEAD_DIM] for g in range(NSA_GROUP)], axis=0)
    q3b = q3.astype(BF16)
    nt_dims = (((1,), (1,)), ((), ()))
    t_col = t0 + lax.broadcasted_iota(jnp.int32, (tq, 1), 0)

    @pl.when(i == 0)
    def _():
        r = lax.broadcasted_iota(jnp.int32, (tq, kb), 0)
        c = lax.broadcasted_iota(jnp.int32, (tq, kb), 1)
        mb_scr[MASK_NONE] = jnp.zeros((tq, kb), F32)
        mb_scr[MASK_CAUSAL] = jnp.where(c <= r, 0.0, -MASK_BIG)
        mb_scr[MASK_BAND] = jnp.where(c >= r, 0.0, -MASK_BIG)

    m_scr[...] = jnp.full(m_scr.shape, M_INIT, F32)
    l_scr[...] = jnp.zeros(l_scr.shape, F32)
    acc_scr[...] = jnp.zeros(acc_scr.shape, F32)

    pos = lax.broadcasted_iota(jnp.int32, (1, N_CMP_POS), 1)
    blk_n = (pos % N_SEL_POS) * (N_CMP_POS // N_SEL_POS) + pos // N_SEL_POS
    c_end = blk_n * CMP_STRIDE + (CMP_LEN - 1)
    n_real = t_len // CMP_STRIDE - 1
    vis_c = (c_end <= t_col) & (blk_n < n_real)
    c_end_f = c_end.astype(F32)
    s_c = lax.dot_general(q3b, ck_ref[0, 0], nt_dims, preferred_element_type=F32)
    p_c = []
    for g in range(NSA_GROUP):
        sg = jnp.where(vis_c, s_c[g * tq:(g + 1) * tq] + slopes[g] * c_end_f, NEG_INF)
        mg = jnp.max(sg, axis=1, keepdims=True)
        eg = jnp.where(vis_c, jnp.exp2(sg - mg), 0.0)
        dg = jnp.sum(eg, axis=1, keepdims=True)
        p_c.append(eg / jnp.where(dg > 0.0, dg, 1.0))
    o_c = jnp.dot(jnp.concatenate(p_c, axis=0).astype(BF16), cv_ref[0, 0], preferred_element_type=F32)
    imp = p_c[0] + p_c[1] + p_c[2]
    imp = imp[:, :LANES] + imp[:, LANES:]
    imp = imp[:, :N_SEL_POS] + imp[:, N_SEL_POS:]

    blk = lax.broadcasted_iota(jnp.int32, (1, N_SEL_POS), 1)
    cur = t_col // SEL_BLOCK
    forced = (blk == 0) | (blk == cur) | (blk == cur - 1)
    score = jnp.where(forced, FORCE_SCORE, jnp.where(blk <= cur, imp, -jnp.inf))
    sel, blk_any = _topk_mask(score)
    notsel = (1.0 - sel).astype(BF16)
    q_scr[...] = jnp.concatenate([q3b, jnp.concatenate([notsel] * NSA_GROUP, axis=0)], axis=1)

    lane_k = lax.broadcasted_iota(jnp.int32, (1, kb), 1)
    n_win = WINDOW // kb + 1

    words = _chunk_words(blk_any, kb // SEL_BLOCK)
    n_sel = jnp.int32(0)
    for c in range(N_SEL_POS * SEL_BLOCK // kb):
        active = (((words[c // 8] >> (3 * (c % 8))) & 7) != 0) & (c <= i)
        lst_ref[n_sel] = c
        n_sel = n_sel + active.astype(jnp.int32)
    n_chunks = n_sel + n_win

    def chunk(p):
        is_sel = p < n_sel
        is_pad = p >= n_chunks
        c = lst_ref[jnp.minimum(p, n_sel - 1)]
        w = p - n_sel
        kpos0 = jnp.where(is_sel, c * kb, jnp.where(is_pad, -kb, t0 - WINDOW + w * kb))
        row0 = jnp.maximum(kpos0, 0) + jnp.where(is_sel | is_pad, 0, t_len)
        mtype = jnp.where(is_sel, jnp.where(c == i, MASK_CAUSAL, MASK_NONE),
                          jnp.where(w == 0, MASK_BAND, jnp.where(w == n_win - 1, MASK_CAUSAL, MASK_NONE)))
        return pl.multiple_of(row0, kb), kpos0, mtype, jnp.where(is_sel | is_pad, 0, 1)

    def stage_logits(c, par):
        s_scr = s_scrs[par]
        row0, kpos0, mtype, _ = chunk(c)
        k = kc_ref[0, 0, pl.ds(row0, kb), :]
        kpos_f = (kpos0 + lane_k).astype(F32)
        off = jnp.where(kpos0 >= 0, 0.0, -MASK_BIG)
        for g in range(NSA_GROUP):
            bias = slopes[g] * kpos_f + off
            for r in range(0, tq, mm_rows):
                rows = slice(g * tq + r, g * tq + r + mm_rows)
                s = lax.dot_general(q_scr[rows, :], k, nt_dims, preferred_element_type=F32)
                s_scr[rows, :] = (s + mb_scr[mtype, r:r + mm_rows, :]) + bias

    def stage_softmax(c, par):
        s_scr, p_scr, a_scr = s_scrs[par], p_scrs[par], a_scrs[par]
        st = chunk(c)[3]
        for r in range(0, NSA_GROUP * tq, sm_rows):
            rows = slice(r, r + sm_rows)
            s = s_scr[rows, :]
            m_prev = m_scr[st, rows, :]
            m_new = jnp.maximum(m_prev, jnp.max(s, axis=1, keepdims=True))
            alpha = jnp.exp2(m_prev - m_new)
            p = jnp.exp2(s - jnp.concatenate([m_new] * (kb // LANES), axis=1))
            l_scr[st, rows, :] = alpha * l_scr[st, rows, :] + jnp.sum(p, axis=1, keepdims=True)
            p_scr[rows, :] = p.astype(BF16)
            a_scr[rows, :] = alpha
            m_scr[st, rows, :] = m_new

    def stage_values(c, par):
        p_scr, a_scr = p_scrs[par], a_scrs[par]
        row0, _, _, st = chunk(c)
        v = vc_ref[0, 0, pl.ds(row0, kb), :]
        for r in range(0, NSA_GROUP * tq, mm_rows):
            rows = slice(r, r + mm_rows)
            pv = jnp.dot(p_scr[rows, :], v, preferred_element_type=F32)
            acc_scr[st, rows, :] = acc_scr[st, rows, :] * a_scr[rows, :HEAD_DIM] + pv

    stage_logits(0, 0)
    stage_softmax(0, 0)
    stage_logits(1, 1)

    def pipe_body(j, carry):
        c = 2 * j
        stage_values(c - 2, 0)
        stage_softmax(c - 1, 1)
        stage_logits(c, 0)
        stage_values(c - 1, 1)
        stage_softmax(c, 0)
        stage_logits(c + 1, 1)
        return carry

    n_even = n_chunks + n_chunks % 2
    lax.fori_loop(1, n_even // 2, pipe_body, 0)
    stage_values(n_even - 2, 0)
    stage_softmax(n_even - 1, 1)
    stage_values(n_even - 1, 1)
    o_s = acc_scr[0] / l_scr[0][:, :HEAD_DIM]
    o_w = acc_scr[1] / l_scr[1][:, :HEAD_DIM]

    gates = jax.nn.sigmoid(gt_ref[...])
    outs = []
    for g in range(NSA_GROUP):
        rows = slice(g * tq, (g + 1) * tq)
        outs.append(gates[:, 3 * g:3 * g + 1] * o_c[rows] + gates[:, 3 * g + 1:3 * g + 2] * o_s[rows]
                    + gates[:, 3 * g + 2:3 * g + 3] * o_w[rows])
    o_ref[0, 0] = jnp.concatenate(outs, axis=1)


def nsa_prompt(proj, slopes, ckp, cvp, kcat, vcat, batch, t, *, tq=256, mm_rows=256, sm_rows=256):
    assert t % tq == 0 and WINDOW % tq == 0 and WINDOW >= tq
    nt = t // tq
    rows = NSA_GROUP * tq
    gcol = B_GATE_COL // LANES
    grid_spec = pltpu.PrefetchScalarGridSpec(
        num_scalar_prefetch=1,
        grid=(batch, NSA_KV_HEADS, nt),
        in_specs=[
            pl.BlockSpec((tq, Q_SLOT), lambda b, h, i, sl: (b * nt + i, h)),
            pl.BlockSpec((tq, LANES), lambda b, h, i, sl: (b * nt + i, gcol + h)),
            pl.BlockSpec((1, 1, N_CMP_POS, HEAD_DIM), lambda b, h, i, sl: (b, h, 0, 0)),
            pl.BlockSpec((1, 1, N_CMP_POS, HEAD_DIM), lambda b, h, i, sl: (b, h, 0, 0)),
            pl.BlockSpec((1, 1, 2 * t, LANES), lambda b, h, i, sl: (b, h, 0, 0)),
            pl.BlockSpec((1, 1, 2 * t, HEAD_DIM), lambda b, h, i, sl: (b, h, 0, 0)),
        ],
        out_specs=pl.BlockSpec((1, 1, tq, NSA_GROUP * HEAD_DIM), lambda b, h, i, sl: (b, h, i, 0)),
        scratch_shapes=[
            pltpu.VMEM((rows, LANES), BF16),
            pltpu.VMEM((rows, tq), F32), pltpu.VMEM((rows, tq), F32),
            pltpu.VMEM((rows, tq), BF16), pltpu.VMEM((rows, tq), BF16),
            pltpu.VMEM((rows, LANES), F32), pltpu.VMEM((rows, LANES), F32),
            pltpu.VMEM((2, rows, LANES), F32),
            pltpu.VMEM((2, rows, LANES), F32),
            pltpu.VMEM((2, rows, HEAD_DIM), F32),
            pltpu.VMEM((3, tq, tq), F32),
            pltpu.SMEM((N_SEL_POS * SEL_BLOCK // tq,), jnp.int32),
        ],
    )
    return pl.pallas_call(
        functools.partial(_nsa_prompt_kernel, mm_rows=mm_rows, sm_rows=sm_rows),
        out_shape=jax.ShapeDtypeStruct((batch, NSA_KV_HEADS, t, NSA_GROUP * HEAD_DIM), F32),
        grid_spec=grid_spec,
        compiler_params=_cparams("parallel", "parallel", "arbitrary"),
        name="nsa_prompt",
    )(slopes, proj, proj, ckp, cvp, kcat, vcat)


def _kv_layout_kernel(x_ref, kc_ref, vc_ref):
    tm = x_ref.shape[0]
    pos = pl.program_id(1) * tm + lax.broadcasted_iota(jnp.int32, (tm, N_SEL_POS), 0)
    blk = lax.broadcasted_iota(jnp.int32, (tm, N_SEL_POS), 1)
    onehot = jnp.where(pos // SEL_BLOCK == blk, -MASK_BIG, 0.0).astype(BF16)
    zeros = jnp.zeros((tm, N_SEL_POS), BF16)
    for h in range(NSA_KV_HEADS):
        head = lambda c0: x_ref[:, c0 + h * HEAD_DIM:c0 + (h + 1) * HEAD_DIM].astype(BF16)
        kc_ref[0, h, 0] = jnp.concatenate([head(2 * KV_DIM), onehot], axis=1)
        kc_ref[0, h, 1] = jnp.concatenate([head(4 * KV_DIM), zeros], axis=1)
        vc_ref[0, h, 0] = head(3 * KV_DIM)
        vc_ref[0, h, 1] = head(5 * KV_DIM)


def kv_layout(kv_rows, batch, t, *, tm=512):
    tm = _row_tile(t, tm)
    nt = t // tm
    kc, vc = pl.pallas_call(
        _kv_layout_kernel,
        out_shape=(jax.ShapeDtypeStruct((batch, NSA_KV_HEADS, 2, t, LANES), BF16),
                   jax.ShapeDtypeStruct((batch, NSA_KV_HEADS, 2, t, HEAD_DIM), BF16)),
        grid=(batch, nt),
        in_specs=[pl.BlockSpec((tm, kv_rows.shape[1]), lambda b, i: (b * nt + i, 0))],
        out_specs=(pl.BlockSpec((1, NSA_KV_HEADS, 2, tm, LANES), lambda b, i: (b, 0, 0, i, 0)),
                   pl.BlockSpec((1, NSA_KV_HEADS, 2, tm, HEAD_DIM), lambda b, i: (b, 0, 0, i, 0))),
        compiler_params=_cparams("parallel", "parallel"),
        name="kv_layout",
    )(kv_rows)
    return (kc.reshape(batch, NSA_KV_HEADS, 2 * t, LANES), vc.reshape(batch, NSA_KV_HEADS, 2 * t, HEAD_DIM))


def _cmp_slots(c, batch):
    n_blk = c.shape[1]
    per = N_CMP_POS // N_SEL_POS
    c = jnp.pad(c, ((0, 0), (0, N_CMP_POS - n_blk), (0, 0)))
    c = c.reshape(batch, N_SEL_POS, per, NSA_KV_HEADS, HEAD_DIM).transpose(0, 3, 2, 1, 4)
    return c.reshape(batch, NSA_KV_HEADS, N_CMP_POS, HEAD_DIM).astype(BF16)


def _compress_sample_kernel(pt_ref, pool_ref, new_ref, pek_ref, pev_ref, w1k_ref, w1v_ref, w2k_ref, w2v_ref,
                            ck_ref, cv_ref, buf_ref, sem_ref, col_ref, *, n_pages, pg):
    b = pl.program_id(0)
    g = pl.program_id(1)
    ng = n_pages // pg
    step = b * ng + g
    n_steps = pl.num_programs(0) * ng
    slot = step % 2
    page_rows = buf_ref.shape[3]
    rows = pg * page_rows
    ts = new_ref.shape[1]

    def pages(st, sl, go):
        bb = st // ng
        gg = st % ng

        def copy(k, dst):
            cp = pltpu.make_async_copy(pool_ref.at[pt_ref[bb * n_pages + gg * pg + k]], buf_ref.at[sl, dst],
                                       sem_ref.at[sl])
            cp.start() if go == "start" else cp.wait()

        for k in range(pg):
            copy(k, k)

        @pl.when(gg < ng - 1)
        def _():
            copy(pg, pg)

    @pl.when(step == 0)
    def _():
        pages(0, 0, "start")

    @pl.when(step + 1 < n_steps)
    def _():
        pages(step + 1, 1 - slot, "start")

    pages(step, slot, "wait")
    n_col = col_ref.shape[0]
    per_page = page_rows // CMP_STRIDE
    for k in range(pg):
        for c in range(n_col):
            tile = buf_ref[slot, k, c * LANES:(c + 1) * LANES, :].T
            for q in range(per_page):
                r0 = (k * per_page + q) * CMP_PITCH
                col_ref[c, r0:r0 + CMP_STRIDE, :] = tile[q * CMP_STRIDE:(q + 1) * CMP_STRIDE]
    halo = pg * per_page * CMP_PITCH

    @pl.when(g < ng - 1)
    def _():
        for c in range(n_col):
            col_ref[c, halo:halo + CMP_STRIDE, :] = buf_ref[slot, pg, c * LANES:(c + 1) * LANES, :].T[0:CMP_STRIDE]

    @pl.when(g == ng - 1)
    def _():
        for c in range(n_col):
            col_ref[c, halo:halo + ts, :] = new_ref[0, :, c * LANES:(c + 1) * LANES]
            col_ref[c, halo + ts:halo + CMP_STRIDE, :] = jnp.zeros((CMP_STRIDE - ts, LANES), F32)

    ck, cv = _compress_tile(col_ref, 0, rows // CMP_STRIDE, (pek_ref, pev_ref), (w1k_ref, w1v_ref),
                            (w2k_ref, w2v_ref), pitch=CMP_PITCH)
    ck_ref[0] = ck
    cv_ref[0] = cv


def compress_sample(page_table, pool_t, new_rows, cw, *, pg):
    pek, w1k, w2k, pev, w1v, w2v = cw
    batch, n_pages = page_table.shape
    page_rows = pool_t.shape[2]
    ts = new_rows.shape[1]
    ng = n_pages // pg
    m = pg * page_rows // CMP_STRIDE
    full = lambda a: pl.BlockSpec(a.shape, lambda b, g, pt: (0,) * a.ndim)
    grid_spec = pltpu.PrefetchScalarGridSpec(
        num_scalar_prefetch=1,
        grid=(batch, ng),
        in_specs=[pl.BlockSpec(memory_space=pl.ANY),
                  pl.BlockSpec((1, ts, new_rows.shape[2]), lambda b, g, pt: (b, 0, 0)),
                  full(pek), full(pev), full(w1k), full(w1v), full(w2k), full(w2v)],
        out_specs=(pl.BlockSpec((1, m, KV_DIM), lambda b, g, pt: (b, g, 0)),) * 2,
        scratch_shapes=[pltpu.VMEM((2, pg + 1, 2 * KV_DIM, page_rows), F32), pltpu.SemaphoreType.DMA((2,)),
                        pltpu.VMEM((2 * KV_DIM // LANES, -(-(m + 1) * CMP_PITCH // 8) * 8, LANES), F32)],
    )
    return pl.pallas_call(
        functools.partial(_compress_sample_kernel, n_pages=n_pages, pg=pg),
        out_shape=(jax.ShapeDtypeStruct((batch, ng * m, KV_DIM), F32),) * 2,
        grid_spec=grid_spec,
        compiler_params=_cparams("arbitrary", "arbitrary"),
        name="compress_sample",
    )(page_table.reshape(-1), pool_t, new_rows, pek, pev, w1k, w1v, w2k, w2v)


S_COL_G = 32


def _softmax_lanes(pieces):
    m = functools.reduce(jnp.maximum, [jnp.max(s, axis=1, keepdims=True) for s in pieces])
    es = [jnp.where(s > 0.5 * NEG_INF, jnp.exp(s - m), 0.0) for s in pieces]
    d = functools.reduce(jnp.add, [jnp.sum(e, axis=1, keepdims=True) for e in es])
    inv = 1.0 / jnp.where(d > 0.0, d, 1.0)
    return [e * inv for e in es]


def _nsa_decode_kernel(sl_ref, pt_ref, proj_ref, ck_ref, cv_ref, new_ref, win_ref, e_ref, pool_ref, y_ref,
                       buf_ref, sem_ref, q_scr, m_scr, l_scr, acc_scr, oc_scr, ow_scr,
                       *, n_pages, pg, past):
    b = pl.program_id(0)
    g = pl.program_id(1)
    ng = n_pages // pg
    step = b * ng + g
    n_steps = pl.num_programs(0) * ng
    slot = step % 2
    page_rows = buf_ref.shape[3]
    rows = pg * page_rows
    ts = proj_ref.shape[0]
    nt_dims = (((1,), (1,)), ((), ()))

    def pages(st, sl, go):
        bb = st // ng
        gg = st % ng
        for k in range(pg):
            cp = pltpu.make_async_copy(pool_ref.at[pt_ref[bb * n_pages + gg * pg + k]], buf_ref.at[sl, k],
                                       sem_ref.at[sl])
            cp.start() if go == "start" else cp.wait()

    @pl.when(step == 0)
    def _():
        pages(0, 0, "start")

    @pl.when(step + 1 < n_steps)
    def _():
        pages(step + 1, 1 - slot, "start")

    row = lax.broadcasted_iota(jnp.int32, (LANES, 1), 0)
    row_g = row // S_COL_G
    row_h = (row % S_COL_G) // ts
    t_col = past + row % ts
    slope_col = jnp.zeros((LANES, 1), F32)
    for h in range(NSA_HEADS):
        slope_col = jnp.where((row_h == h // NSA_GROUP) & (row_g == h % NSA_GROUP), sl_ref[h], slope_col)
    n_sb = past // SEL_BLOCK + 1
    zpad = jnp.zeros((LANES - ts, KV_DIM), F32)
    lane_new = lax.broadcasted_iota(jnp.int32, (1, LANES), 1)
    kp_new = past + lane_new
    vis_new = (kp_new <= t_col) & (lane_new < ts)

    @pl.when(g == 0)
    def _():
        lane_h = lax.broadcasted_iota(jnp.int32, (ts, KV_DIM), 1) // HEAD_DIM
        tiles = []
        for gg in range(NSA_GROUP):
            qg = proj_ref[:, gg * KV_DIM:(gg + 1) * KV_DIM] * ATT_SCALE
            for h in range(NSA_KV_HEADS):
                tiles.append(jnp.where(lane_h == h, qg, 0.0))
        tiles.append(jnp.zeros((LANES - NSA_HEADS * ts, KV_DIM), F32))
        bdq = jnp.concatenate(tiles, axis=0).astype(BF16)
        q_scr[:, 0:KV_DIM] = bdq

        n_c = ck_ref.shape[1]
        per = SEL_BLOCK // CMP_STRIDE
        n_q = n_c // per
        s_c = lax.dot_general(bdq, ck_ref[0].astype(BF16), nt_dims, preferred_element_type=F32)
        slot_c = lax.broadcasted_iota(jnp.int32, (1, n_c), 1)
        c_end = ((slot_c % n_q) * per + slot_c // n_q) * CMP_STRIDE + (CMP_LEN - 1)
        dist = t_col - c_end
        s_c = jnp.where(dist >= 0, s_c - slope_col * dist.astype(F32), NEG_INF)
        (p_c,) = _softmax_lanes([s_c])
        oc_scr[...] = jnp.dot(p_c.astype(BF16), cv_ref[0].astype(BF16), preferred_element_type=F32)
        imp = p_c[0:S_COL_G] + p_c[S_COL_G:2 * S_COL_G] + p_c[2 * S_COL_G:3 * S_COL_G]
        imp = functools.reduce(jnp.add, [imp[:, r * n_q:(r + 1) * n_q] for r in range(per)])
        n_blk = q_scr.shape[1] - KV_DIM
        imp = jnp.concatenate([imp, jnp.zeros((S_COL_G, n_blk - n_q), F32)], axis=1)
        blk = lax.broadcasted_iota(jnp.int32, (1, n_blk), 1)
        cur = t_col[0:S_COL_G] // SEL_BLOCK
        forced = (blk == 0) | (blk == cur) | (blk == cur - 1)
        score = jnp.where(forced, FORCE_SCORE, jnp.where((blk <= cur) & (blk < n_sb), imp, -jnp.inf))
        rank = jnp.zeros(score.shape, F32)
        for j in range(n_sb):
            rj = score[:, j:j + 1]
            rank = rank + jnp.where(blk > j, (rj >= score).astype(F32), (rj > score).astype(F32))
        notsel = (rank >= N_SEL).astype(F32)
        notsel = jnp.concatenate([notsel] * NSA_GROUP + [jnp.ones((LANES - NSA_GROUP * S_COL_G, n_blk), F32)], axis=0)
        q_scr[:, KV_DIM:KV_DIM + n_blk] = notsel.astype(BF16)

        wl = win_ref.shape[2]
        kw_new = jnp.concatenate([new_ref[0, :, 4 * KV_DIM:5 * KV_DIM], zpad], axis=0).astype(BF16)
        vw_new = jnp.concatenate([new_ref[0, :, 5 * KV_DIM:6 * KV_DIM], zpad], axis=0).astype(BF16)
        s_w = jnp.dot(bdq, win_ref[0, 0:KV_DIM, :].astype(BF16), preferred_element_type=F32)
        w_pos = past - wl + lax.broadcasted_iota(jnp.int32, (1, wl), 1)
        dist_w = t_col - w_pos
        s_w = jnp.where((dist_w >= 0) & (dist_w <= WINDOW) & (w_pos >= 0), s_w - slope_col * dist_w.astype(F32),
                        NEG_INF)
        s_wn = lax.dot_general(bdq, kw_new, nt_dims, preferred_element_type=F32)
        dist_n = t_col - kp_new
        s_wn = jnp.where(vis_new & (dist_n <= WINDOW), s_wn - slope_col * dist_n.astype(F32), NEG_INF)
        p_w, p_wn = _softmax_lanes([s_w, s_wn])
        ow_scr[...] = (lax.dot_general(p_w.astype(BF16), win_ref[0, KV_DIM:2 * KV_DIM, :].astype(BF16), nt_dims,
                                       preferred_element_type=F32)
                       + jnp.dot(p_wn.astype(BF16), vw_new, preferred_element_type=F32))
        m_scr[...] = jnp.full(m_scr.shape, M_INIT, F32)
        l_scr[...] = jnp.zeros(l_scr.shape, F32)
        acc_scr[...] = jnp.zeros(acc_scr.shape, F32)

    def online_update(s, pv_fn):
        n = s.shape[1]
        m_prev = m_scr[...]
        m_new = jnp.maximum(m_prev, jnp.max(s, axis=1, keepdims=True))
        alpha = jnp.exp(m_prev - m_new)
        p = jnp.exp(s - jnp.concatenate([m_new] * (n // LANES), axis=1))
        l_scr[...] = alpha * l_scr[...] + jnp.sum(p, axis=1, keepdims=True)
        acc_scr[...] = acc_scr[...] * jnp.concatenate([alpha] * (KV_DIM // LANES), axis=1) + pv_fn(p.astype(BF16))
        m_scr[...] = m_new

    pages(step, slot, "wait")
    kt = jnp.concatenate([buf_ref[slot, k, 0:KV_DIM, :] for k in range(pg)], axis=1).astype(BF16)
    vt = jnp.concatenate([buf_ref[slot, k, KV_DIM:2 * KV_DIM, :] for k in range(pg)], axis=1).astype(BF16)
    n_blk = q_scr.shape[1] - KV_DIM
    s_g = (jnp.dot(q_scr[:, 0:KV_DIM], kt, preferred_element_type=F32)
           + jnp.dot(q_scr[:, KV_DIM:KV_DIM + n_blk], e_ref[0], preferred_element_type=F32))
    kpos = g * rows + lax.broadcasted_iota(jnp.int32, (1, rows), 1)
    s_g = s_g - slope_col * (t_col - kpos).astype(F32)
    online_update(s_g, lambda p: lax.dot_general(p, vt, nt_dims, preferred_element_type=F32))

    @pl.when(g == ng - 1)
    def _():
        k_new = jnp.concatenate([new_ref[0, :, 2 * KV_DIM:3 * KV_DIM], zpad], axis=0).astype(BF16)
        v_new = jnp.concatenate([new_ref[0, :, 3 * KV_DIM:4 * KV_DIM], zpad], axis=0).astype(BF16)
        s_n = lax.dot_general(q_scr[:, 0:KV_DIM], k_new, nt_dims, preferred_element_type=F32)
        unsel_last = q_scr[:, KV_DIM + n_sb - 1:KV_DIM + n_sb].astype(F32)
        dist_n = t_col - kp_new
        s_n = jnp.where(vis_new & (unsel_last < 0.5), s_n - slope_col * dist_n.astype(F32), -MASK_BIG)
        online_update(s_n, lambda p: jnp.dot(p, v_new, preferred_element_type=F32))
        o_s = acc_scr[...] / jnp.concatenate([l_scr[...]] * (KV_DIM // LANES), axis=1)
        o_c = oc_scr[...]
        o_w = ow_scr[...]
        gates = jax.nn.sigmoid(proj_ref[:, (NSA_GROUP + 1) * KV_DIM:(NSA_GROUP + 1) * KV_DIM + LANES])
        outs = []
        for h in range(NSA_HEADS):
            kvh, gg = h // NSA_GROUP, h % NSA_GROUP
            r0 = gg * S_COL_G + kvh * ts
            lanes = slice(kvh * HEAD_DIM, (kvh + 1) * HEAD_DIM)
            outs.append(gates[:, 3 * h:3 * h + 1] * o_c[r0:r0 + ts, lanes]
                        + gates[:, 3 * h + 1:3 * h + 2] * o_s[r0:r0 + ts, lanes]
                        + gates[:, 3 * h + 2:3 * h + 3] * o_w[r0:r0 + ts, lanes])
        y_ref[...] = jnp.concatenate(outs, axis=1)


def nsa_decode(proj, slopes, page_table, ckp, cvp, new_rows, win_t, pool_t, *, pg):
    batch, n_pages = page_table.shape
    page_rows = pool_t.shape[2]
    ts = new_rows.shape[1]
    past = n_pages * page_rows
    ng = n_pages // pg
    rows = pg * page_rows
    n_c = ckp.shape[1]
    wl = win_t.shape[2]
    n_blk = -(-(past // SEL_BLOCK + 1) // LANES) * LANES
    key_blk = jnp.arange(past, dtype=jnp.int32).reshape(ng, 1, rows) // SEL_BLOCK
    e_tab = jnp.where(key_blk == jnp.arange(n_blk, dtype=jnp.int32)[None, :, None], -MASK_BIG, 0.0).astype(BF16)
    grid_spec = pltpu.PrefetchScalarGridSpec(
        num_scalar_prefetch=2,
        grid=(batch, ng),
        in_specs=[
            pl.BlockSpec((ts, proj.shape[1]), lambda b, g, sl, pt: (b, 0)),
            pl.BlockSpec((1, n_c, KV_DIM), lambda b, g, sl, pt: (b, 0, 0)),
            pl.BlockSpec((1, n_c, KV_DIM), lambda b, g, sl, pt: (b, 0, 0)),
            pl.BlockSpec((1, ts, new_rows.shape[2]), lambda b, g, sl, pt: (b, 0, 0)),
            pl.BlockSpec((1, 2 * KV_DIM, wl), lambda b, g, sl, pt: (b, 0, 0)),
            pl.BlockSpec((1, n_blk, rows), lambda b, g, sl, pt: (g, 0, 0)),
            pl.BlockSpec(memory_space=pl.ANY),
        ],
        out_specs=pl.BlockSpec((ts, NSA_DIM), lambda b, g, sl, pt: (b, 0)),
        scratch_shapes=[
            pltpu.VMEM((2, pg, 2 * KV_DIM, page_rows), F32), pltpu.SemaphoreType.DMA((2,)),
            pltpu.VMEM((LANES, KV_DIM + n_blk), BF16),
            pltpu.VMEM((LANES, LANES), F32), pltpu.VMEM((LANES, LANES), F32), pltpu.VMEM((LANES, KV_DIM), F32),
            pltpu.VMEM((LANES, KV_DIM), F32), pltpu.VMEM((LANES, KV_DIM), F32),
        ],
    )
    return pl.pallas_call(
        functools.partial(_nsa_decode_kernel, n_pages=n_pages, pg=pg, past=past),
        out_shape=jax.ShapeDtypeStruct((batch * ts, NSA_DIM), F32),
        grid_spec=grid_spec,
        compiler_params=_cparams("arbitrary", "arbitrary"),
        name="nsa_decode",
    )(slopes, page_table.reshape(-1), proj, ckp, cvp, new_rows, win_t, e_tab, pool_t)


S_MQ_COL =NSA_GROUP * KV_DIM
S_COLS = S_MQ_COL + MEM_DIM + LANES


def _layout_w_in_b_sample(w):
    d = w.shape[0]
    qw = w[:, :NSA_DIM].reshape(d, NSA_KV_HEADS, NSA_GROUP, HEAD_DIM).transpose(0, 2, 1, 3).reshape(d, NSA_DIM)
    gw = jnp.pad(w[:, NSA_DIM:NSA_DIM + 3 * NSA_HEADS], ((0, 0), (0, LANES - 3 * NSA_HEADS)))
    mw = w[:, NSA_DIM + 3 * NSA_HEADS:]
    return jnp.concatenate([qw, mw, gw], axis=1).astype(BF16)


Q_SLOT = 256
B_MQ_COL = NSA_KV_HEADS * Q_SLOT
B_GATE_COL = B_MQ_COL + MEM_DIM
B_COLS = B_GATE_COL + NSA_KV_HEADS * LANES


def _layout_w_in_b(w):
    d = w.shape[0]
    qw = w[:, :NSA_DIM].reshape(d, NSA_KV_HEADS, NSA_GROUP * HEAD_DIM)
    qw = jnp.pad(qw, ((0, 0), (0, 0), (0, Q_SLOT - NSA_GROUP * HEAD_DIM))).reshape(d, NSA_KV_HEADS * Q_SLOT)
    gw = w[:, NSA_DIM:NSA_DIM + 3 * NSA_HEADS].reshape(d, NSA_KV_HEADS, 3 * NSA_GROUP)
    gw = jnp.pad(gw, ((0, 0), (0, 0), (0, LANES - 3 * NSA_GROUP))).reshape(d, NSA_KV_HEADS * LANES)
    mw = w[:, NSA_DIM + 3 * NSA_HEADS:]
    return jnp.concatenate([qw, mw, gw], axis=1).astype(BF16)


def kernel(x_prompt, x_sample, state_conv, cache_mem_kv, cache_cmp_kv, cache_slc_kv, state_win_kv, page_table,
           mem_prompt, g_mix, w_in_a, conv_w, w_in_b, w_o, w_mkv, g_mem, g_kv, w_kv, pe_ck, w1_ck, w2_ck,
           pe_cv, w1_cv, w2_cv, g_ffn, w_gu, w_dn, g_final):
    bp, tp, d = x_prompt.shape
    bs, ts = x_sample.shape[:2]
    depth = g_mix.shape[0]
    n_a = w_in_a.shape[0]
    n_mem = mem_prompt.shape[1]
    win_len = state_win_kv.shape[1]
    past_len = page_table.shape[1] * cache_cmp_kv.shape[1]
    conv_dim = conv_w.shape[2]
    slopes = jnp.asarray(np.array(_alibi_list(NSA_HEADS), dtype=np.float32))

    w_in_a16 = w_in_a.astype(BF16)
    w_in_b16 = [_layout_w_in_b(w_in_b[j]) for j in range(depth - n_a)]
    w_o16 = w_o.astype(BF16)
    w_gu16 = w_gu.astype(BF16)
    w_dn16 = w_dn.astype(BF16)
    w_kv16 = w_kv.astype(BF16)
    w_mkv16 = w_mkv.transpose(1, 0, 2).reshape(d, depth * 2 * MEM_DIM).astype(BF16)

    mkv = rms_matmul(mem_prompt.reshape(bp * n_mem, d), g_mem, w_mkv16)
    mem_kv_p = mkv.reshape(bp, n_mem, depth, 2 * MEM_DIM).transpose(2, 0, 1, 3)
    mem_kv_s = cache_mem_kv.reshape(depth, bs, n_mem, 2 * MEM_DIM)

    groups = [
        dict(x=x_prompt.reshape(bp * tp, d), b=bp, t=tp, mem=mem_kv_p, st=jnp.zeros((n_a, bp, 2, conv_dim), F32)),
        dict(x=x_sample.reshape(bs * ts, d), b=bs, t=ts, mem=mem_kv_s, st=state_conv),
    ]
    conv_out = [[], []]
    kv_rows = [None, None]

    cw = _compress_weights(pe_ck, w1_ck, w2_ck) + _compress_weights(pe_cv, w1_cv, w2_cv)
    nsa_in = None
    w_in_bs16 = [_layout_w_in_b_sample(w_in_b[j]) for j in range(depth - n_a)]
    n_pages = page_table.shape[1]
    page_rows = cache_cmp_kv.shape[1]
    pg = 16 if n_pages % 16 == 0 else n_pages
    assert page_rows == LANES
    pool_cmp = cache_cmp_kv.transpose(0, 2, 3, 4, 1).reshape(cache_cmp_kv.shape[0], 2 * KV_DIM, page_rows)
    pool_slc = cache_slc_kv.transpose(0, 2, 3, 4, 1).reshape(cache_slc_kv.shape[0], 2 * KV_DIM, page_rows)
    win_state = state_win_kv.transpose(0, 2, 3, 4, 1).reshape(bs, 2 * KV_DIM, win_len)
    assert ts == 8 and past_len % SEL_BLOCK == 0 and tp % 256 == 0 and tp // SEL_BLOCK <= N_SEL_POS

    def wo_pairs(l, n, y_main, y_mem, nsa_layout, b, t):
        tm = _row_tile(n, 1024)
        if nsa_layout:
            hw = NSA_GROUP * HEAD_DIM
            nt = t // tm
            y2d = y_main.reshape(b * NSA_KV_HEADS * t, hw)
            pairs = [(y2d, (tm, hw), (lambda i, h=h: (((i // nt) * NSA_KV_HEADS + h) * nt + i % nt, 0)),
                      w_o16[l, h * hw:(h + 1) * hw]) for h in range(NSA_KV_HEADS)]
            km = NSA_DIM
        else:
            km = y_main.shape[1]
            pairs = [(y_main, (tm, km), lambda i: (i, 0), w_o16[l, :km])]
        return pairs + [(y_mem, (tm, MEM_DIM), lambda i: (i, 0), w_o16[l, km:])]

    for l in range(depth):
        for gi, gr in enumerate(groups):
            x, b, t = gr["x"], gr["b"], gr["t"]
            n = b * t
            nsa_layout = False
            if l < n_a:
                proj = rms_matmul(x, g_mix[l], w_in_a16[l], tn=w_in_a16.shape[2] // 2)
                y_main, new_st = gated_conv(proj, gr["st"][l], conv_w[l], b, t)
                conv_out[gi].append(new_st)
                y_mem = mem_attention(proj, 3 * conv_dim // MEM_DIM, gr["mem"][l], b, t)
            else:
                if l == n_a:
                    kv2d = rms_matmul(x, g_kv, w_kv16, tn=3 * KV_DIM)
                    kv_rows[gi] = kv2d
                    if gi == 0:
                        ck, cv = compress_prompt(kv2d, b, t, cw)
                        nsa_in = (_cmp_slots(ck, b), _cmp_slots(cv, b)) + kv_layout(kv2d, b, t)
                    else:
                        new3 = kv2d.reshape(b, t, 6 * KV_DIM)
                        ck_s, cv_s = compress_sample(page_table, pool_cmp, new3, cw, pg=pg)
                        per = SEL_BLOCK // CMP_STRIDE
                        slots = lambda c: c.reshape(b, c.shape[1] // per, per, KV_DIM).transpose(0, 2, 1, 3).reshape(
                            c.shape)
                        ck_s, cv_s = slots(ck_s), slots(cv_s)
                j = l - n_a
                if gi == 0:
                    proj = rms_matmul(x, g_mix[l], w_in_b16[j], tn=B_COLS // 2)
                    y_main = nsa_prompt(proj, slopes, *nsa_in, b, t)
                    nsa_layout = True
                    y_mem = mem_attention(proj, B_MQ_COL // MEM_DIM, gr["mem"][l], b, t)
                else:
                    proj = rms_matmul(x, g_mix[l], w_in_bs16[j], tn=S_COLS)
                    y_main = nsa_decode(proj, slopes, page_table, ck_s, cv_s, new3, win_state, pool_slc, pg=pg)
                    y_mem = mem_attention(proj, S_MQ_COL // MEM_DIM, gr["mem"][l], b, t)
            x = proj_residual(x, wo_pairs(l, n, y_main, y_mem, nsa_layout, b, t))
            x = ffn(x, g_ffn[l], w_gu16[l], w_dn16[l])
            gr["x"] = x

    y_prompt = rmsnorm_rows(groups[0]["x"], g_final).reshape(bp, tp, d)
    y_sample = rmsnorm_rows(groups[1]["x"], g_final).reshape(bs, ts, d)
    conv_state_p = jnp.stack(conv_out[0])
    conv_state_s = jnp.stack(conv_out[1])
    mem_kv_out = mem_kv_p.reshape(depth, bp, n_mem, 2, MEM_HEADS, HEAD_DIM)
    def branch_rows(kv2d, br, b, t):
        return kv2d[:, br * 2 * KV_DIM:(br + 1) * 2 * KV_DIM].reshape(b, t, 2, NSA_KV_HEADS, HEAD_DIM)

    kvp, kvs = kv_rows
    win_kv_p = branch_rows(kvp, 2, bp, tp)[:, tp - min(WINDOW, tp):]
    win_kv_s = jnp.concatenate([state_win_kv, branch_rows(kvs, 2, bs, ts)], axis=1)[:, ts:]
    return (y_prompt, y_sample, conv_state_p, conv_state_s, mem_kv_out, branch_rows(kvp, 0, bp, tp),
            branch_rows(kvp, 1, bp, tp), win_kv_p, branch_rows(kvs, 0, bs, ts), branch_rows(kvs, 1, bs, ts), win_kv_s)
```

```python
import functools
import math

import numpy as np
import jax
import jax.numpy as jnp
from jax import lax
from jax.experimental import pallas as pl
from jax.experimental.pallas import tpu as pltpu

F32 = jnp.float32
BF16 = jnp.bfloat16

HEAD_DIM = 64
MEM_HEADS = 4
MEM_DIM = MEM_HEADS * HEAD_DIM
NSA_KV_HEADS = 4
NSA_GROUP = 3
NSA_HEADS = NSA_KV_HEADS * NSA_GROUP
NSA_DIM = NSA_HEADS * HEAD_DIM
KV_DIM = NSA_KV_HEADS * HEAD_DIM
CONV_WIDTH = 3
CMP_STRIDE = 16
CMP_LEN = 32
SEL_BLOCK = 64
N_SEL = 16
WINDOW = 512
Q_BLOCK = 64
RMS_EPS = 1e-6
NEG_INF = -1e30
FORCE_SCORE = 1e4
ATT_SCALE = HEAD_DIM ** -0.5

VMEM_LIMIT = 48 * 1024 * 1024


def _cparams(*sem):
    return pltpu.CompilerParams(dimension_semantics=sem, vmem_limit_bytes=VMEM_LIMIT)


def _row_tile(n, want):
    t = min(n, want)
    while n % t:
        t //= 2
    return t


def _rms_matmul_kernel(x_ref, g_ref, w_ref, o_ref, h_ref):
    @pl.when(pl.program_id(1) == 0)
    def _():
        x = x_ref[...]
        ms = jnp.mean(x * x, axis=-1, keepdims=True)
        h_ref[...] = (x * lax.rsqrt(ms + RMS_EPS) * g_ref[...]).astype(BF16)

    o_ref[...] = jnp.dot(h_ref[...], w_ref[...], preferred_element_type=F32)


def _rms_matmul_t_kernel(x_ref, g_ref, w_ref, o_ref, h_ref):
    @pl.when(pl.program_id(1) == 0)
    def _():
        x = x_ref[...]
        ms = jnp.mean(x * x, axis=-1, keepdims=True)
        h_ref[...] = (x * lax.rsqrt(ms + RMS_EPS) * g_ref[...]).astype(BF16)

    o_ref[0] = jnp.dot(h_ref[...], w_ref[...], preferred_element_type=F32).T


def rms_matmul(x, g, w, *, tm=1024, tn=512, feature_major_batch=None):
    n, d = x.shape
    c = w.shape[1]
    tn = _row_tile(c, tn)
    if feature_major_batch is None:
        tm = _row_tile(n, tm)
        kern, out_shape = _rms_matmul_kernel, jax.ShapeDtypeStruct((n, c), F32)
        out_spec = pl.BlockSpec((tm, tn), lambda i, j: (i, j))
    else:
        t = n // feature_major_batch
        tm = _row_tile(t, tm)
        nt = t // tm
        kern, out_shape = _rms_matmul_t_kernel, jax.ShapeDtypeStruct((feature_major_batch, c, t), F32)
        out_spec = pl.BlockSpec((1, tn, tm), lambda i, j: (i // nt, j, i % nt))
    return pl.pallas_call(
        kern,
        out_shape=out_shape,
        grid=(n // tm, c // tn),
        in_specs=[
            pl.BlockSpec((tm, d), lambda i, j: (i, 0)),
            pl.BlockSpec((1, d), lambda i, j: (0, 0)),
            pl.BlockSpec((d, tn), lambda i, j: (0, j)),
        ],
        out_specs=out_spec,
        scratch_shapes=[pltpu.VMEM((tm, d), BF16)],
        compiler_params=_cparams("parallel", "arbitrary"),
        name="rms_matmul",
    )(x, g.reshape(1, d), w)


def _proj_residual_kernel(n_pairs, x_ref, *refs):
    a_refs = refs[:n_pairs]
    w_refs = refs[n_pairs:2 * n_pairs]
    o_ref = refs[2 * n_pairs]
    acc = x_ref[...]
    for a_ref, w_ref in zip(a_refs, w_refs):
        acc = acc + jnp.dot(a_ref[...].astype(BF16), w_ref[...], preferred_element_type=F32)
    o_ref[...] = acc


def proj_residual(x, pairs, *, tm=1024):
    n, d = x.shape
    tm = _row_tile(n, tm)
    in_specs = [pl.BlockSpec((tm, d), lambda i: (i, 0))]
    args = [x]
    for a, blk, imap, _ in pairs:
        in_specs.append(pl.BlockSpec(blk, imap))
        args.append(a)
    for _, _, _, w in pairs:
        in_specs.append(pl.BlockSpec(w.shape, lambda i: (0, 0)))
        args.append(w)
    return pl.pallas_call(
        functools.partial(_proj_residual_kernel, len(pairs)),
        out_shape=jax.ShapeDtypeStruct((n, d), F32),
        grid=(n // tm,),
        in_specs=in_specs,
        out_specs=pl.BlockSpec((tm, d), lambda i: (i, 0)),
        compiler_params=_cparams("parallel"),
        name="proj_residual",
    )(*args)


def _ffn_kernel(x_ref, g_ref, wg_ref, wu_ref, wd_ref, o_ref, h_ref, acc_ref):
    f = pl.program_id(1)

    @pl.when(f == 0)
    def _():
        x = x_ref[...]
        ms = jnp.mean(x * x, axis=-1, keepdims=True)
        h_ref[...] = (x * lax.rsqrt(ms + RMS_EPS) * g_ref[...]).astype(BF16)
        acc_ref[...] = x

    h = h_ref[...]
    gate = jnp.dot(h, wg_ref[...], preferred_element_type=F32)
    up = jnp.dot(h, wu_ref[...], preferred_element_type=F32)
    act = (gate * jax.nn.sigmoid(gate) * up).astype(BF16)
    acc_ref[...] += jnp.dot(act, wd_ref[...], preferred_element_type=F32)

    @pl.when(f == pl.num_programs(1) - 1)
    def _():
        o_ref[...] = acc_ref[...]


def ffn(x, g, w_gu, w_dn, *, tm=512, tf=1408):
    n, d = x.shape
    dff = w_dn.shape[0]
    tm = _row_tile(n, tm)
    nf = dff // tf
    return pl.pallas_call(
        _ffn_kernel,
        out_shape=jax.ShapeDtypeStruct((n, d), F32),
        grid=(n // tm, nf),
        in_specs=[
            pl.BlockSpec((tm, d), lambda i, f: (i, 0)),
            pl.BlockSpec((1, d), lambda i, f: (0, 0)),
            pl.BlockSpec((d, tf), lambda i, f: (0, f)),
            pl.BlockSpec((d, tf), lambda i, f: (0, f + nf)),
            pl.BlockSpec((tf, d), lambda i, f: (f, 0)),
        ],
        out_specs=pl.BlockSpec((tm, d), lambda i, f: (i, 0)),
        scratch_shapes=[pltpu.VMEM((tm, d), BF16), pltpu.VMEM((tm, d), F32)],
        compiler_params=_cparams("parallel", "arbitrary"),
        name="ffn",
    )(x, g.reshape(1, d), w_gu, w_gu, w_dn)


def _rmsnorm_kernel(x_ref, g_ref, o_ref):
    x = x_ref[...]
    ms = jnp.mean(x * x, axis=-1, keepdims=True)
    o_ref[...] = x * lax.rsqrt(ms + RMS_EPS) * g_ref[...]


def rmsnorm_rows(x, g, *, tm=1024):
    n, d = x.shape
    tm = _row_tile(n, tm)
    return pl.pallas_call(
        _rmsnorm_kernel,
        out_shape=jax.ShapeDtypeStruct((n, d), F32),
        grid=(n // tm,),
        in_specs=[pl.BlockSpec((tm, d), lambda i: (i, 0)), pl.BlockSpec((1, d), lambda i: (0, 0))],
        out_specs=pl.BlockSpec((tm, d), lambda i: (i, 0)),
        compiler_params=_cparams("parallel"),
        name="rmsnorm",
    )(x, g.reshape(1, d))


def _conv_kernel(b_ref, c_ref, h_ref, cp_ref, hp_ref, st_ref, w_ref, y_ref, ns_ref):
    i = pl.program_id(1)
    u = c_ref[...] * h_ref[...]
    tt = u.shape[0]
    prev = cp_ref[...] * hp_ref[...]
    st = st_ref[0]
    first = i == 0
    p1 = jnp.where(first, st[1:2], prev[7:8])
    p2 = jnp.where(first, st[0:1], prev[6:7])
    row = lax.broadcasted_iota(jnp.int32, u.shape, 0)
    u1 = jnp.where(row == 0, p1, pltpu.roll(u, 1, 0))
    u2 = jnp.where(row == 0, p2, jnp.where(row == 1, p1, pltpu.roll(u, 2, 0)))
    w = w_ref[...]
    y = w[0:1] * u2 + w[1:2] * u1 + w[2:3] * u
    y_ref[...] = b_ref[...] * y
    ns_ref[0] = u[tt - 2:tt]


def gated_conv(proj, state, conv_w, batch, t, *, tt=512):
    c = conv_w.shape[1]
    tt = _row_tile(t, tt)
    nt = t // tt
    r8 = tt // 8

    def prev_map(col):
        return lambda b, i: (jnp.maximum((b * nt + i) * r8 - 1, 0), col)

    return pl.pallas_call(
        _conv_kernel,
        out_shape=(jax.ShapeDtypeStruct((batch * t, c), F32), jax.ShapeDtypeStruct((batch, 2, c), F32)),
        grid=(batch, nt),
        in_specs=[
            pl.BlockSpec((tt, c), lambda b, i: (b * nt + i, 0)),
            pl.BlockSpec((tt, c), lambda b, i: (b * nt + i, 1)),
            pl.BlockSpec((tt, c), lambda b, i: (b * nt + i, 2)),
            pl.BlockSpec((8, c), prev_map(1)),
            pl.BlockSpec((8, c), prev_map(2)),
            pl.BlockSpec((1, 2, c), lambda b, i: (b, 0, 0)),
            pl.BlockSpec((CONV_WIDTH, c), lambda b, i: (0, 0)),
        ],
        out_specs=(
            pl.BlockSpec((tt, c), lambda b, i: (b * nt + i, 0)),
            pl.BlockSpec((1, 2, c), lambda b, i: (b, 0, 0)),
        ),
        compiler_params=_cparams("parallel", "arbitrary"),
        name="gated_conv",
    )(proj, proj, proj, proj, proj, state, conv_w)


def _mem_attn_kernel(q_ref, kv_ref, o_ref):
    nb = kv_ref.shape[0]
    tt = q_ref.shape[0] // nb
    for i in range(nb):
        q = q_ref[i * tt:(i + 1) * tt, :] * ATT_SCALE
        kv = kv_ref[i]
        outs = []
        for h in range(MEM_HEADS):
            qh = q[:, h * HEAD_DIM:(h + 1) * HEAD_DIM].astype(BF16)
            kh = kv[:, h * HEAD_DIM:(h + 1) * HEAD_DIM].astype(BF16)
            vh = kv[:, MEM_DIM + h * HEAD_DIM:MEM_DIM + (h + 1) * HEAD_DIM].astype(BF16)
            s = lax.dot_general(qh, kh, (((1,), (1,)), ((), ())), preferred_element_type=F32)
            m = jnp.max(s, axis=-1, keepdims=True)
            e = jnp.exp(s - m)
            p = e / jnp.sum(e, axis=-1, keepdims=True)
            outs.append(jnp.dot(p.astype(BF16), vh, preferred_element_type=F32))
        o_ref[i * tt:(i + 1) * tt, :] = jnp.concatenate(outs, axis=-1)


def mem_attention(proj, col_block, mem_kv, batch, t, *, tt=1024, group=4):
    tt = _row_tile(t, tt)
    nt = t // tt
    nb = group if (nt == 1 and batch % group == 0) else 1
    n_mem = mem_kv.shape[1]
    return pl.pallas_call(
        _mem_attn_kernel,
        out_shape=jax.ShapeDtypeStruct((batch * t, MEM_DIM), F32),
        grid=(batch // nb, nt),
        in_specs=[
            pl.BlockSpec((nb * tt, MEM_DIM), lambda b, i: (b * nt + i, col_block)),
            pl.BlockSpec((nb, n_mem, 2 * MEM_DIM), lambda b, i: (b, 0, 0)),
        ],
        out_specs=pl.BlockSpec((nb * tt, MEM_DIM), lambda b, i: (b * nt + i, 0)),
        compiler_params=_cparams("parallel", "arbitrary"),
        name="mem_attention",
    )(proj, mem_kv)


def _alibi_list(n):
    def pow2(m):
        start = 2.0 ** (-8.0 / m)
        return [start ** (i + 1) for i in range(m)]
    if n & (n - 1) == 0:
        return pow2(n)
    c = 2 ** int(math.floor(math.log2(n)))
    return pow2(c) + _alibi_list(2 * c)[0::2][: n - c]


LANES = 128
PAIR = LANES // HEAD_DIM


CMP_PITCH = CMP_STRIDE + 1


def _to_lane_columns(src, dst_ref, n_rows):
    for c in range(dst_ref.shape[0]):
        dst_ref[c, 0:n_rows, :] = src[0:n_rows, c * LANES:(c + 1) * LANES]


def _compress_tile(buf_ref, row0, m, pe_refs, w1_refs, w2_refs, pitch=CMP_STRIDE):
    outs = []
    base = row0 // CMP_STRIDE * pitch
    for kv in range(2):
        pe = pe_refs[kv][...]
        lhs = []
        for p in range(NSA_KV_HEADS // PAIR):
            col = kv * (KV_DIM // LANES) + p
            pieces = [
                (buf_ref[col, pl.ds(base + j // CMP_STRIDE * pitch + j % CMP_STRIDE, m, stride=pitch), :]
                 + pe[j:j + 1]).astype(BF16)
                for j in range(CMP_LEN)
            ]
            lhs.append(jnp.concatenate(pieces, axis=1))
        lhs = jnp.concatenate(lhs, axis=0)
        hid = jnp.dot(lhs, w1_refs[kv][...], preferred_element_type=F32)
        act = (hid * jax.nn.sigmoid(hid)).astype(BF16)
        out = jnp.dot(act, w2_refs[kv][...], preferred_element_type=F32)
        outs.append(jnp.concatenate([out[0:m], out[m:2 * m]], axis=1))
    return outs


def _compress_prompt_kernel(x_ref, pek_ref, pev_ref, w1k_ref, w1v_ref, w2k_ref, w2v_ref, ck_ref, cv_ref, buf_ref):
    t = x_ref.shape[2]
    for c in range(buf_ref.shape[0]):
        for r in range(0, t, LANES):
            buf_ref[c, r:r + LANES, :] = x_ref[0, c * LANES:(c + 1) * LANES, r:r + LANES].T
    buf_ref[:, t:t + CMP_STRIDE, :] = jnp.zeros((buf_ref.shape[0], CMP_STRIDE, LANES), F32)
    n_blk = t // CMP_STRIDE
    m = min(n_blk, 128)
    for s in range(n_blk // m):
        ck, cv = _compress_tile(buf_ref, s * m * CMP_STRIDE, m, (pek_ref, pev_ref), (w1k_ref, w1v_ref),
                                (w2k_ref, w2v_ref))
        ck_ref[0, s * m:(s + 1) * m, :] = ck
        cv_ref[0, s * m:(s + 1) * m, :] = cv


def _compress_weights(pe, w1, w2):
    eye = jnp.eye(PAIR, dtype=F32)
    w1p = jnp.einsum('jde,qr->jqdre', w1, eye).reshape(CMP_LEN * PAIR * HEAD_DIM, PAIR * w1.shape[2])
    w2p = jnp.einsum('ed,qr->qerd', w2, eye).reshape(PAIR * w2.shape[0], PAIR * HEAD_DIM)
    pe2 = jnp.tile(pe, (1, PAIR))
    return pe2, w1p.astype(BF16), w2p.astype(BF16)


def compress_prompt(kv_t, cw):
    pek, w1k, w2k, pev, w1v, w2v = cw
    batch, _, t = kv_t.shape
    n_blk = t // CMP_STRIDE
    full = lambda a: pl.BlockSpec(a.shape, lambda b: (0,) * a.ndim)
    return pl.pallas_call(
        _compress_prompt_kernel,
        out_shape=(jax.ShapeDtypeStruct((batch, n_blk, KV_DIM), F32),) * 2,
        grid=(batch,),
        in_specs=[pl.BlockSpec((1, 2 * KV_DIM, t), lambda b: (b, 0, 0)), full(pek), full(pev), full(w1k), full(w1v),
                  full(w2k), full(w2v)],
        out_specs=(pl.BlockSpec((1, n_blk, KV_DIM), lambda b: (b, 0, 0)),) * 2,
        scratch_shapes=[pltpu.VMEM((2 * KV_DIM // LANES, t + CMP_STRIDE, LANES), F32)],
        compiler_params=_cparams("parallel"),
        name="compress_prompt",
    )(kv_t, pek, pev, w1k, w1v, w2k, w2v)


N_CMP_POS = 256
N_SEL_POS = N_CMP_POS * CMP_STRIDE // SEL_BLOCK
MASK_BIG = 2.0 ** 100
M_INIT = -1e38


LOG2E = 1.4426950408889634
MASK_NONE, MASK_CAUSAL, MASK_BAND = 0, 1, 2


def _topk_mask(score):
    tq, nb = score.shape
    pad = jnp.full((tq, LANES - nb), -jnp.inf, F32)
    st = jnp.concatenate([score, pad], axis=1).T[:nb]
    sub = 8
    idx = lax.broadcasted_iota(jnp.int32, (sub, tq), 0)
    groups = [st[r:r + sub] for r in range(0, nb, sub)]
    ranks = [jnp.zeros((sub, tq), F32) for _ in groups]
    for j in range(nb):
        rj = st[j:j + 1, :]
        for gi, sg in enumerate(groups):
            if gi * sub > j:
                before = rj >= sg
            elif gi * sub + sub - 1 < j:
                before = rj > sg
            else:
                before = jnp.where(idx > j - gi * sub, (rj >= sg).astype(F32), (rj > sg).astype(F32)) > 0.0
            ranks[gi] = jnp.where(before, ranks[gi] + 1.0, ranks[gi])
    sel_t = (jnp.concatenate(ranks, axis=0) < N_SEL).astype(F32)
    blk_any = jnp.max(sel_t, axis=1, keepdims=True)
    sel_t = jnp.concatenate([sel_t, jnp.zeros((LANES - nb, tq), F32)], axis=0)
    return sel_t.T[:, :nb], blk_any


def _chunk_words(blk_any, per_chunk):
    nb = blk_any.shape[0]
    per_word = 8 * per_chunk
    bidx = lax.broadcasted_iota(jnp.int32, (nb, 1), 0)
    wgt = jnp.left_shift(1, 3 * ((bidx % per_word) // per_chunk)).astype(F32)
    words = []
    for w in range(nb // per_word):
        v = jnp.sum(jnp.where(bidx // per_word == w, blk_any * wgt, 0.0), axis=0, keepdims=True)
        words.append(v.astype(jnp.int32)[0, 0])
    return words


def _nsa_prompt_kernel(sl_ref, q_ref, ck_ref, cv_ref, kc_ref, vc_ref, o_ref,
                       q_scr, s0_scr, s1_scr, p0_scr, p1_scr, a0_scr, a1_scr, m_scr, l_scr, acc_scr, mb_scr, lst_ref,
                       *, mm_rows, sm_rows):
    kvh = pl.program_id(1)
    i = pl.program_id(2)
    s_scrs, p_scrs, a_scrs = (s0_scr, s1_scr), (p0_scr, p1_scr), (a0_scr, a1_scr)
    tq = q_ref.shape[0]
    kb = tq
    t_len = kc_ref.shape[2] // 2
    t0 = i * tq
    slopes = [sl_ref[kvh * NSA_GROUP + g] * LOG2E for g in range(NSA_GROUP)]
    q = q_ref[...] * (ATT_SCALE * LOG2E)
    q3 = jnp.concatenate([q[:, g * HEAD_DIM:(g + 1) * HEAD_DIM] for g in range(NSA_GROUP)], axis=0)
    q3b = q3.astype(BF16)
    nt_dims = (((1,), (1,)), ((), ()))
    t_col = t0 + lax.broadcasted_iota(jnp.int32, (tq, 1), 0)

    @pl.when(i == 0)
    def _():
        r = lax.broadcasted_iota(jnp.int32, (tq, kb), 0)
        c = lax.broadcasted_iota(jnp.int32, (tq, kb), 1)
        mb_scr[MASK_NONE] = jnp.zeros((tq, kb), F32)
        mb_scr[MASK_CAUSAL] = jnp.where(c <= r, 0.0, -MASK_BIG)
        mb_scr[MASK_BAND] = jnp.where(c >= r, 0.0, -MASK_BIG)

    m_scr[...] = jnp.full(m_scr.shape, M_INIT, F32)
    l_scr[...] = jnp.zeros(l_scr.shape, F32)
    acc_scr[...] = jnp.zeros(acc_scr.shape, F32)

    pos = lax.broadcasted_iota(jnp.int32, (1, N_CMP_POS), 1)
    blk_n = (pos % N_SEL_POS) * (N_CMP_POS // N_SEL_POS) + pos // N_SEL_POS
    c_end = blk_n * CMP_STRIDE + (CMP_LEN - 1)
    n_real = t_len // CMP_STRIDE - 1
    vis_c = (c_end <= t_col) & (blk_n < n_real)
    c_end_f = c_end.astype(F32)
    s_c = lax.dot_general(q3b, ck_ref[0, 0], nt_dims, preferred_element_type=F32)
    p_c = []
    for g in range(NSA_GROUP):
        sg = jnp.where(vis_c, s_c[g * tq:(g + 1) * tq] + slopes[g] * c_end_f, NEG_INF)
        mg = jnp.max(sg, axis=1, keepdims=True)
        eg = jnp.where(vis_c, jnp.exp2(sg - mg), 0.0)
        dg = jnp.sum(eg, axis=1, keepdims=True)
        p_c.append(eg / jnp.where(dg > 0.0, dg, 1.0))
    o_c = jnp.dot(jnp.concatenate(p_c, axis=0).astype(BF16), cv_ref[0, 0], preferred_element_type=F32)
    imp = p_c[0] + p_c[1] + p_c[2]
    imp = imp[:, :LANES] + imp[:, LANES:]
    imp = imp[:, :N_SEL_POS] + imp[:, N_SEL_POS:]

    blk = lax.broadcasted_iota(jnp.int32, (1, N_SEL_POS), 1)
    cur = t_col // SEL_BLOCK
    forced = (blk == 0) | (blk == cur) | (blk == cur - 1)
    score = jnp.where(forced, FORCE_SCORE, jnp.where(blk <= cur, imp, -jnp.inf))
    sel, blk_any = _topk_mask(score)
    notsel = (1.0 - sel).astype(BF16)
    q_scr[...] = jnp.concatenate([q3b, jnp.concatenate([notsel] * NSA_GROUP, axis=0)], axis=1)

    lane_k = lax.broadcasted_iota(jnp.int32, (1, kb), 1)
    n_win = WINDOW // kb + 1

    words = _chunk_words(blk_any, kb // SEL_BLOCK)
    n_sel = jnp.int32(0)
    for c in range(N_SEL_POS * SEL_BLOCK // kb):
        active = (((words[c // 8] >> (3 * (c % 8))) & 7) != 0) & (c <= i)
        lst_ref[n_sel] = c
        n_sel = n_sel + active.astype(jnp.int32)
    n_chunks = n_sel + n_win

    def chunk(p):
        is_sel = p < n_sel
        is_pad = p >= n_chunks
        c = lst_ref[jnp.minimum(p, n_sel - 1)]
        w = p - n_sel
        kpos0 = jnp.where(is_sel, c * kb, jnp.where(is_pad, -kb, t0 - WINDOW + w * kb))
        row0 = jnp.maximum(kpos0, 0) + jnp.where(is_sel | is_pad, 0, t_len)
        mtype = jnp.where(is_sel, jnp.where(c == i, MASK_CAUSAL, MASK_NONE),
                          jnp.where(w == 0, MASK_BAND, jnp.where(w == n_win - 1, MASK_CAUSAL, MASK_NONE)))
        return pl.multiple_of(row0, kb), kpos0, mtype, jnp.where(is_sel | is_pad, 0, 1)

    def stage_logits(c, par):
        s_scr = s_scrs[par]
        row0, kpos0, mtype, _ = chunk(c)
        k = kc_ref[0, 0, pl.ds(row0, kb), :]
        kpos_f = (kpos0 + lane_k).astype(F32)
        off = jnp.where(kpos0 >= 0, 0.0, -MASK_BIG)
        for g in range(NSA_GROUP):
            bias = slopes[g] * kpos_f + off
            for r in range(0, tq, mm_rows):
                rows = slice(g * tq + r, g * tq + r + mm_rows)
                s = lax.dot_general(q_scr[rows, :], k, nt_dims, preferred_element_type=F32)
                s_scr[rows, :] = (s + mb_scr[mtype, r:r + mm_rows, :]) + bias

    def stage_softmax(c, par):
        s_scr, p_scr, a_scr = s_scrs[par], p_scrs[par], a_scrs[par]
        st = chunk(c)[3]
        for r in range(0, NSA_GROUP * tq, sm_rows):
            rows = slice(r, r + sm_rows)
            s = s_scr[rows, :]
            m_prev = m_scr[st, rows, :]
            m_new = jnp.maximum(m_prev, jnp.max(s, axis=1, keepdims=True))
            alpha = jnp.exp2(m_prev - m_new)
            p = jnp.exp2(s - jnp.concatenate([m_new] * (kb // LANES), axis=1))
            l_scr[st, rows, :] = alpha * l_scr[st, rows, :] + jnp.sum(p, axis=1, keepdims=True)
            p_scr[rows, :] = p.astype(BF16)
            a_scr[rows, :] = alpha
            m_scr[st, rows, :] = m_new

    def stage_values(c, par):
        p_scr, a_scr = p_scrs[par], a_scrs[par]
        row0, _, _, st = chunk(c)
        v = vc_ref[0, 0, pl.ds(row0, kb), :]
        for r in range(0, NSA_GROUP * tq, mm_rows):
            rows = slice(r, r + mm_rows)
            pv = jnp.dot(p_scr[rows, :], v, preferred_element_type=F32)
            acc_scr[st, rows, :] = acc_scr[st, rows, :] * a_scr[rows, :HEAD_DIM] + pv

    stage_logits(0, 0)
    stage_softmax(0, 0)
    stage_logits(1, 1)

    def pipe_body(j, carry):
        c = 2 * j
        stage_values(c - 2, 0)
        stage_softmax(c - 1, 1)
        stage_logits(c, 0)
        stage_values(c - 1, 1)
        stage_softmax(c, 0)
        stage_logits(c + 1, 1)
        return carry

    n_even = n_chunks + n_chunks % 2
    lax.fori_loop(1, n_even // 2, pipe_body, 0)
    stage_values(n_even - 2, 0)
    stage_softmax(n_even - 1, 1)
    stage_values(n_even - 1, 1)
    o_s = acc_scr[0] / l_scr[0][:, :HEAD_DIM]
    o_w = acc_scr[1] / l_scr[1][:, :HEAD_DIM]

    gates = jax.nn.sigmoid(q_ref[:, Q_GATE_LANE:Q_SLOT])
    outs = []
    for g in range(NSA_GROUP):
        rows = slice(g * tq, (g + 1) * tq)
        outs.append(gates[:, 3 * g:3 * g + 1] * o_c[rows] + gates[:, 3 * g + 1:3 * g + 2] * o_s[rows]
                    + gates[:, 3 * g + 2:3 * g + 3] * o_w[rows])
    o_ref[0, 0] = jnp.concatenate(outs, axis=1)


def nsa_prompt(proj, slopes, ckp, cvp, kcat, vcat, batch, t, *, tq=256, mm_rows=256, sm_rows=256):
    assert t % tq == 0 and WINDOW % tq == 0 and WINDOW >= tq
    nt = t // tq
    rows = NSA_GROUP * tq
    grid_spec = pltpu.PrefetchScalarGridSpec(
        num_scalar_prefetch=1,
        grid=(batch, NSA_KV_HEADS, nt),
        in_specs=[
            pl.BlockSpec((tq, Q_SLOT), lambda b, h, i, sl: (b * nt + i, h)),
            pl.BlockSpec((1, 1, N_CMP_POS, HEAD_DIM), lambda b, h, i, sl: (b, h, 0, 0)),
            pl.BlockSpec((1, 1, N_CMP_POS, HEAD_DIM), lambda b, h, i, sl: (b, h, 0, 0)),
            pl.BlockSpec((1, 1, 2 * t, LANES), lambda b, h, i, sl: (b, h, 0, 0)),
            pl.BlockSpec((1, 1, 2 * t, HEAD_DIM), lambda b, h, i, sl: (b, h, 0, 0)),
        ],
        out_specs=pl.BlockSpec((1, 1, tq, NSA_GROUP * HEAD_DIM), lambda b, h, i, sl: (b, h, i, 0)),
        scratch_shapes=[
            pltpu.VMEM((rows, LANES), BF16),
            pltpu.VMEM((rows, tq), F32), pltpu.VMEM((rows, tq), F32),
            pltpu.VMEM((rows, tq), BF16), pltpu.VMEM((rows, tq), BF16),
            pltpu.VMEM((rows, LANES), F32), pltpu.VMEM((rows, LANES), F32),
            pltpu.VMEM((2, rows, LANES), F32),
            pltpu.VMEM((2, rows, LANES), F32),
            pltpu.VMEM((2, rows, HEAD_DIM), F32),
            pltpu.VMEM((3, tq, tq), F32),
            pltpu.SMEM((N_SEL_POS * SEL_BLOCK // tq,), jnp.int32),
        ],
    )
    return pl.pallas_call(
        functools.partial(_nsa_prompt_kernel, mm_rows=mm_rows, sm_rows=sm_rows),
        out_shape=jax.ShapeDtypeStruct((batch, NSA_KV_HEADS, t, NSA_GROUP * HEAD_DIM), F32),
        grid_spec=grid_spec,
        compiler_params=_cparams("parallel", "parallel", "arbitrary"),
        name="nsa_prompt",
    )(slopes, proj, ckp, cvp, kcat, vcat)


def _kv_layout_kernel(x_ref, kc_ref, vc_ref):
    tm = x_ref.shape[2]
    pos = pl.program_id(1) * tm + lax.broadcasted_iota(jnp.int32, (tm, N_SEL_POS), 0)
    blk = lax.broadcasted_iota(jnp.int32, (tm, N_SEL_POS), 1)
    onehot = jnp.where(pos // SEL_BLOCK == blk, -MASK_BIG, 0.0).astype(BF16)
    zeros = jnp.zeros((tm, N_SEL_POS), BF16)
    for p in range(NSA_KV_HEADS // PAIR):
        pair = lambda c0: x_ref[0, c0 + p * LANES:c0 + (p + 1) * LANES, :].T.astype(BF16)
        ks, vs, kw, vw = pair(2 * KV_DIM), pair(3 * KV_DIM), pair(4 * KV_DIM), pair(5 * KV_DIM)
        for q in range(PAIR):
            h = p * PAIR + q
            lanes = slice(q * HEAD_DIM, (q + 1) * HEAD_DIM)
            kc_ref[0, h, 0] = jnp.concatenate([ks[:, lanes], onehot], axis=1)
            kc_ref[0, h, 1] = jnp.concatenate([kw[:, lanes], zeros], axis=1)
            vc_ref[0, h, 0] = vs[:, lanes]
            vc_ref[0, h, 1] = vw[:, lanes]


def kv_layout(kv_t, *, tm=512):
    batch, n_feat, t = kv_t.shape
    tm = _row_tile(t, tm)
    nt = t // tm
    kc, vc = pl.pallas_call(
        _kv_layout_kernel,
        out_shape=(jax.ShapeDtypeStruct((batch, NSA_KV_HEADS, 2, t, LANES), BF16),
                   jax.ShapeDtypeStruct((batch, NSA_KV_HEADS, 2, t, HEAD_DIM), BF16)),
        grid=(batch, nt),
        in_specs=[pl.BlockSpec((1, n_feat, tm), lambda b, i: (b, 0, i))],
        out_specs=(pl.BlockSpec((1, NSA_KV_HEADS, 2, tm, LANES), lambda b, i: (b, 0, 0, i, 0)),
                   pl.BlockSpec((1, NSA_KV_HEADS, 2, tm, HEAD_DIM), lambda b, i: (b, 0, 0, i, 0))),
        compiler_params=_cparams("parallel", "parallel"),
        name="kv_layout",
    )(kv_t)
    return (kc.reshape(batch, NSA_KV_HEADS, 2 * t, LANES), vc.reshape(batch, NSA_KV_HEADS, 2 * t, HEAD_DIM))


def _cmp_slots(c, batch):
    n_blk = c.shape[1]
    per = N_CMP_POS // N_SEL_POS
    c = jnp.pad(c, ((0, 0), (0, N_CMP_POS - n_blk), (0, 0)))
    c = c.reshape(batch, N_SEL_POS, per, NSA_KV_HEADS, HEAD_DIM).transpose(0, 3, 2, 1, 4)
    return c.reshape(batch, NSA_KV_HEADS, N_CMP_POS, HEAD_DIM).astype(BF16)


def _compress_sample_kernel(pt_ref, pool_ref, new_ref, pek_ref, pev_ref, w1k_ref, w1v_ref, w2k_ref, w2v_ref,
                            ck_ref, cv_ref, buf_ref, sem_ref, col_ref, *, n_pages, pg):
    b = pl.program_id(0)
    g = pl.program_id(1)
    ng = n_pages // pg
    step = b * ng + g
    n_steps = pl.num_programs(0) * ng
    slot = step % 2
    page_rows = buf_ref.shape[3]
    rows = pg * page_rows
    ts = new_ref.shape[1]

    def pages(st, sl, go):
        bb = st // ng
        gg = st % ng

        def copy(k, dst):
            cp = pltpu.make_async_copy(pool_ref.at[pt_ref[bb * n_pages + gg * pg + k]], buf_ref.at[sl, dst],
                                       sem_ref.at[sl])
            cp.start() if go == "start" else cp.wait()

        for k in range(pg):
            copy(k, k)

        @pl.when(gg < ng - 1)
        def _():
            copy(pg, pg)

    @pl.when(step == 0)
    def _():
        pages(0, 0, "start")

    @pl.when(step + 1 < n_steps)
    def _():
        pages(step + 1, 1 - slot, "start")

    pages(step, slot, "wait")
    n_col = col_ref.shape[0]
    per_page = page_rows // CMP_STRIDE
    for k in range(pg):
        for c in range(n_col):
            tile = buf_ref[slot, k, c * LANES:(c + 1) * LANES, :].T
            for q in range(per_page):
                r0 = (k * per_page + q) * CMP_PITCH
                col_ref[c, r0:r0 + CMP_STRIDE, :] = tile[q * CMP_STRIDE:(q + 1) * CMP_STRIDE]
    halo = pg * per_page * CMP_PITCH

    @pl.when(g < ng - 1)
    def _():
        for c in range(n_col):
            col_ref[c, halo:halo + CMP_STRIDE, :] = buf_ref[slot, pg, c * LANES:(c + 1) * LANES, :].T[0:CMP_STRIDE]

    @pl.when(g == ng - 1)
    def _():
        for c in range(n_col):
            col_ref[c, halo:halo + ts, :] = new_ref[0, :, c * LANES:(c + 1) * LANES]
            col_ref[c, halo + ts:halo + CMP_STRIDE, :] = jnp.zeros((CMP_STRIDE - ts, LANES), F32)

    ck, cv = _compress_tile(col_ref, 0, rows // CMP_STRIDE, (pek_ref, pev_ref), (w1k_ref, w1v_ref),
                            (w2k_ref, w2v_ref), pitch=CMP_PITCH)
    ck_ref[0] = ck
    cv_ref[0] = cv


def compress_sample(page_table, pool_t, new_rows, cw, *, pg):
    pek, w1k, w2k, pev, w1v, w2v = cw
    batch, n_pages = page_table.shape
    page_rows = pool_t.shape[2]
    ts = new_rows.shape[1]
    ng = n_pages // pg
    m = pg * page_rows // CMP_STRIDE
    full = lambda a: pl.BlockSpec(a.shape, lambda b, g, pt: (0,) * a.ndim)
    grid_spec = pltpu.PrefetchScalarGridSpec(
        num_scalar_prefetch=1,
        grid=(batch, ng),
        in_specs=[pl.BlockSpec(memory_space=pl.ANY),
                  pl.BlockSpec((1, ts, new_rows.shape[2]), lambda b, g, pt: (b, 0, 0)),
                  full(pek), full(pev), full(w1k), full(w1v), full(w2k), full(w2v)],
        out_specs=(pl.BlockSpec((1, m, KV_DIM), lambda b, g, pt: (b, g, 0)),) * 2,
        scratch_shapes=[pltpu.VMEM((2, pg + 1, 2 * KV_DIM, page_rows), F32), pltpu.SemaphoreType.DMA((2,)),
                        pltpu.VMEM((2 * KV_DIM // LANES, -(-(m + 1) * CMP_PITCH // 8) * 8, LANES), F32)],
    )
    return pl.pallas_call(
        functools.partial(_compress_sample_kernel, n_pages=n_pages, pg=pg),
        out_shape=(jax.ShapeDtypeStruct((batch, ng * m, KV_DIM), F32),) * 2,
        grid_spec=grid_spec,
        compiler_params=_cparams("arbitrary", "arbitrary"),
        name="compress_sample",
    )(page_table.reshape(-1), pool_t, new_rows, pek, pev, w1k, w1v, w2k, w2v)


S_COL_G = 32


def _softmax_lanes(pieces):
    m = functools.reduce(jnp.maximum, [jnp.max(s, axis=1, keepdims=True) for s in pieces])
    es = [jnp.where(s > 0.5 * NEG_INF, jnp.exp(s - m), 0.0) for s in pieces]
    d = functools.reduce(jnp.add, [jnp.sum(e, axis=1, keepdims=True) for e in es])
    inv = 1.0 / jnp.where(d > 0.0, d, 1.0)
    return [e * inv for e in es]


def _nsa_decode_kernel(sl_ref, pt_ref, proj_ref, ck_ref, cv_ref, new_ref, win_ref, e_ref, pool_ref, y_ref,
                       buf_ref, sem_ref, q_scr, m_scr, l_scr, acc_scr, oc_scr, ow_scr,
                       *, n_pages, pg, past):
    b = pl.program_id(0)
    g = pl.program_id(1)
    ng = n_pages // pg
    step = b * ng + g
    n_steps = pl.num_programs(0) * ng
    slot = step % 2
    page_rows = buf_ref.shape[3]
    rows = pg * page_rows
    ts = proj_ref.shape[0]
    nt_dims = (((1,), (1,)), ((), ()))

    def pages(st, sl, go):
        bb = st // ng
        gg = st % ng
        for k in range(pg):
            cp = pltpu.make_async_copy(pool_ref.at[pt_ref[bb * n_pages + gg * pg + k]], buf_ref.at[sl, k],
                                       sem_ref.at[sl])
            cp.start() if go == "start" else cp.wait()

    @pl.when(step == 0)
    def _():
        pages(0, 0, "start")

    @pl.when(step + 1 < n_steps)
    def _():
        pages(step + 1, 1 - slot, "start")

    row = lax.broadcasted_iota(jnp.int32, (LANES, 1), 0)
    row_g = row // S_COL_G
    row_h = (row % S_COL_G) // ts
    t_col = past + row % ts
    slope_col = jnp.zeros((LANES, 1), F32)
    for h in range(NSA_HEADS):
        slope_col = jnp.where((row_h == h // NSA_GROUP) & (row_g == h % NSA_GROUP), sl_ref[h], slope_col)
    n_sb = past // SEL_BLOCK + 1
    zpad = jnp.zeros((LANES - ts, KV_DIM), F32)
    lane_new = lax.broadcasted_iota(jnp.int32, (1, LANES), 1)
    kp_new = past + lane_new
    vis_new = (kp_new <= t_col) & (lane_new < ts)

    @pl.when(g == 0)
    def _():
        lane_h = lax.broadcasted_iota(jnp.int32, (ts, KV_DIM), 1) // HEAD_DIM
        tiles = []
        for gg in range(NSA_GROUP):
            qg = proj_ref[:, gg * KV_DIM:(gg + 1) * KV_DIM] * ATT_SCALE
            for h in range(NSA_KV_HEADS):
                tiles.append(jnp.where(lane_h == h, qg, 0.0))
        tiles.append(jnp.zeros((LANES - NSA_HEADS * ts, KV_DIM), F32))
        bdq = jnp.concatenate(tiles, axis=0).astype(BF16)
        q_scr[:, 0:KV_DIM] = bdq

        n_c = ck_ref.shape[1]
        per = SEL_BLOCK // CMP_STRIDE
        n_q = n_c // per
        s_c = lax.dot_general(bdq, ck_ref[0].astype(BF16), nt_dims, preferred_element_type=F32)
        slot_c = lax.broadcasted_iota(jnp.int32, (1, n_c), 1)
        c_end = ((slot_c % n_q) * per + slot_c // n_q) * CMP_STRIDE + (CMP_LEN - 1)
        dist = t_col - c_end
        s_c = jnp.where(dist >= 0, s_c - slope_col * dist.astype(F32), NEG_INF)
        (p_c,) = _softmax_lanes([s_c])
        oc_scr[...] = jnp.dot(p_c.astype(BF16), cv_ref[0].astype(BF16), preferred_element_type=F32)
        imp = p_c[0:S_COL_G] + p_c[S_COL_G:2 * S_COL_G] + p_c[2 * S_COL_G:3 * S_COL_G]
        imp = functools.reduce(jnp.add, [imp[:, r * n_q:(r + 1) * n_q] for r in range(per)])
        n_blk = q_scr.shape[1] - KV_DIM
        imp = jnp.concatenate([imp, jnp.zeros((S_COL_G, n_blk - n_q), F32)], axis=1)
        blk = lax.broadcasted_iota(jnp.int32, (1, n_blk), 1)
        cur = t_col[0:S_COL_G] // SEL_BLOCK
        forced = (blk == 0) | (blk == cur) | (blk == cur - 1)
        score = jnp.where(forced, FORCE_SCORE, jnp.where((blk <= cur) & (blk < n_sb), imp, -jnp.inf))
        rank = jnp.zeros(score.shape, F32)
        for j in range(n_sb):
            rj = score[:, j:j + 1]
            rank = rank + jnp.where(blk > j, (rj >= score).astype(F32), (rj > score).astype(F32))
        notsel = (rank >= N_SEL).astype(F32)
        notsel = jnp.concatenate([notsel] * NSA_GROUP + [jnp.ones((LANES - NSA_GROUP * S_COL_G, n_blk), F32)], axis=0)
        q_scr[:, KV_DIM:KV_DIM + n_blk] = notsel.astype(BF16)

        wl = win_ref.shape[2]
        kw_new = jnp.concatenate([new_ref[0, :, 4 * KV_DIM:5 * KV_DIM], zpad], axis=0).astype(BF16)
        vw_new = jnp.concatenate([new_ref[0, :, 5 * KV_DIM:6 * KV_DIM], zpad], axis=0).astype(BF16)
        s_w = jnp.dot(bdq, win_ref[0, 0:KV_DIM, :].astype(BF16), preferred_element_type=F32)
        w_pos = past - wl + lax.broadcasted_iota(jnp.int32, (1, wl), 1)
        dist_w = t_col - w_pos
        s_w = jnp.where((dist_w >= 0) & (dist_w <= WINDOW) & (w_pos >= 0), s_w - slope_col * dist_w.astype(F32),
                        NEG_INF)
        s_wn = lax.dot_general(bdq, kw_new, nt_dims, preferred_element_type=F32)
        dist_n = t_col - kp_new
        s_wn = jnp.where(vis_new & (dist_n <= WINDOW), s_wn - slope_col * dist_n.astype(F32), NEG_INF)
        p_w, p_wn = _softmax_lanes([s_w, s_wn])
        ow_scr[...] = (lax.dot_general(p_w.astype(BF16), win_ref[0, KV_DIM:2 * KV_DIM, :].astype(BF16), nt_dims,
                                       preferred_element_type=F32)
                       + jnp.dot(p_wn.astype(BF16), vw_new, preferred_element_type=F32))
        m_scr[...] = jnp.full(m_scr.shape, M_INIT, F32)
        l_scr[...] = jnp.zeros(l_scr.shape, F32)
        acc_scr[...] = jnp.zeros(acc_scr.shape, F32)

    def online_update(s, pv_fn):
        n = s.shape[1]
        m_prev = m_scr[...]
        m_new = jnp.maximum(m_prev, jnp.max(s, axis=1, keepdims=True))
        alpha = jnp.exp(m_prev - m_new)
        p = jnp.exp(s - jnp.concatenate([m_new] * (n // LANES), axis=1))
        l_scr[...] = alpha * l_scr[...] + jnp.sum(p, axis=1, keepdims=True)
        acc_scr[...] = acc_scr[...] * jnp.concatenate([alpha] * (KV_DIM // LANES), axis=1) + pv_fn(p.astype(BF16))
        m_scr[...] = m_new

    pages(step, slot, "wait")
    kt = jnp.concatenate([buf_ref[slot, k, 0:KV_DIM, :] for k in range(pg)], axis=1).astype(BF16)
    vt = jnp.concatenate([buf_ref[slot, k, KV_DIM:2 * KV_DIM, :] for k in range(pg)], axis=1).astype(BF16)
    n_blk = q_scr.shape[1] - KV_DIM
    s_g = (jnp.dot(q_scr[:, 0:KV_DIM], kt, preferred_element_type=F32)
           + jnp.dot(q_scr[:, KV_DIM:KV_DIM + n_blk], e_ref[0], preferred_element_type=F32))
    kpos = g * rows + lax.broadcasted_iota(jnp.int32, (1, rows), 1)
    s_g = s_g - slope_col * (t_col - kpos).astype(F32)
    online_update(s_g, lambda p: lax.dot_general(p, vt, nt_dims, preferred_element_type=F32))

    @pl.when(g == ng - 1)
    def _():
        k_new = jnp.concatenate([new_ref[0, :, 2 * KV_DIM:3 * KV_DIM], zpad], axis=0).astype(BF16)
        v_new = jnp.concatenate([new_ref[0, :, 3 * KV_DIM:4 * KV_DIM], zpad], axis=0).astype(BF16)
        s_n = lax.dot_general(q_scr[:, 0:KV_DIM], k_new, nt_dims, preferred_element_type=F32)
        unsel_last = q_scr[:, KV_DIM + n_sb - 1:KV_DIM + n_sb].astype(F32)
        dist_n = t_col - kp_new
        s_n = jnp.where(vis_new & (unsel_last < 0.5), s_n - slope_col * dist_n.astype(F32), -MASK_BIG)
        online_update(s_n, lambda p: jnp.dot(p, v_new, preferred_element_type=F32))
        o_s = acc_scr[...] / jnp.concatenate([l_scr[...]] * (KV_DIM // LANES), axis=1)
        o_c = oc_scr[...]
        o_w = ow_scr[...]
        gates = jax.nn.sigmoid(proj_ref[:, (NSA_GROUP + 1) * KV_DIM:(NSA_GROUP + 1) * KV_DIM + LANES])
        outs = []
        for h in range(NSA_HEADS):
            kvh, gg = h // NSA_GROUP, h % NSA_GROUP
            r0 = gg * S_COL_G + kvh * ts
            lanes = slice(kvh * HEAD_DIM, (kvh + 1) * HEAD_DIM)
            outs.append(gates[:, 3 * h:3 * h + 1] * o_c[r0:r0 + ts, lanes]
                        + gates[:, 3 * h + 1:3 * h + 2] * o_s[r0:r0 + ts, lanes]
                        + gates[:, 3 * h + 2:3 * h + 3] * o_w[r0:r0 + ts, lanes])
        y_ref[...] = jnp.concatenate(outs, axis=1)


def nsa_decode(proj, slopes, page_table, ckp, cvp, new_rows, win_t, pool_t, *, pg):
    batch, n_pages = page_table.shape
    page_rows = pool_t.shape[2]
    ts = new_rows.shape[1]
    past = n_pages * page_rows
    ng = n_pages // pg
    rows = pg * page_rows
    n_c = ckp.shape[1]
    wl = win_t.shape[2]
    n_blk = -(-(past // SEL_BLOCK + 1) // LANES) * LANES
    key_blk = jnp.arange(past, dtype=jnp.int32).reshape(ng, 1, rows) // SEL_BLOCK
    e_tab = jnp.where(key_blk == jnp.arange(n_blk, dtype=jnp.int32)[None, :, None], -MASK_BIG, 0.0).astype(BF16)
    grid_spec = pltpu.PrefetchScalarGridSpec(
        num_scalar_prefetch=2,
        grid=(batch, ng),
        in_specs=[
            pl.BlockSpec((ts, proj.shape[1]), lambda b, g, sl, pt: (b, 0)),
            pl.BlockSpec((1, n_c, KV_DIM), lambda b, g, sl, pt: (b, 0, 0)),
            pl.BlockSpec((1, n_c, KV_DIM), lambda b, g, sl, pt: (b, 0, 0)),
            pl.BlockSpec((1, ts, new_rows.shape[2]), lambda b, g, sl, pt: (b, 0, 0)),
            pl.BlockSpec((1, 2 * KV_DIM, wl), lambda b, g, sl, pt: (b, 0, 0)),
            pl.BlockSpec((1, n_blk, rows), lambda b, g, sl, pt: (g, 0, 0)),
            pl.BlockSpec(memory_space=pl.ANY),
        ],
        out_specs=pl.BlockSpec((ts, NSA_DIM), lambda b, g, sl, pt: (b, 0)),
        scratch_shapes=[
            pltpu.VMEM((2, pg, 2 * KV_DIM, page_rows), F32), pltpu.SemaphoreType.DMA((2,)),
            pltpu.VMEM((LANES, KV_DIM + n_blk), BF16),
            pltpu.VMEM((LANES, LANES), F32), pltpu.VMEM((LANES, LANES), F32), pltpu.VMEM((LANES, KV_DIM), F32),
            pltpu.VMEM((LANES, KV_DIM), F32), pltpu.VMEM((LANES, KV_DIM), F32),
        ],
    )
    return pl.pallas_call(
        functools.partial(_nsa_decode_kernel, n_pages=n_pages, pg=pg, past=past),
        out_shape=jax.ShapeDtypeStruct((batch * ts, NSA_DIM), F32),
        grid_spec=grid_spec,
        compiler_params=_cparams("arbitrary", "arbitrary"),
        name="nsa_decode",
    )(slopes, page_table.reshape(-1), proj, ckp, cvp, new_rows, win_t, e_tab, pool_t)


S_MQ_COL =NSA_GROUP * KV_DIM
S_COLS = S_MQ_COL + MEM_DIM + LANES


def _layout_w_in_b_sample(w):
    d = w.shape[0]
    qw = w[:, :NSA_DIM].reshape(d, NSA_KV_HEADS, NSA_GROUP, HEAD_DIM).transpose(0, 2, 1, 3).reshape(d, NSA_DIM)
    gw = jnp.pad(w[:, NSA_DIM:NSA_DIM + 3 * NSA_HEADS], ((0, 0), (0, LANES - 3 * NSA_HEADS)))
    mw = w[:, NSA_DIM + 3 * NSA_HEADS:]
    return jnp.concatenate([qw, mw, gw], axis=1).astype(BF16)


Q_SLOT = 256
Q_GATE_LANE = NSA_GROUP * HEAD_DIM
B_MQ_COL = NSA_KV_HEADS * Q_SLOT
B_COLS = B_MQ_COL + MEM_DIM


def _layout_w_in_b(w):
    d = w.shape[0]
    qw = w[:, :NSA_DIM].reshape(d, NSA_KV_HEADS, NSA_GROUP * HEAD_DIM)
    gw = w[:, NSA_DIM:NSA_DIM + 3 * NSA_HEADS].reshape(d, NSA_KV_HEADS, 3 * NSA_GROUP)
    slot = jnp.concatenate([qw, gw], axis=2)
    slot = jnp.pad(slot, ((0, 0), (0, 0), (0, Q_SLOT - slot.shape[2]))).reshape(d, NSA_KV_HEADS * Q_SLOT)
    mw = w[:, NSA_DIM + 3 * NSA_HEADS:]
    return jnp.concatenate([slot, mw], axis=1).astype(BF16)


def kernel(x_prompt, x_sample, state_conv, cache_mem_kv, cache_cmp_kv, cache_slc_kv, state_win_kv, page_table,
           mem_prompt, g_mix, w_in_a, conv_w, w_in_b, w_o, w_mkv, g_mem, g_kv, w_kv, pe_ck, w1_ck, w2_ck,
           pe_cv, w1_cv, w2_cv, g_ffn, w_gu, w_dn, g_final):
    bp, tp, d = x_prompt.shape
    bs, ts = x_sample.shape[:2]
    depth = g_mix.shape[0]
    n_a = w_in_a.shape[0]
    n_mem = mem_prompt.shape[1]
    win_len = state_win_kv.shape[1]
    past_len = page_table.shape[1] * cache_cmp_kv.shape[1]
    conv_dim = conv_w.shape[2]
    slopes = jnp.asarray(np.array(_alibi_list(NSA_HEADS), dtype=np.float32))

    w_in_a16 = w_in_a.astype(BF16)
    w_in_b16 = [_layout_w_in_b(w_in_b[j]) for j in range(depth - n_a)]
    w_o16 = w_o.astype(BF16)
    w_gu16 = w_gu.astype(BF16)
    w_dn16 = w_dn.astype(BF16)
    w_kv16 = w_kv.astype(BF16)
    w_mkv16 = w_mkv.transpose(1, 0, 2).reshape(d, depth * 2 * MEM_DIM).astype(BF16)

    mkv = rms_matmul(mem_prompt.reshape(bp * n_mem, d), g_mem, w_mkv16)
    mem_kv_p = mkv.reshape(bp, n_mem, depth, 2 * MEM_DIM).transpose(2, 0, 1, 3)
    mem_kv_s = cache_mem_kv.reshape(depth, bs, n_mem, 2 * MEM_DIM)

    groups = [
        dict(x=x_prompt.reshape(bp * tp, d), b=bp, t=tp, mem=mem_kv_p, st=jnp.zeros((n_a, bp, 2, conv_dim), F32)),
        dict(x=x_sample.reshape(bs * ts, d), b=bs, t=ts, mem=mem_kv_s, st=state_conv),
    ]
    conv_out = [[], []]
    kv_rows = [None, None]

    cw = _compress_weights(pe_ck, w1_ck, w2_ck) + _compress_weights(pe_cv, w1_cv, w2_cv)
    nsa_in = None
    w_in_bs16 = [_layout_w_in_b_sample(w_in_b[j]) for j in range(depth - n_a)]
    n_pages = page_table.shape[1]
    page_rows = cache_cmp_kv.shape[1]
    pg = 16 if n_pages % 16 == 0 else n_pages
    assert page_rows == LANES
    pool_cmp = cache_cmp_kv.transpose(0, 2, 3, 4, 1).reshape(cache_cmp_kv.shape[0], 2 * KV_DIM, page_rows)
    pool_slc = cache_slc_kv.transpose(0, 2, 3, 4, 1).reshape(cache_slc_kv.shape[0], 2 * KV_DIM, page_rows)
    win_state = state_win_kv.transpose(0, 2, 3, 4, 1).reshape(bs, 2 * KV_DIM, win_len)
    assert ts == 8 and past_len % SEL_BLOCK == 0 and tp % 256 == 0 and tp // SEL_BLOCK <= N_SEL_POS

    def wo_pairs(l, n, y_main, y_mem, nsa_layout, b, t):
        tm = _row_tile(n, 1024)
        if nsa_layout:
            hw = NSA_GROUP * HEAD_DIM
            nt = t // tm
            y2d = y_main.reshape(b * NSA_KV_HEADS * t, hw)
            pairs = [(y2d, (tm, hw), (lambda i, h=h: (((i // nt) * NSA_KV_HEADS + h) * nt + i % nt, 0)),
                      w_o16[l, h * hw:(h + 1) * hw]) for h in range(NSA_KV_HEADS)]
            km = NSA_DIM
        else:
            km = y_main.shape[1]
            pairs = [(y_main, (tm, km), lambda i: (i, 0), w_o16[l, :km])]
        return pairs + [(y_mem, (tm, MEM_DIM), lambda i: (i, 0), w_o16[l, km:])]

    for l in range(depth):
        for gi, gr in enumerate(groups):
            x, b, t = gr["x"], gr["b"], gr["t"]
            n = b * t
            nsa_layout = False
            if l < n_a:
                proj = rms_matmul(x, g_mix[l], w_in_a16[l], tn=w_in_a16.shape[2] // 2)
                y_main, new_st = gated_conv(proj, gr["st"][l], conv_w[l], b, t)
                conv_out[gi].append(new_st)
                y_mem = mem_attention(proj, 3 * conv_dim // MEM_DIM, gr["mem"][l], b, t)
            else:
                if l == n_a:
                    if gi == 0:
                        kv_t = rms_matmul(x, g_kv, w_kv16, tn=3 * KV_DIM, feature_major_batch=b)
                        kv_rows[gi] = kv_t
                        ck, cv = compress_prompt(kv_t, cw)
                        nsa_in = (_cmp_slots(ck, b), _cmp_slots(cv, b)) + kv_layout(kv_t)
                    else:
                        kv2d = rms_matmul(x, g_kv, w_kv16, tn=3 * KV_DIM)
                        kv_rows[gi] = kv2d
                        new3 = kv2d.reshape(b, t, 6 * KV_DIM)
                        ck_s, cv_s = compress_sample(page_table, pool_cmp, new3, cw, pg=pg)
                        per = SEL_BLOCK // CMP_STRIDE
                        slots = lambda c: c.reshape(b, c.shape[1] // per, per, KV_DIM).transpose(0, 2, 1, 3).reshape(
                            c.shape)
                        ck_s, cv_s = slots(ck_s), slots(cv_s)
                j = l - n_a
                if gi == 0:
                    proj = rms_matmul(x, g_mix[l], w_in_b16[j], tn=B_COLS // 2)
                    y_main = nsa_prompt(proj, slopes, *nsa_in, b, t)
                    nsa_layout = True
                    y_mem = mem_attention(proj, B_MQ_COL // MEM_DIM, gr["mem"][l], b, t)
                else:
                    proj = rms_matmul(x, g_mix[l], w_in_bs16[j], tn=S_COLS)
                    y_main = nsa_decode(proj, slopes, page_table, ck_s, cv_s, new3, win_state, pool_slc, pg=pg)
                    y_mem = mem_attention(proj, S_MQ_COL // MEM_DIM, gr["mem"][l], b, t)
            x = proj_residual(x, wo_pairs(l, n, y_main, y_mem, nsa_layout, b, t))
            x = ffn(x, g_ffn[l], w_gu16[l], w_dn16[l])
            gr["x"] = x

    y_prompt = rmsnorm_rows(groups[0]["x"], g_final).reshape(bp, tp, d)
    y_sample = rmsnorm_rows(groups[1]["x"], g_final).reshape(bs, ts, d)
    conv_state_p = jnp.stack(conv_out[0])
    conv_state_s = jnp.stack(conv_out[1])
    mem_kv_out = mem_kv_p.reshape(depth, bp, n_mem, 2, MEM_HEADS, HEAD_DIM)
    def branch_rows(kv2d, br, b, t):
        return kv2d[:, br * 2 * KV_DIM:(br + 1) * 2 * KV_DIM].reshape(b, t, 2, NSA_KV_HEADS, HEAD_DIM)

    def branch_rows_t(kv_t, br, t0):
        rows = kv_t[:, br * 2 * KV_DIM:(br + 1) * 2 * KV_DIM, t0:]
        return rows.reshape(kv_t.shape[0], 2, NSA_KV_HEADS, HEAD_DIM, -1).transpose(0, 4, 1, 2, 3)

    kvp, kvs = kv_rows
    win_kv_s = jnp.concatenate([state_win_kv, branch_rows(kvs, 2, bs, ts)], axis=1)[:, ts:]
    return (y_prompt, y_sample, conv_state_p, conv_state_s, mem_kv_out, branch_rows_t(kvp, 0, 0),
            branch_rows_t(kvp, 1, 0), branch_rows_t(kvp, 2, tp - min(WINDOW, tp)), branch_rows(kvs, 0, bs, ts),
            branch_rows(kvs, 1, bs, ts), win_kv_s)
```

```python
import functools
import math

import numpy as np
import jax
import jax.numpy as jnp
from jax import lax
from jax.experimental import pallas as pl
from jax.experimental.pallas import tpu as pltpu

F32 = jnp.float32
BF16 = jnp.bfloat16

HEAD_DIM = 64
MEM_HEADS = 4
MEM_DIM = MEM_HEADS * HEAD_DIM
NSA_KV_HEADS = 4
NSA_GROUP = 3
NSA_HEADS = NSA_KV_HEADS * NSA_GROUP
NSA_DIM = NSA_HEADS * HEAD_DIM
KV_DIM = NSA_KV_HEADS * HEAD_DIM
CONV_WIDTH = 3
CMP_STRIDE = 16
CMP_LEN = 32
SEL_BLOCK = 64
N_SEL = 16
WINDOW = 512
Q_BLOCK = 64
RMS_EPS = 1e-6
NEG_INF = -1e30
FORCE_SCORE = 1e4
ATT_SCALE = HEAD_DIM ** -0.5

VMEM_LIMIT = 48 * 1024 * 1024


def _cparams(*sem):
    return pltpu.CompilerParams(dimension_semantics=sem, vmem_limit_bytes=VMEM_LIMIT)


def _row_tile(n, want):
    t = min(n, want)
    while n % t:
        t //= 2
    return t


def _rms_matmul_kernel(x_ref, g_ref, w_ref, o_ref, h_ref):
    @pl.when(pl.program_id(1) == 0)
    def _():
        x = x_ref[...]
        ms = jnp.mean(x * x, axis=-1, keepdims=True)
        h_ref[...] = (x * lax.rsqrt(ms + RMS_EPS) * g_ref[...]).astype(BF16)

    o_ref[...] = jnp.dot(h_ref[...], w_ref[...], preferred_element_type=F32)


def _rms_matmul_t_kernel(x_ref, g_ref, w_ref, o_ref, h_ref):
    @pl.when(pl.program_id(1) == 0)
    def _():
        x = x_ref[...]
        ms = jnp.mean(x * x, axis=-1, keepdims=True)
        h_ref[...] = (x * lax.rsqrt(ms + RMS_EPS) * g_ref[...]).astype(BF16)

    o_ref[0] = jnp.dot(h_ref[...], w_ref[...], preferred_element_type=F32).T


def rms_matmul(x, g, w, *, tm=1024, tn=512, feature_major_batch=None):
    n, d = x.shape
    c = w.shape[1]
    tn = _row_tile(c, tn)
    if feature_major_batch is None:
        tm = _row_tile(n, tm)
        kern, out_shape = _rms_matmul_kernel, jax.ShapeDtypeStruct((n, c), F32)
        out_spec = pl.BlockSpec((tm, tn), lambda i, j: (i, j))
    else:
        t = n // feature_major_batch
        tm = _row_tile(t, tm)
        nt = t // tm
        kern, out_shape = _rms_matmul_t_kernel, jax.ShapeDtypeStruct((feature_major_batch, c, t), F32)
        out_spec = pl.BlockSpec((1, tn, tm), lambda i, j: (i // nt, j, i % nt))
    return pl.pallas_call(
        kern,
        out_shape=out_shape,
        grid=(n // tm, c // tn),
        in_specs=[
            pl.BlockSpec((tm, d), lambda i, j: (i, 0)),
            pl.BlockSpec((1, d), lambda i, j: (0, 0)),
            pl.BlockSpec((d, tn), lambda i, j: (0, j)),
        ],
        out_specs=out_spec,
        scratch_shapes=[pltpu.VMEM((tm, d), BF16)],
        compiler_params=_cparams("parallel", "arbitrary"),
        name="rms_matmul",
    )(x, g.reshape(1, d), w)


def _proj_residual_kernel(n_pairs, x_ref, *refs):
    a_refs = refs[:n_pairs]
    w_refs = refs[n_pairs:2 * n_pairs]
    o_ref = refs[2 * n_pairs]
    acc = x_ref[...]
    for a_ref, w_ref in zip(a_refs, w_refs):
        acc = acc + jnp.dot(a_ref[...].astype(BF16), w_ref[...], preferred_element_type=F32)
    o_ref[...] = acc


def proj_residual(x, pairs, *, tm=1024):
    n, d = x.shape
    tm = _row_tile(n, tm)
    in_specs = [pl.BlockSpec((tm, d), lambda i: (i, 0))]
    args = [x]
    for a, blk, imap, _ in pairs:
        in_specs.append(pl.BlockSpec(blk, imap))
        args.append(a)
    for _, _, _, w in pairs:
        in_specs.append(pl.BlockSpec(w.shape, lambda i: (0, 0)))
        args.append(w)
    return pl.pallas_call(
        functools.partial(_proj_residual_kernel, len(pairs)),
        out_shape=jax.ShapeDtypeStruct((n, d), F32),
        grid=(n // tm,),
        in_specs=in_specs,
        out_specs=pl.BlockSpec((tm, d), lambda i: (i, 0)),
        compiler_params=_cparams("parallel"),
        name="proj_residual",
    )(*args)


def _ffn_kernel(x_ref, g_ref, wg_ref, wu_ref, wd_ref, o_ref, h_ref, acc_ref):
    f = pl.program_id(1)

    @pl.when(f == 0)
    def _():
        x = x_ref[...]
        ms = jnp.mean(x * x, axis=-1, keepdims=True)
        h_ref[...] = (x * lax.rsqrt(ms + RMS_EPS) * g_ref[...]).astype(BF16)
        acc_ref[...] = x

    h = h_ref[...]
    gate = jnp.dot(h, wg_ref[...], preferred_element_type=F32)
    up = jnp.dot(h, wu_ref[...], preferred_element_type=F32)
    act = (gate * jax.nn.sigmoid(gate) * up).astype(BF16)
    acc_ref[...] += jnp.dot(act, wd_ref[...], preferred_element_type=F32)

    @pl.when(f == pl.num_programs(1) - 1)
    def _():
        o_ref[...] = acc_ref[...]


def ffn(x, g, w_gu, w_dn, *, tm=512, tf=1408):
    n, d = x.shape
    dff = w_dn.shape[0]
    tm = _row_tile(n, tm)
    nf = dff // tf
    return pl.pallas_call(
        _ffn_kernel,
        out_shape=jax.ShapeDtypeStruct((n, d), F32),
        grid=(n // tm, nf),
        in_specs=[
            pl.BlockSpec((tm, d), lambda i, f: (i, 0)),
            pl.BlockSpec((1, d), lambda i, f: (0, 0)),
            pl.BlockSpec((d, tf), lambda i, f: (0, f)),
            pl.BlockSpec((d, tf), lambda i, f: (0, f + nf)),
            pl.BlockSpec((tf, d), lambda i, f: (f, 0)),
        ],
        out_specs=pl.BlockSpec((tm, d), lambda i, f: (i, 0)),
        scratch_shapes=[pltpu.VMEM((tm, d), BF16), pltpu.VMEM((tm, d), F32)],
        compiler_params=_cparams("parallel", "arbitrary"),
        name="ffn",
    )(x, g.reshape(1, d), w_gu, w_gu, w_dn)


def _rmsnorm_kernel(x_ref, g_ref, o_ref):
    x = x_ref[...]
    ms = jnp.mean(x * x, axis=-1, keepdims=True)
    o_ref[...] = x * lax.rsqrt(ms + RMS_EPS) * g_ref[...]


def rmsnorm_rows(x, g, *, tm=1024):
    n, d = x.shape
    tm = _row_tile(n, tm)
    return pl.pallas_call(
        _rmsnorm_kernel,
        out_shape=jax.ShapeDtypeStruct((n, d), F32),
        grid=(n // tm,),
        in_specs=[pl.BlockSpec((tm, d), lambda i: (i, 0)), pl.BlockSpec((1, d), lambda i: (0, 0))],
        out_specs=pl.BlockSpec((tm, d), lambda i: (i, 0)),
        compiler_params=_cparams("parallel"),
        name="rmsnorm",
    )(x, g.reshape(1, d))


def _conv_kernel(b_ref, c_ref, h_ref, cp_ref, hp_ref, st_ref, w_ref, y_ref, ns_ref):
    i = pl.program_id(1)
    u = c_ref[...] * h_ref[...]
    tt = u.shape[0]
    prev = cp_ref[...] * hp_ref[...]
    st = st_ref[0]
    first = i == 0
    p1 = jnp.where(first, st[1:2], prev[7:8])
    p2 = jnp.where(first, st[0:1], prev[6:7])
    row = lax.broadcasted_iota(jnp.int32, u.shape, 0)
    u1 = jnp.where(row == 0, p1, pltpu.roll(u, 1, 0))
    u2 = jnp.where(row == 0, p2, jnp.where(row == 1, p1, pltpu.roll(u, 2, 0)))
    w = w_ref[...]
    y = w[0:1] * u2 + w[1:2] * u1 + w[2:3] * u
    y_ref[...] = b_ref[...] * y
    ns_ref[0] = u[tt - 2:tt]


def gated_conv(proj, state, conv_w, batch, t, *, tt=512):
    c = conv_w.shape[1]
    tt = _row_tile(t, tt)
    nt = t // tt
    r8 = tt // 8

    def prev_map(col):
        return lambda b, i: (jnp.maximum((b * nt + i) * r8 - 1, 0), col)

    return pl.pallas_call(
        _conv_kernel,
        out_shape=(jax.ShapeDtypeStruct((batch * t, c), F32), jax.ShapeDtypeStruct((batch, 2, c), F32)),
        grid=(batch, nt),
        in_specs=[
            pl.BlockSpec((tt, c), lambda b, i: (b * nt + i, 0)),
            pl.BlockSpec((tt, c), lambda b, i: (b * nt + i, 1)),
            pl.BlockSpec((tt, c), lambda b, i: (b * nt + i, 2)),
            pl.BlockSpec((8, c), prev_map(1)),
            pl.BlockSpec((8, c), prev_map(2)),
            pl.BlockSpec((1, 2, c), lambda b, i: (b, 0, 0)),
            pl.BlockSpec((CONV_WIDTH, c), lambda b, i: (0, 0)),
        ],
        out_specs=(
            pl.BlockSpec((tt, c), lambda b, i: (b * nt + i, 0)),
            pl.BlockSpec((1, 2, c), lambda b, i: (b, 0, 0)),
        ),
        compiler_params=_cparams("parallel", "arbitrary"),
        name="gated_conv",
    )(proj, proj, proj, proj, proj, state, conv_w)


def _mem_attn_kernel(q_ref, kv_ref, o_ref):
    nb = kv_ref.shape[0]
    tt = q_ref.shape[0] // nb
    for i in range(nb):
        q = q_ref[i * tt:(i + 1) * tt, :] * ATT_SCALE
        kv = kv_ref[i]
        outs = []
        for h in range(MEM_HEADS):
            qh = q[:, h * HEAD_DIM:(h + 1) * HEAD_DIM].astype(BF16)
            kh = kv[:, h * HEAD_DIM:(h + 1) * HEAD_DIM].astype(BF16)
            vh = kv[:, MEM_DIM + h * HEAD_DIM:MEM_DIM + (h + 1) * HEAD_DIM].astype(BF16)
            s = lax.dot_general(qh, kh, (((1,), (1,)), ((), ())), preferred_element_type=F32)
            m = jnp.max(s, axis=-1, keepdims=True)
            e = jnp.exp(s - m)
            p = e / jnp.sum(e, axis=-1, keepdims=True)
            outs.append(jnp.dot(p.astype(BF16), vh, preferred_element_type=F32))
        o_ref[i * tt:(i + 1) * tt, :] = jnp.concatenate(outs, axis=-1)


def mem_attention(proj, col_block, mem_kv, batch, t, *, tt=1024):
    tt = _row_tile(t, tt)
    nt = t // tt
    nb = 1
    n_mem = mem_kv.shape[1]
    return pl.pallas_call(
        _mem_attn_kernel,
        out_shape=jax.ShapeDtypeStruct((batch * t, MEM_DIM), F32),
        grid=(batch // nb, nt),
        in_specs=[
            pl.BlockSpec((nb * tt, MEM_DIM), lambda b, i: (b * nt + i, col_block)),
            pl.BlockSpec((nb, n_mem, 2 * MEM_DIM), lambda b, i: (b, 0, 0)),
        ],
        out_specs=pl.BlockSpec((nb * tt, MEM_DIM), lambda b, i: (b * nt + i, 0)),
        compiler_params=_cparams("parallel", "arbitrary"),
        name="mem_attention",
    )(proj, mem_kv)


def _mem_attn_decode_kernel(q_ref, kv_ref, o_ref):
    nb = kv_ref.shape[0]
    ts = q_ref.shape[0] // nb
    lane_h = lax.broadcasted_iota(jnp.int32, (ts, MEM_DIM), 1) // HEAD_DIM
    nt_dims = (((1,), (1,)), ((), ()))
    for i in range(nb):
        q = q_ref[i * ts:(i + 1) * ts, :] * ATT_SCALE
        qbd = jnp.concatenate([jnp.where(lane_h == h, q, 0.0) for h in range(MEM_HEADS)], axis=0).astype(BF16)
        s = jnp.dot(qbd, kv_ref[i, 0:MEM_DIM, :].astype(BF16), preferred_element_type=F32)
        m = jnp.max(s, axis=-1, keepdims=True)
        e = jnp.exp(s - m)
        p = e / jnp.sum(e, axis=-1, keepdims=True)
        o = lax.dot_general(p.astype(BF16), kv_ref[i, MEM_DIM:2 * MEM_DIM, :].astype(BF16), nt_dims,
                            preferred_element_type=F32)
        o_ref[i * ts:(i + 1) * ts, :] = jnp.concatenate(
            [o[h * ts:(h + 1) * ts, h * HEAD_DIM:(h + 1) * HEAD_DIM] for h in range(MEM_HEADS)], axis=1)


def mem_attention_decode(proj, col_block, mem_kv_t, batch, ts, *, group=4):
    nb = group if batch % group == 0 else 1
    n_mem = mem_kv_t.shape[2]
    return pl.pallas_call(
        _mem_attn_decode_kernel,
        out_shape=jax.ShapeDtypeStruct((batch * ts, MEM_DIM), F32),
        grid=(batch // nb,),
        in_specs=[
            pl.BlockSpec((nb * ts, MEM_DIM), lambda b: (b, col_block)),
            pl.BlockSpec((nb, 2 * MEM_DIM, n_mem), lambda b: (b, 0, 0)),
        ],
        out_specs=pl.BlockSpec((nb * ts, MEM_DIM), lambda b: (b, 0)),
        compiler_params=_cparams("parallel"),
        name="mem_attention_decode",
    )(proj, mem_kv_t)


def _alibi_list(n):
    def pow2(m):
        start = 2.0 ** (-8.0 / m)
        return [start ** (i + 1) for i in range(m)]
    if n & (n - 1) == 0:
        return pow2(n)
    c = 2 ** int(math.floor(math.log2(n)))
    return pow2(c) + _alibi_list(2 * c)[0::2][: n - c]


LANES = 128
PAIR = LANES // HEAD_DIM


CMP_PITCH = CMP_STRIDE + 1


def _to_lane_columns(src, dst_ref, n_rows):
    for c in range(dst_ref.shape[0]):
        dst_ref[c, 0:n_rows, :] = src[0:n_rows, c * LANES:(c + 1) * LANES]


def _compress_tile(buf_ref, row0, m, pe_refs, w1_refs, w2_refs, pitch=CMP_STRIDE):
    outs = []
    base = row0 // CMP_STRIDE * pitch
    for kv in range(2):
        pe = pe_refs[kv][...]
        lhs = []
        for p in range(NSA_KV_HEADS // PAIR):
            col = kv * (KV_DIM // LANES) + p
            pieces = [
                (buf_ref[col, pl.ds(base + j // CMP_STRIDE * pitch + j % CMP_STRIDE, m, stride=pitch), :]
                 + pe[j:j + 1]).astype(BF16)
                for j in range(CMP_LEN)
            ]
            lhs.append(jnp.concatenate(pieces, axis=1))
        lhs = jnp.concatenate(lhs, axis=0)
        hid = jnp.dot(lhs, w1_refs[kv][...], preferred_element_type=F32)
        act = (hid * jax.nn.sigmoid(hid)).astype(BF16)
        out = jnp.dot(act, w2_refs[kv][...], preferred_element_type=F32)
        outs.append(jnp.concatenate([out[0:m], out[m:2 * m]], axis=1))
    return outs


def _compress_prompt_kernel(x_ref, pek_ref, pev_ref, w1k_ref, w1v_ref, w2k_ref, w2v_ref, ck_ref, cv_ref, buf_ref):
    t = x_ref.shape[2]
    for c in range(buf_ref.shape[0]):
        for r in range(0, t, LANES):
            buf_ref[c, r:r + LANES, :] = x_ref[0, c * LANES:(c + 1) * LANES, r:r + LANES].T
    buf_ref[:, t:t + CMP_STRIDE, :] = jnp.zeros((buf_ref.shape[0], CMP_STRIDE, LANES), F32)
    n_blk = t // CMP_STRIDE
    m = min(n_blk, 128)
    for s in range(n_blk // m):
        ck, cv = _compress_tile(buf_ref, s * m * CMP_STRIDE, m, (pek_ref, pev_ref), (w1k_ref, w1v_ref),
                                (w2k_ref, w2v_ref))
        ck_ref[0, s * m:(s + 1) * m, :] = ck
        cv_ref[0, s * m:(s + 1) * m, :] = cv


def _compress_weights(pe, w1, w2):
    eye = jnp.eye(PAIR, dtype=F32)
    w1p = jnp.einsum('jde,qr->jqdre', w1, eye).reshape(CMP_LEN * PAIR * HEAD_DIM, PAIR * w1.shape[2])
    w2p = jnp.einsum('ed,qr->qerd', w2, eye).reshape(PAIR * w2.shape[0], PAIR * HEAD_DIM)
    pe2 = jnp.tile(pe, (1, PAIR))
    return pe2, w1p.astype(BF16), w2p.astype(BF16)


def compress_prompt(kv_t, cw):
    pek, w1k, w2k, pev, w1v, w2v = cw
    batch, _, t = kv_t.shape
    n_blk = t // CMP_STRIDE
    full = lambda a: pl.BlockSpec(a.shape, lambda b: (0,) * a.ndim)
    return pl.pallas_call(
        _compress_prompt_kernel,
        out_shape=(jax.ShapeDtypeStruct((batch, n_blk, KV_DIM), F32),) * 2,
        grid=(batch,),
        in_specs=[pl.BlockSpec((1, 2 * KV_DIM, t), lambda b: (b, 0, 0)), full(pek), full(pev), full(w1k), full(w1v),
                  full(w2k), full(w2v)],
        out_specs=(pl.BlockSpec((1, n_blk, KV_DIM), lambda b: (b, 0, 0)),) * 2,
        scratch_shapes=[pltpu.VMEM((2 * KV_DIM // LANES, t + CMP_STRIDE, LANES), F32)],
        compiler_params=_cparams("parallel"),
        name="compress_prompt",
    )(kv_t, pek, pev, w1k, w1v, w2k, w2v)


N_CMP_POS = 256
N_SEL_POS = N_CMP_POS * CMP_STRIDE // SEL_BLOCK
MASK_BIG = 2.0 ** 100
M_INIT = -1e38


LOG2E = 1.4426950408889634
MASK_NONE, MASK_CAUSAL, MASK_BAND = 0, 1, 2


def _topk_mask(score):
    tq, nb = score.shape
    pad = jnp.full((tq, LANES - nb), -jnp.inf, F32)
    st = jnp.concatenate([score, pad], axis=1).T[:nb]
    sub = 8
    idx = lax.broadcasted_iota(jnp.int32, (sub, tq), 0)
    groups = [st[r:r + sub] for r in range(0, nb, sub)]
    ranks = [jnp.zeros((sub, tq), F32) for _ in groups]
    for j in range(nb):
        rj = st[j:j + 1, :]
        for gi, sg in enumerate(groups):
            if gi * sub > j:
                before = rj >= sg
            elif gi * sub + sub - 1 < j:
                before = rj > sg
            else:
                before = jnp.where(idx > j - gi * sub, (rj >= sg).astype(F32), (rj > sg).astype(F32)) > 0.0
            ranks[gi] = jnp.where(before, ranks[gi] + 1.0, ranks[gi])
    sel_t = (jnp.concatenate(ranks, axis=0) < N_SEL).astype(F32)
    blk_any = jnp.max(sel_t, axis=1, keepdims=True)
    sel_t = jnp.concatenate([sel_t, jnp.zeros((LANES - nb, tq), F32)], axis=0)
    return sel_t.T[:, :nb], blk_any


def _chunk_words(blk_any, per_chunk):
    nb = blk_any.shape[0]
    per_word = 8 * per_chunk
    bidx = lax.broadcasted_iota(jnp.int32, (nb, 1), 0)
    wgt = jnp.left_shift(1, 3 * ((bidx % per_word) // per_chunk)).astype(F32)
    words = []
    for w in range(nb // per_word):
        v = jnp.sum(jnp.where(bidx // per_word == w, blk_any * wgt, 0.0), axis=0, keepdims=True)
        words.append(v.astype(jnp.int32)[0, 0])
    return words


def _nsa_prompt_kernel(sl_ref, q_ref, ck_ref, cv_ref, kc_ref, vc_ref, o_ref,
                       q_scr, s0_scr, s1_scr, p0_scr, p1_scr, a0_scr, a1_scr, m_scr, l_scr, acc_scr, mb_scr, lst_ref,
                       *, mm_rows, sm_rows):
    kvh = pl.program_id(1)
    i = pl.program_id(2)
    s_scrs, p_scrs, a_scrs = (s0_scr, s1_scr), (p0_scr, p1_scr), (a0_scr, a1_scr)
    tq = q_ref.shape[0]
    kb = tq
    t_len = kc_ref.shape[2] // 2
    t0 = i * tq
    slopes = [sl_ref[kvh * NSA_GROUP + g] * LOG2E for g in range(NSA_GROUP)]
    q = q_ref[...] * (ATT_SCALE * LOG2E)
    q3 = jnp.concatenate([q[:, g * HEAD_DIM:(g + 1) * HEAD_DIM] for g in range(NSA_GROUP)], axis=0)
    q3b = q3.astype(BF16)
    nt_dims = (((1,), (1,)), ((), ()))
    t_col = t0 + lax.broadcasted_iota(jnp.int32, (tq, 1), 0)

    @pl.when(i == 0)
    def _():
        r = lax.broadcasted_iota(jnp.int32, (tq, kb), 0)
        c = lax.broadcasted_iota(jnp.int32, (tq, kb), 1)
        mb_scr[MASK_NONE] = jnp.zeros((tq, kb), F32)
        mb_scr[MASK_CAUSAL] = jnp.where(c <= r, 0.0, -MASK_BIG)
        mb_scr[MASK_BAND] = jnp.where(c >= r, 0.0, -MASK_BIG)

    m_scr[...] = jnp.full(m_scr.shape, M_INIT, F32)
    l_scr[...] = jnp.zeros(l_scr.shape, F32)
    acc_scr[...] = jnp.zeros(acc_scr.shape, F32)

    pos = lax.broadcasted_iota(jnp.int32, (1, N_CMP_POS), 1)
    blk_n = (pos % N_SEL_POS) * (N_CMP_POS // N_SEL_POS) + pos // N_SEL_POS
    c_end = blk_n * CMP_STRIDE + (CMP_LEN - 1)
    n_real = t_len // CMP_STRIDE - 1
    vis_c = (c_end <= t_col) & (blk_n < n_real)
    c_end_f = c_end.astype(F32)
    s_c = lax.dot_general(q3b, ck_ref[0, 0], nt_dims, preferred_element_type=F32)
    p_c = []
    for g in range(NSA_GROUP):
        sg = jnp.where(vis_c, s_c[g * tq:(g + 1) * tq] + slopes[g] * c_end_f, NEG_INF)
        mg = jnp.max(sg, axis=1, keepdims=True)
        eg = jnp.where(vis_c, jnp.exp2(sg - mg), 0.0)
        dg = jnp.sum(eg, axis=1, keepdims=True)
        p_c.append(eg / jnp.where(dg > 0.0, dg, 1.0))
    o_c = jnp.dot(jnp.concatenate(p_c, axis=0).astype(BF16), cv_ref[0, 0], preferred_element_type=F32)
    imp = p_c[0] + p_c[1] + p_c[2]
    imp = imp[:, :LANES] + imp[:, LANES:]
    imp = imp[:, :N_SEL_POS] + imp[:, N_SEL_POS:]

    blk = lax.broadcasted_iota(jnp.int32, (1, N_SEL_POS), 1)
    cur = t_col // SEL_BLOCK
    forced = (blk == 0) | (blk == cur) | (blk == cur - 1)
    score = jnp.where(forced, FORCE_SCORE, jnp.where(blk <= cur, imp, -jnp.inf))
    sel, blk_any = _topk_mask(score)
    notsel = (1.0 - sel).astype(BF16)
    q_scr[...] = jnp.concatenate([q3b, jnp.concatenate([notsel] * NSA_GROUP, axis=0)], axis=1)

    lane_k = lax.broadcasted_iota(jnp.int32, (1, kb), 1)
    n_win = WINDOW // kb + 1

    words = _chunk_words(blk_any, kb // SEL_BLOCK)
    n_sel = jnp.int32(0)
    for c in range(N_SEL_POS * SEL_BLOCK // kb):
        active = (((words[c // 8] >> (3 * (c % 8))) & 7) != 0) & (c <= i)
        lst_ref[n_sel] = c
        n_sel = n_sel + active.astype(jnp.int32)
    n_chunks = n_sel + n_win

    def chunk(p):
        is_sel = p < n_sel
        is_pad = p >= n_chunks
        c = lst_ref[jnp.minimum(p, n_sel - 1)]
        w = p - n_sel
        kpos0 = jnp.where(is_sel, c * kb, jnp.where(is_pad, -kb, t0 - WINDOW + w * kb))
        row0 = jnp.maximum(kpos0, 0) + jnp.where(is_sel | is_pad, 0, t_len)
        mtype = jnp.where(is_sel, jnp.where(c == i, MASK_CAUSAL, MASK_NONE),
                          jnp.where(w == 0, MASK_BAND, jnp.where(w == n_win - 1, MASK_CAUSAL, MASK_NONE)))
        return pl.multiple_of(row0, kb), kpos0, mtype, jnp.where(is_sel | is_pad, 0, 1)

    def stage_logits(c, par):
        s_scr = s_scrs[par]
        row0, kpos0, mtype, _ = chunk(c)
        k = kc_ref[0, 0, pl.ds(row0, kb), :]
        kpos_f = (kpos0 + lane_k).astype(F32)
        off = jnp.where(kpos0 >= 0, 0.0, -MASK_BIG)
        for g in range(NSA_GROUP):
            bias = slopes[g] * kpos_f + off
            for r in range(0, tq, mm_rows):
                rows = slice(g * tq + r, g * tq + r + mm_rows)
                s = lax.dot_general(q_scr[rows, :], k, nt_dims, preferred_element_type=F32)
                s_scr[rows, :] = (s + mb_scr[mtype, r:r + mm_rows, :]) + bias

    def stage_softmax(c, par):
        s_scr, p_scr, a_scr = s_scrs[par], p_scrs[par], a_scrs[par]
        st = chunk(c)[3]
        for r in range(0, NSA_GROUP * tq, sm_rows):
            rows = slice(r, r + sm_rows)
            s = s_scr[rows, :]
            m_prev = m_scr[st, rows, :]
            m_new = jnp.maximum(m_prev, jnp.max(s, axis=1, keepdims=True))
            alpha = jnp.exp2(m_prev - m_new)
            p = jnp.exp2(s - jnp.concatenate([m_new] * (kb // LANES), axis=1))
            l_scr[st, rows, :] = alpha * l_scr[st, rows, :] + jnp.sum(p, axis=1, keepdims=True)
            p_scr[rows, :] = p.astype(BF16)
            a_scr[rows, :] = alpha
            m_scr[st, rows, :] = m_new

    def stage_values(c, par):
        p_scr, a_scr = p_scrs[par], a_scrs[par]
        row0, _, _, st = chunk(c)
        v = vc_ref[0, 0, pl.ds(row0, kb), :]
        for r in range(0, NSA_GROUP * tq, mm_rows):
            rows = slice(r, r + mm_rows)
            pv = jnp.dot(p_scr[rows, :], v, preferred_element_type=F32)
            acc_scr[st, rows, :] = acc_scr[st, rows, :] * a_scr[rows, :HEAD_DIM] + pv

    stage_logits(0, 0)
    stage_softmax(0, 0)
    stage_logits(1, 1)

    def pipe_body(j, carry):
        c = 2 * j
        stage_values(c - 2, 0)
        stage_softmax(c - 1, 1)
        stage_logits(c, 0)
        stage_values(c - 1, 1)
        stage_softmax(c, 0)
        stage_logits(c + 1, 1)
        return carry

    n_even = n_chunks + n_chunks % 2
    lax.fori_loop(1, n_even // 2, pipe_body, 0)
    stage_values(n_even - 2, 0)
    stage_softmax(n_even - 1, 1)
    stage_values(n_even - 1, 1)
    o_s = acc_scr[0] / l_scr[0][:, :HEAD_DIM]
    o_w = acc_scr[1] / l_scr[1][:, :HEAD_DIM]

    gates = jax.nn.sigmoid(q_ref[:, Q_GATE_LANE:Q_SLOT])
    outs = []
    for g in range(NSA_GROUP):
        rows = slice(g * tq, (g + 1) * tq)
        outs.append(gates[:, 3 * g:3 * g + 1] * o_c[rows] + gates[:, 3 * g + 1:3 * g + 2] * o_s[rows]
                    + gates[:, 3 * g + 2:3 * g + 3] * o_w[rows])
    o_ref[0, 0] = jnp.concatenate(outs, axis=1)


def nsa_prompt(proj, slopes, ckp, cvp, kcat, vcat, batch, t, *, tq=256, mm_rows=256, sm_rows=256):
    assert t % tq == 0 and WINDOW % tq == 0 and WINDOW >= tq
    nt = t // tq
    rows = NSA_GROUP * tq
    grid_spec = pltpu.PrefetchScalarGridSpec(
        num_scalar_prefetch=1,
        grid=(batch, NSA_KV_HEADS, nt),
        in_specs=[
            pl.BlockSpec((tq, Q_SLOT), lambda b, h, i, sl: (b * nt + i, h)),
            pl.BlockSpec((1, 1, N_CMP_POS, HEAD_DIM), lambda b, h, i, sl: (b, h, 0, 0)),
            pl.BlockSpec((1, 1, N_CMP_POS, HEAD_DIM), lambda b, h, i, sl: (b, h, 0, 0)),
            pl.BlockSpec((1, 1, 2 * t, LANES), lambda b, h, i, sl: (b, h, 0, 0)),
            pl.BlockSpec((1, 1, 2 * t, HEAD_DIM), lambda b, h, i, sl: (b, h, 0, 0)),
        ],
        out_specs=pl.BlockSpec((1, 1, tq, NSA_GROUP * HEAD_DIM), lambda b, h, i, sl: (b, h, i, 0)),
        scratch_shapes=[
            pltpu.VMEM((rows, LANES), BF16),
            pltpu.VMEM((rows, tq), F32), pltpu.VMEM((rows, tq), F32),
            pltpu.VMEM((rows, tq), BF16), pltpu.VMEM((rows, tq), BF16),
            pltpu.VMEM((rows, LANES), F32), pltpu.VMEM((rows, LANES), F32),
            pltpu.VMEM((2, rows, LANES), F32),
            pltpu.VMEM((2, rows, LANES), F32),
            pltpu.VMEM((2, rows, HEAD_DIM), F32),
            pltpu.VMEM((3, tq, tq), F32),
            pltpu.SMEM((N_SEL_POS * SEL_BLOCK // tq,), jnp.int32),
        ],
    )
    return pl.pallas_call(
        functools.partial(_nsa_prompt_kernel, mm_rows=mm_rows, sm_rows=sm_rows),
        out_shape=jax.ShapeDtypeStruct((batch, NSA_KV_HEADS, t, NSA_GROUP * HEAD_DIM), F32),
        grid_spec=grid_spec,
        compiler_params=_cparams("parallel", "parallel", "arbitrary"),
        name="nsa_prompt",
    )(slopes, proj, ckp, cvp, kcat, vcat)


def _kv_layout_kernel(x_ref, kc_ref, vc_ref):
    tm = x_ref.shape[2]
    pos = pl.program_id(1) * tm + lax.broadcasted_iota(jnp.int32, (tm, N_SEL_POS), 0)
    blk = lax.broadcasted_iota(jnp.int32, (tm, N_SEL_POS), 1)
    onehot = jnp.where(pos // SEL_BLOCK == blk, -MASK_BIG, 0.0).astype(BF16)
    zeros = jnp.zeros((tm, N_SEL_POS), BF16)
    for p in range(NSA_KV_HEADS // PAIR):
        pair = lambda c0: x_ref[0, c0 + p * LANES:c0 + (p + 1) * LANES, :].T.astype(BF16)
        ks, vs, kw, vw = pair(2 * KV_DIM), pair(3 * KV_DIM), pair(4 * KV_DIM), pair(5 * KV_DIM)
        for q in range(PAIR):
            h = p * PAIR + q
            lanes = slice(q * HEAD_DIM, (q + 1) * HEAD_DIM)
            kc_ref[0, h, 0] = jnp.concatenate([ks[:, lanes], onehot], axis=1)
            kc_ref[0, h, 1] = jnp.concatenate([kw[:, lanes], zeros], axis=1)
            vc_ref[0, h, 0] = vs[:, lanes]
            vc_ref[0, h, 1] = vw[:, lanes]


def kv_layout(kv_t, *, tm=512):
    batch, n_feat, t = kv_t.shape
    tm = _row_tile(t, tm)
    nt = t // tm
    kc, vc = pl.pallas_call(
        _kv_layout_kernel,
        out_shape=(jax.ShapeDtypeStruct((batch, NSA_KV_HEADS, 2, t, LANES), BF16),
                   jax.ShapeDtypeStruct((batch, NSA_KV_HEADS, 2, t, HEAD_DIM), BF16)),
        grid=(batch, nt),
        in_specs=[pl.BlockSpec((1, n_feat, tm), lambda b, i: (b, 0, i))],
        out_specs=(pl.BlockSpec((1, NSA_KV_HEADS, 2, tm, LANES), lambda b, i: (b, 0, 0, i, 0)),
                   pl.BlockSpec((1, NSA_KV_HEADS, 2, tm, HEAD_DIM), lambda b, i: (b, 0, 0, i, 0))),
        compiler_params=_cparams("parallel", "parallel"),
        name="kv_layout",
    )(kv_t)
    return (kc.reshape(batch, NSA_KV_HEADS, 2 * t, LANES), vc.reshape(batch, NSA_KV_HEADS, 2 * t, HEAD_DIM))


def _cmp_slots(c, batch):
    n_blk = c.shape[1]
    per = N_CMP_POS // N_SEL_POS
    c = jnp.pad(c, ((0, 0), (0, N_CMP_POS - n_blk), (0, 0)))
    c = c.reshape(batch, N_SEL_POS, per, NSA_KV_HEADS, HEAD_DIM).transpose(0, 3, 2, 1, 4)
    return c.reshape(batch, NSA_KV_HEADS, N_CMP_POS, HEAD_DIM).astype(BF16)


def _compress_sample_kernel(pt_ref, pool_ref, new_ref, pek_ref, pev_ref, w1k_ref, w1v_ref, w2k_ref, w2v_ref,
                            ck_ref, cv_ref, buf_ref, sem_ref, col_ref, *, n_pages, pg):
    b = pl.program_id(0)
    g = pl.program_id(1)
    ng = n_pages // pg
    step = b * ng + g
    n_steps = pl.num_programs(0) * ng
    slot = step % 2
    page_rows = buf_ref.shape[3]
    rows = pg * page_rows
    ts = new_ref.shape[1]

    def pages(st, sl, go):
        bb = st // ng
        gg = st % ng

        def copy(k, dst):
            cp = pltpu.make_async_copy(pool_ref.at[pt_ref[bb * n_pages + gg * pg + k]], buf_ref.at[sl, dst],
                                       sem_ref.at[sl])
            cp.start() if go == "start" else cp.wait()

        for k in range(pg):
            copy(k, k)

        @pl.when(gg < ng - 1)
        def _():
            copy(pg, pg)

    @pl.when(step == 0)
    def _():
        pages(0, 0, "start")

    @pl.when(step + 1 < n_steps)
    def _():
        pages(step + 1, 1 - slot, "start")

    pages(step, slot, "wait")
    n_col = col_ref.shape[0]
    per_page = page_rows // CMP_STRIDE
    for k in range(pg):
        for c in range(n_col):
            tile = buf_ref[slot, k, c * LANES:(c + 1) * LANES, :].T
            for q in range(per_page):
                r0 = (k * per_page + q) * CMP_PITCH
                col_ref[c, r0:r0 + CMP_STRIDE, :] = tile[q * CMP_STRIDE:(q + 1) * CMP_STRIDE]
    halo = pg * per_page * CMP_PITCH

    @pl.when(g < ng - 1)
    def _():
        for c in range(n_col):
            col_ref[c, halo:halo + CMP_STRIDE, :] = buf_ref[slot, pg, c * LANES:(c + 1) * LANES, :].T[0:CMP_STRIDE]

    @pl.when(g == ng - 1)
    def _():
        for c in range(n_col):
            col_ref[c, halo:halo + ts, :] = new_ref[0, :, c * LANES:(c + 1) * LANES]
            col_ref[c, halo + ts:halo + CMP_STRIDE, :] = jnp.zeros((CMP_STRIDE - ts, LANES), F32)

    ck, cv = _compress_tile(col_ref, 0, rows // CMP_STRIDE, (pek_ref, pev_ref), (w1k_ref, w1v_ref),
                            (w2k_ref, w2v_ref), pitch=CMP_PITCH)
    ck_ref[0] = ck
    cv_ref[0] = cv


def compress_sample(page_table, pool_t, new_rows, cw, *, pg):
    pek, w1k, w2k, pev, w1v, w2v = cw
    batch, n_pages = page_table.shape
    page_rows = pool_t.shape[2]
    ts = new_rows.shape[1]
    ng = n_pages // pg
    m = pg * page_rows // CMP_STRIDE
    full = lambda a: pl.BlockSpec(a.shape, lambda b, g, pt: (0,) * a.ndim)
    grid_spec = pltpu.PrefetchScalarGridSpec(
        num_scalar_prefetch=1,
        grid=(batch, ng),
        in_specs=[pl.BlockSpec(memory_space=pl.ANY),
                  pl.BlockSpec((1, ts, new_rows.shape[2]), lambda b, g, pt: (b, 0, 0)),
                  full(pek), full(pev), full(w1k), full(w1v), full(w2k), full(w2v)],
        out_specs=(pl.BlockSpec((1, m, KV_DIM), lambda b, g, pt: (b, g, 0)),) * 2,
        scratch_shapes=[pltpu.VMEM((2, pg + 1, 2 * KV_DIM, page_rows), F32), pltpu.SemaphoreType.DMA((2,)),
                        pltpu.VMEM((2 * KV_DIM // LANES, -(-(m + 1) * CMP_PITCH // 8) * 8, LANES), F32)],
    )
    return pl.pallas_call(
        functools.partial(_compress_sample_kernel, n_pages=n_pages, pg=pg),
        out_shape=(jax.ShapeDtypeStruct((batch, ng * m, KV_DIM), F32),) * 2,
        grid_spec=grid_spec,
        compiler_params=_cparams("arbitrary", "arbitrary"),
        name="compress_sample",
    )(page_table.reshape(-1), pool_t, new_rows, pek, pev, w1k, w1v, w2k, w2v)


S_COL_G = 32


def _softmax_lanes(pieces):
    m = functools.reduce(jnp.maximum, [jnp.max(s, axis=1, keepdims=True) for s in pieces])
    es = [jnp.where(s > 0.5 * NEG_INF, jnp.exp(s - m), 0.0) for s in pieces]
    d = functools.reduce(jnp.add, [jnp.sum(e, axis=1, keepdims=True) for e in es])
    inv = 1.0 / jnp.where(d > 0.0, d, 1.0)
    return [e * inv for e in es]


def _nsa_decode_kernel(sl_ref, pt_ref, proj_ref, ck_ref, cv_ref, new_ref, win_ref, e_ref, pool_ref, y_ref,
                       buf_ref, sem_ref, q_scr, m_scr, l_scr, acc_scr, oc_scr, ow_scr,
                       *, n_pages, pg, past):
    b = pl.program_id(0)
    g = pl.program_id(1)
    ng = n_pages // pg
    step = b * ng + g
    n_steps = pl.num_programs(0) * ng
    slot = step % 2
    page_rows = buf_ref.shape[3]
    rows = pg * page_rows
    ts = proj_ref.shape[0]
    nt_dims = (((1,), (1,)), ((), ()))

    def pages(st, sl, go):
        bb = st // ng
        gg = st % ng
        for k in range(pg):
            cp = pltpu.make_async_copy(pool_ref.at[pt_ref[bb * n_pages + gg * pg + k]], buf_ref.at[sl, k],
                                       sem_ref.at[sl])
            cp.start() if go == "start" else cp.wait()

    @pl.when(step == 0)
    def _():
        pages(0, 0, "start")

    @pl.when(step + 1 < n_steps)
    def _():
        pages(step + 1, 1 - slot, "start")

    row = lax.broadcasted_iota(jnp.int32, (LANES, 1), 0)
    row_g = row // S_COL_G
    row_h = (row % S_COL_G) // ts
    t_col = past + row % ts
    slope_col = jnp.zeros((LANES, 1), F32)
    for h in range(NSA_HEADS):
        slope_col = jnp.where((row_h == h // NSA_GROUP) & (row_g == h % NSA_GROUP), sl_ref[h], slope_col)
    n_sb = past // SEL_BLOCK + 1
    zpad = jnp.zeros((LANES - ts, KV_DIM), F32)
    lane_new = lax.broadcasted_iota(jnp.int32, (1, LANES), 1)
    kp_new = past + lane_new
    vis_new = (kp_new <= t_col) & (lane_new < ts)

    @pl.when(g == 0)
    def _():
        lane_h = lax.broadcasted_iota(jnp.int32, (ts, KV_DIM), 1) // HEAD_DIM
        tiles = []
        for gg in range(NSA_GROUP):
            qg = proj_ref[:, gg * KV_DIM:(gg + 1) * KV_DIM] * ATT_SCALE
            for h in range(NSA_KV_HEADS):
                tiles.append(jnp.where(lane_h == h, qg, 0.0))
        tiles.append(jnp.zeros((LANES - NSA_HEADS * ts, KV_DIM), F32))
        bdq = jnp.concatenate(tiles, axis=0).astype(BF16)
        q_scr[:, 0:KV_DIM] = bdq

        n_c = ck_ref.shape[1]
        per = SEL_BLOCK // CMP_STRIDE
        n_q = n_c // per
        s_c = lax.dot_general(bdq, ck_ref[0].astype(BF16), nt_dims, preferred_element_type=F32)
        slot_c = lax.broadcasted_iota(jnp.int32, (1, n_c), 1)
        c_end = ((slot_c % n_q) * per + slot_c // n_q) * CMP_STRIDE + (CMP_LEN - 1)
        dist = t_col - c_end
        s_c = jnp.where(dist >= 0, s_c - slope_col * dist.astype(F32), NEG_INF)
        (p_c,) = _softmax_lanes([s_c])
        oc_scr[...] = jnp.dot(p_c.astype(BF16), cv_ref[0].astype(BF16), preferred_element_type=F32)
        imp = p_c[0:S_COL_G] + p_c[S_COL_G:2 * S_COL_G] + p_c[2 * S_COL_G:3 * S_COL_G]
        imp = functools.reduce(jnp.add, [imp[:, r * n_q:(r + 1) * n_q] for r in range(per)])
        n_blk = q_scr.shape[1] - KV_DIM
        imp = jnp.concatenate([imp, jnp.zeros((S_COL_G, n_blk - n_q), F32)], axis=1)
        blk = lax.broadcasted_iota(jnp.int32, (1, n_blk), 1)
        cur = t_col[0:S_COL_G] // SEL_BLOCK
        forced = (blk == 0) | (blk == cur) | (blk == cur - 1)
        score = jnp.where(forced, FORCE_SCORE, jnp.where((blk <= cur) & (blk < n_sb), imp, -jnp.inf))
        rank = jnp.zeros(score.shape, F32)
        for j in range(n_sb):
            rj = score[:, j:j + 1]
            rank = rank + jnp.where(blk > j, (rj >= score).astype(F32), (rj > score).astype(F32))
        notsel = (rank >= N_SEL).astype(F32)
        notsel = jnp.concatenate([notsel] * NSA_GROUP + [jnp.ones((LANES - NSA_GROUP * S_COL_G, n_blk), F32)], axis=0)
        q_scr[:, KV_DIM:KV_DIM + n_blk] = notsel.astype(BF16)

        wl = win_ref.shape[2]
        kw_new = jnp.concatenate([new_ref[0, :, 4 * KV_DIM:5 * KV_DIM], zpad], axis=0).astype(BF16)
        vw_new = jnp.concatenate([new_ref[0, :, 5 * KV_DIM:6 * KV_DIM], zpad], axis=0).astype(BF16)
        s_w = jnp.dot(bdq, win_ref[0, 0:KV_DIM, :].astype(BF16), preferred_element_type=F32)
        w_pos = past - wl + lax.broadcasted_iota(jnp.int32, (1, wl), 1)
        dist_w = t_col - w_pos
        s_w = jnp.where((dist_w >= 0) & (dist_w <= WINDOW) & (w_pos >= 0), s_w - slope_col * dist_w.astype(F32),
                        NEG_INF)
        s_wn = lax.dot_general(bdq, kw_new, nt_dims, preferred_element_type=F32)
        dist_n = t_col - kp_new
        s_wn = jnp.where(vis_new & (dist_n <= WINDOW), s_wn - slope_col * dist_n.astype(F32), NEG_INF)
        p_w, p_wn = _softmax_lanes([s_w, s_wn])
        ow_scr[...] = (lax.dot_general(p_w.astype(BF16), win_ref[0, KV_DIM:2 * KV_DIM, :].astype(BF16), nt_dims,
                                       preferred_element_type=F32)
                       + jnp.dot(p_wn.astype(BF16), vw_new, preferred_element_type=F32))
        m_scr[...] = jnp.full(m_scr.shape, M_INIT, F32)
        l_scr[...] = jnp.zeros(l_scr.shape, F32)
        acc_scr[...] = jnp.zeros(acc_scr.shape, F32)

    def online_update(s, pv_fn):
        n = s.shape[1]
        m_prev = m_scr[...]
        m_new = jnp.maximum(m_prev, jnp.max(s, axis=1, keepdims=True))
        alpha = jnp.exp(m_prev - m_new)
        p = jnp.exp(s - jnp.concatenate([m_new] * (n // LANES), axis=1))
        l_scr[...] = alpha * l_scr[...] + jnp.sum(p, axis=1, keepdims=True)
        acc_scr[...] = acc_scr[...] * jnp.concatenate([alpha] * (KV_DIM // LANES), axis=1) + pv_fn(p.astype(BF16))
        m_scr[...] = m_new

    pages(step, slot, "wait")
    kt = jnp.concatenate([buf_ref[slot, k, 0:KV_DIM, :] for k in range(pg)], axis=1).astype(BF16)
    vt = jnp.concatenate([buf_ref[slot, k, KV_DIM:2 * KV_DIM, :] for k in range(pg)], axis=1).astype(BF16)
    n_blk = q_scr.shape[1] - KV_DIM
    s_g = (jnp.dot(q_scr[:, 0:KV_DIM], kt, preferred_element_type=F32)
           + jnp.dot(q_scr[:, KV_DIM:KV_DIM + n_blk], e_ref[0], preferred_element_type=F32))
    kpos = g * rows + lax.broadcasted_iota(jnp.int32, (1, rows), 1)
    s_g = s_g - slope_col * (t_col - kpos).astype(F32)
    online_update(s_g, lambda p: lax.dot_general(p, vt, nt_dims, preferred_element_type=F32))

    @pl.when(g == ng - 1)
    def _():
        k_new = jnp.concatenate([new_ref[0, :, 2 * KV_DIM:3 * KV_DIM], zpad], axis=0).astype(BF16)
        v_new = jnp.concatenate([new_ref[0, :, 3 * KV_DIM:4 * KV_DIM], zpad], axis=0).astype(BF16)
        s_n = lax.dot_general(q_scr[:, 0:KV_DIM], k_new, nt_dims, preferred_element_type=F32)
        unsel_last = q_scr[:, KV_DIM + n_sb - 1:KV_DIM + n_sb].astype(F32)
        dist_n = t_col - kp_new
        s_n = jnp.where(vis_new & (unsel_last < 0.5), s_n - slope_col * dist_n.astype(F32), -MASK_BIG)
        online_update(s_n, lambda p: jnp.dot(p, v_new, preferred_element_type=F32))
        o_s = acc_scr[...] / jnp.concatenate([l_scr[...]] * (KV_DIM // LANES), axis=1)
        o_c = oc_scr[...]
        o_w = ow_scr[...]
        gates = jax.nn.sigmoid(proj_ref[:, (NSA_GROUP + 1) * KV_DIM:(NSA_GROUP + 1) * KV_DIM + LANES])
        outs = []
        for h in range(NSA_HEADS):
            kvh, gg = h // NSA_GROUP, h % NSA_GROUP
            r0 = gg * S_COL_G + kvh * ts
            lanes = slice(kvh * HEAD_DIM, (kvh + 1) * HEAD_DIM)
            outs.append(gates[:, 3 * h:3 * h + 1] * o_c[r0:r0 + ts, lanes]
                        + gates[:, 3 * h + 1:3 * h + 2] * o_s[r0:r0 + ts, lanes]
                        + gates[:, 3 * h + 2:3 * h + 3] * o_w[r0:r0 + ts, lanes])
        y_ref[...] = jnp.concatenate(outs, axis=1)


def nsa_decode(proj, slopes, page_table, ckp, cvp, new_rows, win_t, pool_t, *, pg):
    batch, n_pages = page_table.shape
    page_rows = pool_t.shape[2]
    ts = new_rows.shape[1]
    past = n_pages * page_rows
    ng = n_pages // pg
    rows = pg * page_rows
    n_c = ckp.shape[1]
    wl = win_t.shape[2]
    n_blk = -(-(past // SEL_BLOCK + 1) // LANES) * LANES
    key_blk = jnp.arange(past, dtype=jnp.int32).reshape(ng, 1, rows) // SEL_BLOCK
    e_tab = jnp.where(key_blk == jnp.arange(n_blk, dtype=jnp.int32)[None, :, None], -MASK_BIG, 0.0).astype(BF16)
    grid_spec = pltpu.PrefetchScalarGridSpec(
        num_scalar_prefetch=2,
        grid=(batch, ng),
        in_specs=[
            pl.BlockSpec((ts, proj.shape[1]), lambda b, g, sl, pt: (b, 0)),
            pl.BlockSpec((1, n_c, KV_DIM), lambda b, g, sl, pt: (b, 0, 0)),
            pl.BlockSpec((1, n_c, KV_DIM), lambda b, g, sl, pt: (b, 0, 0)),
            pl.BlockSpec((1, ts, new_rows.shape[2]), lambda b, g, sl, pt: (b, 0, 0)),
            pl.BlockSpec((1, 2 * KV_DIM, wl), lambda b, g, sl, pt: (b, 0, 0)),
            pl.BlockSpec((1, n_blk, rows), lambda b, g, sl, pt: (g, 0, 0)),
            pl.BlockSpec(memory_space=pl.ANY),
        ],
        out_specs=pl.BlockSpec((ts, NSA_DIM), lambda b, g, sl, pt: (b, 0)),
        scratch_shapes=[
            pltpu.VMEM((2, pg, 2 * KV_DIM, page_rows), F32), pltpu.SemaphoreType.DMA((2,)),
            pltpu.VMEM((LANES, KV_DIM + n_blk), BF16),
            pltpu.VMEM((LANES, LANES), F32), pltpu.VMEM((LANES, LANES), F32), pltpu.VMEM((LANES, KV_DIM), F32),
            pltpu.VMEM((LANES, KV_DIM), F32), pltpu.VMEM((LANES, KV_DIM), F32),
        ],
    )
    return pl.pallas_call(
        functools.partial(_nsa_decode_kernel, n_pages=n_pages, pg=pg, past=past),
        out_shape=jax.ShapeDtypeStruct((batch * ts, NSA_DIM), F32),
        grid_spec=grid_spec,
        compiler_params=_cparams("arbitrary", "arbitrary"),
        name="nsa_decode",
    )(slopes, page_table.reshape(-1), proj, ckp, cvp, new_rows, win_t, e_tab, pool_t)


S_MQ_COL =NSA_GROUP * KV_DIM
S_COLS = S_MQ_COL + MEM_DIM + LANES


def _layout_w_in_b_sample(w):
    d = w.shape[0]
    qw = w[:, :NSA_DIM].reshape(d, NSA_KV_HEADS, NSA_GROUP, HEAD_DIM).transpose(0, 2, 1, 3).reshape(d, NSA_DIM)
    gw = jnp.pad(w[:, NSA_DIM:NSA_DIM + 3 * NSA_HEADS], ((0, 0), (0, LANES - 3 * NSA_HEADS)))
    mw = w[:, NSA_DIM + 3 * NSA_HEADS:]
    return jnp.concatenate([qw, mw, gw], axis=1).astype(BF16)


Q_SLOT = 256
Q_GATE_LANE = NSA_GROUP * HEAD_DIM
B_MQ_COL = NSA_KV_HEADS * Q_SLOT
B_COLS = B_MQ_COL + MEM_DIM


def _layout_w_in_b(w):
    d = w.shape[0]
    qw = w[:, :NSA_DIM].reshape(d, NSA_KV_HEADS, NSA_GROUP * HEAD_DIM)
    gw = w[:, NSA_DIM:NSA_DIM + 3 * NSA_HEADS].reshape(d, NSA_KV_HEADS, 3 * NSA_GROUP)
    slot = jnp.concatenate([qw, gw], axis=2)
    slot = jnp.pad(slot, ((0, 0), (0, 0), (0, Q_SLOT - slot.shape[2]))).reshape(d, NSA_KV_HEADS * Q_SLOT)
    mw = w[:, NSA_DIM + 3 * NSA_HEADS:]
    return jnp.concatenate([slot, mw], axis=1).astype(BF16)


def kernel(x_prompt, x_sample, state_conv, cache_mem_kv, cache_cmp_kv, cache_slc_kv, state_win_kv, page_table,
           mem_prompt, g_mix, w_in_a, conv_w, w_in_b, w_o, w_mkv, g_mem, g_kv, w_kv, pe_ck, w1_ck, w2_ck,
           pe_cv, w1_cv, w2_cv, g_ffn, w_gu, w_dn, g_final):
    bp, tp, d = x_prompt.shape
    bs, ts = x_sample.shape[:2]
    depth = g_mix.shape[0]
    n_a = w_in_a.shape[0]
    n_mem = mem_prompt.shape[1]
    win_len = state_win_kv.shape[1]
    past_len = page_table.shape[1] * cache_cmp_kv.shape[1]
    conv_dim = conv_w.shape[2]
    slopes = jnp.asarray(np.array(_alibi_list(NSA_HEADS), dtype=np.float32))

    w_in_a16 = w_in_a.astype(BF16)
    w_in_b16 = [_layout_w_in_b(w_in_b[j]) for j in range(depth - n_a)]
    w_o16 = w_o.astype(BF16)
    w_gu16 = w_gu.astype(BF16)
    w_dn16 = w_dn.astype(BF16)
    w_kv16 = w_kv.astype(BF16)
    w_mkv16 = w_mkv.transpose(1, 0, 2).reshape(d, depth * 2 * MEM_DIM).astype(BF16)

    mkv = rms_matmul(mem_prompt.reshape(bp * n_mem, d), g_mem, w_mkv16)
    mem_kv_p = mkv.reshape(bp, n_mem, depth, 2 * MEM_DIM).transpose(2, 0, 1, 3)
    mem_kv_s = cache_mem_kv.transpose(0, 1, 3, 4, 5, 2).reshape(depth, bs, 2 * MEM_DIM, n_mem)

    groups = [
        dict(x=x_prompt.reshape(bp * tp, d), b=bp, t=tp, mem=mem_kv_p, st=jnp.zeros((n_a, bp, 2, conv_dim), F32)),
        dict(x=x_sample.reshape(bs * ts, d), b=bs, t=ts, mem=mem_kv_s, st=state_conv),
    ]
    conv_out = [[], []]
    kv_rows = [None, None]

    cw = _compress_weights(pe_ck, w1_ck, w2_ck) + _compress_weights(pe_cv, w1_cv, w2_cv)
    nsa_in = None
    w_in_bs16 = [_layout_w_in_b_sample(w_in_b[j]) for j in range(depth - n_a)]
    n_pages = page_table.shape[1]
    page_rows = cache_cmp_kv.shape[1]
    pg = 16 if n_pages % 16 == 0 else n_pages
    assert page_rows == LANES
    pool_cmp = cache_cmp_kv.transpose(0, 2, 3, 4, 1).reshape(cache_cmp_kv.shape[0], 2 * KV_DIM, page_rows)
    pool_slc = cache_slc_kv.transpose(0, 2, 3, 4, 1).reshape(cache_slc_kv.shape[0], 2 * KV_DIM, page_rows)
    win_state = state_win_kv.transpose(0, 2, 3, 4, 1).reshape(bs, 2 * KV_DIM, win_len)
    assert ts == 8 and past_len % SEL_BLOCK == 0 and tp % 256 == 0 and tp // SEL_BLOCK <= N_SEL_POS

    def wo_pairs(l, n, y_main, y_mem, nsa_layout, b, t):
        tm = _row_tile(n, 1024)
        if nsa_layout:
            hw = NSA_GROUP * HEAD_DIM
            nt = t // tm
            y2d = y_main.reshape(b * NSA_KV_HEADS * t, hw)
            pairs = [(y2d, (tm, hw), (lambda i, h=h: (((i // nt) * NSA_KV_HEADS + h) * nt + i % nt, 0)),
                      w_o16[l, h * hw:(h + 1) * hw]) for h in range(NSA_KV_HEADS)]
            km = NSA_DIM
        else:
            km = y_main.shape[1]
            pairs = [(y_main, (tm, km), lambda i: (i, 0), w_o16[l, :km])]
        return pairs + [(y_mem, (tm, MEM_DIM), lambda i: (i, 0), w_o16[l, km:])]

    for l in range(depth):
        for gi, gr in enumerate(groups):
            x, b, t = gr["x"], gr["b"], gr["t"]
            n = b * t
            nsa_layout = False
            if l < n_a:
                proj = rms_matmul(x, g_mix[l], w_in_a16[l], tn=w_in_a16.shape[2] // 2)
                y_main, new_st = gated_conv(proj, gr["st"][l], conv_w[l], b, t)
                conv_out[gi].append(new_st)
                mem_fn = mem_attention if gi == 0 else mem_attention_decode
                y_mem = mem_fn(proj, 3 * conv_dim // MEM_DIM, gr["mem"][l], b, t)
            else:
                if l == n_a:
                    if gi == 0:
                        kv_t = rms_matmul(x, g_kv, w_kv16, tn=3 * KV_DIM, feature_major_batch=b)
                        kv_rows[gi] = kv_t
                        ck, cv = compress_prompt(kv_t, cw)
                        nsa_in = (_cmp_slots(ck, b), _cmp_slots(cv, b)) + kv_layout(kv_t)
                    else:
                        kv2d = rms_matmul(x, g_kv, w_kv16, tn=3 * KV_DIM)
                        kv_rows[gi] = kv2d
                        new3 = kv2d.reshape(b, t, 6 * KV_DIM)
                        ck_s, cv_s = compress_sample(page_table, pool_cmp, new3, cw, pg=pg)
                        per = SEL_BLOCK // CMP_STRIDE
                        slots = lambda c: c.reshape(b, c.shape[1] // per, per, KV_DIM).transpose(0, 2, 1, 3).reshape(
                            c.shape)
                        ck_s, cv_s = slots(ck_s), slots(cv_s)
                j = l - n_a
                if gi == 0:
                    proj = rms_matmul(x, g_mix[l], w_in_b16[j], tn=B_COLS // 2)
                    y_main = nsa_prompt(proj, slopes, *nsa_in, b, t)
                    nsa_layout = True
                    y_mem = mem_attention(proj, B_MQ_COL // MEM_DIM, gr["mem"][l], b, t)
                else:
                    proj = rms_matmul(x, g_mix[l], w_in_bs16[j], tn=S_COLS)
                    y_main = nsa_decode(proj, slopes, page_table, ck_s, cv_s, new3, win_state, pool_slc, pg=pg)
                    y_mem = mem_attention_decode(proj, S_MQ_COL // MEM_DIM, gr["mem"][l], b, t)
            x = proj_residual(x, wo_pairs(l, n, y_main, y_mem, nsa_layout, b, t))
            x = ffn(x, g_ffn[l], w_gu16[l], w_dn16[l])
            gr["x"] = x

    y_prompt = rmsnorm_rows(groups[0]["x"], g_final).reshape(bp, tp, d)
    y_sample = rmsnorm_rows(groups[1]["x"], g_final).reshape(bs, ts, d)
    conv_state_p = jnp.stack(conv_out[0])
    conv_state_s = jnp.stack(conv_out[1])
    mem_kv_out = mem_kv_p.reshape(depth, bp, n_mem, 2, MEM_HEADS, HEAD_DIM)
    def branch_rows(kv2d, br, b, t):
        return kv2d[:, br * 2 * KV_DIM:(br + 1) * 2 * KV_DIM].reshape(b, t, 2, NSA_KV_HEADS, HEAD_DIM)

    def branch_rows_t(kv_t, br, t0):
        rows = kv_t[:, br * 2 * KV_DIM:(br + 1) * 2 * KV_DIM, t0:]
        return rows.reshape(kv_t.shape[0], 2, NSA_KV_HEADS, HEAD_DIM, -1).transpose(0, 4, 1, 2, 3)

    kvp, kvs = kv_rows
    win_kv_s = jnp.concatenate([state_win_kv, branch_rows(kvs, 2, bs, ts)], axis=1)[:, ts:]
    return (y_prompt, y_sample, conv_state_p, conv_state_s, mem_kv_out, branch_rows_t(kvp, 0, 0),
            branch_rows_t(kvp, 1, 0), branch_rows_t(kvp, 2, tp - min(WINDOW, tp)), branch_rows(kvs, 0, bs, ts),
            branch_rows(kvs, 1, bs, ts), win_kv_s)
```

```python
import functools
import math

import numpy as np
import jax
import jax.numpy as jnp
from jax import lax
from jax.experimental import pallas as pl
from jax.experimental.pallas import tpu as pltpu

F32 = jnp.float32
BF16 = jnp.bfloat16

HEAD_DIM = 64
MEM_HEADS = 4
MEM_DIM = MEM_HEADS * HEAD_DIM
NSA_KV_HEADS = 4
NSA_GROUP = 3
NSA_HEADS = NSA_KV_HEADS * NSA_GROUP
NSA_DIM = NSA_HEADS * HEAD_DIM
KV_DIM = NSA_KV_HEADS * HEAD_DIM
CONV_WIDTH = 3
CMP_STRIDE = 16
CMP_LEN = 32
SEL_BLOCK = 64
N_SEL = 16
WINDOW = 512
Q_BLOCK = 64
RMS_EPS = 1e-6
NEG_INF = -1e30
FORCE_SCORE = 1e4
ATT_SCALE = HEAD_DIM ** -0.5

VMEM_LIMIT = 48 * 1024 * 1024


def _cparams(*sem):
    return pltpu.CompilerParams(dimension_semantics=sem, vmem_limit_bytes=VMEM_LIMIT)


def _row_tile(n, want):
    t = min(n, want)
    while n % t:
        t //= 2
    return t


def _rms_matmul_kernel(x_ref, g_ref, w_ref, o_ref, h_ref):
    @pl.when(pl.program_id(1) == 0)
    def _():
        x = x_ref[...]
        ms = jnp.mean(x * x, axis=-1, keepdims=True)
        h_ref[...] = (x * lax.rsqrt(ms + RMS_EPS) * g_ref[...]).astype(BF16)

    o_ref[...] = jnp.dot(h_ref[...], w_ref[...], preferred_element_type=F32)


def _rms_matmul_t_kernel(x_ref, g_ref, w_ref, o_ref, h_ref):
    @pl.when(pl.program_id(1) == 0)
    def _():
        x = x_ref[...]
        ms = jnp.mean(x * x, axis=-1, keepdims=True)
        h_ref[...] = (x * lax.rsqrt(ms + RMS_EPS) * g_ref[...]).astype(BF16)

    o_ref[0] = jnp.dot(h_ref[...], w_ref[...], preferred_element_type=F32).T


def rms_matmul(x, g, w, *, tm=1024, tn=512, feature_major_batch=None):
    n, d = x.shape
    c = w.shape[1]
    tn = _row_tile(c, tn)
    if feature_major_batch is None:
        tm = _row_tile(n, tm)
        kern, out_shape = _rms_matmul_kernel, jax.ShapeDtypeStruct((n, c), F32)
        out_spec = pl.BlockSpec((tm, tn), lambda i, j: (i, j))
    else:
        t = n // feature_major_batch
        tm = _row_tile(t, tm)
        nt = t // tm
        kern, out_shape = _rms_matmul_t_kernel, jax.ShapeDtypeStruct((feature_major_batch, c, t), F32)
        out_spec = pl.BlockSpec((1, tn, tm), lambda i, j: (i // nt, j, i % nt))
    return pl.pallas_call(
        kern,
        out_shape=out_shape,
        grid=(n // tm, c // tn),
        in_specs=[
            pl.BlockSpec((tm, d), lambda i, j: (i, 0)),
            pl.BlockSpec((1, d), lambda i, j: (0, 0)),
            pl.BlockSpec((d, tn), lambda i, j: (0, j)),
        ],
        out_specs=out_spec,
        scratch_shapes=[pltpu.VMEM((tm, d), BF16)],
        compiler_params=_cparams("parallel", "arbitrary"),
        name="rms_matmul",
    )(x, g.reshape(1, d), w)


def _proj_residual_kernel(n_pairs, x_ref, *refs):
    a_refs = refs[:n_pairs]
    w_refs = refs[n_pairs:2 * n_pairs]
    o_ref = refs[2 * n_pairs]
    acc = x_ref[...]
    for a_ref, w_ref in zip(a_refs, w_refs):
        acc = acc + jnp.dot(a_ref[...].astype(BF16), w_ref[...], preferred_element_type=F32)
    o_ref[...] = acc


def proj_residual(x, pairs, *, tm=1024):
    n, d = x.shape
    tm = _row_tile(n, tm)
    in_specs = [pl.BlockSpec((tm, d), lambda i: (i, 0))]
    args = [x]
    for a, blk, imap, _ in pairs:
        in_specs.append(pl.BlockSpec(blk, imap))
        args.append(a)
    for _, _, _, w in pairs:
        in_specs.append(pl.BlockSpec(w.shape, lambda i: (0, 0)))
        args.append(w)
    return pl.pallas_call(
        functools.partial(_proj_residual_kernel, len(pairs)),
        out_shape=jax.ShapeDtypeStruct((n, d), F32),
        grid=(n // tm,),
        in_specs=in_specs,
        out_specs=pl.BlockSpec((tm, d), lambda i: (i, 0)),
        compiler_params=_cparams("parallel"),
        name="proj_residual",
    )(*args)


def _ffn_kernel(x_ref, g_ref, wg_ref, wu_ref, wd_ref, o_ref, h_ref, acc_ref):
    f = pl.program_id(1)

    @pl.when(f == 0)
    def _():
        x = x_ref[...]
        ms = jnp.mean(x * x, axis=-1, keepdims=True)
        h_ref[...] = (x * lax.rsqrt(ms + RMS_EPS) * g_ref[...]).astype(BF16)
        acc_ref[...] = x

    h = h_ref[...]
    gate = jnp.dot(h, wg_ref[...], preferred_element_type=F32)
    up = jnp.dot(h, wu_ref[...], preferred_element_type=F32)
    act = (gate * jax.nn.sigmoid(gate) * up).astype(BF16)
    acc_ref[...] += jnp.dot(act, wd_ref[...], preferred_element_type=F32)

    @pl.when(f == pl.num_programs(1) - 1)
    def _():
        o_ref[...] = acc_ref[...]


def ffn(x, g, w_gu, w_dn, *, tm=512, tf=1408):
    n, d = x.shape
    dff = w_dn.shape[0]
    tm = _row_tile(n, tm)
    nf = dff // tf
    return pl.pallas_call(
        _ffn_kernel,
        out_shape=jax.ShapeDtypeStruct((n, d), F32),
        grid=(n // tm, nf),
        in_specs=[
            pl.BlockSpec((tm, d), lambda i, f: (i, 0)),
            pl.BlockSpec((1, d), lambda i, f: (0, 0)),
            pl.BlockSpec((d, tf), lambda i, f: (0, f)),
            pl.BlockSpec((d, tf), lambda i, f: (0, f + nf)),
            pl.BlockSpec((tf, d), lambda i, f: (f, 0)),
        ],
        out_specs=pl.BlockSpec((tm, d), lambda i, f: (i, 0)),
        scratch_shapes=[pltpu.VMEM((tm, d), BF16), pltpu.VMEM((tm, d), F32)],
        compiler_params=_cparams("parallel", "arbitrary"),
        name="ffn",
    )(x, g.reshape(1, d), w_gu, w_gu, w_dn)


def _rmsnorm_kernel(x_ref, g_ref, o_ref):
    x = x_ref[...]
    ms = jnp.mean(x * x, axis=-1, keepdims=True)
    o_ref[...] = x * lax.rsqrt(ms + RMS_EPS) * g_ref[...]


def rmsnorm_rows(x, g, *, tm=1024):
    n, d = x.shape
    tm = _row_tile(n, tm)
    return pl.pallas_call(
        _rmsnorm_kernel,
        out_shape=jax.ShapeDtypeStruct((n, d), F32),
        grid=(n // tm,),
        in_specs=[pl.BlockSpec((tm, d), lambda i: (i, 0)), pl.BlockSpec((1, d), lambda i: (0, 0))],
        out_specs=pl.BlockSpec((tm, d), lambda i: (i, 0)),
        compiler_params=_cparams("parallel"),
        name="rmsnorm",
    )(x, g.reshape(1, d))


def _conv_kernel(b_ref, c_ref, h_ref, cp_ref, hp_ref, st_ref, w_ref, y_ref, ns_ref):
    i = pl.program_id(1)
    u = c_ref[...] * h_ref[...]
    tt = u.shape[0]
    prev = cp_ref[...] * hp_ref[...]
    st = st_ref[0]
    first = i == 0
    p1 = jnp.where(first, st[1:2], prev[7:8])
    p2 = jnp.where(first, st[0:1], prev[6:7])
    row = lax.broadcasted_iota(jnp.int32, u.shape, 0)
    u1 = jnp.where(row == 0, p1, pltpu.roll(u, 1, 0))
    u2 = jnp.where(row == 0, p2, jnp.where(row == 1, p1, pltpu.roll(u, 2, 0)))
    w = w_ref[...]
    y = w[0:1] * u2 + w[1:2] * u1 + w[2:3] * u
    y_ref[...] = b_ref[...] * y
    ns_ref[0] = u[tt - 2:tt]


def gated_conv(proj, state, conv_w, batch, t, *, tt=512):
    c = conv_w.shape[1]
    tt = _row_tile(t, tt)
    nt = t // tt
    r8 = tt // 8

    def prev_map(col):
        return lambda b, i: (jnp.maximum((b * nt + i) * r8 - 1, 0), col)

    return pl.pallas_call(
        _conv_kernel,
        out_shape=(jax.ShapeDtypeStruct((batch * t, c), F32), jax.ShapeDtypeStruct((batch, 2, c), F32)),
        grid=(batch, nt),
        in_specs=[
            pl.BlockSpec((tt, c), lambda b, i: (b * nt + i, 0)),
            pl.BlockSpec((tt, c), lambda b, i: (b * nt + i, 1)),
            pl.BlockSpec((tt, c), lambda b, i: (b * nt + i, 2)),
            pl.BlockSpec((8, c), prev_map(1)),
            pl.BlockSpec((8, c), prev_map(2)),
            pl.BlockSpec((1, 2, c), lambda b, i: (b, 0, 0)),
            pl.BlockSpec((CONV_WIDTH, c), lambda b, i: (0, 0)),
        ],
        out_specs=(
            pl.BlockSpec((tt, c), lambda b, i: (b * nt + i, 0)),
            pl.BlockSpec((1, 2, c), lambda b, i: (b, 0, 0)),
        ),
        compiler_params=_cparams("parallel", "arbitrary"),
        name="gated_conv",
    )(proj, proj, proj, proj, proj, state, conv_w)


def _mem_attn_kernel(q_ref, kv_ref, o_ref):
    nb = kv_ref.shape[0]
    tt = q_ref.shape[0] // nb
    for i in range(nb):
        q = q_ref[i * tt:(i + 1) * tt, :] * ATT_SCALE
        kv = kv_ref[i]
        outs = []
        for h in range(MEM_HEADS):
            qh = q[:, h * HEAD_DIM:(h + 1) * HEAD_DIM].astype(BF16)
            kh = kv[:, h * HEAD_DIM:(h + 1) * HEAD_DIM].astype(BF16)
            vh = kv[:, MEM_DIM + h * HEAD_DIM:MEM_DIM + (h + 1) * HEAD_DIM].astype(BF16)
            s = lax.dot_general(qh, kh, (((1,), (1,)), ((), ())), preferred_element_type=F32)
            m = jnp.max(s, axis=-1, keepdims=True)
            e = jnp.exp(s - m)
            p = e / jnp.sum(e, axis=-1, keepdims=True)
            outs.append(jnp.dot(p.astype(BF16), vh, preferred_element_type=F32))
        o_ref[i * tt:(i + 1) * tt, :] = jnp.concatenate(outs, axis=-1)


def mem_attention(proj, col_block, mem_kv, batch, t, *, tt=1024):
    tt = _row_tile(t, tt)
    nt = t // tt
    nb = 1
    n_mem = mem_kv.shape[1]
    return pl.pallas_call(
        _mem_attn_kernel,
        out_shape=jax.ShapeDtypeStruct((batch * t, MEM_DIM), F32),
        grid=(batch // nb, nt),
        in_specs=[
            pl.BlockSpec((nb * tt, MEM_DIM), lambda b, i: (b * nt + i, col_block)),
            pl.BlockSpec((nb, n_mem, 2 * MEM_DIM), lambda b, i: (b, 0, 0)),
        ],
        out_specs=pl.BlockSpec((nb * tt, MEM_DIM), lambda b, i: (b * nt + i, 0)),
        compiler_params=_cparams("parallel", "arbitrary"),
        name="mem_attention",
    )(proj, mem_kv)


def _mem_attn_decode_kernel(q_ref, kv_ref, o_ref):
    nb = kv_ref.shape[0]
    ts = q_ref.shape[0] // nb
    lane_h = lax.broadcasted_iota(jnp.int32, (ts, MEM_DIM), 1) // HEAD_DIM
    nt_dims = (((1,), (1,)), ((), ()))
    for i in range(nb):
        q = q_ref[i * ts:(i + 1) * ts, :] * ATT_SCALE
        qbd = jnp.concatenate([jnp.where(lane_h == h, q, 0.0) for h in range(MEM_HEADS)], axis=0).astype(BF16)
        s = jnp.dot(qbd, kv_ref[i, 0:MEM_DIM, :].astype(BF16), preferred_element_type=F32)
        m = jnp.max(s, axis=-1, keepdims=True)
        e = jnp.exp(s - m)
        p = e / jnp.sum(e, axis=-1, keepdims=True)
        o = lax.dot_general(p.astype(BF16), kv_ref[i, MEM_DIM:2 * MEM_DIM, :].astype(BF16), nt_dims,
                            preferred_element_type=F32)
        o_ref[i * ts:(i + 1) * ts, :] = jnp.concatenate(
            [o[h * ts:(h + 1) * ts, h * HEAD_DIM:(h + 1) * HEAD_DIM] for h in range(MEM_HEADS)], axis=1)


def mem_attention_decode(proj, col_block, mem_kv_t, batch, ts, *, group=4):
    nb = group if batch % group == 0 else 1
    n_mem = mem_kv_t.shape[2]
    return pl.pallas_call(
        _mem_attn_decode_kernel,
        out_shape=jax.ShapeDtypeStruct((batch * ts, MEM_DIM), F32),
        grid=(batch // nb,),
        in_specs=[
            pl.BlockSpec((nb * ts, MEM_DIM), lambda b: (b, col_block)),
            pl.BlockSpec((nb, 2 * MEM_DIM, n_mem), lambda b: (b, 0, 0)),
        ],
        out_specs=pl.BlockSpec((nb * ts, MEM_DIM), lambda b: (b, 0)),
        compiler_params=_cparams("parallel"),
        name="mem_attention_decode",
    )(proj, mem_kv_t)


def _alibi_list(n):
    def pow2(m):
        start = 2.0 ** (-8.0 / m)
        return [start ** (i + 1) for i in range(m)]
    if n & (n - 1) == 0:
        return pow2(n)
    c = 2 ** int(math.floor(math.log2(n)))
    return pow2(c) + _alibi_list(2 * c)[0::2][: n - c]


LANES = 128
PAIR = LANES // HEAD_DIM


CMP_PITCH = CMP_STRIDE + 1


def _to_lane_columns(src, dst_ref, n_rows):
    for c in range(dst_ref.shape[0]):
        dst_ref[c, 0:n_rows, :] = src[0:n_rows, c * LANES:(c + 1) * LANES]


def _compress_tile(buf_ref, row0, m, pe_refs, w1_refs, w2_refs, pitch=CMP_STRIDE):
    outs = []
    base = row0 // CMP_STRIDE * pitch
    for kv in range(2):
        pe = pe_refs[kv][...]
        lhs = []
        for p in range(NSA_KV_HEADS // PAIR):
            col = kv * (KV_DIM // LANES) + p
            pieces = [
                (buf_ref[col, pl.ds(base + j // CMP_STRIDE * pitch + j % CMP_STRIDE, m, stride=pitch), :]
                 + pe[j:j + 1]).astype(BF16)
                for j in range(CMP_LEN)
            ]
            lhs.append(jnp.concatenate(pieces, axis=1))
        lhs = jnp.concatenate(lhs, axis=0)
        hid = jnp.dot(lhs, w1_refs[kv][...], preferred_element_type=F32)
        act = (hid * jax.nn.sigmoid(hid)).astype(BF16)
        out = jnp.dot(act, w2_refs[kv][...], preferred_element_type=F32)
        outs.append(jnp.concatenate([out[0:m], out[m:2 * m]], axis=1))
    return outs


def _compress_prompt_kernel(x_ref, pek_ref, pev_ref, w1k_ref, w1v_ref, w2k_ref, w2v_ref, ck_ref, cv_ref, buf_ref):
    t = x_ref.shape[2]
    for c in range(buf_ref.shape[0]):
        for r in range(0, t, LANES):
            buf_ref[c, r:r + LANES, :] = x_ref[0, c * LANES:(c + 1) * LANES, r:r + LANES].T
    buf_ref[:, t:t + CMP_STRIDE, :] = jnp.zeros((buf_ref.shape[0], CMP_STRIDE, LANES), F32)
    n_blk = t // CMP_STRIDE
    m = min(n_blk, 128)
    for s in range(n_blk // m):
        ck, cv = _compress_tile(buf_ref, s * m * CMP_STRIDE, m, (pek_ref, pev_ref), (w1k_ref, w1v_ref),
                                (w2k_ref, w2v_ref))
        ck_ref[0, s * m:(s + 1) * m, :] = ck
        cv_ref[0, s * m:(s + 1) * m, :] = cv


def _compress_weights(pe, w1, w2):
    eye = jnp.eye(PAIR, dtype=F32)
    w1p = jnp.einsum('jde,qr->jqdre', w1, eye).reshape(CMP_LEN * PAIR * HEAD_DIM, PAIR * w1.shape[2])
    w2p = jnp.einsum('ed,qr->qerd', w2, eye).reshape(PAIR * w2.shape[0], PAIR * HEAD_DIM)
    pe2 = jnp.tile(pe, (1, PAIR))
    return pe2, w1p.astype(BF16), w2p.astype(BF16)


def compress_prompt(kv_t, cw):
    pek, w1k, w2k, pev, w1v, w2v = cw
    batch, _, t = kv_t.shape
    n_blk = t // CMP_STRIDE
    full = lambda a: pl.BlockSpec(a.shape, lambda b: (0,) * a.ndim)
    return pl.pallas_call(
        _compress_prompt_kernel,
        out_shape=(jax.ShapeDtypeStruct((batch, n_blk, KV_DIM), F32),) * 2,
        grid=(batch,),
        in_specs=[pl.BlockSpec((1, 2 * KV_DIM, t), lambda b: (b, 0, 0)), full(pek), full(pev), full(w1k), full(w1v),
                  full(w2k), full(w2v)],
        out_specs=(pl.BlockSpec((1, n_blk, KV_DIM), lambda b: (b, 0, 0)),) * 2,
        scratch_shapes=[pltpu.VMEM((2 * KV_DIM // LANES, t + CMP_STRIDE, LANES), F32)],
        compiler_params=_cparams("parallel"),
        name="compress_prompt",
    )(kv_t, pek, pev, w1k, w1v, w2k, w2v)


N_CMP_POS = 256
N_SEL_POS = N_CMP_POS * CMP_STRIDE // SEL_BLOCK
MASK_BIG = 2.0 ** 100
M_INIT = -1e38


LOG2E = 1.4426950408889634
MASK_NONE, MASK_CAUSAL, MASK_BAND = 0, 1, 2


def _topk_mask(score):
    tq, nb = score.shape
    pad = jnp.full((tq, LANES - nb), -jnp.inf, F32)
    st = jnp.concatenate([score, pad], axis=1).T[:nb]
    sub = 8
    idx = lax.broadcasted_iota(jnp.int32, (sub, tq), 0)
    groups = [st[r:r + sub] for r in range(0, nb, sub)]
    ranks = [jnp.zeros((sub, tq), F32) for _ in groups]
    for j in range(nb):
        rj = st[j:j + 1, :]
        for gi, sg in enumerate(groups):
            if gi * sub > j:
                before = rj >= sg
            elif gi * sub + sub - 1 < j:
                before = rj > sg
            else:
                before = jnp.where(idx > j - gi * sub, (rj >= sg).astype(F32), (rj > sg).astype(F32)) > 0.0
            ranks[gi] = jnp.where(before, ranks[gi] + 1.0, ranks[gi])
    sel_t = (jnp.concatenate(ranks, axis=0) < N_SEL).astype(F32)
    blk_any = jnp.max(sel_t, axis=1, keepdims=True)
    sel_t = jnp.concatenate([sel_t, jnp.zeros((LANES - nb, tq), F32)], axis=0)
    return sel_t.T[:, :nb], blk_any


def _chunk_words(blk_any, per_chunk):
    nb = blk_any.shape[0]
    per_word = 8 * per_chunk
    bidx = lax.broadcasted_iota(jnp.int32, (nb, 1), 0)
    wgt = jnp.left_shift(1, 3 * ((bidx % per_word) // per_chunk)).astype(F32)
    words = []
    for w in range(nb // per_word):
        v = jnp.sum(jnp.where(bidx // per_word == w, blk_any * wgt, 0.0), axis=0, keepdims=True)
        words.append(v.astype(jnp.int32)[0, 0])
    return words


def _nsa_prompt_kernel(sl_ref, q_ref, ck_ref, cv_ref, kc_ref, vc_ref, o_ref,
                       q_scr, s0_scr, s1_scr, p0_scr, p1_scr, a0_scr, a1_scr, m_scr, l_scr, acc_scr, mb_scr, lst_ref,
                       *, mm_rows, sm_rows):
    kvh = pl.program_id(1)
    i = pl.program_id(2)
    s_scrs, p_scrs, a_scrs = (s0_scr, s1_scr), (p0_scr, p1_scr), (a0_scr, a1_scr)
    tq = q_ref.shape[0]
    kb = tq
    t_len = kc_ref.shape[2] // 2
    t0 = i * tq
    slopes = [sl_ref[kvh * NSA_GROUP + g] * LOG2E for g in range(NSA_GROUP)]
    q = q_ref[...] * (ATT_SCALE * LOG2E)
    q3 = jnp.concatenate([q[:, g * HEAD_DIM:(g + 1) * HEAD_DIM] for g in range(NSA_GROUP)], axis=0)
    q3b = q3.astype(BF16)
    nt_dims = (((1,), (1,)), ((), ()))
    t_col = t0 + lax.broadcasted_iota(jnp.int32, (tq, 1), 0)

    @pl.when(i == 0)
    def _():
        r = lax.broadcasted_iota(jnp.int32, (tq, kb), 0)
        c = lax.broadcasted_iota(jnp.int32, (tq, kb), 1)
        mb_scr[MASK_NONE] = jnp.zeros((tq, kb), F32)
        mb_scr[MASK_CAUSAL] = jnp.where(c <= r, 0.0, -MASK_BIG)
        mb_scr[MASK_BAND] = jnp.where(c >= r, 0.0, -MASK_BIG)

    m_scr[...] = jnp.full(m_scr.shape, M_INIT, F32)
    l_scr[...] = jnp.zeros(l_scr.shape, F32)
    acc_scr[...] = jnp.zeros(acc_scr.shape, F32)

    pos = lax.broadcasted_iota(jnp.int32, (1, N_CMP_POS), 1)
    blk_n = (pos % N_SEL_POS) * (N_CMP_POS // N_SEL_POS) + pos // N_SEL_POS
    c_end = blk_n * CMP_STRIDE + (CMP_LEN - 1)
    n_real = t_len // CMP_STRIDE - 1
    vis_c = (c_end <= t_col) & (blk_n < n_real)
    c_end_f = c_end.astype(F32)
    s_c = lax.dot_general(q3b, ck_ref[0, 0], nt_dims, preferred_element_type=F32)
    p_c = []
    for g in range(NSA_GROUP):
        sg = jnp.where(vis_c, s_c[g * tq:(g + 1) * tq] + slopes[g] * c_end_f, NEG_INF)
        mg = jnp.max(sg, axis=1, keepdims=True)
        eg = jnp.where(vis_c, jnp.exp2(sg - mg), 0.0)
        dg = jnp.sum(eg, axis=1, keepdims=True)
        p_c.append(eg / jnp.where(dg > 0.0, dg, 1.0))
    o_c = jnp.dot(jnp.concatenate(p_c, axis=0).astype(BF16), cv_ref[0, 0], preferred_element_type=F32)
    imp = p_c[0] + p_c[1] + p_c[2]
    imp = imp[:, :LANES] + imp[:, LANES:]
    imp = imp[:, :N_SEL_POS] + imp[:, N_SEL_POS:]

    blk = lax.broadcasted_iota(jnp.int32, (1, N_SEL_POS), 1)
    cur = t_col // SEL_BLOCK
    forced = (blk == 0) | (blk == cur) | (blk == cur - 1)
    score = jnp.where(forced, FORCE_SCORE, jnp.where(blk <= cur, imp, -jnp.inf))
    sel, blk_any = _topk_mask(score)
    notsel = (1.0 - sel).astype(BF16)
    q_scr[...] = jnp.concatenate([q3b, jnp.concatenate([notsel] * NSA_GROUP, axis=0)], axis=1)

    lane_k = lax.broadcasted_iota(jnp.int32, (1, kb), 1)
    n_win = WINDOW // kb + 1

    words = _chunk_words(blk_any, kb // SEL_BLOCK)
    n_sel = jnp.int32(0)
    for c in range(N_SEL_POS * SEL_BLOCK // kb):
        active = (((words[c // 8] >> (3 * (c % 8))) & 7) != 0) & (c <= i)
        lst_ref[n_sel] = c
        n_sel = n_sel + active.astype(jnp.int32)
    n_chunks = n_sel + n_win

    def chunk(p):
        is_sel = p < n_sel
        is_pad = p >= n_chunks
        c = lst_ref[jnp.minimum(p, n_sel - 1)]
        w = p - n_sel
        kpos0 = jnp.where(is_sel, c * kb, jnp.where(is_pad, -kb, t0 - WINDOW + w * kb))
        row0 = jnp.maximum(kpos0, 0) + jnp.where(is_sel | is_pad, 0, t_len)
        mtype = jnp.where(is_sel, jnp.where(c == i, MASK_CAUSAL, MASK_NONE),
                          jnp.where(w == 0, MASK_BAND, jnp.where(w == n_win - 1, MASK_CAUSAL, MASK_NONE)))
        return pl.multiple_of(row0, kb), kpos0, mtype, jnp.where(is_sel | is_pad, 0, 1)

    def stage_logits(c, par):
        s_scr = s_scrs[par]
        row0, kpos0, mtype, _ = chunk(c)
        k = kc_ref[0, 0, pl.ds(row0, kb), :]
        kpos_f = (kpos0 + lane_k).astype(F32)
        off = jnp.where(kpos0 >= 0, 0.0, -MASK_BIG)
        for g in range(NSA_GROUP):
            bias = slopes[g] * kpos_f + off
            for r in range(0, tq, mm_rows):
                rows = slice(g * tq + r, g * tq + r + mm_rows)
                s = lax.dot_general(q_scr[rows, :], k, nt_dims, preferred_element_type=F32)
                s_scr[rows, :] = (s + mb_scr[mtype, r:r + mm_rows, :]) + bias

    def stage_softmax(c, par):
        s_scr, p_scr, a_scr = s_scrs[par], p_scrs[par], a_scrs[par]
        st = chunk(c)[3]
        for r in range(0, NSA_GROUP * tq, sm_rows):
            rows = slice(r, r + sm_rows)
            s = s_scr[rows, :]
            m_prev = m_scr[st, rows, :]
            m_new = jnp.maximum(m_prev, jnp.max(s, axis=1, keepdims=True))
            alpha = jnp.exp2(m_prev - m_new)
            p = jnp.exp2(s - jnp.concatenate([m_new] * (kb // LANES), axis=1))
            l_scr[st, rows, :] = alpha * l_scr[st, rows, :] + jnp.sum(p, axis=1, keepdims=True)
            p_scr[rows, :] = p.astype(BF16)
            a_scr[rows, :] = alpha
            m_scr[st, rows, :] = m_new

    def stage_values(c, par):
        p_scr, a_scr = p_scrs[par], a_scrs[par]
        row0, _, _, st = chunk(c)
        v = vc_ref[0, 0, pl.ds(row0, kb), :]
        for r in range(0, NSA_GROUP * tq, mm_rows):
            rows = slice(r, r + mm_rows)
            pv = jnp.dot(p_scr[rows, :], v, preferred_element_type=F32)
            acc_scr[st, rows, :] = acc_scr[st, rows, :] * a_scr[rows, :HEAD_DIM] + pv

    stage_logits(0, 0)
    stage_softmax(0, 0)
    stage_logits(1, 1)

    def pipe_body(j, carry):
        c = 2 * j
        stage_values(c - 2, 0)
        stage_softmax(c - 1, 1)
        stage_logits(c, 0)
        stage_values(c - 1, 1)
        stage_softmax(c, 0)
        stage_logits(c + 1, 1)
        return carry

    n_even = n_chunks + n_chunks % 2
    lax.fori_loop(1, n_even // 2, pipe_body, 0)
    stage_values(n_even - 2, 0)
    stage_softmax(n_even - 1, 1)
    stage_values(n_even - 1, 1)
    o_s = acc_scr[0] / l_scr[0][:, :HEAD_DIM]
    o_w = acc_scr[1] / l_scr[1][:, :HEAD_DIM]

    gates = jax.nn.sigmoid(q_ref[:, Q_GATE_LANE:Q_SLOT])
    outs = []
    for g in range(NSA_GROUP):
        rows = slice(g * tq, (g + 1) * tq)
        outs.append(gates[:, 3 * g:3 * g + 1] * o_c[rows] + gates[:, 3 * g + 1:3 * g + 2] * o_s[rows]
                    + gates[:, 3 * g + 2:3 * g + 3] * o_w[rows])
    o_ref[0, 0] = jnp.concatenate(outs, axis=1)


def nsa_prompt(proj, slopes, ckp, cvp, kcat, vcat, batch, t, *, tq=256, mm_rows=256, sm_rows=256):
    assert t % tq == 0 and WINDOW % tq == 0 and WINDOW >= tq
    nt = t // tq
    rows = NSA_GROUP * tq
    grid_spec = pltpu.PrefetchScalarGridSpec(
        num_scalar_prefetch=1,
        grid=(batch, NSA_KV_HEADS, nt),
        in_specs=[
            pl.BlockSpec((tq, Q_SLOT), lambda b, h, i, sl: (b * nt + i, h)),
            pl.BlockSpec((1, 1, N_CMP_POS, HEAD_DIM), lambda b, h, i, sl: (b, h, 0, 0)),
            pl.BlockSpec((1, 1, N_CMP_POS, HEAD_DIM), lambda b, h, i, sl: (b, h, 0, 0)),
            pl.BlockSpec((1, 1, 2 * t, LANES), lambda b, h, i, sl: (b, h, 0, 0)),
            pl.BlockSpec((1, 1, 2 * t, HEAD_DIM), lambda b, h, i, sl: (b, h, 0, 0)),
        ],
        out_specs=pl.BlockSpec((1, 1, tq, NSA_GROUP * HEAD_DIM), lambda b, h, i, sl: (b, h, i, 0)),
        scratch_shapes=[
            pltpu.VMEM((rows, LANES), BF16),
            pltpu.VMEM((rows, tq), F32), pltpu.VMEM((rows, tq), F32),
            pltpu.VMEM((rows, tq), BF16), pltpu.VMEM((rows, tq), BF16),
            pltpu.VMEM((rows, LANES), F32), pltpu.VMEM((rows, LANES), F32),
            pltpu.VMEM((2, rows, LANES), F32),
            pltpu.VMEM((2, rows, LANES), F32),
            pltpu.VMEM((2, rows, HEAD_DIM), F32),
            pltpu.VMEM((3, tq, tq), F32),
            pltpu.SMEM((N_SEL_POS * SEL_BLOCK // tq,), jnp.int32),
        ],
    )
    return pl.pallas_call(
        functools.partial(_nsa_prompt_kernel, mm_rows=mm_rows, sm_rows=sm_rows),
        out_shape=jax.ShapeDtypeStruct((batch, NSA_KV_HEADS, t, NSA_GROUP * HEAD_DIM), F32),
        grid_spec=grid_spec,
        compiler_params=_cparams("parallel", "parallel", "arbitrary"),
        name="nsa_prompt",
    )(slopes, proj, ckp, cvp, kcat, vcat)


def _kv_layout_kernel(x_ref, kc_ref, vc_ref):
    tm = x_ref.shape[2]
    pos = pl.program_id(1) * tm + lax.broadcasted_iota(jnp.int32, (tm, N_SEL_POS), 0)
    blk = lax.broadcasted_iota(jnp.int32, (tm, N_SEL_POS), 1)
    onehot = jnp.where(pos // SEL_BLOCK == blk, -MASK_BIG, 0.0).astype(BF16)
    zeros = jnp.zeros((tm, N_SEL_POS), BF16)
    for p in range(NSA_KV_HEADS // PAIR):
        pair = lambda c0: x_ref[0, c0 + p * LANES:c0 + (p + 1) * LANES, :].T.astype(BF16)
        ks, vs, kw, vw = pair(2 * KV_DIM), pair(3 * KV_DIM), pair(4 * KV_DIM), pair(5 * KV_DIM)
        for q in range(PAIR):
            h = p * PAIR + q
            lanes = slice(q * HEAD_DIM, (q + 1) * HEAD_DIM)
            kc_ref[0, h, 0] = jnp.concatenate([ks[:, lanes], onehot], axis=1)
            kc_ref[0, h, 1] = jnp.concatenate([kw[:, lanes], zeros], axis=1)
            vc_ref[0, h, 0] = vs[:, lanes]
            vc_ref[0, h, 1] = vw[:, lanes]


def kv_layout(kv_t, *, tm=512):
    batch, n_feat, t = kv_t.shape
    tm = _row_tile(t, tm)
    nt = t // tm
    kc, vc = pl.pallas_call(
        _kv_layout_kernel,
        out_shape=(jax.ShapeDtypeStruct((batch, NSA_KV_HEADS, 2, t, LANES), BF16),
                   jax.ShapeDtypeStruct((batch, NSA_KV_HEADS, 2, t, HEAD_DIM), BF16)),
        grid=(batch, nt),
        in_specs=[pl.BlockSpec((1, n_feat, tm), lambda b, i: (b, 0, i))],
        out_specs=(pl.BlockSpec((1, NSA_KV_HEADS, 2, tm, LANES), lambda b, i: (b, 0, 0, i, 0)),
                   pl.BlockSpec((1, NSA_KV_HEADS, 2, tm, HEAD_DIM), lambda b, i: (b, 0, 0, i, 0))),
        compiler_params=_cparams("parallel", "parallel"),
        name="kv_layout",
    )(kv_t)
    return (kc.reshape(batch, NSA_KV_HEADS, 2 * t, LANES), vc.reshape(batch, NSA_KV_HEADS, 2 * t, HEAD_DIM))


def _cmp_slots(c, batch):
    n_blk = c.shape[1]
    per = N_CMP_POS // N_SEL_POS
    c = jnp.pad(c, ((0, 0), (0, N_CMP_POS - n_blk), (0, 0)))
    c = c.reshape(batch, N_SEL_POS, per, NSA_KV_HEADS, HEAD_DIM).transpose(0, 3, 2, 1, 4)
    return c.reshape(batch, NSA_KV_HEADS, N_CMP_POS, HEAD_DIM).astype(BF16)


def _compress_sample_kernel(pt_ref, pool_ref, new_ref, pek_ref, pev_ref, w1k_ref, w1v_ref, w2k_ref, w2v_ref,
                            ck_ref, cv_ref, buf_ref, sem_ref, col_ref, *, n_pages, pg):
    b = pl.program_id(0)
    g = pl.program_id(1)
    ng = n_pages // pg
    step = b * ng + g
    n_steps = pl.num_programs(0) * ng
    slot = step % 2
    page_rows = buf_ref.shape[3]
    rows = pg * page_rows
    ts = new_ref.shape[1]

    def pages(st, sl, go):
        bb = st // ng
        gg = st % ng

        def copy(k, dst):
            cp = pltpu.make_async_copy(pool_ref.at[pt_ref[bb * n_pages + gg * pg + k]], buf_ref.at[sl, dst],
                                       sem_ref.at[sl])
            cp.start() if go == "start" else cp.wait()

        for k in range(pg):
            copy(k, k)

        @pl.when(gg < ng - 1)
        def _():
            copy(pg, pg)

    @pl.when(step == 0)
    def _():
        pages(0, 0, "start")

    @pl.when(step + 1 < n_steps)
    def _():
        pages(step + 1, 1 - slot, "start")

    pages(step, slot, "wait")
    n_col = col_ref.shape[0]
    per_page = page_rows // CMP_STRIDE
    for k in range(pg):
        for c in range(n_col):
            tile = buf_ref[slot, k, c * LANES:(c + 1) * LANES, :].T
            for q in range(per_page):
                r0 = (k * per_page + q) * CMP_PITCH
                col_ref[c, r0:r0 + CMP_STRIDE, :] = tile[q * CMP_STRIDE:(q + 1) * CMP_STRIDE]
    halo = pg * per_page * CMP_PITCH
    m_half = rows // CMP_STRIDE // 2
    refs = ((pek_ref, pev_ref), (w1k_ref, w1v_ref), (w2k_ref, w2v_ref))
    ck, cv = _compress_tile(col_ref, 0, m_half, *refs, pitch=CMP_PITCH)
    ck_ref[0, 0:m_half, :] = ck
    cv_ref[0, 0:m_half, :] = cv

    @pl.when(g < ng - 1)
    def _():
        for c in range(n_col):
            col_ref[c, halo:halo + CMP_STRIDE, :] = buf_ref[slot, pg, c * LANES:(c + 1) * LANES, :].T[0:CMP_STRIDE]

    @pl.when(g == ng - 1)
    def _():
        for c in range(n_col):
            col_ref[c, halo:halo + ts, :] = new_ref[0, :, c * LANES:(c + 1) * LANES]
            col_ref[c, halo + ts:halo + CMP_STRIDE, :] = jnp.zeros((CMP_STRIDE - ts, LANES), F32)

    ck, cv = _compress_tile(col_ref, m_half * CMP_STRIDE, m_half, *refs, pitch=CMP_PITCH)
    ck_ref[0, m_half:2 * m_half, :] = ck
    cv_ref[0, m_half:2 * m_half, :] = cv


def compress_sample(page_table, pool_t, new_rows, cw, *, pg):
    pek, w1k, w2k, pev, w1v, w2v = cw
    batch, n_pages = page_table.shape
    page_rows = pool_t.shape[2]
    ts = new_rows.shape[1]
    ng = n_pages // pg
    m = pg * page_rows // CMP_STRIDE
    full = lambda a: pl.BlockSpec(a.shape, lambda b, g, pt: (0,) * a.ndim)
    grid_spec = pltpu.PrefetchScalarGridSpec(
        num_scalar_prefetch=1,
        grid=(batch, ng),
        in_specs=[pl.BlockSpec(memory_space=pl.ANY),
                  pl.BlockSpec((1, ts, new_rows.shape[2]), lambda b, g, pt: (b, 0, 0)),
                  full(pek), full(pev), full(w1k), full(w1v), full(w2k), full(w2v)],
        out_specs=(pl.BlockSpec((1, m, KV_DIM), lambda b, g, pt: (b, g, 0)),) * 2,
        scratch_shapes=[pltpu.VMEM((2, pg + 1, 2 * KV_DIM, page_rows), F32), pltpu.SemaphoreType.DMA((2,)),
                        pltpu.VMEM((2 * KV_DIM // LANES, -(-(m + 1) * CMP_PITCH // 8) * 8, LANES), F32)],
    )
    return pl.pallas_call(
        functools.partial(_compress_sample_kernel, n_pages=n_pages, pg=pg),
        out_shape=(jax.ShapeDtypeStruct((batch, ng * m, KV_DIM), F32),) * 2,
        grid_spec=grid_spec,
        compiler_params=_cparams("arbitrary", "arbitrary"),
        name="compress_sample",
    )(page_table.reshape(-1), pool_t, new_rows, pek, pev, w1k, w1v, w2k, w2v)


S_COL_G = 32


def _softmax_lanes(pieces):
    m = functools.reduce(jnp.maximum, [jnp.max(s, axis=1, keepdims=True) for s in pieces])
    es = [jnp.where(s > 0.5 * NEG_INF, jnp.exp(s - m), 0.0) for s in pieces]
    d = functools.reduce(jnp.add, [jnp.sum(e, axis=1, keepdims=True) for e in es])
    inv = 1.0 / jnp.where(d > 0.0, d, 1.0)
    return [e * inv for e in es]


def _nsa_decode_kernel(sl_ref, pt_ref, proj_ref, ck_ref, cv_ref, new_ref, win_ref, e_ref, pool_ref, y_ref,
                       buf_ref, sem_ref, q_scr, m_scr, l_scr, acc_scr, oc_scr, ow_scr,
                       *, n_pages, pg, past):
    b = pl.program_id(0)
    g = pl.program_id(1)
    ng = n_pages // pg
    step = b * ng + g
    n_steps = pl.num_programs(0) * ng
    slot = step % 2
    page_rows = buf_ref.shape[3]
    rows = pg * page_rows
    ts = proj_ref.shape[0]
    nt_dims = (((1,), (1,)), ((), ()))

    def pages(st, sl, go):
        bb = st // ng
        gg = st % ng
        for k in range(pg):
            cp = pltpu.make_async_copy(pool_ref.at[pt_ref[bb * n_pages + gg * pg + k]], buf_ref.at[sl, k],
                                       sem_ref.at[sl])
            cp.start() if go == "start" else cp.wait()

    @pl.when(step == 0)
    def _():
        pages(0, 0, "start")

    @pl.when(step + 1 < n_steps)
    def _():
        pages(step + 1, 1 - slot, "start")

    row = lax.broadcasted_iota(jnp.int32, (LANES, 1), 0)
    row_g = row // S_COL_G
    row_h = (row % S_COL_G) // ts
    t_col = past + row % ts
    slope_col = jnp.zeros((LANES, 1), F32)
    for h in range(NSA_HEADS):
        slope_col = jnp.where((row_h == h // NSA_GROUP) & (row_g == h % NSA_GROUP), sl_ref[h], slope_col)
    n_sb = past // SEL_BLOCK + 1
    zpad = jnp.zeros((LANES - ts, KV_DIM), F32)
    lane_new = lax.broadcasted_iota(jnp.int32, (1, LANES), 1)
    kp_new = past + lane_new
    vis_new = (kp_new <= t_col) & (lane_new < ts)

    @pl.when(g == 0)
    def _():
        lane_h = lax.broadcasted_iota(jnp.int32, (ts, KV_DIM), 1) // HEAD_DIM
        tiles = []
        for gg in range(NSA_GROUP):
            qg = proj_ref[:, gg * KV_DIM:(gg + 1) * KV_DIM] * ATT_SCALE
            for h in range(NSA_KV_HEADS):
                tiles.append(jnp.where(lane_h == h, qg, 0.0))
        tiles.append(jnp.zeros((LANES - NSA_HEADS * ts, KV_DIM), F32))
        bdq = jnp.concatenate(tiles, axis=0).astype(BF16)
        q_scr[:, 0:KV_DIM] = bdq

        n_c = ck_ref.shape[1]
        per = SEL_BLOCK // CMP_STRIDE
        n_q = n_c // per
        s_c = lax.dot_general(bdq, ck_ref[0].astype(BF16), nt_dims, preferred_element_type=F32)
        slot_c = lax.broadcasted_iota(jnp.int32, (1, n_c), 1)
        c_end = ((slot_c % n_q) * per + slot_c // n_q) * CMP_STRIDE + (CMP_LEN - 1)
        dist = t_col - c_end
        s_c = jnp.where(dist >= 0, s_c - slope_col * dist.astype(F32), NEG_INF)
        (p_c,) = _softmax_lanes([s_c])
        oc_scr[...] = jnp.dot(p_c.astype(BF16), cv_ref[0].astype(BF16), preferred_element_type=F32)
        imp = p_c[0:S_COL_G] + p_c[S_COL_G:2 * S_COL_G] + p_c[2 * S_COL_G:3 * S_COL_G]
        imp = functools.reduce(jnp.add, [imp[:, r * n_q:(r + 1) * n_q] for r in range(per)])
        n_blk = q_scr.shape[1] - KV_DIM
        imp = jnp.concatenate([imp, jnp.zeros((S_COL_G, n_blk - n_q), F32)], axis=1)
        blk = lax.broadcasted_iota(jnp.int32, (1, n_blk), 1)
        cur = t_col[0:S_COL_G] // SEL_BLOCK
        forced = (blk == 0) | (blk == cur) | (blk == cur - 1)
        score = jnp.where(forced, FORCE_SCORE, jnp.where((blk <= cur) & (blk < n_sb), imp, -jnp.inf))
        rank = jnp.zeros(score.shape, F32)
        for j in range(n_sb):
            rj = score[:, j:j + 1]
            rank = rank + jnp.where(blk > j, (rj >= score).astype(F32), (rj > score).astype(F32))
        notsel = (rank >= N_SEL).astype(F32)
        notsel = jnp.concatenate([notsel] * NSA_GROUP + [jnp.ones((LANES - NSA_GROUP * S_COL_G, n_blk), F32)], axis=0)
        q_scr[:, KV_DIM:KV_DIM + n_blk] = notsel.astype(BF16)

        wl = win_ref.shape[2]
        kw_new = jnp.concatenate([new_ref[0, :, 4 * KV_DIM:5 * KV_DIM], zpad], axis=0).astype(BF16)
        vw_new = jnp.concatenate([new_ref[0, :, 5 * KV_DIM:6 * KV_DIM], zpad], axis=0).astype(BF16)
        s_w = jnp.dot(bdq, win_ref[0, 0:KV_DIM, :].astype(BF16), preferred_element_type=F32)
        w_pos = past - wl + lax.broadcasted_iota(jnp.int32, (1, wl), 1)
        dist_w = t_col - w_pos
        s_w = jnp.where((dist_w >= 0) & (dist_w <= WINDOW) & (w_pos >= 0), s_w - slope_col * dist_w.astype(F32),
                        NEG_INF)
        s_wn = lax.dot_general(bdq, kw_new, nt_dims, preferred_element_type=F32)
        dist_n = t_col - kp_new
        s_wn = jnp.where(vis_new & (dist_n <= WINDOW), s_wn - slope_col * dist_n.astype(F32), NEG_INF)
        p_w, p_wn = _softmax_lanes([s_w, s_wn])
        ow_scr[...] = (lax.dot_general(p_w.astype(BF16), win_ref[0, KV_DIM:2 * KV_DIM, :].astype(BF16), nt_dims,
                                       preferred_element_type=F32)
                       + jnp.dot(p_wn.astype(BF16), vw_new, preferred_element_type=F32))
        m_scr[...] = jnp.full(m_scr.shape, M_INIT, F32)
        l_scr[...] = jnp.zeros(l_scr.shape, F32)
        acc_scr[...] = jnp.zeros(acc_scr.shape, F32)

    def online_update(s, pv_fn):
        n = s.shape[1]
        m_prev = m_scr[...]
        m_new = jnp.maximum(m_prev, jnp.max(s, axis=1, keepdims=True))
        alpha = jnp.exp(m_prev - m_new)
        p = jnp.exp(s - jnp.concatenate([m_new] * (n // LANES), axis=1))
        l_scr[...] = alpha * l_scr[...] + jnp.sum(p, axis=1, keepdims=True)
        acc_scr[...] = acc_scr[...] * jnp.concatenate([alpha] * (KV_DIM // LANES), axis=1) + pv_fn(p.astype(BF16))
        m_scr[...] = m_new

    pages(step, slot, "wait")
    kt = jnp.concatenate([buf_ref[slot, k, 0:KV_DIM, :] for k in range(pg)], axis=1).astype(BF16)
    vt = jnp.concatenate([buf_ref[slot, k, KV_DIM:2 * KV_DIM, :] for k in range(pg)], axis=1).astype(BF16)
    n_blk = q_scr.shape[1] - KV_DIM
    s_g = (jnp.dot(q_scr[:, 0:KV_DIM], kt, preferred_element_type=F32)
           + jnp.dot(q_scr[:, KV_DIM:KV_DIM + n_blk], e_ref[0], preferred_element_type=F32))
    kpos = g * rows + lax.broadcasted_iota(jnp.int32, (1, rows), 1)
    s_g = s_g - slope_col * (t_col - kpos).astype(F32)
    online_update(s_g, lambda p: lax.dot_general(p, vt, nt_dims, preferred_element_type=F32))

    @pl.when(g == ng - 1)
    def _():
        k_new = jnp.concatenate([new_ref[0, :, 2 * KV_DIM:3 * KV_DIM], zpad], axis=0).astype(BF16)
        v_new = jnp.concatenate([new_ref[0, :, 3 * KV_DIM:4 * KV_DIM], zpad], axis=0).astype(BF16)
        s_n = lax.dot_general(q_scr[:, 0:KV_DIM], k_new, nt_dims, preferred_element_type=F32)
        unsel_last = q_scr[:, KV_DIM + n_sb - 1:KV_DIM + n_sb].astype(F32)
        dist_n = t_col - kp_new
        s_n = jnp.where(vis_new & (unsel_last < 0.5), s_n - slope_col * dist_n.astype(F32), -MASK_BIG)
        online_update(s_n, lambda p: jnp.dot(p, v_new, preferred_element_type=F32))
        o_s = acc_scr[...] / jnp.concatenate([l_scr[...]] * (KV_DIM // LANES), axis=1)
        o_c = oc_scr[...]
        o_w = ow_scr[...]
        gates = jax.nn.sigmoid(proj_ref[:, (NSA_GROUP + 1) * KV_DIM:(NSA_GROUP + 1) * KV_DIM + LANES])
        outs = []
        for h in range(NSA_HEADS):
            kvh, gg = h // NSA_GROUP, h % NSA_GROUP
            r0 = gg * S_COL_G + kvh * ts
            lanes = slice(kvh * HEAD_DIM, (kvh + 1) * HEAD_DIM)
            outs.append(gates[:, 3 * h:3 * h + 1] * o_c[r0:r0 + ts, lanes]
                        + gates[:, 3 * h + 1:3 * h + 2] * o_s[r0:r0 + ts, lanes]
                        + gates[:, 3 * h + 2:3 * h + 3] * o_w[r0:r0 + ts, lanes])
        y_ref[...] = jnp.concatenate(outs, axis=1)


def nsa_decode(proj, slopes, page_table, ckp, cvp, new_rows, win_t, pool_t, *, pg):
    batch, n_pages = page_table.shape
    page_rows = pool_t.shape[2]
    ts = new_rows.shape[1]
    past = n_pages * page_rows
    ng = n_pages // pg
    rows = pg * page_rows
    n_c = ckp.shape[1]
    wl = win_t.shape[2]
    n_blk = -(-(past // SEL_BLOCK + 1) // LANES) * LANES
    key_blk = jnp.arange(past, dtype=jnp.int32).reshape(ng, 1, rows) // SEL_BLOCK
    e_tab = jnp.where(key_blk == jnp.arange(n_blk, dtype=jnp.int32)[None, :, None], -MASK_BIG, 0.0).astype(BF16)
    grid_spec = pltpu.PrefetchScalarGridSpec(
        num_scalar_prefetch=2,
        grid=(batch, ng),
        in_specs=[
            pl.BlockSpec((ts, proj.shape[1]), lambda b, g, sl, pt: (b, 0)),
            pl.BlockSpec((1, n_c, KV_DIM), lambda b, g, sl, pt: (b, 0, 0)),
            pl.BlockSpec((1, n_c, KV_DIM), lambda b, g, sl, pt: (b, 0, 0)),
            pl.BlockSpec((1, ts, new_rows.shape[2]), lambda b, g, sl, pt: (b, 0, 0)),
            pl.BlockSpec((1, 2 * KV_DIM, wl), lambda b, g, sl, pt: (b, 0, 0)),
            pl.BlockSpec((1, n_blk, rows), lambda b, g, sl, pt: (g, 0, 0)),
            pl.BlockSpec(memory_space=pl.ANY),
        ],
        out_specs=pl.BlockSpec((ts, NSA_DIM), lambda b, g, sl, pt: (b, 0)),
        scratch_shapes=[
            pltpu.VMEM((2, pg, 2 * KV_DIM, page_rows), F32), pltpu.SemaphoreType.DMA((2,)),
            pltpu.VMEM((LANES, KV_DIM + n_blk), BF16),
            pltpu.VMEM((LANES, LANES), F32), pltpu.VMEM((LANES, LANES), F32), pltpu.VMEM((LANES, KV_DIM), F32),
            pltpu.VMEM((LANES, KV_DIM), F32), pltpu.VMEM((LANES, KV_DIM), F32),
        ],
    )
    return pl.pallas_call(
        functools.partial(_nsa_decode_kernel, n_pages=n_pages, pg=pg, past=past),
        out_shape=jax.ShapeDtypeStruct((batch * ts, NSA_DIM), F32),
        grid_spec=grid_spec,
        compiler_params=_cparams("arbitrary", "arbitrary"),
        name="nsa_decode",
    )(slopes, page_table.reshape(-1), proj, ckp, cvp, new_rows, win_t, e_tab, pool_t)


S_MQ_COL =NSA_GROUP * KV_DIM
S_COLS = S_MQ_COL + MEM_DIM + LANES


def _layout_w_in_b_sample(w):
    d = w.shape[0]
    qw = w[:, :NSA_DIM].reshape(d, NSA_KV_HEADS, NSA_GROUP, HEAD_DIM).transpose(0, 2, 1, 3).reshape(d, NSA_DIM)
    gw = jnp.pad(w[:, NSA_DIM:NSA_DIM + 3 * NSA_HEADS], ((0, 0), (0, LANES - 3 * NSA_HEADS)))
    mw = w[:, NSA_DIM + 3 * NSA_HEADS:]
    return jnp.concatenate([qw, mw, gw], axis=1).astype(BF16)


Q_SLOT = 256
Q_GATE_LANE = NSA_GROUP * HEAD_DIM
B_MQ_COL = NSA_KV_HEADS * Q_SLOT
B_COLS = B_MQ_COL + MEM_DIM


def _layout_w_in_b(w):
    d = w.shape[0]
    qw = w[:, :NSA_DIM].reshape(d, NSA_KV_HEADS, NSA_GROUP * HEAD_DIM)
    gw = w[:, NSA_DIM:NSA_DIM + 3 * NSA_HEADS].reshape(d, NSA_KV_HEADS, 3 * NSA_GROUP)
    slot = jnp.concatenate([qw, gw], axis=2)
    slot = jnp.pad(slot, ((0, 0), (0, 0), (0, Q_SLOT - slot.shape[2]))).reshape(d, NSA_KV_HEADS * Q_SLOT)
    mw = w[:, NSA_DIM + 3 * NSA_HEADS:]
    return jnp.concatenate([slot, mw], axis=1).astype(BF16)


def kernel(x_prompt, x_sample, state_conv, cache_mem_kv, cache_cmp_kv, cache_slc_kv, state_win_kv, page_table,
           mem_prompt, g_mix, w_in_a, conv_w, w_in_b, w_o, w_mkv, g_mem, g_kv, w_kv, pe_ck, w1_ck, w2_ck,
           pe_cv, w1_cv, w2_cv, g_ffn, w_gu, w_dn, g_final):
    bp, tp, d = x_prompt.shape
    bs, ts = x_sample.shape[:2]
    depth = g_mix.shape[0]
    n_a = w_in_a.shape[0]
    n_mem = mem_prompt.shape[1]
    win_len = state_win_kv.shape[1]
    past_len = page_table.shape[1] * cache_cmp_kv.shape[1]
    conv_dim = conv_w.shape[2]
    slopes = jnp.asarray(np.array(_alibi_list(NSA_HEADS), dtype=np.float32))

    w_in_a16 = w_in_a.astype(BF16)
    w_in_b16 = [_layout_w_in_b(w_in_b[j]) for j in range(depth - n_a)]
    w_o16 = w_o.astype(BF16)
    w_gu16 = w_gu.astype(BF16)
    w_dn16 = w_dn.astype(BF16)
    w_kv16 = w_kv.astype(BF16)
    w_mkv16 = w_mkv.transpose(1, 0, 2).reshape(d, depth * 2 * MEM_DIM).astype(BF16)

    mkv = rms_matmul(mem_prompt.reshape(bp * n_mem, d), g_mem, w_mkv16)
    mem_kv_p = mkv.reshape(bp, n_mem, depth, 2 * MEM_DIM).transpose(2, 0, 1, 3)
    mem_kv_s = cache_mem_kv.transpose(0, 1, 3, 4, 5, 2).reshape(depth, bs, 2 * MEM_DIM, n_mem)

    groups = [
        dict(x=x_prompt.reshape(bp * tp, d), b=bp, t=tp, mem=mem_kv_p, st=jnp.zeros((n_a, bp, 2, conv_dim), F32)),
        dict(x=x_sample.reshape(bs * ts, d), b=bs, t=ts, mem=mem_kv_s, st=state_conv),
    ]
    conv_out = [[], []]
    kv_rows = [None, None]

    cw = _compress_weights(pe_ck, w1_ck, w2_ck) + _compress_weights(pe_cv, w1_cv, w2_cv)
    nsa_in = None
    w_in_bs16 = [_layout_w_in_b_sample(w_in_b[j]) for j in range(depth - n_a)]
    n_pages = page_table.shape[1]
    page_rows = cache_cmp_kv.shape[1]
    pg = 16 if n_pages % 16 == 0 else n_pages
    assert page_rows == LANES
    pool_cmp = cache_cmp_kv.transpose(0, 2, 3, 4, 1).reshape(cache_cmp_kv.shape[0], 2 * KV_DIM, page_rows)
    pool_slc = cache_slc_kv.transpose(0, 2, 3, 4, 1).reshape(cache_slc_kv.shape[0], 2 * KV_DIM, page_rows)
    win_state = state_win_kv.transpose(0, 2, 3, 4, 1).reshape(bs, 2 * KV_DIM, win_len)
    assert ts == 8 and past_len % SEL_BLOCK == 0 and tp % 256 == 0 and tp // SEL_BLOCK <= N_SEL_POS

    def wo_pairs(l, n, y_main, y_mem, nsa_layout, b, t):
        tm = _row_tile(n, 1024)
        if nsa_layout:
            hw = NSA_GROUP * HEAD_DIM
            nt = t // tm
            y2d = y_main.reshape(b * NSA_KV_HEADS * t, hw)
            pairs = [(y2d, (tm, hw), (lambda i, h=h: (((i // nt) * NSA_KV_HEADS + h) * nt + i % nt, 0)),
                      w_o16[l, h * hw:(h + 1) * hw]) for h in range(NSA_KV_HEADS)]
            km = NSA_DIM
        else:
            km = y_main.shape[1]
            pairs = [(y_main, (tm, km), lambda i: (i, 0), w_o16[l, :km])]
        return pairs + [(y_mem, (tm, MEM_DIM), lambda i: (i, 0), w_o16[l, km:])]

    for l in range(depth):
        for gi, gr in enumerate(groups):
            x, b, t = gr["x"], gr["b"], gr["t"]
            n = b * t
            nsa_layout = False
            if l < n_a:
                proj = rms_matmul(x, g_mix[l], w_in_a16[l], tn=w_in_a16.shape[2] // 2)
                y_main, new_st = gated_conv(proj, gr["st"][l], conv_w[l], b, t)
                conv_out[gi].append(new_st)
                mem_fn = mem_attention if gi == 0 else mem_attention_decode
                y_mem = mem_fn(proj, 3 * conv_dim // MEM_DIM, gr["mem"][l], b, t)
            else:
                if l == n_a:
                    if gi == 0:
                        kv_t = rms_matmul(x, g_kv, w_kv16, tn=3 * KV_DIM, feature_major_batch=b)
                        kv_rows[gi] = kv_t
                        ck, cv = compress_prompt(kv_t, cw)
                        nsa_in = (_cmp_slots(ck, b), _cmp_slots(cv, b)) + kv_layout(kv_t)
                    else:
                        kv2d = rms_matmul(x, g_kv, w_kv16, tn=3 * KV_DIM)
                        kv_rows[gi] = kv2d
                        new3 = kv2d.reshape(b, t, 6 * KV_DIM)
                        ck_s, cv_s = compress_sample(page_table, pool_cmp, new3, cw, pg=pg)
                        per = SEL_BLOCK // CMP_STRIDE
                        slots = lambda c: c.reshape(b, c.shape[1] // per, per, KV_DIM).transpose(0, 2, 1, 3).reshape(
                            c.shape)
                        ck_s, cv_s = slots(ck_s), slots(cv_s)
                j = l - n_a
                if gi == 0:
                    proj = rms_matmul(x, g_mix[l], w_in_b16[j], tn=B_COLS // 2)
                    y_main = nsa_prompt(proj, slopes, *nsa_in, b, t)
                    nsa_layout = True
                    y_mem = mem_attention(proj, B_MQ_COL // MEM_DIM, gr["mem"][l], b, t)
                else:
                    proj = rms_matmul(x, g_mix[l], w_in_bs16[j], tn=S_COLS)
                    y_main = nsa_decode(proj, slopes, page_table, ck_s, cv_s, new3, win_state, pool_slc, pg=pg)
                    y_mem = mem_attention_decode(proj, S_MQ_COL // MEM_DIM, gr["mem"][l], b, t)
            x = proj_residual(x, wo_pairs(l, n, y_main, y_mem, nsa_layout, b, t))
            x = ffn(x, g_ffn[l], w_gu16[l], w_dn16[l])
            gr["x"] = x

    y_prompt = rmsnorm_rows(groups[0]["x"], g_final).reshape(bp, tp, d)
    y_sample = rmsnorm_rows(groups[1]["x"], g_final).reshape(bs, ts, d)
    conv_state_p = jnp.stack(conv_out[0])
    conv_state_s = jnp.stack(conv_out[1])
    mem_kv_out = mem_kv_p.reshape(depth, bp, n_mem, 2, MEM_HEADS, HEAD_DIM)
    def branch_rows(kv2d, br, b, t):
        return kv2d[:, br * 2 * KV_DIM:(br + 1) * 2 * KV_DIM].reshape(b, t, 2, NSA_KV_HEADS, HEAD_DIM)

    def branch_rows_t(kv_t, br, t0):
        rows = kv_t[:, br * 2 * KV_DIM:(br + 1) * 2 * KV_DIM, t0:]
        return rows.reshape(kv_t.shape[0], 2, NSA_KV_HEADS, HEAD_DIM, -1).transpose(0, 4, 1, 2, 3)

    kvp, kvs = kv_rows
    win_kv_s = jnp.concatenate([state_win_kv, branch_rows(kvs, 2, bs, ts)], axis=1)[:, ts:]
    return (y_prompt, y_sample, conv_state_p, conv_state_s, mem_kv_out, branch_rows_t(kvp, 0, 0),
            branch_rows_t(kvp, 1, 0), branch_rows_t(kvp, 2, tp - min(WINDOW, tp)), branch_rows(kvs, 0, bs, ts),
            branch_rows(kvs, 1, bs, ts), win_kv_s)
```
